```python
import jax, jax.numpy as jnp
from jax import lax
import numpy as np

D_MODEL = 1024
BATCH = 8
SEQ = 16384
DEPTH = 4

N_MIXERS = 3
N_A = len(range(0, DEPTH, N_MIXERS))
N_B = len(range(1, DEPTH, N_MIXERS))
N_C = len(range(2, DEPTH, N_MIXERS))
N_SUBLAYERS = 3
N_MOD = 3
CONV_WIDTH = 31
N_HEADS = 16
HEAD_DIM = D_MODEL // N_HEADS
Q_BLOCK = 128
POOL_WINDOWS = (2, 4, 8, 16)
POOL_GROUPS = len(POOL_WINDOWS)
POOL_GROUP_DIM = D_MODEL // POOL_GROUPS
D_FF = ((8 * D_MODEL // 3 + 127) // 128) * 128
EPS = 1e-6

kernel_name = "hybrid_conv_stickbreak_pool_macaron"


def _bc(v, ndim):
    return v.reshape((1,) * (ndim - v.ndim) + v.shape)


def rms_norm(x, g):
    xf = x.astype(jnp.float32)
    y = xf * lax.rsqrt(jnp.mean(xf * xf, axis=-1, keepdims=True) + EPS)
    return (y * _bc(g.astype(jnp.float32), x.ndim)).astype(x.dtype)


def layer_norm(x, g, b):
    xf = x.astype(jnp.float32)
    mu = jnp.mean(xf, axis=-1, keepdims=True)
    var = jnp.mean(jnp.square(xf - mu), axis=-1, keepdims=True)
    y = (xf - mu) * lax.rsqrt(var + EPS)
    y = y * _bc(g.astype(jnp.float32), x.ndim) + _bc(b.astype(jnp.float32), x.ndim)
    return y.astype(x.dtype)


def modulate(x, g, shift, scale):
    return rms_norm(x, g) * (1 + scale[:, None, :]) + shift[:, None, :]


def swiglu(h, w1, w3, w2):
    return (jax.nn.silu(h @ w1) * (h @ w3)) @ w2


def conformer_conv(h, w_in, b_in, dw, dw_b, ln_g, ln_b, w_out, b_out):
    u = jax.nn.glu(h @ w_in + _bc(b_in, 3), axis=-1)
    v = lax.conv_general_dilated(
        u, dw[:, None, :].astype(u.dtype), window_strides=(1,),
        padding=[(CONV_WIDTH - 1, 0)],
        dimension_numbers=("NWC", "WIO", "NWC"),
        feature_group_count=D_MODEL) + _bc(dw_b, 3)
    v = jax.nn.silu(layer_norm(v, ln_g, ln_b))
    return v @ w_out + _bc(b_out, 3)


def stick_breaking_attention(h, w_qkv, q_g, k_g, w_o):
    B, S, _ = h.shape
    qkv = h @ w_qkv
    q, k, v = jnp.split(qkv, 3, axis=-1)
    to_heads = lambda t: t.reshape(B, S, N_HEADS, HEAD_DIM).transpose(0, 2, 1, 3)
    q = rms_norm(to_heads(q), q_g)
    k = rms_norm(to_heads(k), k_g)
    qf = q.astype(jnp.float32) * (HEAD_DIM ** -0.5)
    kf = k.astype(jnp.float32)
    vf = to_heads(v).astype(jnp.float32)
    nb = S // Q_BLOCK
    q_blocks = qf.reshape(B, N_HEADS, nb, Q_BLOCK, HEAD_DIM).transpose(2, 0, 1, 3, 4)
    key_pos = jnp.arange(S)

    def block(args):
        qb, bi = args
        z = jnp.einsum('bhqd,bhkd->bhqk', qb, kf)
        t = bi * Q_BLOCK + jnp.arange(Q_BLOCK)
        mask = (key_pos[None, :] < t[:, None])[None, None]
        neg_log_keep = jnp.where(mask, jax.nn.softplus(z), 0.0)
        rcs = lax.cumsum(neg_log_keep, axis=3, reverse=True)
        suffix = jnp.concatenate([rcs[..., 1:], jnp.zeros_like(rcs[..., :1])], axis=-1)
        a = jnp.where(mask, jnp.exp(jax.nn.log_sigmoid(z) - suffix), 0.0)
        return jnp.einsum('bhqk,bhkd->bhqd', a, vf)

    o = lax.map(block, (q_blocks, jnp.arange(nb)))
    o = o.transpose(1, 0, 3, 2, 4).reshape(B, S, D_MODEL).astype(h.dtype)
    return o @ w_o


def multiscale_pool(h, p_w, p_b, p_scale):
    B, S, _ = h.shape
    hf = h.astype(jnp.float32)
    cs0 = jnp.concatenate([jnp.zeros((B, 1, D_MODEL), jnp.float32), lax.cumsum(hf, axis=1)], axis=1)
    t = jnp.arange(S)
    diffs = []
    for g, w in enumerate(POOL_WINDOWS):
        sl = slice(g * POOL_GROUP_DIM, (g + 1) * POOL_GROUP_DIM)
        c_g = cs0[:, :, sl]
        lag = jnp.concatenate([jnp.zeros((B, w - 1, POOL_GROUP_DIM), jnp.float32), c_g[:, :S + 1 - w]], axis=1)
        cnt = jnp.minimum(t + 1, w).astype(jnp.float32)[None, :, None]
        diffs.append((c_g[:, 1:] - lag) / cnt - hf[:, :, sl])
    d = jnp.stack(diffs, axis=2)
    y = jnp.einsum('bsgc,gcd->bsgd', d, p_w.astype(jnp.float32)) + _bc(p_b.astype(jnp.float32), 4)
    return (y.reshape(B, S, D_MODEL) * _bc(p_scale.astype(jnp.float32), 3)).astype(h.dtype)


def _fwd_setup_inputs(seed: int = 0) -> dict:
    key = jax.random.key(seed)
    ks = jax.random.split(key, 26)
    n = lambda k, shape, s: jax.random.normal(k, shape, jnp.float32) * s
    D, F, G = D_MODEL, D_FF, POOL_GROUP_DIM
    return {
        "x": n(ks[0], (BATCH, SEQ, D), 1.0),
        "c": n(ks[1], (BATCH, D), 1.0),
        "cond_w": n(ks[2], (D, D), D ** -0.5),
        "cond_b": n(ks[3], (D,), 0.02),
        "ada_w": n(ks[4], (DEPTH, D, N_SUBLAYERS * N_MOD * D), 0.1 * D ** -0.5),
        "ada_b": n(ks[5], (DEPTH, N_SUBLAYERS * N_MOD * D), 0.02),
        "norm_g": 1.0 + n(ks[6], (DEPTH, N_SUBLAYERS, D), 0.02),
        "ffn_w1": n(ks[7], (DEPTH, 2, D, F), D ** -0.5),
        "ffn_w3": n(ks[8], (DEPTH, 2, D, F), D ** -0.5),
        "ffn_w2": n(ks[9], (DEPTH, 2, F, D), F ** -0.5),
        "a_w_in": n(ks[10], (N_A, D, 2 * D), D ** -0.5),
        "a_b_in": n(ks[11], (N_A, 2 * D), 0.02),
        "a_dw": n(ks[12], (N_A, CONV_WIDTH, D), CONV_WIDTH ** -0.5),
        "a_dw_b": n(ks[13], (N_A, D), 0.02),
        "a_ln_g": 1.0 + n(ks[14], (N_A, D), 0.02),
        "a_ln_b": n(ks[15], (N_A, D), 0.02),
        "a_w_out": n(ks[16], (N_A, D, D), D ** -0.5),
        "a_b_out": n(ks[17], (N_A, D), 0.02),
        "b_w_qkv": n(ks[18], (N_B, D, 3 * D), D ** -0.5),
        "b_q_g": 1.0 + n(ks[19], (N_B, HEAD_DIM), 0.02),
        "b_k_g": 1.0 + n(ks[20], (N_B, HEAD_DIM), 0.02),
        "b_w_o": n(ks[21], (N_B, D, D), D ** -0.5),
        "p_w": n(ks[22], (N_C, POOL_GROUPS, G, G), G ** -0.5),
        "p_b": n(ks[23], (N_C, POOL_GROUPS, G), 0.02),
        "p_scale": 1.0 + n(ks[24], (N_C, D), 0.1),
    }


def _fwd_reference(x, c, cond_w, cond_b, ada_w, ada_b, norm_g, ffn_w1, ffn_w3, ffn_w2,
              a_w_in, a_b_in, a_dw, a_dw_b, a_ln_g, a_ln_b, a_w_out, a_b_out,
              b_w_qkv, b_q_g, b_k_g, b_w_o, p_w, p_b, p_scale):
    B = x.shape[0]
    e = jax.nn.silu(c @ cond_w + cond_b[None, :])
    ia = ib = ic = 0
    for i in range(DEPTH):
        mod = (e @ ada_w[i] + ada_b[i][None, :]).reshape(B, N_SUBLAYERS, N_MOD, D_MODEL)
        shift, scale, gate = mod[:, :, 0], mod[:, :, 1], 1 + mod[:, :, 2]
        h = modulate(x, norm_g[i, 0], shift[:, 0], scale[:, 0])
        x = x + 0.5 * gate[:, 0][:, None, :] * swiglu(h, ffn_w1[i, 0], ffn_w3[i, 0], ffn_w2[i, 0])
        h = modulate(x, norm_g[i, 1], shift[:, 1], scale[:, 1])
        kind = i % N_MIXERS
        if kind == 0:
            y = conformer_conv(h, a_w_in[ia], a_b_in[ia], a_dw[ia], a_dw_b[ia],
                               a_ln_g[ia], a_ln_b[ia], a_w_out[ia], a_b_out[ia])
            ia += 1
        elif kind == 1:
            y = stick_breaking_attention(h, b_w_qkv[ib], b_q_g[ib], b_k_g[ib], b_w_o[ib])
            ib += 1
        else:
            y = multiscale_pool(h, p_w[ic], p_b[ic], p_scale[ic])
            ic += 1
        x = x + gate[:, 1][:, None, :] * y
        h = modulate(x, norm_g[i, 2], shift[:, 2], scale[:, 2])
        x = x + 0.5 * gate[:, 2][:, None, :] * swiglu(h, ffn_w1[i, 1], ffn_w3[i, 1], ffn_w2[i, 1])
    return x


import jax as _jax
import jax.numpy as _jnp

TWIN_FORMAT = 'train_step'
FWD_PARAMS = ['x', 'c', 'cond_w', 'cond_b', 'ada_w', 'ada_b', 'norm_g', 'ffn_w1', 'ffn_w3', 'ffn_w2', 'a_w_in', 'a_b_in', 'a_dw', 'a_dw_b', 'a_ln_g', 'a_ln_b', 'a_w_out', 'a_b_out', 'b_w_qkv', 'b_q_g', 'b_k_g', 'b_w_o', 'p_w', 'p_b', 'p_scale']
TWIN_WEIGHTS = ['cond_w', 'cond_b', 'ada_w', 'ada_b', 'norm_g', 'ffn_w1', 'ffn_w3', 'ffn_w2', 'a_w_in', 'a_b_in', 'a_dw', 'a_dw_b', 'a_ln_g', 'a_ln_b', 'a_w_out', 'a_b_out', 'b_w_qkv', 'b_q_g', 'b_k_g', 'b_w_o', 'p_w', 'p_b', 'p_scale']
TWIN_DIFF_INPUT = 'x'
TWIN_INPUTS = ['x', 'c', 'cond_w', 'cond_b', 'ada_w', 'ada_b', 'norm_g', 'ffn_w1', 'ffn_w3', 'ffn_w2', 'a_w_in', 'a_b_in', 'a_dw', 'a_dw_b', 'a_ln_g', 'a_ln_b', 'a_w_out', 'a_b_out', 'b_w_qkv', 'b_q_g', 'b_k_g', 'b_w_o', 'p_w', 'p_b', 'p_scale', 'loss_target', 'm_cond_w', 'm_cond_b', 'm_ada_w', 'm_ada_b', 'm_norm_g', 'm_ffn_w1', 'm_ffn_w3', 'm_ffn_w2', 'm_a_w_in', 'm_a_b_in', 'm_a_dw', 'm_a_dw_b', 'm_a_ln_g', 'm_a_ln_b', 'm_a_w_out', 'm_a_b_out', 'm_b_w_qkv', 'm_b_q_g', 'm_b_k_g', 'm_b_w_o', 'm_p_w', 'm_p_b', 'm_p_scale', 'v_cond_w', 'v_cond_b', 'v_ada_w', 'v_ada_b', 'v_norm_g', 'v_ffn_w1', 'v_ffn_w3', 'v_ffn_w2', 'v_a_w_in', 'v_a_b_in', 'v_a_dw', 'v_a_dw_b', 'v_a_ln_g', 'v_a_ln_b', 'v_a_w_out', 'v_a_b_out', 'v_b_w_qkv', 'v_b_q_g', 'v_b_k_g', 'v_b_w_o', 'v_p_w', 'v_p_b', 'v_p_scale']
TWIN_OUTPUTS = ['loss', 'grad_x', 'grad_cond_w', 'grad_cond_b', 'grad_ada_w', 'grad_ada_b', 'grad_norm_g', 'grad_ffn_w1', 'grad_ffn_w3', 'grad_ffn_w2', 'grad_a_w_in', 'grad_a_b_in', 'grad_a_dw', 'grad_a_dw_b', 'grad_a_ln_g', 'grad_a_ln_b', 'grad_a_w_out', 'grad_a_b_out', 'grad_b_w_qkv', 'grad_b_q_g', 'grad_b_k_g', 'grad_b_w_o', 'grad_p_w', 'grad_p_b', 'grad_p_scale', 'delta_cond_w', 'delta_cond_b', 'delta_ada_w', 'delta_ada_b', 'delta_norm_g', 'delta_ffn_w1', 'delta_ffn_w3', 'delta_ffn_w2', 'delta_a_w_in', 'delta_a_b_in', 'delta_a_dw', 'delta_a_dw_b', 'delta_a_ln_g', 'delta_a_ln_b', 'delta_a_w_out', 'delta_a_b_out', 'delta_b_w_qkv', 'delta_b_q_g', 'delta_b_k_g', 'delta_b_w_o', 'delta_p_w', 'delta_p_b', 'delta_p_scale', 'new_m_cond_w', 'new_m_cond_b', 'new_m_ada_w', 'new_m_ada_b', 'new_m_norm_g', 'new_m_ffn_w1', 'new_m_ffn_w3', 'new_m_ffn_w2', 'new_m_a_w_in', 'new_m_a_b_in', 'new_m_a_dw', 'new_m_a_dw_b', 'new_m_a_ln_g', 'new_m_a_ln_b', 'new_m_a_w_out', 'new_m_a_b_out', 'new_m_b_w_qkv', 'new_m_b_q_g', 'new_m_b_k_g', 'new_m_b_w_o', 'new_m_p_w', 'new_m_p_b', 'new_m_p_scale', 'new_v_cond_w', 'new_v_cond_b', 'new_v_ada_w', 'new_v_ada_b', 'new_v_norm_g', 'new_v_ffn_w1', 'new_v_ffn_w3', 'new_v_ffn_w2', 'new_v_a_w_in', 'new_v_a_b_in', 'new_v_a_dw', 'new_v_a_dw_b', 'new_v_a_ln_g', 'new_v_a_ln_b', 'new_v_a_w_out', 'new_v_a_b_out', 'new_v_b_w_qkv', 'new_v_b_q_g', 'new_v_b_k_g', 'new_v_b_w_o', 'new_v_p_w', 'new_v_p_b', 'new_v_p_scale']
TWIN_LEAF_KINDS = {'loss': 'loss', 'grad_x': 'grad_x', 'grad_cond_w': 'grad_w', 'grad_cond_b': 'grad_w', 'grad_ada_w': 'grad_w', 'grad_ada_b': 'grad_w', 'grad_norm_g': 'grad_w', 'grad_ffn_w1': 'grad_w', 'grad_ffn_w3': 'grad_w', 'grad_ffn_w2': 'grad_w', 'grad_a_w_in': 'grad_w', 'grad_a_b_in': 'grad_w', 'grad_a_dw': 'grad_w', 'grad_a_dw_b': 'grad_w', 'grad_a_ln_g': 'grad_w', 'grad_a_ln_b': 'grad_w', 'grad_a_w_out': 'grad_w', 'grad_a_b_out': 'grad_w', 'grad_b_w_qkv': 'grad_w', 'grad_b_q_g': 'grad_w', 'grad_b_k_g': 'grad_w', 'grad_b_w_o': 'grad_w', 'grad_p_w': 'grad_w', 'grad_p_b': 'grad_w', 'grad_p_scale': 'grad_w', 'delta_cond_w': 'delta_w', 'delta_cond_b': 'delta_w', 'delta_ada_w': 'delta_w', 'delta_ada_b': 'delta_w', 'delta_norm_g': 'delta_w', 'delta_ffn_w1': 'delta_w', 'delta_ffn_w3': 'delta_w', 'delta_ffn_w2': 'delta_w', 'delta_a_w_in': 'delta_w', 'delta_a_b_in': 'delta_w', 'delta_a_dw': 'delta_w', 'delta_a_dw_b': 'delta_w', 'delta_a_ln_g': 'delta_w', 'delta_a_ln_b': 'delta_w', 'delta_a_w_out': 'delta_w', 'delta_a_b_out': 'delta_w', 'delta_b_w_qkv': 'delta_w', 'delta_b_q_g': 'delta_w', 'delta_b_k_g': 'delta_w', 'delta_b_w_o': 'delta_w', 'delta_p_w': 'delta_w', 'delta_p_b': 'delta_w', 'delta_p_scale': 'delta_w', 'new_m_cond_w': 'new_m', 'new_m_cond_b': 'new_m', 'new_m_ada_w': 'new_m', 'new_m_ada_b': 'new_m', 'new_m_norm_g': 'new_m', 'new_m_ffn_w1': 'new_m', 'new_m_ffn_w3': 'new_m', 'new_m_ffn_w2': 'new_m', 'new_m_a_w_in': 'new_m', 'new_m_a_b_in': 'new_m', 'new_m_a_dw': 'new_m', 'new_m_a_dw_b': 'new_m', 'new_m_a_ln_g': 'new_m', 'new_m_a_ln_b': 'new_m', 'new_m_a_w_out': 'new_m', 'new_m_a_b_out': 'new_m', 'new_m_b_w_qkv': 'new_m', 'new_m_b_q_g': 'new_m', 'new_m_b_k_g': 'new_m', 'new_m_b_w_o': 'new_m', 'new_m_p_w': 'new_m', 'new_m_p_b': 'new_m', 'new_m_p_scale': 'new_m', 'new_v_cond_w': 'new_v', 'new_v_cond_b': 'new_v', 'new_v_ada_w': 'new_v', 'new_v_ada_b': 'new_v', 'new_v_norm_g': 'new_v', 'new_v_ffn_w1': 'new_v', 'new_v_ffn_w3': 'new_v', 'new_v_ffn_w2': 'new_v', 'new_v_a_w_in': 'new_v', 'new_v_a_b_in': 'new_v', 'new_v_a_dw': 'new_v', 'new_v_a_dw_b': 'new_v', 'new_v_a_ln_g': 'new_v', 'new_v_a_ln_b': 'new_v', 'new_v_a_w_out': 'new_v', 'new_v_a_b_out': 'new_v', 'new_v_b_w_qkv': 'new_v', 'new_v_b_q_g': 'new_v', 'new_v_b_k_g': 'new_v', 'new_v_b_w_o': 'new_v', 'new_v_p_w': 'new_v', 'new_v_p_b': 'new_v', 'new_v_p_scale': 'new_v'}


def _forward(args):
    return _fwd_reference(*[args[k] for k in FWD_PARAMS])


def _output_shape():
    def fwd():
        inp = _fwd_setup_inputs(0)
        return _fwd_reference(*[inp[k] for k in FWD_PARAMS])
    out = _jax.eval_shape(fwd)
    return out.shape, out.dtype

N_MICROBATCH = 1
ADAM_LR = 0.001
ADAM_B1 = 0.9
ADAM_B2 = 0.999
ADAM_EPS = 1e-08
ADAM_WD = 0.01
ADAM_STEP = 10
PER_EXAMPLE_BATCH_AXIS = {'x': 0, 'c': 0, 'loss_target': 0}
SHARED_INPUTS = []
_WEIGHT_DTYPES = {'cond_w': _jnp.float32, 'cond_b': _jnp.float32, 'ada_w': _jnp.float32, 'ada_b': _jnp.float32, 'norm_g': _jnp.float32, 'ffn_w1': _jnp.float32, 'ffn_w3': _jnp.float32, 'ffn_w2': _jnp.float32, 'a_w_in': _jnp.float32, 'a_b_in': _jnp.float32, 'a_dw': _jnp.float32, 'a_dw_b': _jnp.float32, 'a_ln_g': _jnp.float32, 'a_ln_b': _jnp.float32, 'a_w_out': _jnp.float32, 'a_b_out': _jnp.float32, 'b_w_qkv': _jnp.float32, 'b_q_g': _jnp.float32, 'b_k_g': _jnp.float32, 'b_w_o': _jnp.float32, 'p_w': _jnp.float32, 'p_b': _jnp.float32, 'p_scale': _jnp.float32}
MOMENT_SCALE = {'cond_w': 4.774382e+00, 'cond_b': 1.119608e+01, 'ada_w': 9.229254e+00, 'ada_b': 3.344718e+01, 'norm_g': 3.813088e+01, 'ffn_w1': 4.459963e-01, 'ffn_w3': 5.115134e-01, 'ffn_w2': 8.320866e-01, 'a_w_in': 1.411953e+00, 'a_b_in': 2.046201e+01, 'a_dw': 3.158207e+00, 'a_dw_b': 4.522935e+01, 'a_ln_g': 5.967608e+01, 'a_ln_b': 4.384799e+01, 'a_w_out': 1.052942e+01, 'a_b_out': 5.314085e+01, 'b_w_qkv': 3.751522e+00, 'b_q_g': 1.271386e+02, 'b_k_g': 1.270499e+02, 'b_w_o': 6.393478e+00, 'p_w': 9.548498e+00, 'p_b': 4.895680e+01, 'p_scale': 9.680847e+01}


def _to_microbatches(a, axis):
    t = _jnp.moveaxis(a, axis, 0)
    t = t.reshape((N_MICROBATCH, t.shape[0] // N_MICROBATCH) + t.shape[1:])
    return _jnp.moveaxis(t, 1, axis + 1)


def setup_inputs(seed: int = 0) -> dict:
    inp = _fwd_setup_inputs(seed)
    key = _jax.random.fold_in(_jax.random.key(seed), 7919)
    shape, _ = _output_shape()
    out = dict(inp)
    out["loss_target"] = _jax.random.normal(_jax.random.fold_in(key, 0), shape, _jnp.float32)
    for i, name in enumerate(TWIN_WEIGHTS):
        w = inp[name].astype(_jnp.float32)
        if MOMENT_SCALE is None:
            s = _jnp.sqrt(_jnp.mean(_jnp.square(w)) + 1e-30)
        else:
            s = MOMENT_SCALE[name]
        km, kv = _jax.random.split(_jax.random.fold_in(key, i + 1))
        out[name] = w
        out["m_" + name] = s * _jax.random.normal(km, w.shape, _jnp.float32)
        out["v_" + name] = (s * s) * _jax.random.uniform(kv, w.shape, _jnp.float32, 0.5, 1.5)
    if N_MICROBATCH > 1:
        for name, axis in PER_EXAMPLE_BATCH_AXIS.items():
            out[name] = _to_microbatches(out[name], axis)
    return {'x': out['x'], 'c': out['c'], 'cond_w': out['cond_w'], 'cond_b': out['cond_b'], 'ada_w': out['ada_w'], 'ada_b': out['ada_b'], 'norm_g': out['norm_g'], 'ffn_w1': out['ffn_w1'], 'ffn_w3': out['ffn_w3'], 'ffn_w2': out['ffn_w2'], 'a_w_in': out['a_w_in'], 'a_b_in': out['a_b_in'], 'a_dw': out['a_dw'], 'a_dw_b': out['a_dw_b'], 'a_ln_g': out['a_ln_g'], 'a_ln_b': out['a_ln_b'], 'a_w_out': out['a_w_out'], 'a_b_out': out['a_b_out'], 'b_w_qkv': out['b_w_qkv'], 'b_q_g': out['b_q_g'], 'b_k_g': out['b_k_g'], 'b_w_o': out['b_w_o'], 'p_w': out['p_w'], 'p_b': out['p_b'], 'p_scale': out['p_scale'], 'loss_target': out['loss_target'], 'm_cond_w': out['m_cond_w'], 'm_cond_b': out['m_cond_b'], 'm_ada_w': out['m_ada_w'], 'm_ada_b': out['m_ada_b'], 'm_norm_g': out['m_norm_g'], 'm_ffn_w1': out['m_ffn_w1'], 'm_ffn_w3': out['m_ffn_w3'], 'm_ffn_w2': out['m_ffn_w2'], 'm_a_w_in': out['m_a_w_in'], 'm_a_b_in': out['m_a_b_in'], 'm_a_dw': out['m_a_dw'], 'm_a_dw_b': out['m_a_dw_b'], 'm_a_ln_g': out['m_a_ln_g'], 'm_a_ln_b': out['m_a_ln_b'], 'm_a_w_out': out['m_a_w_out'], 'm_a_b_out': out['m_a_b_out'], 'm_b_w_qkv': out['m_b_w_qkv'], 'm_b_q_g': out['m_b_q_g'], 'm_b_k_g': out['m_b_k_g'], 'm_b_w_o': out['m_b_w_o'], 'm_p_w': out['m_p_w'], 'm_p_b': out['m_p_b'], 'm_p_scale': out['m_p_scale'], 'v_cond_w': out['v_cond_w'], 'v_cond_b': out['v_cond_b'], 'v_ada_w': out['v_ada_w'], 'v_ada_b': out['v_ada_b'], 'v_norm_g': out['v_norm_g'], 'v_ffn_w1': out['v_ffn_w1'], 'v_ffn_w3': out['v_ffn_w3'], 'v_ffn_w2': out['v_ffn_w2'], 'v_a_w_in': out['v_a_w_in'], 'v_a_b_in': out['v_a_b_in'], 'v_a_dw': out['v_a_dw'], 'v_a_dw_b': out['v_a_dw_b'], 'v_a_ln_g': out['v_a_ln_g'], 'v_a_ln_b': out['v_a_ln_b'], 'v_a_w_out': out['v_a_w_out'], 'v_a_b_out': out['v_a_b_out'], 'v_b_w_qkv': out['v_b_w_qkv'], 'v_b_q_g': out['v_b_q_g'], 'v_b_k_g': out['v_b_k_g'], 'v_b_w_o': out['v_b_w_o'], 'v_p_w': out['v_p_w'], 'v_p_b': out['v_p_b'], 'v_p_scale': out['v_p_scale']}


def _loss(weights, diff, rest, loss_target):
    with _jax.named_scope("forward"):
        args = {**rest, TWIN_DIFF_INPUT: diff, **{k: w.astype(_WEIGHT_DTYPES[k]) for k, w in weights.items()}}
        y = _forward(args)
    with _jax.named_scope("loss_head"):
        err = _jnp.square(y.astype(_jnp.float32) - loss_target)
        return 0.5 * _jnp.sum(_jnp.mean(err, axis=-1)) if err.ndim else 0.5 * err


def _adamw(w, g, m, v):
    m = ADAM_B1 * m + (1.0 - ADAM_B1) * g
    v = ADAM_B2 * v + (1.0 - ADAM_B2) * _jnp.square(g)
    m_hat = m / (1.0 - ADAM_B1 ** ADAM_STEP)
    v_hat = v / (1.0 - ADAM_B2 ** ADAM_STEP)
    delta = -ADAM_LR * (m_hat / (_jnp.sqrt(v_hat) + ADAM_EPS) + ADAM_WD * w)
    return delta, m, v


def reference(x, c, cond_w, cond_b, ada_w, ada_b, norm_g, ffn_w1, ffn_w3, ffn_w2, a_w_in, a_b_in, a_dw, a_dw_b, a_ln_g, a_ln_b, a_w_out, a_b_out, b_w_qkv, b_q_g, b_k_g, b_w_o, p_w, p_b, p_scale, loss_target, m_cond_w, m_cond_b, m_ada_w, m_ada_b, m_norm_g, m_ffn_w1, m_ffn_w3, m_ffn_w2, m_a_w_in, m_a_b_in, m_a_dw, m_a_dw_b, m_a_ln_g, m_a_ln_b, m_a_w_out, m_a_b_out, m_b_w_qkv, m_b_q_g, m_b_k_g, m_b_w_o, m_p_w, m_p_b, m_p_scale, v_cond_w, v_cond_b, v_ada_w, v_ada_b, v_norm_g, v_ffn_w1, v_ffn_w3, v_ffn_w2, v_a_w_in, v_a_b_in, v_a_dw, v_a_dw_b, v_a_ln_g, v_a_ln_b, v_a_w_out, v_a_b_out, v_b_w_qkv, v_b_q_g, v_b_k_g, v_b_w_o, v_p_w, v_p_b, v_p_scale):
    given = dict(x=x, c=c, cond_w=cond_w, cond_b=cond_b, ada_w=ada_w, ada_b=ada_b, norm_g=norm_g, ffn_w1=ffn_w1, ffn_w3=ffn_w3, ffn_w2=ffn_w2, a_w_in=a_w_in, a_b_in=a_b_in, a_dw=a_dw, a_dw_b=a_dw_b, a_ln_g=a_ln_g, a_ln_b=a_ln_b, a_w_out=a_w_out, a_b_out=a_b_out, b_w_qkv=b_w_qkv, b_q_g=b_q_g, b_k_g=b_k_g, b_w_o=b_w_o, p_w=p_w, p_b=p_b, p_scale=p_scale, loss_target=loss_target, m_cond_w=m_cond_w, m_cond_b=m_cond_b, m_ada_w=m_ada_w, m_ada_b=m_ada_b, m_norm_g=m_norm_g, m_ffn_w1=m_ffn_w1, m_ffn_w3=m_ffn_w3, m_ffn_w2=m_ffn_w2, m_a_w_in=m_a_w_in, m_a_b_in=m_a_b_in, m_a_dw=m_a_dw, m_a_dw_b=m_a_dw_b, m_a_ln_g=m_a_ln_g, m_a_ln_b=m_a_ln_b, m_a_w_out=m_a_w_out, m_a_b_out=m_a_b_out, m_b_w_qkv=m_b_w_qkv, m_b_q_g=m_b_q_g, m_b_k_g=m_b_k_g, m_b_w_o=m_b_w_o, m_p_w=m_p_w, m_p_b=m_p_b, m_p_scale=m_p_scale, v_cond_w=v_cond_w, v_cond_b=v_cond_b, v_ada_w=v_ada_w, v_ada_b=v_ada_b, v_norm_g=v_norm_g, v_ffn_w1=v_ffn_w1, v_ffn_w3=v_ffn_w3, v_ffn_w2=v_ffn_w2, v_a_w_in=v_a_w_in, v_a_b_in=v_a_b_in, v_a_dw=v_a_dw, v_a_dw_b=v_a_dw_b, v_a_ln_g=v_a_ln_g, v_a_ln_b=v_a_ln_b, v_a_w_out=v_a_w_out, v_a_b_out=v_a_b_out, v_b_w_qkv=v_b_w_qkv, v_b_q_g=v_b_q_g, v_b_k_g=v_b_k_g, v_b_w_o=v_b_w_o, v_p_w=v_p_w, v_p_b=v_p_b, v_p_scale=v_p_scale)
    weights = {n: given[n] for n in TWIN_WEIGHTS}
    shared = {n: given[n] for n in SHARED_INPUTS}
    per_example = {n: given[n] for n in ['x', 'c']}
    grad_fn = _jax.value_and_grad(_loss, argnums=(0, 1))

    def one_microbatch(ex, loss_target):
        ex = dict(ex)
        diff = ex.pop(TWIN_DIFF_INPUT)
        return grad_fn(weights, diff, {**shared, **ex}, loss_target)

    if N_MICROBATCH == 1:
        loss, (grad_w, grad_x) = one_microbatch(per_example, given["loss_target"])
    else:
        def body(carry, xs):
            loss_sum, grad_sum = carry
            l_k, (gw_k, gx_k) = one_microbatch(xs[0], xs[1])
            with _jax.named_scope("update"):
                return (loss_sum + l_k, _jax.tree.map(_jnp.add, grad_sum, gw_k)), gx_k

        init = (_jnp.zeros((), _jnp.float32), _jax.tree.map(_jnp.zeros_like, weights))
        (loss, grad_w), grad_x = _jax.lax.scan(body, init, (per_example, given["loss_target"]))
    with _jax.named_scope("update"):
        delta_w, new_m, new_v = {}, {}, {}
        for n in TWIN_WEIGHTS:
            delta_w[n], new_m[n], new_v[n] = _adamw(weights[n], grad_w[n], given["m_" + n], given["v_" + n])
    return (loss, grad_x, *[grad_w[n] for n in TWIN_WEIGHTS], *[delta_w[n] for n in TWIN_WEIGHTS],
            *[new_m[n] for n in TWIN_WEIGHTS], *[new_v[n] for n in TWIN_WEIGHTS])
```

```python
import math

import jax
import jax.numpy as jnp
from jax import lax
from jax.experimental import pallas as pl
from jax.experimental.pallas import tpu as pltpu

F32 = jnp.float32
BF = jnp.bfloat16
EPS = 1e-6
N_DEV = 8
N_HEADS = 16
HEAD_DIM = 64
LANES = 128
CONV_W = 31
CONV_PAD = 32
POOL_WINDOWS = (2, 4, 8, 16)
POOL_PAD = 16
VMEM_LIMIT = 56 * 1024 * 1024
ADAM_LR, ADAM_B1, ADAM_B2, ADAM_EPS, ADAM_WD, ADAM_STEP = 0.001, 0.9, 0.999, 1e-08, 0.01, 10
MESH = pl.DeviceIdType.MESH

WEIGHTS = ['cond_w', 'cond_b', 'ada_w', 'ada_b', 'norm_g', 'ffn_w1', 'ffn_w3', 'ffn_w2', 'a_w_in', 'a_b_in',
           'a_dw', 'a_dw_b', 'a_ln_g', 'a_ln_b', 'a_w_out', 'a_b_out', 'b_w_qkv', 'b_q_g', 'b_k_g', 'b_w_o',
           'p_w', 'p_b', 'p_scale']
ARGS = ['x', 'c'] + WEIGHTS + ['loss_target'] + ['m_' + n for n in WEIGHTS] + ['v_' + n for n in WEIGHTS]


def _pcall(body, **kw):
    return pl.pallas_call(body, **kw)


def _cp(*sem):
    return pltpu.CompilerParams(dimension_semantics=sem if sem else None, vmem_limit_bytes=VMEM_LIMIT)


def _nn(a, b):
    return lax.dot_general(a, b, (((1,), (0,)), ((), ())), preferred_element_type=F32)


def _nt(a, b):
    return lax.dot_general(a, b, (((1,), (1,)), ((), ())), preferred_element_type=F32)


def _tn(a, b):
    return lax.dot_general(a, b, (((0,), (0,)), ((), ())), preferred_element_type=F32)


def _sig(z):
    return 1.0 / (1.0 + jnp.exp(-z))


def _tile(n, pref, unit):
    t = (min(n, pref) // unit) * unit
    while t >= unit:
        if n % t == 0:
            return t
        t -= unit
    return n


def _colsum(v):
    return jnp.sum(v, axis=0, keepdims=True)


def _acc(ref, first, val):
    @pl.when(first)
    def _():
        ref[...] = val

    @pl.when(jnp.logical_not(first))
    def _():
        ref[...] += val


def _sds(shape, dt):
    return jax.ShapeDtypeStruct(shape, dt)


def mod_fwd(x, g, scale, shift):
    S, D = x.shape
    tm = _tile(S, 512, 8)

    def body(x_ref, g_ref, sc_ref, sh_ref, h_ref):
        xv = x_ref[...]
        r = lax.rsqrt(jnp.mean(xv * xv, axis=-1, keepdims=True) + EPS)
        h_ref[...] = (((xv * r) * g_ref[...]) * (1.0 + sc_ref[...]) + sh_ref[...]).astype(h_ref.dtype)

    row = pl.BlockSpec((tm, D), lambda i: (i, 0))
    vec = pl.BlockSpec((1, D), lambda i: (0, 0))
    return _pcall(body, name="mod_fwd", grid=(S // tm,), in_specs=[row, vec, vec, vec], out_specs=row,
                  out_shape=_sds((S, D), BF), compiler_params=_cp("parallel"))(x, g, scale, shift)


def resid_fwd(x, y, gate, coef):
    S, D = x.shape
    tm = _tile(S, 512, 8)

    def body(x_ref, y_ref, g_ref, o_ref):
        o_ref[...] = x_ref[...] + (coef * g_ref[...]) * y_ref[...]

    row = pl.BlockSpec((tm, D), lambda i: (i, 0))
    vec = pl.BlockSpec((1, D), lambda i: (0, 0))
    return _pcall(body, name="resid_fwd", grid=(S // tm,), in_specs=[row, row, vec], out_specs=row,
                  out_shape=_sds((S, D), F32), compiler_params=_cp("parallel"))(x, y, gate)


def gate_bwd(dxo, y, gate, coef):
    S, D = dxo.shape
    tm = _tile(S, 512, 8)

    def body(d_ref, y_ref, g_ref, dy_ref, dg_ref, ds_ref):
        first = pl.program_id(0) == 0
        d = d_ref[...]
        dy = (coef * g_ref[...]) * d
        dy_ref[...] = dy.astype(dy_ref.dtype)
        _acc(dg_ref, first, _colsum(coef * d * y_ref[...]))
        _acc(ds_ref, first, _colsum(dy))

    row = pl.BlockSpec((tm, D), lambda i: (i, 0))
    vec = pl.BlockSpec((1, D), lambda i: (0, 0))
    return _pcall(body, name="gate_bwd", grid=(S // tm,), in_specs=[row, row, vec], out_specs=[row, vec, vec],
                  out_shape=[_sds((S, D), BF), _sds((1, D), F32), _sds((1, D), F32)],
                  compiler_params=_cp("arbitrary"))(dxo, y, gate)


def mod_bwd(x, dh, dxo, g, scale):
    S, D = x.shape
    tm = _tile(S, 512, 8)

    def body(x_ref, dh_ref, do_ref, g_ref, sc_ref, dx_ref, dsh_ref, dsc_ref, dg_ref):
        first = pl.program_id(0) == 0
        xv = x_ref[...]
        dh = dh_ref[...].astype(F32)
        gv = g_ref[...]
        r = lax.rsqrt(jnp.mean(xv * xv, axis=-1, keepdims=True) + EPS)
        xh = xv * r
        dn = dh * (1.0 + sc_ref[...])
        u = dn * gv
        dx_ref[...] = do_ref[...] + r * (u - xh * jnp.mean(u * xh, axis=-1, keepdims=True))
        _acc(dsh_ref, first, _colsum(dh))
        _acc(dsc_ref, first, _colsum(dh * (xh * gv)))
        _acc(dg_ref, first, _colsum(dn * xh))

    row = pl.BlockSpec((tm, D), lambda i: (i, 0))
    vec = pl.BlockSpec((1, D), lambda i: (0, 0))
    return _pcall(body, name="mod_bwd", grid=(S // tm,), in_specs=[row, row, row, vec, vec],
                  out_specs=[row, vec, vec, vec],
                  out_shape=[_sds((S, D), F32), _sds((1, D), F32), _sds((1, D), F32), _sds((1, D), F32)],
                  compiler_params=_cp("arbitrary"))(x, dh, dxo, g, scale)


def loss_fwd_bwd(y, target):
    S, D = y.shape
    tm = _tile(S, 512, 8)

    def body(y_ref, t_ref, l_ref, d_ref):
        first = pl.program_id(0) == 0
        e = y_ref[...] - t_ref[...]
        d_ref[...] = e * (1.0 / D)
        part = 0.5 * jnp.sum(jnp.mean(e * e, axis=-1, keepdims=True), axis=0, keepdims=True)
        _acc(l_ref, first, part)

    row = pl.BlockSpec((tm, D), lambda i: (i, 0))
    return _pcall(body, name="loss", grid=(S // tm,), in_specs=[row, row],
                  out_specs=[pl.BlockSpec((1, 1), lambda i: (0, 0)), row],
                  out_shape=[_sds((1, 1), F32), _sds((S, D), F32)], compiler_params=_cp("arbitrary"))(y, target)


def matmul_nt(x, wt, bias=None, out_dtype=F32):
    S, K = x.shape
    N = wt.shape[0]
    tm, tn = _tile(S, 512, 8), _tile(N, 512, LANES)

    def body(*refs):
        x_ref, w_ref = refs[0], refs[1]
        o_ref = refs[-1]
        r = _nt(x_ref[...].astype(BF), w_ref[...].astype(BF))
        if bias is not None:
            r = r + refs[2][...]
        o_ref[...] = r.astype(o_ref.dtype)

    in_specs = [pl.BlockSpec((tm, K), lambda i, j: (i, 0)), pl.BlockSpec((tn, K), lambda i, j: (j, 0))]
    ops = [x, wt]
    if bias is not None:
        in_specs.append(pl.BlockSpec((1, tn), lambda i, j: (0, j)))
        ops.append(bias)
    return _pcall(body, name="matmul_nt", grid=(S // tm, N // tn), in_specs=in_specs,
                  out_specs=pl.BlockSpec((tm, tn), lambda i, j: (i, j)), out_shape=_sds((S, N), out_dtype),
                  compiler_params=_cp("parallel", "parallel"))(*ops)


def matmul_nn(x, w, bias=None, out_dtype=F32):
    S, K = x.shape
    N = w.shape[1]
    tm, tn = _tile(S, 512, 8), _tile(N, 512, LANES)

    def body(*refs):
        x_ref, w_ref = refs[0], refs[1]
        o_ref = refs[-1]
        r = _nn(x_ref[...].astype(BF), w_ref[...].astype(BF))
        if bias is not None:
            r = r + refs[2][...]
        o_ref[...] = r.astype(o_ref.dtype)

    in_specs = [pl.BlockSpec((tm, K), lambda i, j: (i, 0)), pl.BlockSpec((K, tn), lambda i, j: (0, j))]
    ops = [x, w]
    if bias is not None:
        in_specs.append(pl.BlockSpec((1, tn), lambda i, j: (0, j)))
        ops.append(bias)
    return _pcall(body, name="matmul_nn", grid=(S // tm, N // tn), in_specs=in_specs,
                  out_specs=pl.BlockSpec((tm, tn), lambda i, j: (i, j)), out_shape=_sds((S, N), out_dtype),
                  compiler_params=_cp("parallel", "parallel"))(*ops)


def matmul_tn(a, b):
    S, M = a.shape
    N = b.shape[1]
    bm, bn, tk = _tile(M, 1408, LANES), _tile(N, 1024, LANES), _tile(S, 512, 8)

    def body(a_ref, b_ref, o_ref):
        _acc(o_ref, pl.program_id(2) == 0, _tn(a_ref[...].astype(BF), b_ref[...].astype(BF)))

    return _pcall(body, name="matmul_tn", grid=(M // bm, N // bn, S // tk),
                  in_specs=[pl.BlockSpec((tk, bm), lambda i, j, k: (k, i)), pl.BlockSpec((tk, bn), lambda i, j, k: (k, j))],
                  out_specs=pl.BlockSpec((bm, bn), lambda i, j, k: (i, j)), out_shape=_sds((M, N), F32),
                  compiler_params=_cp("parallel", "parallel", "arbitrary"))(a, b)


def outer_rows(at, b):
    M, R = at.shape
    N = b.shape[1]
    tm = _tile(M, 256, 8)

    def body(a_ref, b_ref, o_ref):
        av, bv = a_ref[...], b_ref[...]
        acc = av[:, 0:1] * bv[0:1, :]
        for r in range(1, R):
            acc = acc + av[:, r:r + 1] * bv[r:r + 1, :]
        o_ref[...] = acc

    return _pcall(body, name="outer_rows", grid=(M // tm,),
                  in_specs=[pl.BlockSpec((tm, R), lambda i: (i, 0)), pl.BlockSpec((R, N), lambda i: (0, 0))],
                  out_specs=pl.BlockSpec((tm, N), lambda i: (i, 0)), out_shape=_sds((M, N), F32),
                  compiler_params=_cp("parallel"))(at, b)


def ffn_fwd(h, w1t, w3t, w2):
    S, D = h.shape
    Fd = w2.shape[0]
    tm, tf = _tile(S, 1024, 8), _tile(Fd, 256, LANES)

    def body(h_ref, w1_ref, w3_ref, w2_ref, y_ref):
        hv = h_ref[...]
        g1 = _nt(hv, w1_ref[...])
        g3 = _nt(hv, w3_ref[...])
        a = (g1 * _sig(g1)) * g3
        _acc(y_ref, pl.program_id(1) == 0, _nn(a.astype(BF), w2_ref[...]))

    row = pl.BlockSpec((tm, D), lambda i, f: (i, 0))
    wsp = pl.BlockSpec((tf, D), lambda i, f: (f, 0))
    return _pcall(body, name="ffn_fwd", grid=(S // tm, Fd // tf), in_specs=[row, wsp, wsp, wsp], out_specs=row,
                  out_shape=_sds((S, D), F32), compiler_params=_cp("parallel", "arbitrary"))(h, w1t, w3t, w2)


def ffn_bwd(dy, h, w1t, w3t, w2):
    S, D = h.shape
    Fd = w2.shape[0]
    tm, tf = _tile(S, 512, 8), _tile(Fd, 256, LANES)

    def body(dy_ref, h_ref, w1_ref, w3_ref, w2_ref, dh_ref, a_ref, d1_ref, d3_ref):
        hv = h_ref[...]
        w1, w3 = w1_ref[...], w3_ref[...]
        g1 = _nt(hv, w1)
        g3 = _nt(hv, w3)
        s = _sig(g1)
        si = g1 * s
        a_ref[...] = (si * g3).astype(a_ref.dtype)
        da = _nt(dy_ref[...], w2_ref[...])
        d3 = (da * si).astype(BF)
        d1 = (da * g3 * (s * (1.0 + g1 * (1.0 - s)))).astype(BF)
        d1_ref[...] = d1
        d3_ref[...] = d3
        _acc(dh_ref, pl.program_id(1) == 0, _nn(d1, w1) + _nn(d3, w3))

    row = pl.BlockSpec((tm, D), lambda i, f: (i, 0))
    wsp = pl.BlockSpec((tf, D), lambda i, f: (f, 0))
    act = pl.BlockSpec((tm, tf), lambda i, f: (i, f))
    return _pcall(body, name="ffn_bwd", grid=(S // tm, Fd // tf), in_specs=[row, row, wsp, wsp, wsp],
                  out_specs=[row, act, act, act],
                  out_shape=[_sds((S, D), F32), _sds((S, Fd), BF), _sds((S, Fd), BF), _sds((S, Fd), BF)],
                  compiler_params=_cp("parallel", "arbitrary"))(dy, h, w1t, w3t, w2)


def _layer_norm_parts(v):
    mu = jnp.mean(v, axis=-1, keepdims=True)
    vc = v - mu
    rs = lax.rsqrt(jnp.mean(vc * vc, axis=-1, keepdims=True) + EPS)
    return vc * rs, rs


def conv_mid_fwd(pre, dw, dw_b, ln_g, ln_b):
    S, D2 = pre.shape
    D = D2 // 2
    tm = _tile(S, 256, CONV_PAD)

    def body(pre_ref, dw_ref, dwb_ref, g_ref, b_ref, v_ref, sw_ref, ubuf):
        @pl.when(pl.program_id(0) == 0)
        def _():
            ubuf[pl.ds(0, CONV_PAD), :] = jnp.zeros((CONV_PAD, D), F32)

        @pl.when(pl.program_id(0) > 0)
        def _():
            ubuf[pl.ds(0, CONV_PAD), :] = ubuf[pl.ds(tm, CONV_PAD), :]

        ubuf[pl.ds(CONV_PAD, tm), :] = pre_ref[:, pl.ds(0, D)] * _sig(pre_ref[:, pl.ds(D, D)])
        acc = jnp.zeros((tm, D), F32) + dwb_ref[...]
        for k in range(CONV_W):
            acc = acc + dw_ref[pl.ds(k, 1), :] * ubuf[pl.ds(CONV_PAD - (CONV_W - 1) + k, tm), :]
        v_ref[...] = acc
        vh, _ = _layer_norm_parts(acc)
        ln = vh * g_ref[...] + b_ref[...]
        sw_ref[...] = (ln * _sig(ln)).astype(sw_ref.dtype)

    row = pl.BlockSpec((tm, D), lambda i: (i, 0))
    vec = pl.BlockSpec((1, D), lambda i: (0, 0))
    return _pcall(body, name="conv_mid_fwd", grid=(S // tm,),
                  in_specs=[pl.BlockSpec((tm, D2), lambda i: (i, 0)), pl.BlockSpec((CONV_PAD, D), lambda i: (0, 0)), vec, vec, vec],
                  out_specs=[row, row], out_shape=[_sds((S, D), F32), _sds((S, D), BF)],
                  scratch_shapes=[pltpu.VMEM((tm + CONV_PAD, D), F32)], compiler_params=_cp("arbitrary"))(pre, dw, dw_b, ln_g, ln_b)


def conv_mid_bwd(pre, v, dsw, dw, ln_g, ln_b):
    S, D2 = pre.shape
    D = D2 // 2
    tm = _tile(S, 256, CONV_PAD)
    nt = S // tm

    def body(pre_ref, v_ref, dsw_ref, dw_ref, g_ref, b_ref, dpre_ref, dbin_ref, ddw_ref, ddwb_ref, dg_ref, db_ref, dvbuf):
        first = pl.program_id(0) == 0

        @pl.when(first)
        def _():
            dvbuf[pl.ds(tm, CONV_PAD), :] = jnp.zeros((CONV_PAD, D), F32)
            ddw_ref[...] = jnp.zeros((CONV_PAD, D), F32)

        @pl.when(jnp.logical_not(first))
        def _():
            dvbuf[pl.ds(tm, CONV_PAD), :] = dvbuf[pl.ds(0, CONV_PAD), :]

        a = pre_ref[:, pl.ds(0, D)]
        sb = _sig(pre_ref[:, pl.ds(D, D)])
        u = a * sb
        gv = g_ref[...]
        vh, rs = _layer_norm_parts(v_ref[...])
        ln = vh * gv + b_ref[...]
        sg = _sig(ln)
        dln = dsw_ref[...] * (sg * (1.0 + ln * (1.0 - sg)))
        _acc(dg_ref, first, _colsum(dln * vh))
        _acc(db_ref, first, _colsum(dln))
        dvh = dln * gv
        dv = rs * (dvh - jnp.mean(dvh, axis=-1, keepdims=True) - vh * jnp.mean(dvh * vh, axis=-1, keepdims=True))
        _acc(ddwb_ref, first, _colsum(dv))
        dvbuf[pl.ds(0, tm), :] = dv
        du = jnp.zeros((tm, D), F32)
        for k in range(CONV_W):
            sh = dvbuf[pl.ds(CONV_W - 1 - k, tm), :]
            du = du + dw_ref[pl.ds(k, 1), :] * sh
            ddw_ref[pl.ds(k, 1), :] += _colsum(u * sh)
        da = du * sb
        db = du * a * sb * (1.0 - sb)
        dpre_ref[:, pl.ds(0, D)] = da.astype(dpre_ref.dtype)
        dpre_ref[:, pl.ds(D, D)] = db.astype(dpre_ref.dtype)
        _acc(dbin_ref, first, jnp.concatenate([_colsum(da), _colsum(db)], axis=1))

    rev = lambda i: (nt - 1 - i, 0)
    row = pl.BlockSpec((tm, D), rev)
    row2 = pl.BlockSpec((tm, D2), rev)
    vec = pl.BlockSpec((1, D), lambda i: (0, 0))
    pad = pl.BlockSpec((CONV_PAD, D), lambda i: (0, 0))
    return _pcall(body, name="conv_mid_bwd", grid=(nt,), in_specs=[row2, row, row, pad, vec, vec],
                  out_specs=[row2, pl.BlockSpec((1, D2), lambda i: (0, 0)), pad, vec, vec, vec],
                  out_shape=[_sds((S, D2), BF), _sds((1, D2), F32), _sds((CONV_PAD, D), F32), _sds((1, D), F32),
                             _sds((1, D), F32), _sds((1, D), F32)],
                  scratch_shapes=[pltpu.VMEM((tm + CONV_PAD, D), F32)], compiler_params=_cp("arbitrary"))(pre, v, dsw, dw, ln_g, ln_b)


def _pool_count(i, tm, w):
    t = i * tm + lax.broadcasted_iota(jnp.int32, (tm, 1), 0)
    return jnp.minimum(t + 1, w).astype(F32)


def pool_fwd(h, p_w, p_b, p_scale):
    S, D = h.shape
    G, Dg = p_w.shape[0], p_w.shape[1]
    tm = _tile(S, 256, POOL_PAD)

    def body(h_ref, w_ref, b_ref, s_ref, y_ref, d_ref, hbuf):
        i = pl.program_id(0)

        @pl.when(i == 0)
        def _():
            hbuf[pl.ds(0, POOL_PAD), :] = jnp.zeros((POOL_PAD, D), F32)

        @pl.when(i > 0)
        def _():
            hbuf[pl.ds(0, POOL_PAD), :] = hbuf[pl.ds(tm, POOL_PAD), :]

        hbuf[pl.ds(POOL_PAD, tm), :] = h_ref[...].astype(F32)
        for g, w in enumerate(POOL_WINDOWS):
            cols = pl.ds(g * Dg, Dg)
            hg = hbuf[pl.ds(POOL_PAD, tm), cols]
            win = hg
            for j in range(1, w):
                win = win + hbuf[pl.ds(POOL_PAD - j, tm), cols]
            d = (win / _pool_count(i, tm, w) - hg).astype(BF)
            d_ref[:, cols] = d
            y_ref[:, cols] = (_nn(d, w_ref[g].astype(BF)) + b_ref[:, cols]) * s_ref[:, cols]

    row = pl.BlockSpec((tm, D), lambda i: (i, 0))
    vec = pl.BlockSpec((1, D), lambda i: (0, 0))
    return _pcall(body, name="pool_fwd", grid=(S // tm,),
                  in_specs=[row, pl.BlockSpec((G, Dg, Dg), lambda i: (0, 0, 0)), vec, vec], out_specs=[row, row],
                  out_shape=[_sds((S, D), F32), _sds((S, D), BF)],
                  scratch_shapes=[pltpu.VMEM((tm + POOL_PAD, D), F32)], compiler_params=_cp("arbitrary"))(h, p_w, p_b, p_scale)


def pool_bwd(dy, d, p_w, p_b, p_scale):
    S, D = dy.shape
    G, Dg = p_w.shape[0], p_w.shape[1]
    tm = _tile(S, 256, POOL_PAD)
    nt = S // tm

    def body(dy_ref, d_ref, w_ref, b_ref, s_ref, dh_ref, dyy_ref, dpb_ref, dps_ref, ebuf):
        j = pl.program_id(0)
        i = nt - 1 - j
        first = j == 0

        @pl.when(first)
        def _():
            ebuf[pl.ds(tm, POOL_PAD), :] = jnp.zeros((POOL_PAD, D), F32)

        @pl.when(jnp.logical_not(first))
        def _():
            ebuf[pl.ds(tm, POOL_PAD), :] = ebuf[pl.ds(0, POOL_PAD), :]

        dpb, dps = [], []
        for g, w in enumerate(POOL_WINDOWS):
            cols = pl.ds(g * Dg, Dg)
            wg = w_ref[g].astype(BF)
            dyg = dy_ref[:, cols].astype(F32)
            ypre = _nn(d_ref[:, cols], wg) + b_ref[:, cols]
            dps.append(_colsum(dyg * ypre))
            dyy = dyg * s_ref[:, cols]
            dpb.append(_colsum(dyy))
            dyy = dyy.astype(BF)
            dyy_ref[:, cols] = dyy
            dd = _nt(dyy, wg)
            ebuf[pl.ds(0, tm), cols] = dd / _pool_count(i, tm, w)
            acc = -dd
            for k in range(w):
                acc = acc + ebuf[pl.ds(k, tm), cols]
            dh_ref[:, cols] = acc
        _acc(dpb_ref, first, jnp.concatenate(dpb, axis=1))
        _acc(dps_ref, first, jnp.concatenate(dps, axis=1))

    row = pl.BlockSpec((tm, D), lambda i: (nt - 1 - i, 0))
    vec = pl.BlockSpec((1, D), lambda i: (0, 0))
    return _pcall(body, name="pool_bwd", grid=(nt,),
                  in_specs=[row, row, pl.BlockSpec((G, Dg, Dg), lambda i: (0, 0, 0)), vec, vec],
                  out_specs=[row, row, vec, vec],
                  out_shape=[_sds((S, D), F32), _sds((S, D), BF), _sds((1, D), F32), _sds((1, D), F32)],
                  scratch_shapes=[pltpu.VMEM((tm + POOL_PAD, D), F32)], compiler_params=_cp("arbitrary"))(dy, d, p_w, p_b, p_scale)


def _group_sum(v, low):
    s_lo = jnp.sum(jnp.where(low, v, 0.0), axis=-1, keepdims=True)
    s_hi = jnp.sum(jnp.where(low, 0.0, v), axis=-1, keepdims=True)
    return jnp.where(low, s_lo, s_hi)


def qknorm_fwd(qkv, q_g2, k_g2):
    S, D3 = qkv.shape
    nb = D3 // LANES // 3
    tm = _tile(S, 512, 8)

    def body(x_ref, qg_ref, kg_ref, o_ref):
        j = pl.program_id(1)
        xv = x_ref[...]
        low = lax.broadcasted_iota(jnp.int32, (1, LANES), 1) < HEAD_DIM
        r = lax.rsqrt(_group_sum(xv * xv, low) * (1.0 / HEAD_DIM) + EPS)
        xn = xv * r
        qn = (xn * qg_ref[...]) * (HEAD_DIM ** -0.5)
        kn = xn * kg_ref[...]
        o_ref[...] = jnp.where(j < nb, qn, jnp.where(j < 2 * nb, kn, xv)).astype(o_ref.dtype)

    blk = pl.BlockSpec((tm, LANES), lambda i, j: (i, j))
    vec = pl.BlockSpec((1, LANES), lambda i, j: (0, 0))
    return _pcall(body, name="qknorm_fwd", grid=(S // tm, D3 // LANES), in_specs=[blk, vec, vec], out_specs=blk,
                  out_shape=_sds((S, D3), BF), compiler_params=_cp("parallel", "parallel"))(qkv, q_g2, k_g2)


def qknorm_bwd(qkv, dq, dk, dv, q_g2, k_g2):
    S, D3 = qkv.shape
    nb = D3 // LANES // 3
    tm = _tile(S, 512, 8)

    def body(x_ref, dq_ref, dk_ref, dv_ref, qg_ref, kg_ref, o_ref, dg_ref):
        j = pl.program_id(0)
        is_q, is_k = j < nb, jnp.logical_and(j >= nb, j < 2 * nb)
        xv = x_ref[...]
        low = lax.broadcasted_iota(jnp.int32, (1, LANES), 1) < HEAD_DIM
        r = lax.rsqrt(_group_sum(xv * xv, low) * (1.0 / HEAD_DIM) + EPS)
        xh = xv * r
        dn = jnp.where(is_q, dq_ref[...] * (HEAD_DIM ** -0.5), dk_ref[...])
        gv = jnp.where(is_q, qg_ref[...], kg_ref[...])
        u = dn * gv
        dx = r * (u - xh * (_group_sum(u * xh, low) * (1.0 / HEAD_DIM)))
        o_ref[...] = jnp.where(jnp.logical_or(is_q, is_k), dx, dv_ref[...]).astype(o_ref.dtype)
        _acc(dg_ref, pl.program_id(1) == 0, _colsum(dn * xh))

    blk = pl.BlockSpec((tm, LANES), lambda j, i: (i, j))
    part = lambda lo: pl.BlockSpec((tm, LANES), lambda j, i: (i, jnp.clip(j - lo, 0, nb - 1)))
    vec = pl.BlockSpec((1, LANES), lambda j, i: (0, 0))
    return _pcall(body, name="qknorm_bwd", grid=(D3 // LANES, S // tm),
                  in_specs=[blk, part(0), part(nb), part(2 * nb), vec, vec],
                  out_specs=[blk, pl.BlockSpec((1, LANES), lambda j, i: (0, j))],
                  out_shape=[_sds((S, D3), BF), _sds((1, D3), F32)],
                  compiler_params=_cp("parallel", "arbitrary"))(qkv, dq, dk, dv, q_g2, k_g2)


def _softplus_parts(z):
    e = jnp.exp(-jnp.abs(z))
    return jnp.maximum(z, 0.0) + jnp.log(1.0 + e), e


def _split_bf(v):
    hi = v.astype(BF)
    return hi, (v - hi.astype(F32)).astype(BF)


def _attn_tile(S):
    return _tile(S, 256, LANES) if S >= 1024 else LANES


def attn_fwd(qkvn):
    S, D3 = qkvn.shape
    D = D3 // 3
    nb = D // LANES
    t = _attn_tile(S)
    nq = S // t
    assert nq <= LANES

    def body(q_ref, k_ref, v_ref, o_ref, r_ref, acc_ref, rall_ref, behind_ref):
        qi = pl.program_id(1)
        lane = lax.broadcasted_iota(jnp.int32, (1, LANES), 1)
        rows = lax.broadcasted_iota(jnp.int32, (t, t), 0)
        cols = lax.broadcasted_iota(jnp.int32, (t, t), 1)
        later = (rows > cols).astype(BF)
        q = q_ref[...]
        outs = []
        for hh in range(2):
            in_head = (lane < HEAD_DIM) if hh == 0 else (lane >= HEAD_DIM)
            qh = jnp.where(in_head, q, jnp.zeros_like(q))
            acc_ref[...] = jnp.zeros((t, LANES), F32)
            rall_ref[...] = jnp.zeros((t, LANES), F32)
            behind_ref[...] = jnp.zeros((t, LANES), F32)

            def step(n, carry):
                b = qi - n
                kb = k_ref[pl.ds(pl.multiple_of(b * t, t), t), :]
                vb = v_ref[pl.ds(pl.multiple_of(b * t, t), t), :]
                z = _nt(qh, kb)
                mask = (b * t + cols) < (qi * t + rows)
                sp, _ = _softplus_parts(z)
                sp = jnp.where(mask, sp, 0.0)
                hi, lo = _split_bf(sp)
                behind = behind_ref[...]
                suffix = _nn(hi, later) + _nn(lo, later) + jnp.tile(behind, (1, t // LANES))
                a = jnp.where(mask, jnp.exp(z - sp - suffix), 0.0)
                acc_ref[...] += _nn(a.astype(BF), vb)
                rall_ref[...] = jnp.where(lane == b, behind, rall_ref[...])
                behind_ref[...] = behind + jnp.sum(sp, axis=-1, keepdims=True)
                return carry

            lax.fori_loop(0, qi + 1, step, 0)
            r_ref[hh] = rall_ref[...]
            outs.append(acc_ref[...])
        o_ref[...] = jnp.where(lane < HEAD_DIM, outs[0], outs[1]).astype(o_ref.dtype)

    return _pcall(body, name="attn_fwd", grid=(nb, nq),
                  in_specs=[pl.BlockSpec((t, LANES), lambda h, i: (i, h)),
                            pl.BlockSpec((S, LANES), lambda h, i: (0, nb + h)),
                            pl.BlockSpec((S, LANES), lambda h, i: (0, 2 * nb + h))],
                  out_specs=[pl.BlockSpec((t, LANES), lambda h, i: (i, h)), pl.BlockSpec((2, t, LANES), lambda h, i: (h, i, 0))],
                  out_shape=[_sds((S, D), BF), _sds((2 * nb, S, LANES), F32)],
                  scratch_shapes=[pltpu.VMEM((t, LANES), F32)] * 3,
                  compiler_params=_cp("parallel", "arbitrary"))(qkvn, qkvn, qkvn)


def attn_bwd(qkvn, do, r_all):
    S, D3 = qkvn.shape
    D = D3 // 3
    nb = D // LANES
    t = _attn_tile(S)
    nq = S // t

    def body(q_ref, k_ref, v_ref, do_ref, r_ref, dq_ref, dk_hbm, dv_hbm, dk_acc, dv_acc, dq_acc, before_ref, sem):
        hp = pl.program_id(0)
        qi = pl.program_id(1)

        @pl.when(qi == 0)
        def _():
            dk_acc[...] = jnp.zeros((S, LANES), F32)
            dv_acc[...] = jnp.zeros((S, LANES), F32)

        lane = lax.broadcasted_iota(jnp.int32, (1, LANES), 1)
        rows = lax.broadcasted_iota(jnp.int32, (t, t), 0)
        cols = lax.broadcasted_iota(jnp.int32, (t, t), 1)
        later = (rows > cols).astype(BF)
        earlier = (rows < cols).astype(BF)
        q = q_ref[...]
        dov = do_ref[...]
        dqs = []
        for hh in range(2):
            in_head = (lane < HEAD_DIM) if hh == 0 else (lane >= HEAD_DIM)
            qh = jnp.where(in_head, q, jnp.zeros_like(q))
            doh = jnp.where(in_head, dov, jnp.zeros_like(dov))
            r_head = r_ref[hh]
            dq_acc[...] = jnp.zeros((t, LANES), F32)
            before_ref[...] = jnp.zeros((t, LANES), F32)

            def step(b, carry):
                off = pl.multiple_of(b * t, t)
                kb = k_ref[pl.ds(off, t), :]
                vb = v_ref[pl.ds(off, t), :]
                z = _nt(qh, kb)
                mask = (b * t + cols) < (qi * t + rows)
                sp, e = _softplus_parts(z)
                sigma = jnp.where(z >= 0.0, 1.0, e) / (1.0 + e)
                spm = jnp.where(mask, sp, 0.0)
                hi, lo = _split_bf(spm)
                behind = jnp.sum(jnp.where(lane == b, r_head, 0.0), axis=-1, keepdims=True)
                suffix = _nn(hi, later) + _nn(lo, later) + behind
                a = jnp.where(mask, jnp.exp(z - sp - suffix), 0.0)
                g = a * _nt(doh, vb)
                ghi, glo = _split_bf(g)
                before = before_ref[...]
                prefix = _nn(ghi, earlier) + _nn(glo, earlier) + jnp.tile(before, (1, t // LANES))
                dz = jnp.where(mask, g * (1.0 - sigma) - sigma * prefix, 0.0).astype(BF)
                dk_acc[pl.ds(off, t), :] += _tn(dz, qh)
                dv_acc[pl.ds(off, t), :] += _tn(a.astype(BF), doh)
                dq_acc[...] += _nn(dz, kb)
                before_ref[...] = before + jnp.sum(g, axis=-1, keepdims=True)
                return carry

            lax.fori_loop(0, qi + 1, step, 0)
            dqs.append(dq_acc[...])
        dq_ref[...] = jnp.where(lane < HEAD_DIM, dqs[0], dqs[1])

        @pl.when(qi == nq - 1)
        def _():
            col = pl.multiple_of(hp * LANES, LANES)
            c1 = pltpu.make_async_copy(dk_acc, dk_hbm.at[:, pl.ds(col, LANES)], sem.at[0])
            c2 = pltpu.make_async_copy(dv_acc, dv_hbm.at[:, pl.ds(col, LANES)], sem.at[1])
            c1.start()
            c2.start()
            c1.wait()
            c2.wait()

    blk = pl.BlockSpec((t, LANES), lambda h, i: (i, h))
    hbm = pl.BlockSpec(memory_space=pl.ANY)
    return _pcall(body, name="attn_bwd", grid=(nb, nq),
                  in_specs=[blk, pl.BlockSpec((S, LANES), lambda h, i: (0, nb + h)),
                            pl.BlockSpec((S, LANES), lambda h, i: (0, 2 * nb + h)), blk,
                            pl.BlockSpec((2, t, LANES), lambda h, i: (h, i, 0))],
                  out_specs=[blk, hbm, hbm],
                  out_shape=[_sds((S, D), F32), _sds((S, D), F32), _sds((S, D), F32)],
                  scratch_shapes=[pltpu.VMEM((S, LANES), F32), pltpu.VMEM((S, LANES), F32), pltpu.VMEM((t, LANES), F32),
                                  pltpu.VMEM((t, LANES), F32), pltpu.SemaphoreType.DMA((2,))],
                  compiler_params=_cp("arbitrary", "arbitrary"))(qkvn, qkvn, qkvn, do, r_all)


def cond_embed(parts, cond_b):
    P, B, D = parts.shape

    def body(p_ref, b_ref, pre_ref, e_ref):
        pre = p_ref[0]
        for s in range(1, P):
            pre = pre + p_ref[s]
        pre = pre + b_ref[...]
        pre_ref[...] = pre
        e_ref[...] = pre * _sig(pre)

    return _pcall(body, name="cond_embed", out_shape=[_sds((B, D), F32), _sds((B, D), F32)],
                  compiler_params=_cp())(parts, cond_b)


def cond_embed_bwd(parts, pre):
    P, B, D = parts.shape

    def body(p_ref, pre_ref, dpre_ref, db_ref):
        de = p_ref[0]
        for s in range(1, P):
            de = de + p_ref[s]
        pre = pre_ref[...]
        s = _sig(pre)
        dpre = de * (s * (1.0 + pre * (1.0 - s)))
        dpre_ref[...] = dpre
        db_ref[...] = _colsum(dpre)

    return _pcall(body, name="cond_embed_bwd", out_shape=[_sds((B, D), F32), _sds((1, D), F32)],
                  compiler_params=_cp())(parts, pre)


def adamw(w, parts, m, v):
    R, C = w.shape
    P = parts.shape[0]
    tr = _tile(R, max(8, (1 << 17) // C // 8 * 8), 8)
    c1 = 1.0 - ADAM_B1 ** ADAM_STEP
    c2 = 1.0 - ADAM_B2 ** ADAM_STEP

    def body(w_ref, p_ref, m_ref, v_ref, g_ref, d_ref, nm_ref, nv_ref):
        g = p_ref[0].astype(F32)
        for s in range(1, P):
            g = g + p_ref[s].astype(F32)
        nm = ADAM_B1 * m_ref[...] + (1.0 - ADAM_B1) * g
        nv = ADAM_B2 * v_ref[...] + (1.0 - ADAM_B2) * (g * g)
        g_ref[...] = g
        nm_ref[...] = nm
        nv_ref[...] = nv
        d_ref[...] = -ADAM_LR * ((nm / c1) / (jnp.sqrt(nv / c2) + ADAM_EPS) + ADAM_WD * w_ref[...])

    blk = pl.BlockSpec((tr, C), lambda i: (i, 0))
    return _pcall(body, name="adamw", grid=(R // tr,), in_specs=[blk, pl.BlockSpec((P, tr, C), lambda i: (0, i, 0)), blk, blk],
                  out_specs=[blk] * 4, out_shape=[_sds((R, C), F32)] * 4, compiler_params=_cp("parallel"))(w, parts, m, v)


def exchange(arrs, peer_axes):
    n = len(arrs)
    shapes = [a.shape if ax is None else a.shape[:ax] + a.shape[ax + 1:] for a, ax in zip(arrs, peer_axes)]

    def body(*refs):
        ins, outs = refs[:n], refs[n:2 * n]
        send, recv, local = refs[2 * n:]
        x, y, c = lax.axis_index("x"), lax.axis_index("y"), lax.axis_index("c")
        me = 4 * x + 2 * y + c

        def piece(i, d):
            if peer_axes[i] is None:
                return ins[i]
            return ins[i].at[(slice(None),) * peer_axes[i] + (d,)]

        mine = [pltpu.make_async_copy(piece(i, me), outs[i].at[me], local.at[i]) for i in range(n)]
        for cp in mine:
            cp.start()
        sends, recvs = [], []
        for p in range(1, N_DEV):
            peer = (1 - x if p & 4 else x, 1 - y if p & 2 else y, 1 - c if p & 1 else c)
            them = 4 * peer[0] + 2 * peer[1] + peer[2]
            for i in range(n):
                sends.append(pltpu.make_async_remote_copy(piece(i, them), outs[i].at[me], send.at[i, p - 1], recv.at[i, p - 1],
                                                          device_id=peer, device_id_type=MESH))
                recvs.append(pltpu.make_async_remote_copy(piece(i, me), outs[i].at[them], send.at[i, p - 1], recv.at[i, p - 1],
                                                          device_id=peer, device_id_type=MESH))
        for cp in sends:
            cp.start()
        for cp in recvs:
            cp.wait_recv()
        for cp in sends:
            cp.wait_send()
        for cp in mine:
            cp.wait()

    hbm = pl.BlockSpec(memory_space=pl.ANY)
    return _pcall(body, name="exchange", in_specs=[hbm] * n, out_specs=[hbm] * n,
                  out_shape=[_sds((N_DEV,) + s, a.dtype) for s, a in zip(shapes, arrs)],
                  scratch_shapes=[pltpu.SemaphoreType.DMA((n, N_DEV - 1)), pltpu.SemaphoreType.DMA((n, N_DEV - 1)),
                                  pltpu.SemaphoreType.DMA((n,))])(*arrs)


def _pack(vs):
    flat = jnp.concatenate([v.reshape(-1).astype(F32) for v in vs])
    pad = (-flat.shape[0]) % 1024
    return jnp.pad(flat, (0, pad)).reshape(-1, 1024)


def _unpack(packed, shapes):
    flat = packed.reshape(packed.shape[0], -1)
    out, o = [], 0
    for s in shapes:
        n = math.prod(s)
        out.append(flat[:, o:o + n].reshape((packed.shape[0],) + tuple(s)))
        o += n
    return out


def _cat_dev(g, axis):
    g = jnp.moveaxis(g, 0, axis)
    return g.reshape(g.shape[:axis] + (g.shape[axis] * g.shape[axis + 1],) + g.shape[axis + 2:])


def kernel(x, c, cond_w, cond_b, ada_w, ada_b, norm_g, ffn_w1, ffn_w3, ffn_w2, a_w_in, a_b_in, a_dw, a_dw_b, a_ln_g, a_ln_b, a_w_out, a_b_out, b_w_qkv, b_q_g, b_k_g, b_w_o, p_w, p_b, p_scale, loss_target, m_cond_w, m_cond_b, m_ada_w, m_ada_b, m_norm_g, m_ffn_w1, m_ffn_w3, m_ffn_w2, m_a_w_in, m_a_b_in, m_a_dw, m_a_dw_b, m_a_ln_g, m_a_ln_b, m_a_w_out, m_a_b_out, m_b_w_qkv, m_b_q_g, m_b_k_g, m_b_w_o, m_p_w, m_p_b, m_p_scale, v_cond_w, v_cond_b, v_ada_w, v_ada_b, v_norm_g, v_ffn_w1, v_ffn_w3, v_ffn_w2, v_a_w_in, v_a_b_in, v_a_dw, v_a_dw_b, v_a_ln_g, v_a_ln_b, v_a_w_out, v_a_b_out, v_b_w_qkv, v_b_q_g, v_b_k_g, v_b_w_o, v_p_w, v_p_b, v_p_scale):
    A = dict(zip(ARGS, (x, c, cond_w, cond_b, ada_w, ada_b, norm_g, ffn_w1, ffn_w3, ffn_w2, a_w_in, a_b_in, a_dw, a_dw_b, a_ln_g, a_ln_b, a_w_out, a_b_out, b_w_qkv, b_q_g, b_k_g, b_w_o, p_w, p_b, p_scale, loss_target, m_cond_w, m_cond_b, m_ada_w, m_ada_b, m_norm_g, m_ffn_w1, m_ffn_w3, m_ffn_w2, m_a_w_in, m_a_b_in, m_a_dw, m_a_dw_b, m_a_ln_g, m_a_ln_b, m_a_w_out, m_a_b_out, m_b_w_qkv, m_b_q_g, m_b_k_g, m_b_w_o, m_p_w, m_p_b, m_p_scale, v_cond_w, v_cond_b, v_ada_w, v_ada_b, v_norm_g, v_ffn_w1, v_ffn_w3, v_ffn_w2, v_a_w_in, v_a_b_in, v_a_dw, v_a_dw_b, v_a_ln_g, v_a_ln_b, v_a_w_out, v_a_b_out, v_b_w_qkv, v_b_q_g, v_b_k_g, v_b_w_o, v_p_w, v_p_b, v_p_scale)))
    S, D = x.shape[1], x.shape[2]
    depth = ada_w.shape[0]
    n_a, n_b, n_c = a_w_in.shape[0], b_w_qkv.shape[0], p_w.shape[0]
    me = 4 * lax.axis_index("x") + 2 * lax.axis_index("y") + lax.axis_index("c")
    swap = lambda w: jnp.swapaxes(w, -1, -2)

    small_names = ['norm_g', 'a_b_in', 'a_dw', 'a_dw_b', 'a_ln_g', 'a_ln_b', 'a_b_out', 'p_w', 'p_b', 'p_scale']
    small_in = [c] + [A[n] for n in small_names]
    big_in = [swap(ffn_w1).astype(BF), swap(ffn_w3).astype(BF), ffn_w2.astype(BF), swap(a_w_in).astype(BF),
              a_w_out.astype(BF), swap(b_w_qkv).astype(BF), b_w_o.astype(BF)]
    got = exchange(big_in + [_pack(small_in)], [None] * (len(big_in) + 1))
    w1t, w3t, w2 = _cat_dev(got[0], 2), _cat_dev(got[1], 2), _cat_dev(got[2], 2)
    w_in_t, w_out = _cat_dev(got[3], 1), _cat_dev(got[4], 1)
    w_qkv_t, w_o = _cat_dev(got[5], 1), _cat_dev(got[6], 1)
    sm = dict(zip(['c'] + small_names, _unpack(got[7], [v.shape for v in small_in])))
    c_all = sm['c'][:, 0]
    norm_g_f = _cat_dev(sm['norm_g'], 2)
    a_b_in_f, a_dw_f, a_dw_b_f = _cat_dev(sm['a_b_in'], 1), _cat_dev(sm['a_dw'], 2), _cat_dev(sm['a_dw_b'], 1)
    a_ln_g_f, a_ln_b_f, a_b_out_f = _cat_dev(sm['a_ln_g'], 1), _cat_dev(sm['a_ln_b'], 1), _cat_dev(sm['a_b_out'], 1)
    p_w_f, p_b_f, p_scale_f = _cat_dev(sm['p_w'], 2), _cat_dev(sm['p_b'], 2), _cat_dev(sm['p_scale'], 1)
    a_dw_f = jnp.pad(a_dw_f, ((0, 0), (0, CONV_PAD - CONV_W), (0, 0)))

    rows = D // N_DEV
    c_mine = lax.dynamic_slice_in_dim(c_all, me * rows, rows, axis=1)
    pre_parts = exchange([matmul_nn(c_mine, cond_w)], [None])[0]
    pre_all, e_all = cond_embed(pre_parts, cond_b.reshape(1, D))
    n_mod = ada_w.shape[2]
    ada_b_mine = lax.dynamic_slice_in_dim(ada_b, me * n_mod, n_mod, axis=1)
    mod_part = jnp.stack([matmul_nn(e_all, ada_w[i], ada_b_mine[i:i + 1]) for i in range(depth)], axis=1)
    mod_all = exchange([mod_part], [None])[0]
    mod = lax.dynamic_index_in_dim(mod_all, me, axis=1, keepdims=False)
    mod = jnp.moveaxis(mod, 0, 1).reshape(depth, 3, 3, 1, D)

    xs = x[0]
    saved = []
    ia = ib = ic = 0
    for i in range(depth):
        lay = {}
        for sub in range(3):
            shift, scale, gate = mod[i, sub, 0], mod[i, sub, 1], 1.0 + mod[i, sub, 2]
            g = norm_g_f[i, sub].reshape(1, D)
            h = mod_fwd(xs, g, scale, shift)
            rec = dict(x=xs, h=h, g=g, scale=scale, gate=gate)
            if sub != 1:
                j = 0 if sub == 0 else 1
                y = ffn_fwd(h, w1t[i, j], w3t[i, j], w2[i, j])
                rec.update(kind='ffn', j=j, coef=0.5)
            elif i % 3 == 0:
                pre = matmul_nt(h, w_in_t[ia], a_b_in_f[ia].reshape(1, 2 * D))
                v, sw = conv_mid_fwd(pre, a_dw_f[ia], a_dw_b_f[ia].reshape(1, D), a_ln_g_f[ia].reshape(1, D), a_ln_b_f[ia].reshape(1, D))
                y = matmul_nn(sw, w_out[ia], a_b_out_f[ia].reshape(1, D))
                rec.update(kind='conv', idx=ia, coef=1.0, pre=pre, v=v, sw=sw)
                ia += 1
            elif i % 3 == 1:
                qkv = matmul_nt(h, w_qkv_t[ib])
                qg2 = jnp.tile(b_q_g[ib].reshape(1, HEAD_DIM), (1, 2))
                kg2 = jnp.tile(b_k_g[ib].reshape(1, HEAD_DIM), (1, 2))
                qkvn = qknorm_fwd(qkv, qg2, kg2)
                o, r_all = attn_fwd(qkvn)
                y = matmul_nn(o, w_o[ib])
                rec.update(kind='attn', idx=ib, coef=1.0, qkv=qkv, qkvn=qkvn, o=o, r_all=r_all, qg2=qg2, kg2=kg2)
                ib += 1
            else:
                pb, ps = p_b_f[ic].reshape(1, D), p_scale_f[ic].reshape(1, D)
                y, dpool = pool_fwd(h, p_w_f[ic], pb, ps)
                rec.update(kind='pool', idx=ic, coef=1.0, d=dpool, pb=pb, ps=ps)
                ic += 1
            rec['y'] = y
            xs = resid_fwd(xs, y, gate, rec['coef'])
            lay[sub] = rec
        saved.append(lay)

    loss_part, dx = loss_fwd_bwd(xs, loss_target[0])
    loss = lax.psum(loss_part[0, 0], ("x", "y", "c"))

    Fd = w2.shape[2]
    zeros = lambda *s: jnp.zeros(s, F32)
    g_w1t, g_w3t, g_w2 = [[None, None] for _ in range(depth)], [[None, None] for _ in range(depth)], [[None, None] for _ in range(depth)]
    g_w_in_t, g_w_out, g_w_qkv_t, g_w_o, g_p_w = [None] * n_a, [None] * n_a, [None] * n_b, [None] * n_b, [None] * n_c
    g_b_in, g_dw, g_dw_b, g_ln_g, g_ln_b, g_b_out = ([None] * n_a for _ in range(6))
    g_q_g, g_k_g, g_p_b, g_p_scale = [None] * n_b, [None] * n_b, [None] * n_c, [None] * n_c
    d_mod = [[None] * 3 for _ in range(depth)]
    d_norm_g = [[None] * 3 for _ in range(depth)]
    for i in reversed(range(depth)):
        for sub in reversed(range(3)):
            rec = saved[i][sub]
            dy, d_gate, dy_sum = gate_bwd(dx, rec['y'], rec['gate'], rec['coef'])
            if rec['kind'] == 'ffn':
                j = rec['j']
                dh, act, d1, d3 = ffn_bwd(dy, rec['h'], w1t[i, j], w3t[i, j], w2[i, j])
                g_w1t[i][j] = matmul_tn(d1, rec['h'])
                g_w3t[i][j] = matmul_tn(d3, rec['h'])
                g_w2[i][j] = matmul_tn(act, dy)
            elif rec['kind'] == 'conv':
                k = rec['idx']
                dsw = matmul_nt(dy, w_out[k])
                g_w_out[k] = matmul_tn(rec['sw'], dy)
                g_b_out[k] = dy_sum
                dpre, g_b_in[k], ddw, g_dw_b[k], g_ln_g[k], g_ln_b[k] = conv_mid_bwd(
                    rec['pre'], rec['v'], dsw, a_dw_f[k], a_ln_g_f[k].reshape(1, D), a_ln_b_f[k].reshape(1, D))
                g_dw[k] = ddw[:CONV_W]
                dh = matmul_nn(dpre, w_in_t[k])
                g_w_in_t[k] = matmul_tn(dpre, rec['h'])
            elif rec['kind'] == 'attn':
                k = rec['idx']
                do = matmul_nt(dy, w_o[k], out_dtype=BF)
                g_w_o[k] = matmul_tn(rec['o'], dy)
                dq, dk, dv = attn_bwd(rec['qkvn'], do, rec['r_all'])
                dqkv, dgains = qknorm_bwd(rec['qkv'], dq, dk, dv, rec['qg2'], rec['kg2'])
                dgains = dgains.reshape(3, N_HEADS, HEAD_DIM)
                g_q_g[k], g_k_g[k] = jnp.sum(dgains[0], axis=0), jnp.sum(dgains[1], axis=0)
                dh = matmul_nn(dqkv, w_qkv_t[k])
                g_w_qkv_t[k] = matmul_tn(dqkv, rec['h'])
            else:
                k = rec['idx']
                dh, dyy, g_p_b[k], g_p_scale[k] = pool_bwd(dy, rec['d'], p_w_f[k], rec['pb'], rec['ps'])
                full = matmul_tn(rec['d'], dyy)
                G = p_w_f.shape[1]
                Dg = D // G
                g_p_w[k] = jnp.stack([full[g * Dg:(g + 1) * Dg, g * Dg:(g + 1) * Dg] for g in range(G)])
            dx, d_shift, d_scale, d_norm_g[i][sub] = mod_bwd(rec['x'], dh, dx, rec['g'], rec['scale'])
            d_mod[i][sub] = jnp.concatenate([d_shift, d_scale, d_gate], axis=0)
    grad_x = dx[None]

    d_mod_mine = jnp.stack([jnp.stack(r) for r in d_mod])
    small_g = [d_mod_mine, jnp.stack([jnp.concatenate(r, axis=0) for r in d_norm_g]),
               jnp.stack(g_b_in), jnp.stack(g_dw), jnp.stack(g_dw_b), jnp.stack(g_ln_g), jnp.stack(g_ln_b), jnp.stack(g_b_out),
               jnp.stack(g_q_g), jnp.stack(g_k_g), jnp.stack(g_p_b), jnp.stack(g_p_scale)]
    stack2 = lambda g: jnp.stack([jnp.stack(r) for r in g])
    blocks = lambda g, ax: g.reshape(g.shape[:ax] + (N_DEV, g.shape[ax] // N_DEV) + g.shape[ax + 1:])
    big_g = [blocks(stack2(g_w1t), 2), blocks(stack2(g_w3t), 2), blocks(stack2(g_w2), 2), blocks(jnp.stack(g_w_in_t), 1),
             blocks(jnp.stack(g_w_out), 1), blocks(jnp.stack(g_w_qkv_t), 1), blocks(jnp.stack(g_w_o), 1), blocks(jnp.stack(g_p_w), 2)]
    big_ax = [2, 2, 2, 1, 1, 1, 1, 2]
    got = exchange(big_g + [_pack(small_g)], big_ax + [None])
    p_w1t, p_w3t, p_w2, p_w_in_t, p_w_out, p_w_qkv_t, p_w_o, p_p_w = got[:8]
    (dmod_all, dng_all, dbin_all, ddw_all, ddwb_all, dlng_all, dlnb_all, dbout_all, dqg_all, dkg_all, dpb_all,
     dps_all) = _unpack(got[8], [v.shape for v in small_g])
    dmod_all = dmod_all.reshape(N_DEV, depth, 9 * D)

    dmod_mine = lax.dynamic_slice_in_dim(dmod_all, me * n_mod, n_mod, axis=2)
    e_t = e_all.T
    grad_ada_w = jnp.stack([outer_rows(e_t, dmod_mine[:, i]) for i in range(depth)])
    de_part = matmul_nt(dmod_mine[:, 0], ada_w[0])
    for i in range(1, depth):
        de_part = de_part + matmul_nt(dmod_mine[:, i], ada_w[i])
    de_parts = exchange([de_part], [None])[0]
    dpre_all, grad_cond_b = cond_embed_bwd(de_parts, pre_all)
    grad_cond_w = outer_rows(c_mine.T, dpre_all)

    def mine(g_all, axis, size):
        return lax.dynamic_slice_in_dim(g_all, me * size, size, axis=axis)

    def upd(name, parts, shape2, to_out=lambda t: t, from_in=lambda t: t):
        w, m, v = (from_in(A[p + name]).reshape(shape2) for p in ('', 'm_', 'v_'))
        outs = adamw(w, parts.reshape((parts.shape[0],) + shape2), m, v)
        return [to_out(o).reshape(A[name].shape) for o in outs]

    L = depth
    res = {}
    res['cond_w'] = upd('cond_w', grad_cond_w[None], (rows, D))
    res['cond_b'] = upd('cond_b', grad_cond_b[None], (1, D))
    res['ada_w'] = upd('ada_w', grad_ada_w[None], (L * D, n_mod))
    res['ada_b'] = upd('ada_b', dmod_all, (L, 9 * D))
    res['norm_g'] = upd('norm_g', mine(dng_all, 3, D // N_DEV), (L * 3, D // N_DEV))
    fs = Fd // N_DEV
    res['ffn_w1'] = upd('ffn_w1', p_w1t, (L * 2 * fs, D), to_out=lambda t: swap(t.reshape(L, 2, fs, D)), from_in=swap)
    res['ffn_w3'] = upd('ffn_w3', p_w3t, (L * 2 * fs, D), to_out=lambda t: swap(t.reshape(L, 2, fs, D)), from_in=swap)
    res['ffn_w2'] = upd('ffn_w2', p_w2, (L * 2 * fs, D))
    ws = 2 * D // N_DEV
    res['a_w_in'] = upd('a_w_in', p_w_in_t, (n_a * ws, D), to_out=lambda t: swap(t.reshape(n_a, ws, D)), from_in=swap)
    res['a_b_in'] = upd('a_b_in', mine(dbin_all.reshape(N_DEV, n_a, 2 * D), 2, ws), (n_a, ws))
    res['a_dw'] = upd('a_dw', mine(ddw_all, 3, rows), (n_a * CONV_W, rows))
    res['a_dw_b'] = upd('a_dw_b', mine(ddwb_all.reshape(N_DEV, n_a, D), 2, rows), (n_a, rows))
    res['a_ln_g'] = upd('a_ln_g', mine(dlng_all.reshape(N_DEV, n_a, D), 2, rows), (n_a, rows))
    res['a_ln_b'] = upd('a_ln_b', mine(dlnb_all.reshape(N_DEV, n_a, D), 2, rows), (n_a, rows))
    res['a_w_out'] = upd('a_w_out', p_w_out, (n_a * rows, D))
    res['a_b_out'] = upd('a_b_out', mine(dbout_all.reshape(N_DEV, n_a, D), 2, rows), (n_a, rows))
    qs = 3 * D // N_DEV
    res['b_w_qkv'] = upd('b_w_qkv', p_w_qkv_t, (n_b * qs, D), to_out=lambda t: swap(t.reshape(n_b, qs, D)), from_in=swap)
    res['b_q_g'] = upd('b_q_g', dqg_all, (n_b, HEAD_DIM))
    res['b_k_g'] = upd('b_k_g', dkg_all, (n_b, HEAD_DIM))
    res['b_w_o'] = upd('b_w_o', p_w_o, (n_b * rows, D))
    G = p_w.shape[1]
    Dg = D // G
    res['p_w'] = upd('p_w', p_p_w, (n_c * G * Dg // N_DEV, Dg))
    res['p_b'] = upd('p_b', mine(dpb_all.reshape(N_DEV, n_c, G, Dg), 3, Dg // N_DEV), (n_c * G, Dg // N_DEV))
    res['p_scale'] = upd('p_scale', mine(dps_all.reshape(N_DEV, n_c, D), 2, rows), (n_c, rows))

    outs = [loss, grad_x]
    for k in range(4):
        outs += [res[n][k] for n in WEIGHTS]
    return tuple(outs)
```

```python
import math

import jax
import jax.numpy as jnp
from jax import lax
from jax.experimental import pallas as pl
from jax.experimental.pallas import tpu as pltpu

F32 = jnp.float32
BF = jnp.bfloat16
EPS = 1e-6
N_DEV = 8
N_HEADS = 16
HEAD_DIM = 64
LANES = 128
CONV_W = 31
CONV_PAD = 32
POOL_WINDOWS = (2, 4, 8, 16)
POOL_PAD = 16
VMEM_LIMIT = 56 * 1024 * 1024
ADAM_LR, ADAM_B1, ADAM_B2, ADAM_EPS, ADAM_WD, ADAM_STEP = 0.001, 0.9, 0.999, 1e-08, 0.01, 10
MESH = pl.DeviceIdType.MESH

WEIGHTS = ['cond_w', 'cond_b', 'ada_w', 'ada_b', 'norm_g', 'ffn_w1', 'ffn_w3', 'ffn_w2', 'a_w_in', 'a_b_in',
           'a_dw', 'a_dw_b', 'a_ln_g', 'a_ln_b', 'a_w_out', 'a_b_out', 'b_w_qkv', 'b_q_g', 'b_k_g', 'b_w_o',
           'p_w', 'p_b', 'p_scale']
ARGS = ['x', 'c'] + WEIGHTS + ['loss_target'] + ['m_' + n for n in WEIGHTS] + ['v_' + n for n in WEIGHTS]


def _pcall(body, **kw):
    return pl.pallas_call(body, **kw)


def _cp(*sem):
    return pltpu.CompilerParams(dimension_semantics=sem if sem else None, vmem_limit_bytes=VMEM_LIMIT)


def _nn(a, b):
    return lax.dot_general(a, b, (((1,), (0,)), ((), ())), preferred_element_type=F32)


def _nt(a, b):
    return lax.dot_general(a, b, (((1,), (1,)), ((), ())), preferred_element_type=F32)


def _tn(a, b):
    return lax.dot_general(a, b, (((0,), (0,)), ((), ())), preferred_element_type=F32)


def _sig(z):
    return 1.0 / (1.0 + jnp.exp(-z))


def _tile(n, pref, unit):
    t = (min(n, pref) // unit) * unit
    while t >= unit:
        if n % t == 0:
            return t
        t -= unit
    return n


def _colsum(v):
    return jnp.sum(v, axis=0, keepdims=True)


def _acc(ref, first, val):
    @pl.when(first)
    def _():
        ref[...] = val

    @pl.when(jnp.logical_not(first))
    def _():
        ref[...] += val


def _sds(shape, dt):
    return jax.ShapeDtypeStruct(shape, dt)


def mod_fwd(x, g, scale, shift):
    S, D = x.shape
    tm = _tile(S, 512, 8)

    def body(x_ref, g_ref, sc_ref, sh_ref, h_ref):
        xv = x_ref[...]
        r = lax.rsqrt(jnp.mean(xv * xv, axis=-1, keepdims=True) + EPS)
        h_ref[...] = (((xv * r) * g_ref[...]) * (1.0 + sc_ref[...]) + sh_ref[...]).astype(h_ref.dtype)

    row = pl.BlockSpec((tm, D), lambda i: (i, 0))
    vec = pl.BlockSpec((1, D), lambda i: (0, 0))
    return _pcall(body, name="mod_fwd", grid=(S // tm,), in_specs=[row, vec, vec, vec], out_specs=row,
                  out_shape=_sds((S, D), BF), compiler_params=_cp("parallel"))(x, g, scale, shift)


def resid_fwd(x, y, gate, coef):
    S, D = x.shape
    tm = _tile(S, 512, 8)

    def body(x_ref, y_ref, g_ref, o_ref):
        o_ref[...] = x_ref[...] + (coef * g_ref[...]) * y_ref[...]

    row = pl.BlockSpec((tm, D), lambda i: (i, 0))
    vec = pl.BlockSpec((1, D), lambda i: (0, 0))
    return _pcall(body, name="resid_fwd", grid=(S // tm,), in_specs=[row, row, vec], out_specs=row,
                  out_shape=_sds((S, D), F32), compiler_params=_cp("parallel"))(x, y, gate)


def gate_bwd(dxo, y, gate, coef):
    S, D = dxo.shape
    tm = _tile(S, 512, 8)

    def body(d_ref, y_ref, g_ref, dy_ref, dg_ref, ds_ref):
        first = pl.program_id(0) == 0
        d = d_ref[...]
        dy = (coef * g_ref[...]) * d
        dy_ref[...] = dy.astype(dy_ref.dtype)
        _acc(dg_ref, first, _colsum(coef * d * y_ref[...]))
        _acc(ds_ref, first, _colsum(dy))

    row = pl.BlockSpec((tm, D), lambda i: (i, 0))
    vec = pl.BlockSpec((1, D), lambda i: (0, 0))
    return _pcall(body, name="gate_bwd", grid=(S // tm,), in_specs=[row, row, vec], out_specs=[row, vec, vec],
                  out_shape=[_sds((S, D), BF), _sds((1, D), F32), _sds((1, D), F32)],
                  compiler_params=_cp("arbitrary"))(dxo, y, gate)


def mod_bwd(x, dh, dxo, g, scale):
    S, D = x.shape
    tm = _tile(S, 512, 8)

    def body(x_ref, dh_ref, do_ref, g_ref, sc_ref, dx_ref, dsh_ref, dsc_ref, dg_ref):
        first = pl.program_id(0) == 0
        xv = x_ref[...]
        dh = dh_ref[...].astype(F32)
        gv = g_ref[...]
        r = lax.rsqrt(jnp.mean(xv * xv, axis=-1, keepdims=True) + EPS)
        xh = xv * r
        dn = dh * (1.0 + sc_ref[...])
        u = dn * gv
        dx_ref[...] = do_ref[...] + r * (u - xh * jnp.mean(u * xh, axis=-1, keepdims=True))
        _acc(dsh_ref, first, _colsum(dh))
        _acc(dsc_ref, first, _colsum(dh * (xh * gv)))
        _acc(dg_ref, first, _colsum(dn * xh))

    row = pl.BlockSpec((tm, D), lambda i: (i, 0))
    vec = pl.BlockSpec((1, D), lambda i: (0, 0))
    return _pcall(body, name="mod_bwd", grid=(S // tm,), in_specs=[row, row, row, vec, vec],
                  out_specs=[row, vec, vec, vec],
                  out_shape=[_sds((S, D), F32), _sds((1, D), F32), _sds((1, D), F32), _sds((1, D), F32)],
                  compiler_params=_cp("arbitrary"))(x, dh, dxo, g, scale)


def loss_fwd_bwd(y, target):
    S, D = y.shape
    tm = _tile(S, 512, 8)

    def body(y_ref, t_ref, l_ref, d_ref):
        first = pl.program_id(0) == 0
        e = y_ref[...] - t_ref[...]
        d_ref[...] = e * (1.0 / D)
        part = 0.5 * jnp.sum(jnp.mean(e * e, axis=-1, keepdims=True), axis=0, keepdims=True)
        _acc(l_ref, first, part)

    row = pl.BlockSpec((tm, D), lambda i: (i, 0))
    return _pcall(body, name="loss", grid=(S // tm,), in_specs=[row, row],
                  out_specs=[pl.BlockSpec((1, 1), lambda i: (0, 0)), row],
                  out_shape=[_sds((1, 1), F32), _sds((S, D), F32)], compiler_params=_cp("arbitrary"))(y, target)


def matmul_nt(x, wt, bias=None, out_dtype=F32):
    S, K = x.shape
    N = wt.shape[0]
    tm, tn = _tile(S, 512, 8), _tile(N, 512, LANES)

    def body(*refs):
        x_ref, w_ref = refs[0], refs[1]
        o_ref = refs[-1]
        r = _nt(x_ref[...].astype(BF), w_ref[...].astype(BF))
        if bias is not None:
            r = r + refs[2][...]
        o_ref[...] = r.astype(o_ref.dtype)

    in_specs = [pl.BlockSpec((tm, K), lambda i, j: (i, 0)), pl.BlockSpec((tn, K), lambda i, j: (j, 0))]
    ops = [x, wt]
    if bias is not None:
        in_specs.append(pl.BlockSpec((1, tn), lambda i, j: (0, j)))
        ops.append(bias)
    return _pcall(body, name="matmul_nt", grid=(S // tm, N // tn), in_specs=in_specs,
                  out_specs=pl.BlockSpec((tm, tn), lambda i, j: (i, j)), out_shape=_sds((S, N), out_dtype),
                  compiler_params=_cp("parallel", "parallel"))(*ops)


def matmul_nn(x, w, bias=None, out_dtype=F32):
    S, K = x.shape
    N = w.shape[1]
    tm, tn = _tile(S, 512, 8), _tile(N, 512, LANES)

    def body(*refs):
        x_ref, w_ref = refs[0], refs[1]
        o_ref = refs[-1]
        r = _nn(x_ref[...].astype(BF), w_ref[...].astype(BF))
        if bias is not None:
            r = r + refs[2][...]
        o_ref[...] = r.astype(o_ref.dtype)

    in_specs = [pl.BlockSpec((tm, K), lambda i, j: (i, 0)), pl.BlockSpec((K, tn), lambda i, j: (0, j))]
    ops = [x, w]
    if bias is not None:
        in_specs.append(pl.BlockSpec((1, tn), lambda i, j: (0, j)))
        ops.append(bias)
    return _pcall(body, name="matmul_nn", grid=(S // tm, N // tn), in_specs=in_specs,
                  out_specs=pl.BlockSpec((tm, tn), lambda i, j: (i, j)), out_shape=_sds((S, N), out_dtype),
                  compiler_params=_cp("parallel", "parallel"))(*ops)


def matmul_tn(a, b):
    S, M = a.shape
    N = b.shape[1]
    bm, bn, tk = _tile(M, 1408, LANES), _tile(N, 1024, LANES), _tile(S, 512, 8)

    def body(a_ref, b_ref, o_ref):
        _acc(o_ref, pl.program_id(2) == 0, _tn(a_ref[...].astype(BF), b_ref[...].astype(BF)))

    return _pcall(body, name="matmul_tn", grid=(M // bm, N // bn, S // tk),
                  in_specs=[pl.BlockSpec((tk, bm), lambda i, j, k: (k, i)), pl.BlockSpec((tk, bn), lambda i, j, k: (k, j))],
                  out_specs=pl.BlockSpec((bm, bn), lambda i, j, k: (i, j)), out_shape=_sds((M, N), F32),
                  compiler_params=_cp("parallel", "parallel", "arbitrary"))(a, b)


def outer_rows(at, b):
    M, R = at.shape
    N = b.shape[1]
    tm = _tile(M, 256, 8)

    def body(a_ref, b_ref, o_ref):
        av, bv = a_ref[...], b_ref[...]
        acc = av[:, 0:1] * bv[0:1, :]
        for r in range(1, R):
            acc = acc + av[:, r:r + 1] * bv[r:r + 1, :]
        o_ref[...] = acc

    return _pcall(body, name="outer_rows", grid=(M // tm,),
                  in_specs=[pl.BlockSpec((tm, R), lambda i: (i, 0)), pl.BlockSpec((R, N), lambda i: (0, 0))],
                  out_specs=pl.BlockSpec((tm, N), lambda i: (i, 0)), out_shape=_sds((M, N), F32),
                  compiler_params=_cp("parallel"))(at, b)


def ffn_fwd(h, w1t, w3t, w2):
    S, D = h.shape
    Fd = w2.shape[0]
    tm, tf = _tile(S, 1024, 8), _tile(Fd, 256, LANES)

    def body(h_ref, w1_ref, w3_ref, w2_ref, y_ref):
        hv = h_ref[...]
        g1 = _nt(hv, w1_ref[...])
        g3 = _nt(hv, w3_ref[...])
        a = (g1 * _sig(g1)) * g3
        _acc(y_ref, pl.program_id(1) == 0, _nn(a.astype(BF), w2_ref[...]))

    row = pl.BlockSpec((tm, D), lambda i, f: (i, 0))
    wsp = pl.BlockSpec((tf, D), lambda i, f: (f, 0))
    return _pcall(body, name="ffn_fwd", grid=(S // tm, Fd // tf), in_specs=[row, wsp, wsp, wsp], out_specs=row,
                  out_shape=_sds((S, D), F32), compiler_params=_cp("parallel", "arbitrary"))(h, w1t, w3t, w2)


def ffn_bwd(dy, h, w1t, w3t, w2):
    S, D = h.shape
    Fd = w2.shape[0]
    tm, tf = _tile(S, 512, 8), _tile(Fd, 256, LANES)

    def body(dy_ref, h_ref, w1_ref, w3_ref, w2_ref, dh_ref, a_ref, d1_ref, d3_ref):
        hv = h_ref[...]
        w1, w3 = w1_ref[...], w3_ref[...]
        g1 = _nt(hv, w1)
        g3 = _nt(hv, w3)
        s = _sig(g1)
        si = g1 * s
        a_ref[...] = (si * g3).astype(a_ref.dtype)
        da = _nt(dy_ref[...], w2_ref[...])
        d3 = (da * si).astype(BF)
        d1 = (da * g3 * (s * (1.0 + g1 * (1.0 - s)))).astype(BF)
        d1_ref[...] = d1
        d3_ref[...] = d3
        _acc(dh_ref, pl.program_id(1) == 0, _nn(d1, w1) + _nn(d3, w3))

    row = pl.BlockSpec((tm, D), lambda i, f: (i, 0))
    wsp = pl.BlockSpec((tf, D), lambda i, f: (f, 0))
    act = pl.BlockSpec((tm, tf), lambda i, f: (i, f))
    return _pcall(body, name="ffn_bwd", grid=(S // tm, Fd // tf), in_specs=[row, row, wsp, wsp, wsp],
                  out_specs=[row, act, act, act],
                  out_shape=[_sds((S, D), F32), _sds((S, Fd), BF), _sds((S, Fd), BF), _sds((S, Fd), BF)],
                  compiler_params=_cp("parallel", "arbitrary"))(dy, h, w1t, w3t, w2)


def _layer_norm_parts(v):
    mu = jnp.mean(v, axis=-1, keepdims=True)
    vc = v - mu
    rs = lax.rsqrt(jnp.mean(vc * vc, axis=-1, keepdims=True) + EPS)
    return vc * rs, rs


def conv_mid_fwd(pre, dw, dw_b, ln_g, ln_b):
    S, D2 = pre.shape
    D = D2 // 2
    tm = _tile(S, 256, CONV_PAD)

    def body(pre_ref, dw_ref, dwb_ref, g_ref, b_ref, v_ref, sw_ref, ubuf):
        @pl.when(pl.program_id(0) == 0)
        def _():
            ubuf[pl.ds(0, CONV_PAD), :] = jnp.zeros((CONV_PAD, D), F32)

        @pl.when(pl.program_id(0) > 0)
        def _():
            ubuf[pl.ds(0, CONV_PAD), :] = ubuf[pl.ds(tm, CONV_PAD), :]

        ubuf[pl.ds(CONV_PAD, tm), :] = pre_ref[:, pl.ds(0, D)] * _sig(pre_ref[:, pl.ds(D, D)])
        acc = jnp.zeros((tm, D), F32) + dwb_ref[...]
        for k in range(CONV_W):
            acc = acc + dw_ref[pl.ds(k, 1), :] * ubuf[pl.ds(CONV_PAD - (CONV_W - 1) + k, tm), :]
        v_ref[...] = acc
        vh, _ = _layer_norm_parts(acc)
        ln = vh * g_ref[...] + b_ref[...]
        sw_ref[...] = (ln * _sig(ln)).astype(sw_ref.dtype)

    row = pl.BlockSpec((tm, D), lambda i: (i, 0))
    vec = pl.BlockSpec((1, D), lambda i: (0, 0))
    return _pcall(body, name="conv_mid_fwd", grid=(S // tm,),
                  in_specs=[pl.BlockSpec((tm, D2), lambda i: (i, 0)), pl.BlockSpec((CONV_PAD, D), lambda i: (0, 0)), vec, vec, vec],
                  out_specs=[row, row], out_shape=[_sds((S, D), F32), _sds((S, D), BF)],
                  scratch_shapes=[pltpu.VMEM((tm + CONV_PAD, D), F32)], compiler_params=_cp("arbitrary"))(pre, dw, dw_b, ln_g, ln_b)


def conv_mid_bwd(pre, v, dsw, dw, ln_g, ln_b):
    S, D2 = pre.shape
    D = D2 // 2
    tm = _tile(S, 256, CONV_PAD)
    nt = S // tm

    def body(pre_ref, v_ref, dsw_ref, dw_ref, g_ref, b_ref, dpre_ref, dbin_ref, ddw_ref, ddwb_ref, dg_ref, db_ref, dvbuf):
        first = pl.program_id(0) == 0

        @pl.when(first)
        def _():
            dvbuf[pl.ds(tm, CONV_PAD), :] = jnp.zeros((CONV_PAD, D), F32)
            ddw_ref[...] = jnp.zeros((CONV_PAD, D), F32)

        @pl.when(jnp.logical_not(first))
        def _():
            dvbuf[pl.ds(tm, CONV_PAD), :] = dvbuf[pl.ds(0, CONV_PAD), :]

        a = pre_ref[:, pl.ds(0, D)]
        sb = _sig(pre_ref[:, pl.ds(D, D)])
        u = a * sb
        gv = g_ref[...]
        vh, rs = _layer_norm_parts(v_ref[...])
        ln = vh * gv + b_ref[...]
        sg = _sig(ln)
        dln = dsw_ref[...] * (sg * (1.0 + ln * (1.0 - sg)))
        _acc(dg_ref, first, _colsum(dln * vh))
        _acc(db_ref, first, _colsum(dln))
        dvh = dln * gv
        dv = rs * (dvh - jnp.mean(dvh, axis=-1, keepdims=True) - vh * jnp.mean(dvh * vh, axis=-1, keepdims=True))
        _acc(ddwb_ref, first, _colsum(dv))
        dvbuf[pl.ds(0, tm), :] = dv
        du = jnp.zeros((tm, D), F32)
        for k in range(CONV_W):
            sh = dvbuf[pl.ds(CONV_W - 1 - k, tm), :]
            du = du + dw_ref[pl.ds(k, 1), :] * sh
            ddw_ref[pl.ds(k, 1), :] += _colsum(u * sh)
        da = du * sb
        db = du * a * sb * (1.0 - sb)
        dpre_ref[:, pl.ds(0, D)] = da.astype(dpre_ref.dtype)
        dpre_ref[:, pl.ds(D, D)] = db.astype(dpre_ref.dtype)
        _acc(dbin_ref, first, jnp.concatenate([_colsum(da), _colsum(db)], axis=1))

    rev = lambda i: (nt - 1 - i, 0)
    row = pl.BlockSpec((tm, D), rev)
    row2 = pl.BlockSpec((tm, D2), rev)
    vec = pl.BlockSpec((1, D), lambda i: (0, 0))
    pad = pl.BlockSpec((CONV_PAD, D), lambda i: (0, 0))
    return _pcall(body, name="conv_mid_bwd", grid=(nt,), in_specs=[row2, row, row, pad, vec, vec],
                  out_specs=[row2, pl.BlockSpec((1, D2), lambda i: (0, 0)), pad, vec, vec, vec],
                  out_shape=[_sds((S, D2), BF), _sds((1, D2), F32), _sds((CONV_PAD, D), F32), _sds((1, D), F32),
                             _sds((1, D), F32), _sds((1, D), F32)],
                  scratch_shapes=[pltpu.VMEM((tm + CONV_PAD, D), F32)], compiler_params=_cp("arbitrary"))(pre, v, dsw, dw, ln_g, ln_b)


def _pool_count(i, tm, w):
    t = i * tm + lax.broadcasted_iota(jnp.int32, (tm, 1), 0)
    return jnp.minimum(t + 1, w).astype(F32)


def pool_fwd(h, p_w, p_b, p_scale):
    S, D = h.shape
    G, Dg = p_w.shape[0], p_w.shape[1]
    tm = _tile(S, 256, POOL_PAD)

    def body(h_ref, w_ref, b_ref, s_ref, y_ref, d_ref, hbuf):
        i = pl.program_id(0)

        @pl.when(i == 0)
        def _():
            hbuf[pl.ds(0, POOL_PAD), :] = jnp.zeros((POOL_PAD, D), F32)

        @pl.when(i > 0)
        def _():
            hbuf[pl.ds(0, POOL_PAD), :] = hbuf[pl.ds(tm, POOL_PAD), :]

        hbuf[pl.ds(POOL_PAD, tm), :] = h_ref[...].astype(F32)
        for g, w in enumerate(POOL_WINDOWS):
            cols = pl.ds(g * Dg, Dg)
            hg = hbuf[pl.ds(POOL_PAD, tm), cols]
            win = hg
            for j in range(1, w):
                win = win + hbuf[pl.ds(POOL_PAD - j, tm), cols]
            d = (win / _pool_count(i, tm, w) - hg).astype(BF)
            d_ref[:, cols] = d
            y_ref[:, cols] = (_nn(d, w_ref[g].astype(BF)) + b_ref[:, cols]) * s_ref[:, cols]

    row = pl.BlockSpec((tm, D), lambda i: (i, 0))
    vec = pl.BlockSpec((1, D), lambda i: (0, 0))
    return _pcall(body, name="pool_fwd", grid=(S // tm,),
                  in_specs=[row, pl.BlockSpec((G, Dg, Dg), lambda i: (0, 0, 0)), vec, vec], out_specs=[row, row],
                  out_shape=[_sds((S, D), F32), _sds((S, D), BF)],
                  scratch_shapes=[pltpu.VMEM((tm + POOL_PAD, D), F32)], compiler_params=_cp("arbitrary"))(h, p_w, p_b, p_scale)


def pool_bwd(dy, d, p_w, p_b, p_scale):
    S, D = dy.shape
    G, Dg = p_w.shape[0], p_w.shape[1]
    tm = _tile(S, 256, POOL_PAD)
    nt = S // tm

    def body(dy_ref, d_ref, w_ref, b_ref, s_ref, dh_ref, dyy_ref, dpb_ref, dps_ref, ebuf):
        j = pl.program_id(0)
        i = nt - 1 - j
        first = j == 0

        @pl.when(first)
        def _():
            ebuf[pl.ds(tm, POOL_PAD), :] = jnp.zeros((POOL_PAD, D), F32)

        @pl.when(jnp.logical_not(first))
        def _():
            ebuf[pl.ds(tm, POOL_PAD), :] = ebuf[pl.ds(0, POOL_PAD), :]

        dpb, dps = [], []
        for g, w in enumerate(POOL_WINDOWS):
            cols = pl.ds(g * Dg, Dg)
            wg = w_ref[g].astype(BF)
            dyg = dy_ref[:, cols].astype(F32)
            ypre = _nn(d_ref[:, cols], wg) + b_ref[:, cols]
            dps.append(_colsum(dyg * ypre))
            dyy = dyg * s_ref[:, cols]
            dpb.append(_colsum(dyy))
            dyy = dyy.astype(BF)
            dyy_ref[:, cols] = dyy
            dd = _nt(dyy, wg)
            ebuf[pl.ds(0, tm), cols] = dd / _pool_count(i, tm, w)
            acc = -dd
            for k in range(w):
                acc = acc + ebuf[pl.ds(k, tm), cols]
            dh_ref[:, cols] = acc
        _acc(dpb_ref, first, jnp.concatenate(dpb, axis=1))
        _acc(dps_ref, first, jnp.concatenate(dps, axis=1))

    row = pl.BlockSpec((tm, D), lambda i: (nt - 1 - i, 0))
    vec = pl.BlockSpec((1, D), lambda i: (0, 0))
    return _pcall(body, name="pool_bwd", grid=(nt,),
                  in_specs=[row, row, pl.BlockSpec((G, Dg, Dg), lambda i: (0, 0, 0)), vec, vec],
                  out_specs=[row, row, vec, vec],
                  out_shape=[_sds((S, D), F32), _sds((S, D), BF), _sds((1, D), F32), _sds((1, D), F32)],
                  scratch_shapes=[pltpu.VMEM((tm + POOL_PAD, D), F32)], compiler_params=_cp("arbitrary"))(dy, d, p_w, p_b, p_scale)


def _group_sum(v, low):
    s_lo = jnp.sum(jnp.where(low, v, 0.0), axis=-1, keepdims=True)
    s_hi = jnp.sum(jnp.where(low, 0.0, v), axis=-1, keepdims=True)
    return jnp.where(low, s_lo, s_hi)


def qknorm_fwd(qkv, q_g2, k_g2):
    S, D3 = qkv.shape
    nb = D3 // LANES // 3
    tm = _tile(S, 512, 8)

    def body(x_ref, qg_ref, kg_ref, o_ref):
        j = pl.program_id(1)
        xv = x_ref[...]
        low = lax.broadcasted_iota(jnp.int32, (1, LANES), 1) < HEAD_DIM
        r = lax.rsqrt(_group_sum(xv * xv, low) * (1.0 / HEAD_DIM) + EPS)
        xn = xv * r
        qn = (xn * qg_ref[...]) * (HEAD_DIM ** -0.5)
        kn = xn * kg_ref[...]
        o_ref[...] = jnp.where(j < nb, qn, jnp.where(j < 2 * nb, kn, xv)).astype(o_ref.dtype)

    blk = pl.BlockSpec((tm, LANES), lambda i, j: (i, j))
    vec = pl.BlockSpec((1, LANES), lambda i, j: (0, 0))
    return _pcall(body, name="qknorm_fwd", grid=(S // tm, D3 // LANES), in_specs=[blk, vec, vec], out_specs=blk,
                  out_shape=_sds((S, D3), BF), compiler_params=_cp("parallel", "parallel"))(qkv, q_g2, k_g2)


def qknorm_bwd(qkv, dq, dk, dv, q_g2, k_g2):
    S, D3 = qkv.shape
    nb = D3 // LANES // 3
    tm = _tile(S, 512, 8)

    def body(x_ref, dq_ref, dk_ref, dv_ref, qg_ref, kg_ref, o_ref, dg_ref):
        j = pl.program_id(0)
        is_q, is_k = j < nb, jnp.logical_and(j >= nb, j < 2 * nb)
        xv = x_ref[...]
        low = lax.broadcasted_iota(jnp.int32, (1, LANES), 1) < HEAD_DIM
        r = lax.rsqrt(_group_sum(xv * xv, low) * (1.0 / HEAD_DIM) + EPS)
        xh = xv * r
        dn = jnp.where(is_q, dq_ref[...] * (HEAD_DIM ** -0.5), dk_ref[...])
        gv = jnp.where(is_q, qg_ref[...], kg_ref[...])
        u = dn * gv
        dx = r * (u - xh * (_group_sum(u * xh, low) * (1.0 / HEAD_DIM)))
        o_ref[...] = jnp.where(jnp.logical_or(is_q, is_k), dx, dv_ref[...]).astype(o_ref.dtype)
        _acc(dg_ref, pl.program_id(1) == 0, _colsum(dn * xh))

    blk = pl.BlockSpec((tm, LANES), lambda j, i: (i, j))
    part = lambda lo: pl.BlockSpec((tm, LANES), lambda j, i: (i, jnp.clip(j - lo, 0, nb - 1)))
    vec = pl.BlockSpec((1, LANES), lambda j, i: (0, 0))
    return _pcall(body, name="qknorm_bwd", grid=(D3 // LANES, S // tm),
                  in_specs=[blk, part(0), part(nb), part(2 * nb), vec, vec],
                  out_specs=[blk, pl.BlockSpec((1, LANES), lambda j, i: (0, j))],
                  out_shape=[_sds((S, D3), BF), _sds((1, D3), F32)],
                  compiler_params=_cp("parallel", "arbitrary"))(qkv, dq, dk, dv, q_g2, k_g2)


def _softplus_parts(z):
    e = jnp.exp(-jnp.abs(z))
    return jnp.maximum(z, 0.0) + jnp.log(1.0 + e), e


def _split_bf(v):
    hi = v.astype(BF)
    return hi, (v - hi.astype(F32)).astype(BF)


def _attn_tile(S):
    return _tile(S, 256, LANES) if S >= 1024 else LANES


def attn_fwd(qkvn):
    S, D3 = qkvn.shape
    D = D3 // 3
    nb = D // LANES
    t = _attn_tile(S)
    nq = S // t
    assert nq <= LANES

    def body(q_ref, k_ref, v_ref, o_ref, r_ref, acc_ref, rall_ref, behind_ref, z_ref):
        qi = pl.program_id(1)
        lane = lax.broadcasted_iota(jnp.int32, (1, LANES), 1)
        rows = lax.broadcasted_iota(jnp.int32, (t, t), 0)
        cols = lax.broadcasted_iota(jnp.int32, (t, t), 1)
        later = (rows > cols).astype(BF)
        later2 = jnp.concatenate([later, later], axis=0)
        q = q_ref[...]
        qh = [jnp.where(lane < HEAD_DIM, q, jnp.zeros_like(q)), jnp.where(lane >= HEAD_DIM, q, jnp.zeros_like(q))]
        acc_ref[...] = jnp.zeros((2, t, LANES), F32)
        rall_ref[...] = jnp.zeros((2, t, LANES), F32)
        behind_ref[...] = jnp.zeros((2, t, LANES), F32)

        def scores(b):
            kb = k_ref[pl.ds(pl.multiple_of(b * t, t), t), :]
            return [_nt(qh[0], kb), _nt(qh[1], kb)]

        def block(b, diagonal):
            vb = v_ref[pl.ds(pl.multiple_of(b * t, t), t), :]
            lsig, sp, within = [None, None], [None, None], [None, None]
            for hh in range(2):
                z = z_ref[hh]
                sp_full, _ = _softplus_parts(z)
                lsig[hh] = z - sp_full
                sp[hh] = jnp.where(cols < rows, sp_full, 0.0) if diagonal else sp_full
                within[hh] = _nn(jnp.concatenate(_split_bf(sp[hh]), axis=1), later2)
            z_next = scores(jnp.maximum(b - 1, 0))
            for hh in range(2):
                behind = behind_ref[hh]
                a = jnp.exp(lsig[hh] - (within[hh] + jnp.tile(behind, (1, t // LANES))))
                if diagonal:
                    a = jnp.where(cols < rows, a, 0.0)
                acc_ref[hh] += _nn(a.astype(BF), vb)
                rall_ref[hh] = jnp.where(lane == b, behind, rall_ref[hh])
                behind_ref[hh] = behind + jnp.sum(sp[hh], axis=-1, keepdims=True)
            for hh in range(2):
                z_ref[hh] = z_next[hh]

        z_first = scores(qi)
        for hh in range(2):
            z_ref[hh] = z_first[hh]
        block(qi, True)

        def step(n, carry):
            block(qi - 1 - n, False)
            return carry

        lax.fori_loop(0, qi, step, 0)
        r_ref[...] = rall_ref[...]
        o_ref[...] = jnp.where(lane < HEAD_DIM, acc_ref[0], acc_ref[1]).astype(o_ref.dtype)

    return _pcall(body, name="attn_fwd", grid=(nb, nq),
                  in_specs=[pl.BlockSpec((t, LANES), lambda h, i: (i, h)),
                            pl.BlockSpec((S, LANES), lambda h, i: (0, nb + h)),
                            pl.BlockSpec((S, LANES), lambda h, i: (0, 2 * nb + h))],
                  out_specs=[pl.BlockSpec((t, LANES), lambda h, i: (i, h)), pl.BlockSpec((2, t, LANES), lambda h, i: (h, i, 0))],
                  out_shape=[_sds((S, D), BF), _sds((2 * nb, S, LANES), F32)],
                  scratch_shapes=[pltpu.VMEM((2, t, LANES), F32)] * 3 + [pltpu.VMEM((2, t, t), F32)],
                  compiler_params=_cp("parallel", "arbitrary"))(qkvn, qkvn, qkvn)


def attn_bwd(qkvn, do, r_all):
    S, D3 = qkvn.shape
    D = D3 // 3
    nb = D // LANES
    t = _attn_tile(S)
    nq = S // t

    def body(q_ref, k_ref, v_ref, do_ref, r_ref, dq_ref, dk_hbm, dv_hbm, dk_acc, dv_acc, dq_acc, before_ref, z_ref, da_ref, sem):
        hp = pl.program_id(0)
        qi = pl.program_id(1)

        @pl.when(qi == 0)
        def _():
            dk_acc[...] = jnp.zeros((S, LANES), F32)
            dv_acc[...] = jnp.zeros((S, LANES), F32)

        lane = lax.broadcasted_iota(jnp.int32, (1, LANES), 1)
        rows = lax.broadcasted_iota(jnp.int32, (t, t), 0)
        cols = lax.broadcasted_iota(jnp.int32, (t, t), 1)
        later = (rows > cols).astype(BF)
        earlier = (rows < cols).astype(BF)
        later2 = jnp.concatenate([later, later], axis=0)
        earlier2 = jnp.concatenate([earlier, earlier], axis=0)
        q = q_ref[...]
        dov = do_ref[...]
        low, high = lane < HEAD_DIM, lane >= HEAD_DIM
        qh = [jnp.where(low, q, jnp.zeros_like(q)), jnp.where(high, q, jnp.zeros_like(q))]
        doh = [jnp.where(low, dov, jnp.zeros_like(dov)), jnp.where(high, dov, jnp.zeros_like(dov))]
        q_both = jnp.concatenate(qh, axis=0)
        do_both = jnp.concatenate(doh, axis=0)
        dq_acc[...] = jnp.zeros((2, t, LANES), F32)
        before_ref[...] = jnp.zeros((2, t, LANES), F32)

        def scores(b):
            off = pl.multiple_of(b * t, t)
            kb = k_ref[pl.ds(off, t), :]
            vb = v_ref[pl.ds(off, t), :]
            return [_nt(qh[0], kb), _nt(qh[1], kb)], [_nt(doh[0], vb), _nt(doh[1], vb)]

        def block(b, diagonal):
            off = pl.multiple_of(b * t, t)
            kb = k_ref[pl.ds(off, t), :]
            lsig, sigma, within, g, earlier_g, dz, prob = ([None, None] for _ in range(7))
            for hh in range(2):
                z = z_ref[hh]
                sp, _ = _softplus_parts(z)
                lsig[hh] = z - sp
                if diagonal:
                    sp = jnp.where(cols < rows, sp, 0.0)
                within[hh] = _nn(jnp.concatenate(_split_bf(sp), axis=1), later2)
            z_next, da_next = scores(jnp.minimum(b + 1, qi))
            for hh in range(2):
                behind = jnp.sum(jnp.where(lane == b, r_ref[hh], 0.0), axis=-1, keepdims=True)
                a = jnp.exp(lsig[hh] - (within[hh] + behind))
                if diagonal:
                    a = jnp.where(cols < rows, a, 0.0)
                prob[hh] = a.astype(BF)
                g[hh] = a * da_ref[hh]
                earlier_g[hh] = _nn(jnp.concatenate(_split_bf(g[hh]), axis=1), earlier2)
            for hh in range(2):
                before = before_ref[hh]
                d = g[hh] - jnp.exp(lsig[hh]) * (g[hh] + (earlier_g[hh] + jnp.tile(before, (1, t // LANES))))
                if diagonal:
                    d = jnp.where(cols < rows, d, 0.0)
                dz[hh] = d.astype(BF)
                dq_acc[hh] += _nn(dz[hh], kb)
                before_ref[hh] = before + jnp.sum(g[hh], axis=-1, keepdims=True)
            dk_acc[pl.ds(off, t), :] += _tn(jnp.concatenate(dz, axis=0), q_both)
            dv_acc[pl.ds(off, t), :] += _tn(jnp.concatenate(prob, axis=0), do_both)
            for hh in range(2):
                z_ref[hh] = z_next[hh]
                da_ref[hh] = da_next[hh]

        z_first, da_first = scores(0)
        for hh in range(2):
            z_ref[hh] = z_first[hh]
            da_ref[hh] = da_first[hh]

        def step(b, carry):
            block(b, False)
            return carry

        lax.fori_loop(0, qi, step, 0)
        block(qi, True)
        dq_ref[...] = jnp.where(low, dq_acc[0], dq_acc[1])

        @pl.when(qi == nq - 1)
        def _():
            col = pl.multiple_of(hp * LANES, LANES)
            c1 = pltpu.make_async_copy(dk_acc, dk_hbm.at[:, pl.ds(col, LANES)], sem.at[0])
            c2 = pltpu.make_async_copy(dv_acc, dv_hbm.at[:, pl.ds(col, LANES)], sem.at[1])
            c1.start()
            c2.start()
            c1.wait()
            c2.wait()

    blk = pl.BlockSpec((t, LANES), lambda h, i: (i, h))
    hbm = pl.BlockSpec(memory_space=pl.ANY)
    return _pcall(body, name="attn_bwd", grid=(nb, nq),
                  in_specs=[blk, pl.BlockSpec((S, LANES), lambda h, i: (0, nb + h)),
                            pl.BlockSpec((S, LANES), lambda h, i: (0, 2 * nb + h)), blk,
                            pl.BlockSpec((2, t, LANES), lambda h, i: (h, i, 0))],
                  out_specs=[blk, hbm, hbm],
                  out_shape=[_sds((S, D), F32), _sds((S, D), F32), _sds((S, D), F32)],
                  scratch_shapes=[pltpu.VMEM((S, LANES), F32), pltpu.VMEM((S, LANES), F32), pltpu.VMEM((2, t, LANES), F32),
                                  pltpu.VMEM((2, t, LANES), F32), pltpu.VMEM((2, t, t), F32), pltpu.VMEM((2, t, t), F32),
                                  pltpu.SemaphoreType.DMA((2,))],
                  compiler_params=_cp("arbitrary", "arbitrary"))(qkvn, qkvn, qkvn, do, r_all)


def cond_embed(parts, cond_b):
    P, B, D = parts.shape

    def body(p_ref, b_ref, pre_ref, e_ref):
        pre = p_ref[0]
        for s in range(1, P):
            pre = pre + p_ref[s]
        pre = pre + b_ref[...]
        pre_ref[...] = pre
        e_ref[...] = pre * _sig(pre)

    return _pcall(body, name="cond_embed", out_shape=[_sds((B, D), F32), _sds((B, D), F32)],
                  compiler_params=_cp())(parts, cond_b)


def cond_embed_bwd(parts, pre):
    P, B, D = parts.shape

    def body(p_ref, pre_ref, dpre_ref, db_ref):
        de = p_ref[0]
        for s in range(1, P):
            de = de + p_ref[s]
        pre = pre_ref[...]
        s = _sig(pre)
        dpre = de * (s * (1.0 + pre * (1.0 - s)))
        dpre_ref[...] = dpre
        db_ref[...] = _colsum(dpre)

    return _pcall(body, name="cond_embed_bwd", out_shape=[_sds((B, D), F32), _sds((1, D), F32)],
                  compiler_params=_cp())(parts, pre)


def adamw(w, parts, m, v):
    R, C = w.shape
    P = parts.shape[0]
    tr = _tile(R, max(8, (1 << 17) // C // 8 * 8), 8)
    c1 = 1.0 - ADAM_B1 ** ADAM_STEP
    c2 = 1.0 - ADAM_B2 ** ADAM_STEP

    def body(w_ref, p_ref, m_ref, v_ref, g_ref, d_ref, nm_ref, nv_ref):
        g = p_ref[0].astype(F32)
        for s in range(1, P):
            g = g + p_ref[s].astype(F32)
        nm = ADAM_B1 * m_ref[...] + (1.0 - ADAM_B1) * g
        nv = ADAM_B2 * v_ref[...] + (1.0 - ADAM_B2) * (g * g)
        g_ref[...] = g
        nm_ref[...] = nm
        nv_ref[...] = nv
        d_ref[...] = -ADAM_LR * ((nm / c1) / (jnp.sqrt(nv / c2) + ADAM_EPS) + ADAM_WD * w_ref[...])

    blk = pl.BlockSpec((tr, C), lambda i: (i, 0))
    return _pcall(body, name="adamw", grid=(R // tr,), in_specs=[blk, pl.BlockSpec((P, tr, C), lambda i: (0, i, 0)), blk, blk],
                  out_specs=[blk] * 4, out_shape=[_sds((R, C), F32)] * 4, compiler_params=_cp("parallel"))(w, parts, m, v)


def exchange(arrs, peer_axes):
    n = len(arrs)
    shapes = [a.shape if ax is None else a.shape[:ax] + a.shape[ax + 1:] for a, ax in zip(arrs, peer_axes)]

    def body(*refs):
        ins, outs = refs[:n], refs[n:2 * n]
        send, recv, local = refs[2 * n:]
        x, y, c = lax.axis_index("x"), lax.axis_index("y"), lax.axis_index("c")
        me = 4 * x + 2 * y + c

        def piece(i, d):
            if peer_axes[i] is None:
                return ins[i]
            return ins[i].at[(slice(None),) * peer_axes[i] + (d,)]

        mine = [pltpu.make_async_copy(piece(i, me), outs[i].at[me], local.at[i]) for i in range(n)]
        for cp in mine:
            cp.start()
        sends, recvs = [], []
        for p in range(1, N_DEV):
            peer = (1 - x if p & 4 else x, 1 - y if p & 2 else y, 1 - c if p & 1 else c)
            them = 4 * peer[0] + 2 * peer[1] + peer[2]
            for i in range(n):
                sends.append(pltpu.make_async_remote_copy(piece(i, them), outs[i].at[me], send.at[i, p - 1], recv.at[i, p - 1],
                                                          device_id=peer, device_id_type=MESH))
                recvs.append(pltpu.make_async_remote_copy(piece(i, me), outs[i].at[them], send.at[i, p - 1], recv.at[i, p - 1],
                                                          device_id=peer, device_id_type=MESH))
        for cp in sends:
            cp.start()
        for cp in recvs:
            cp.wait_recv()
        for cp in sends:
            cp.wait_send()
        for cp in mine:
            cp.wait()

    hbm = pl.BlockSpec(memory_space=pl.ANY)
    return _pcall(body, name="exchange", in_specs=[hbm] * n, out_specs=[hbm] * n,
                  out_shape=[_sds((N_DEV,) + s, a.dtype) for s, a in zip(shapes, arrs)],
                  scratch_shapes=[pltpu.SemaphoreType.DMA((n, N_DEV - 1)), pltpu.SemaphoreType.DMA((n, N_DEV - 1)),
                                  pltpu.SemaphoreType.DMA((n,))])(*arrs)


def _pack(vs):
    flat = jnp.concatenate([v.reshape(-1).astype(F32) for v in vs])
    pad = (-flat.shape[0]) % 1024
    return jnp.pad(flat, (0, pad)).reshape(-1, 1024)


def _unpack(packed, shapes):
    flat = packed.reshape(packed.shape[0], -1)
    out, o = [], 0
    for s in shapes:
        n = math.prod(s)
        out.append(flat[:, o:o + n].reshape((packed.shape[0],) + tuple(s)))
        o += n
    return out


def _cat_dev(g, axis):
    g = jnp.moveaxis(g, 0, axis)
    return g.reshape(g.shape[:axis] + (g.shape[axis] * g.shape[axis + 1],) + g.shape[axis + 2:])


def kernel(x, c, cond_w, cond_b, ada_w, ada_b, norm_g, ffn_w1, ffn_w3, ffn_w2, a_w_in, a_b_in, a_dw, a_dw_b, a_ln_g, a_ln_b, a_w_out, a_b_out, b_w_qkv, b_q_g, b_k_g, b_w_o, p_w, p_b, p_scale, loss_target, m_cond_w, m_cond_b, m_ada_w, m_ada_b, m_norm_g, m_ffn_w1, m_ffn_w3, m_ffn_w2, m_a_w_in, m_a_b_in, m_a_dw, m_a_dw_b, m_a_ln_g, m_a_ln_b, m_a_w_out, m_a_b_out, m_b_w_qkv, m_b_q_g, m_b_k_g, m_b_w_o, m_p_w, m_p_b, m_p_scale, v_cond_w, v_cond_b, v_ada_w, v_ada_b, v_norm_g, v_ffn_w1, v_ffn_w3, v_ffn_w2, v_a_w_in, v_a_b_in, v_a_dw, v_a_dw_b, v_a_ln_g, v_a_ln_b, v_a_w_out, v_a_b_out, v_b_w_qkv, v_b_q_g, v_b_k_g, v_b_w_o, v_p_w, v_p_b, v_p_scale):
    A = dict(zip(ARGS, (x, c, cond_w, cond_b, ada_w, ada_b, norm_g, ffn_w1, ffn_w3, ffn_w2, a_w_in, a_b_in, a_dw, a_dw_b, a_ln_g, a_ln_b, a_w_out, a_b_out, b_w_qkv, b_q_g, b_k_g, b_w_o, p_w, p_b, p_scale, loss_target, m_cond_w, m_cond_b, m_ada_w, m_ada_b, m_norm_g, m_ffn_w1, m_ffn_w3, m_ffn_w2, m_a_w_in, m_a_b_in, m_a_dw, m_a_dw_b, m_a_ln_g, m_a_ln_b, m_a_w_out, m_a_b_out, m_b_w_qkv, m_b_q_g, m_b_k_g, m_b_w_o, m_p_w, m_p_b, m_p_scale, v_cond_w, v_cond_b, v_ada_w, v_ada_b, v_norm_g, v_ffn_w1, v_ffn_w3, v_ffn_w2, v_a_w_in, v_a_b_in, v_a_dw, v_a_dw_b, v_a_ln_g, v_a_ln_b, v_a_w_out, v_a_b_out, v_b_w_qkv, v_b_q_g, v_b_k_g, v_b_w_o, v_p_w, v_p_b, v_p_scale)))
    S, D = x.shape[1], x.shape[2]
    depth = ada_w.shape[0]
    n_a, n_b, n_c = a_w_in.shape[0], b_w_qkv.shape[0], p_w.shape[0]
    me = 4 * lax.axis_index("x") + 2 * lax.axis_index("y") + lax.axis_index("c")
    swap = lambda w: jnp.swapaxes(w, -1, -2)

    small_names = ['norm_g', 'a_b_in', 'a_dw', 'a_dw_b', 'a_ln_g', 'a_ln_b', 'a_b_out', 'p_w', 'p_b', 'p_scale']
    small_in = [c] + [A[n] for n in small_names]
    big_in = [swap(ffn_w1).astype(BF), swap(ffn_w3).astype(BF), ffn_w2.astype(BF), swap(a_w_in).astype(BF),
              a_w_out.astype(BF), swap(b_w_qkv).astype(BF), b_w_o.astype(BF)]
    got = exchange(big_in + [_pack(small_in)], [None] * (len(big_in) + 1))
    w1t, w3t, w2 = _cat_dev(got[0], 2), _cat_dev(got[1], 2), _cat_dev(got[2], 2)
    w_in_t, w_out = _cat_dev(got[3], 1), _cat_dev(got[4], 1)
    w_qkv_t, w_o = _cat_dev(got[5], 1), _cat_dev(got[6], 1)
    sm = dict(zip(['c'] + small_names, _unpack(got[7], [v.shape for v in small_in])))
    c_all = sm['c'][:, 0]
    norm_g_f = _cat_dev(sm['norm_g'], 2)
    a_b_in_f, a_dw_f, a_dw_b_f = _cat_dev(sm['a_b_in'], 1), _cat_dev(sm['a_dw'], 2), _cat_dev(sm['a_dw_b'], 1)
    a_ln_g_f, a_ln_b_f, a_b_out_f = _cat_dev(sm['a_ln_g'], 1), _cat_dev(sm['a_ln_b'], 1), _cat_dev(sm['a_b_out'], 1)
    p_w_f, p_b_f, p_scale_f = _cat_dev(sm['p_w'], 2), _cat_dev(sm['p_b'], 2), _cat_dev(sm['p_scale'], 1)
    a_dw_f = jnp.pad(a_dw_f, ((0, 0), (0, CONV_PAD - CONV_W), (0, 0)))

    rows = D // N_DEV
    c_mine = lax.dynamic_slice_in_dim(c_all, me * rows, rows, axis=1)
    pre_parts = exchange([matmul_nn(c_mine, cond_w)], [None])[0]
    pre_all, e_all = cond_embed(pre_parts, cond_b.reshape(1, D))
    n_mod = ada_w.shape[2]
    ada_b_mine = lax.dynamic_slice_in_dim(ada_b, me * n_mod, n_mod, axis=1)
    mod_part = jnp.stack([matmul_nn(e_all, ada_w[i], ada_b_mine[i:i + 1]) for i in range(depth)], axis=1)
    mod_all = exchange([mod_part], [None])[0]
    mod = lax.dynamic_index_in_dim(mod_all, me, axis=1, keepdims=False)
    mod = jnp.moveaxis(mod, 0, 1).reshape(depth, 3, 3, 1, D)

    xs = x[0]
    saved = []
    ia = ib = ic = 0
    for i in range(depth):
        lay = {}
        for sub in range(3):
            shift, scale, gate = mod[i, sub, 0], mod[i, sub, 1], 1.0 + mod[i, sub, 2]
            g = norm_g_f[i, sub].reshape(1, D)
            h = mod_fwd(xs, g, scale, shift)
            rec = dict(x=xs, h=h, g=g, scale=scale, gate=gate)
            if sub != 1:
                j = 0 if sub == 0 else 1
                y = ffn_fwd(h, w1t[i, j], w3t[i, j], w2[i, j])
                rec.update(kind='ffn', j=j, coef=0.5)
            elif i % 3 == 0:
                pre = matmul_nt(h, w_in_t[ia], a_b_in_f[ia].reshape(1, 2 * D))
                v, sw = conv_mid_fwd(pre, a_dw_f[ia], a_dw_b_f[ia].reshape(1, D), a_ln_g_f[ia].reshape(1, D), a_ln_b_f[ia].reshape(1, D))
                y = matmul_nn(sw, w_out[ia], a_b_out_f[ia].reshape(1, D))
                rec.update(kind='conv', idx=ia, coef=1.0, pre=pre, v=v, sw=sw)
                ia += 1
            elif i % 3 == 1:
                qkv = matmul_nt(h, w_qkv_t[ib])
                qg2 = jnp.tile(b_q_g[ib].reshape(1, HEAD_DIM), (1, 2))
                kg2 = jnp.tile(b_k_g[ib].reshape(1, HEAD_DIM), (1, 2))
                qkvn = qknorm_fwd(qkv, qg2, kg2)
                o, r_all = attn_fwd(qkvn)
                y = matmul_nn(o, w_o[ib])
                rec.update(kind='attn', idx=ib, coef=1.0, qkv=qkv, qkvn=qkvn, o=o, r_all=r_all, qg2=qg2, kg2=kg2)
                ib += 1
            else:
                pb, ps = p_b_f[ic].reshape(1, D), p_scale_f[ic].reshape(1, D)
                y, dpool = pool_fwd(h, p_w_f[ic], pb, ps)
                rec.update(kind='pool', idx=ic, coef=1.0, d=dpool, pb=pb, ps=ps)
                ic += 1
            rec['y'] = y
            xs = resid_fwd(xs, y, gate, rec['coef'])
            lay[sub] = rec
        saved.append(lay)

    loss_part, dx = loss_fwd_bwd(xs, loss_target[0])
    loss = lax.psum(loss_part[0, 0], ("x", "y", "c"))

    Fd = w2.shape[2]
    zeros = lambda *s: jnp.zeros(s, F32)
    g_w1t, g_w3t, g_w2 = [[None, None] for _ in range(depth)], [[None, None] for _ in range(depth)], [[None, None] for _ in range(depth)]
    g_w_in_t, g_w_out, g_w_qkv_t, g_w_o, g_p_w = [None] * n_a, [None] * n_a, [None] * n_b, [None] * n_b, [None] * n_c
    g_b_in, g_dw, g_dw_b, g_ln_g, g_ln_b, g_b_out = ([None] * n_a for _ in range(6))
    g_q_g, g_k_g, g_p_b, g_p_scale = [None] * n_b, [None] * n_b, [None] * n_c, [None] * n_c
    d_mod = [[None] * 3 for _ in range(depth)]
    d_norm_g = [[None] * 3 for _ in range(depth)]
    for i in reversed(range(depth)):
        for sub in reversed(range(3)):
            rec = saved[i][sub]
            dy, d_gate, dy_sum = gate_bwd(dx, rec['y'], rec['gate'], rec['coef'])
            if rec['kind'] == 'ffn':
                j = rec['j']
                dh, act, d1, d3 = ffn_bwd(dy, rec['h'], w1t[i, j], w3t[i, j], w2[i, j])
                g_w1t[i][j] = matmul_tn(d1, rec['h'])
                g_w3t[i][j] = matmul_tn(d3, rec['h'])
                g_w2[i][j] = matmul_tn(act, dy)
            elif rec['kind'] == 'conv':
                k = rec['idx']
                dsw = matmul_nt(dy, w_out[k])
                g_w_out[k] = matmul_tn(rec['sw'], dy)
                g_b_out[k] = dy_sum
                dpre, g_b_in[k], ddw, g_dw_b[k], g_ln_g[k], g_ln_b[k] = conv_mid_bwd(
                    rec['pre'], rec['v'], dsw, a_dw_f[k], a_ln_g_f[k].reshape(1, D), a_ln_b_f[k].reshape(1, D))
                g_dw[k] = ddw[:CONV_W]
                dh = matmul_nn(dpre, w_in_t[k])
                g_w_in_t[k] = matmul_tn(dpre, rec['h'])
            elif rec['kind'] == 'attn':
                k = rec['idx']
                do = matmul_nt(dy, w_o[k], out_dtype=BF)
                g_w_o[k] = matmul_tn(rec['o'], dy)
                dq, dk, dv = attn_bwd(rec['qkvn'], do, rec['r_all'])
                dqkv, dgains = qknorm_bwd(rec['qkv'], dq, dk, dv, rec['qg2'], rec['kg2'])
                dgains = dgains.reshape(3, N_HEADS, HEAD_DIM)
                g_q_g[k], g_k_g[k] = jnp.sum(dgains[0], axis=0), jnp.sum(dgains[1], axis=0)
                dh = matmul_nn(dqkv, w_qkv_t[k])
                g_w_qkv_t[k] = matmul_tn(dqkv, rec['h'])
            else:
                k = rec['idx']
                dh, dyy, g_p_b[k], g_p_scale[k] = pool_bwd(dy, rec['d'], p_w_f[k], rec['pb'], rec['ps'])
                full = matmul_tn(rec['d'], dyy)
                G = p_w_f.shape[1]
                Dg = D // G
                g_p_w[k] = jnp.stack([full[g * Dg:(g + 1) * Dg, g * Dg:(g + 1) * Dg] for g in range(G)])
            dx, d_shift, d_scale, d_norm_g[i][sub] = mod_bwd(rec['x'], dh, dx, rec['g'], rec['scale'])
            d_mod[i][sub] = jnp.concatenate([d_shift, d_scale, d_gate], axis=0)
    grad_x = dx[None]

    d_mod_mine = jnp.stack([jnp.stack(r) for r in d_mod])
    small_g = [d_mod_mine, jnp.stack([jnp.concatenate(r, axis=0) for r in d_norm_g]),
               jnp.stack(g_b_in), jnp.stack(g_dw), jnp.stack(g_dw_b), jnp.stack(g_ln_g), jnp.stack(g_ln_b), jnp.stack(g_b_out),
               jnp.stack(g_q_g), jnp.stack(g_k_g), jnp.stack(g_p_b), jnp.stack(g_p_scale)]
    stack2 = lambda g: jnp.stack([jnp.stack(r) for r in g])
    blocks = lambda g, ax: g.reshape(g.shape[:ax] + (N_DEV, g.shape[ax] // N_DEV) + g.shape[ax + 1:])
    big_g = [blocks(stack2(g_w1t), 2), blocks(stack2(g_w3t), 2), blocks(stack2(g_w2), 2), blocks(jnp.stack(g_w_in_t), 1),
             blocks(jnp.stack(g_w_out), 1), blocks(jnp.stack(g_w_qkv_t), 1), blocks(jnp.stack(g_w_o), 1), blocks(jnp.stack(g_p_w), 2)]
    big_ax = [2, 2, 2, 1, 1, 1, 1, 2]
    got = exchange(big_g + [_pack(small_g)], big_ax + [None])
    p_w1t, p_w3t, p_w2, p_w_in_t, p_w_out, p_w_qkv_t, p_w_o, p_p_w = got[:8]
    (dmod_all, dng_all, dbin_all, ddw_all, ddwb_all, dlng_all, dlnb_all, dbout_all, dqg_all, dkg_all, dpb_all,
     dps_all) = _unpack(got[8], [v.shape for v in small_g])
    dmod_all = dmod_all.reshape(N_DEV, depth, 9 * D)

    dmod_mine = lax.dynamic_slice_in_dim(dmod_all, me * n_mod, n_mod, axis=2)
    e_t = e_all.T
    grad_ada_w = jnp.stack([outer_rows(e_t, dmod_mine[:, i]) for i in range(depth)])
    de_part = matmul_nt(dmod_mine[:, 0], ada_w[0])
    for i in range(1, depth):
        de_part = de_part + matmul_nt(dmod_mine[:, i], ada_w[i])
    de_parts = exchange([de_part], [None])[0]
    dpre_all, grad_cond_b = cond_embed_bwd(de_parts, pre_all)
    grad_cond_w = outer_rows(c_mine.T, dpre_all)

    def mine(g_all, axis, size):
        return lax.dynamic_slice_in_dim(g_all, me * size, size, axis=axis)

    def upd(name, parts, shape2, to_out=lambda t: t, from_in=lambda t: t):
        w, m, v = (from_in(A[p + name]).reshape(shape2) for p in ('', 'm_', 'v_'))
        outs = adamw(w, parts.reshape((parts.shape[0],) + shape2), m, v)
        return [to_out(o).reshape(A[name].shape) for o in outs]

    L = depth
    res = {}
    res['cond_w'] = upd('cond_w', grad_cond_w[None], (rows, D))
    res['cond_b'] = upd('cond_b', grad_cond_b[None], (1, D))
    res['ada_w'] = upd('ada_w', grad_ada_w[None], (L * D, n_mod))
    res['ada_b'] = upd('ada_b', dmod_all, (L, 9 * D))
    res['norm_g'] = upd('norm_g', mine(dng_all, 3, D // N_DEV), (L * 3, D // N_DEV))
    fs = Fd // N_DEV
    res['ffn_w1'] = upd('ffn_w1', p_w1t, (L * 2 * fs, D), to_out=lambda t: swap(t.reshape(L, 2, fs, D)), from_in=swap)
    res['ffn_w3'] = upd('ffn_w3', p_w3t, (L * 2 * fs, D), to_out=lambda t: swap(t.reshape(L, 2, fs, D)), from_in=swap)
    res['ffn_w2'] = upd('ffn_w2', p_w2, (L * 2 * fs, D))
    ws = 2 * D // N_DEV
    res['a_w_in'] = upd('a_w_in', p_w_in_t, (n_a * ws, D), to_out=lambda t: swap(t.reshape(n_a, ws, D)), from_in=swap)
    res['a_b_in'] = upd('a_b_in', mine(dbin_all.reshape(N_DEV, n_a, 2 * D), 2, ws), (n_a, ws))
    res['a_dw'] = upd('a_dw', mine(ddw_all, 3, rows), (n_a * CONV_W, rows))
    res['a_dw_b'] = upd('a_dw_b', mine(ddwb_all.reshape(N_DEV, n_a, D), 2, rows), (n_a, rows))
    res['a_ln_g'] = upd('a_ln_g', mine(dlng_all.reshape(N_DEV, n_a, D), 2, rows), (n_a, rows))
    res['a_ln_b'] = upd('a_ln_b', mine(dlnb_all.reshape(N_DEV, n_a, D), 2, rows), (n_a, rows))
    res['a_w_out'] = upd('a_w_out', p_w_out, (n_a * rows, D))
    res['a_b_out'] = upd('a_b_out', mine(dbout_all.reshape(N_DEV, n_a, D), 2, rows), (n_a, rows))
    qs = 3 * D // N_DEV
    res['b_w_qkv'] = upd('b_w_qkv', p_w_qkv_t, (n_b * qs, D), to_out=lambda t: swap(t.reshape(n_b, qs, D)), from_in=swap)
    res['b_q_g'] = upd('b_q_g', dqg_all, (n_b, HEAD_DIM))
    res['b_k_g'] = upd('b_k_g', dkg_all, (n_b, HEAD_DIM))
    res['b_w_o'] = upd('b_w_o', p_w_o, (n_b * rows, D))
    G = p_w.shape[1]
    Dg = D // G
    res['p_w'] = upd('p_w', p_p_w, (n_c * G * Dg // N_DEV, Dg))
    res['p_b'] = upd('p_b', mine(dpb_all.reshape(N_DEV, n_c, G, Dg), 3, Dg // N_DEV), (n_c * G, Dg // N_DEV))
    res['p_scale'] = upd('p_scale', mine(dps_all.reshape(N_DEV, n_c, D), 2, rows), (n_c, rows))

    outs = [loss, grad_x]
    for k in range(4):
        outs += [res[n][k] for n in WEIGHTS]
    return tuple(outs)
```

```python
import math

import jax
import jax.numpy as jnp
from jax import lax
from jax.experimental import pallas as pl
from jax.experimental.pallas import tpu as pltpu

F32 = jnp.float32
BF = jnp.bfloat16
EPS = 1e-6
N_DEV = 8
N_HEADS = 16
HEAD_DIM = 64
LANES = 128
CONV_W = 31
CONV_PAD = 32
POOL_WINDOWS = (2, 4, 8, 16)
POOL_PAD = 16
VMEM_LIMIT = 56 * 1024 * 1024
ADAM_LR, ADAM_B1, ADAM_B2, ADAM_EPS, ADAM_WD, ADAM_STEP = 0.001, 0.9, 0.999, 1e-08, 0.01, 10
MESH = pl.DeviceIdType.MESH

WEIGHTS = ['cond_w', 'cond_b', 'ada_w', 'ada_b', 'norm_g', 'ffn_w1', 'ffn_w3', 'ffn_w2', 'a_w_in', 'a_b_in',
           'a_dw', 'a_dw_b', 'a_ln_g', 'a_ln_b', 'a_w_out', 'a_b_out', 'b_w_qkv', 'b_q_g', 'b_k_g', 'b_w_o',
           'p_w', 'p_b', 'p_scale']
ARGS = ['x', 'c'] + WEIGHTS + ['loss_target'] + ['m_' + n for n in WEIGHTS] + ['v_' + n for n in WEIGHTS]


def _pcall(body, **kw):
    return pl.pallas_call(body, **kw)


def _cp(*sem):
    return pltpu.CompilerParams(dimension_semantics=sem if sem else None, vmem_limit_bytes=VMEM_LIMIT)


def _nn(a, b):
    return lax.dot_general(a, b, (((1,), (0,)), ((), ())), preferred_element_type=F32)


def _nt(a, b):
    return lax.dot_general(a, b, (((1,), (1,)), ((), ())), preferred_element_type=F32)


def _tn(a, b):
    return lax.dot_general(a, b, (((0,), (0,)), ((), ())), preferred_element_type=F32)


def _sig(z):
    return 1.0 / (1.0 + jnp.exp(-z))


def _tile(n, pref, unit):
    t = (min(n, pref) // unit) * unit
    while t >= unit:
        if n % t == 0:
            return t
        t -= unit
    return n


def _colsum(v):
    return jnp.sum(v, axis=0, keepdims=True)


def _acc(ref, first, val):
    @pl.when(first)
    def _():
        ref[...] = val

    @pl.when(jnp.logical_not(first))
    def _():
        ref[...] += val


def _sds(shape, dt):
    return jax.ShapeDtypeStruct(shape, dt)


def mod_fwd(x, g, scale, shift):
    S, D = x.shape
    tm = _tile(S, 512, 8)

    def body(x_ref, g_ref, sc_ref, sh_ref, h_ref):
        xv = x_ref[...]
        r = lax.rsqrt(jnp.mean(xv * xv, axis=-1, keepdims=True) + EPS)
        h_ref[...] = (((xv * r) * g_ref[...]) * (1.0 + sc_ref[...]) + sh_ref[...]).astype(h_ref.dtype)

    row = pl.BlockSpec((tm, D), lambda i: (i, 0))
    vec = pl.BlockSpec((1, D), lambda i: (0, 0))
    return _pcall(body, name="mod_fwd", grid=(S // tm,), in_specs=[row, vec, vec, vec], out_specs=row,
                  out_shape=_sds((S, D), BF), compiler_params=_cp("parallel"))(x, g, scale, shift)


def resid_fwd(x, y, gate, coef):
    S, D = x.shape
    tm = _tile(S, 512, 8)

    def body(x_ref, y_ref, g_ref, o_ref):
        o_ref[...] = x_ref[...] + (coef * g_ref[...]) * y_ref[...]

    row = pl.BlockSpec((tm, D), lambda i: (i, 0))
    vec = pl.BlockSpec((1, D), lambda i: (0, 0))
    return _pcall(body, name="resid_fwd", grid=(S // tm,), in_specs=[row, row, vec], out_specs=row,
                  out_shape=_sds((S, D), F32), compiler_params=_cp("parallel"))(x, y, gate)


def gate_bwd(dxo, y, gate, coef):
    S, D = dxo.shape
    tm = _tile(S, 512, 8)

    def body(d_ref, y_ref, g_ref, dy_ref, dg_ref, ds_ref):
        first = pl.program_id(0) == 0
        d = d_ref[...]
        dy = (coef * g_ref[...]) * d
        dy_ref[...] = dy.astype(dy_ref.dtype)
        _acc(dg_ref, first, _colsum(coef * d * y_ref[...]))
        _acc(ds_ref, first, _colsum(dy))

    row = pl.BlockSpec((tm, D), lambda i: (i, 0))
    vec = pl.BlockSpec((1, D), lambda i: (0, 0))
    return _pcall(body, name="gate_bwd", grid=(S // tm,), in_specs=[row, row, vec], out_specs=[row, vec, vec],
                  out_shape=[_sds((S, D), BF), _sds((1, D), F32), _sds((1, D), F32)],
                  compiler_params=_cp("arbitrary"))(dxo, y, gate)


def mod_bwd(x, dh, dxo, g, scale):
    S, D = x.shape
    tm = _tile(S, 512, 8)

    def body(x_ref, dh_ref, do_ref, g_ref, sc_ref, dx_ref, dsh_ref, dsc_ref, dg_ref):
        first = pl.program_id(0) == 0
        xv = x_ref[...]
        dh = dh_ref[...].astype(F32)
        gv = g_ref[...]
        r = lax.rsqrt(jnp.mean(xv * xv, axis=-1, keepdims=True) + EPS)
        xh = xv * r
        dn = dh * (1.0 + sc_ref[...])
        u = dn * gv
        dx_ref[...] = do_ref[...] + r * (u - xh * jnp.mean(u * xh, axis=-1, keepdims=True))
        _acc(dsh_ref, first, _colsum(dh))
        _acc(dsc_ref, first, _colsum(dh * (xh * gv)))
        _acc(dg_ref, first, _colsum(dn * xh))

    row = pl.BlockSpec((tm, D), lambda i: (i, 0))
    vec = pl.BlockSpec((1, D), lambda i: (0, 0))
    return _pcall(body, name="mod_bwd", grid=(S // tm,), in_specs=[row, row, row, vec, vec],
                  out_specs=[row, vec, vec, vec],
                  out_shape=[_sds((S, D), F32), _sds((1, D), F32), _sds((1, D), F32), _sds((1, D), F32)],
                  compiler_params=_cp("arbitrary"))(x, dh, dxo, g, scale)


def loss_fwd_bwd(y, target):
    S, D = y.shape
    tm = _tile(S, 512, 8)

    def body(y_ref, t_ref, l_ref, d_ref):
        first = pl.program_id(0) == 0
        e = y_ref[...] - t_ref[...]
        d_ref[...] = e * (1.0 / D)
        part = 0.5 * jnp.sum(jnp.mean(e * e, axis=-1, keepdims=True), axis=0, keepdims=True)
        _acc(l_ref, first, part)

    row = pl.BlockSpec((tm, D), lambda i: (i, 0))
    return _pcall(body, name="loss", grid=(S // tm,), in_specs=[row, row],
                  out_specs=[pl.BlockSpec((1, 1), lambda i: (0, 0)), row],
                  out_shape=[_sds((1, 1), F32), _sds((S, D), F32)], compiler_params=_cp("arbitrary"))(y, target)


def matmul_nt(x, wt, bias=None, out_dtype=F32):
    S, K = x.shape
    N = wt.shape[0]
    tm, tn = _tile(S, 512, 8), _tile(N, 512, LANES)

    def body(*refs):
        x_ref, w_ref = refs[0], refs[1]
        o_ref = refs[-1]
        r = _nt(x_ref[...].astype(BF), w_ref[...].astype(BF))
        if bias is not None:
            r = r + refs[2][...]
        o_ref[...] = r.astype(o_ref.dtype)

    in_specs = [pl.BlockSpec((tm, K), lambda i, j: (i, 0)), pl.BlockSpec((tn, K), lambda i, j: (j, 0))]
    ops = [x, wt]
    if bias is not None:
        in_specs.append(pl.BlockSpec((1, tn), lambda i, j: (0, j)))
        ops.append(bias)
    return _pcall(body, name="matmul_nt", grid=(S // tm, N // tn), in_specs=in_specs,
                  out_specs=pl.BlockSpec((tm, tn), lambda i, j: (i, j)), out_shape=_sds((S, N), out_dtype),
                  compiler_params=_cp("parallel", "parallel"))(*ops)


def matmul_nn(x, w, bias=None, out_dtype=F32):
    S, K = x.shape
    N = w.shape[1]
    tm, tn = _tile(S, 512, 8), _tile(N, 512, LANES)

    def body(*refs):
        x_ref, w_ref = refs[0], refs[1]
        o_ref = refs[-1]
        r = _nn(x_ref[...].astype(BF), w_ref[...].astype(BF))
        if bias is not None:
            r = r + refs[2][...]
        o_ref[...] = r.astype(o_ref.dtype)

    in_specs = [pl.BlockSpec((tm, K), lambda i, j: (i, 0)), pl.BlockSpec((K, tn), lambda i, j: (0, j))]
    ops = [x, w]
    if bias is not None:
        in_specs.append(pl.BlockSpec((1, tn), lambda i, j: (0, j)))
        ops.append(bias)
    return _pcall(body, name="matmul_nn", grid=(S // tm, N // tn), in_specs=in_specs,
                  out_specs=pl.BlockSpec((tm, tn), lambda i, j: (i, j)), out_shape=_sds((S, N), out_dtype),
                  compiler_params=_cp("parallel", "parallel"))(*ops)


def matmul_tn(a, b):
    S, M = a.shape
    N = b.shape[1]
    bm, bn, tk = _tile(M, 1408, LANES), _tile(N, 1024, LANES), _tile(S, 512, 8)

    def body(a_ref, b_ref, o_ref):
        _acc(o_ref, pl.program_id(2) == 0, _tn(a_ref[...].astype(BF), b_ref[...].astype(BF)))

    return _pcall(body, name="matmul_tn", grid=(M // bm, N // bn, S // tk),
                  in_specs=[pl.BlockSpec((tk, bm), lambda i, j, k: (k, i)), pl.BlockSpec((tk, bn), lambda i, j, k: (k, j))],
                  out_specs=pl.BlockSpec((bm, bn), lambda i, j, k: (i, j)), out_shape=_sds((M, N), F32),
                  compiler_params=_cp("parallel", "parallel", "arbitrary"))(a, b)


def outer_rows(at, b):
    M, R = at.shape
    N = b.shape[1]
    tm = _tile(M, 256, 8)

    def body(a_ref, b_ref, o_ref):
        av, bv = a_ref[...], b_ref[...]
        acc = av[:, 0:1] * bv[0:1, :]
        for r in range(1, R):
            acc = acc + av[:, r:r + 1] * bv[r:r + 1, :]
        o_ref[...] = acc

    return _pcall(body, name="outer_rows", grid=(M // tm,),
                  in_specs=[pl.BlockSpec((tm, R), lambda i: (i, 0)), pl.BlockSpec((R, N), lambda i: (0, 0))],
                  out_specs=pl.BlockSpec((tm, N), lambda i: (i, 0)), out_shape=_sds((M, N), F32),
                  compiler_params=_cp("parallel"))(at, b)


def ffn_fwd(h, w1t, w3t, w2):
    S, D = h.shape
    Fd = w2.shape[0]
    tm, tf = _tile(S, 1024, 8), _tile(Fd, 256, LANES)

    def body(h_ref, w1_ref, w3_ref, w2_ref, y_ref, g1_ref, g3_ref):
        hv = h_ref[...]
        g1 = _nt(hv, w1_ref[...])
        g3 = _nt(hv, w3_ref[...])
        g1_ref[...] = g1.astype(g1_ref.dtype)
        g3_ref[...] = g3.astype(g3_ref.dtype)
        a = (g1 * _sig(g1)) * g3
        _acc(y_ref, pl.program_id(1) == 0, _nn(a.astype(BF), w2_ref[...]))

    row = pl.BlockSpec((tm, D), lambda i, f: (i, 0))
    wsp = pl.BlockSpec((tf, D), lambda i, f: (f, 0))
    act = pl.BlockSpec((tm, tf), lambda i, f: (i, f))
    return _pcall(body, name="ffn_fwd", grid=(S // tm, Fd // tf), in_specs=[row, wsp, wsp, wsp], out_specs=[row, act, act],
                  out_shape=[_sds((S, D), F32), _sds((S, Fd), BF), _sds((S, Fd), BF)],
                  compiler_params=_cp("parallel", "arbitrary"))(h, w1t, w3t, w2)


def ffn_bwd(dy, g1s, g3s, w1t, w3t, w2):
    S, D = dy.shape
    Fd = w2.shape[0]
    tm, tf = _tile(S, 1024, 8), _tile(Fd, 256, LANES)

    def body(dy_ref, g1_ref, g3_ref, w1_ref, w3_ref, w2_ref, dh_ref, a_ref, d1_ref, d3_ref):
        w1, w3 = w1_ref[...], w3_ref[...]
        g1 = g1_ref[...].astype(F32)
        g3 = g3_ref[...].astype(F32)
        s = _sig(g1)
        si = g1 * s
        a_ref[...] = (si * g3).astype(a_ref.dtype)
        da = _nt(dy_ref[...], w2_ref[...])
        d3 = (da * si).astype(BF)
        d1 = (da * g3 * (s * (1.0 + g1 * (1.0 - s)))).astype(BF)
        d1_ref[...] = d1
        d3_ref[...] = d3
        _acc(dh_ref, pl.program_id(1) == 0, _nn(d1, w1) + _nn(d3, w3))

    row = pl.BlockSpec((tm, D), lambda i, f: (i, 0))
    wsp = pl.BlockSpec((tf, D), lambda i, f: (f, 0))
    act = pl.BlockSpec((tm, tf), lambda i, f: (i, f))
    return _pcall(body, name="ffn_bwd", grid=(S // tm, Fd // tf), in_specs=[row, act, act, wsp, wsp, wsp],
                  out_specs=[row, act, act, act],
                  out_shape=[_sds((S, D), F32), _sds((S, Fd), BF), _sds((S, Fd), BF), _sds((S, Fd), BF)],
                  compiler_params=_cp("parallel", "arbitrary"))(dy, g1s, g3s, w1t, w3t, w2)


def _layer_norm_parts(v):
    mu = jnp.mean(v, axis=-1, keepdims=True)
    vc = v - mu
    rs = lax.rsqrt(jnp.mean(vc * vc, axis=-1, keepdims=True) + EPS)
    return vc * rs, rs


def conv_mid_fwd(pre, dw, dw_b, ln_g, ln_b):
    S, D2 = pre.shape
    D = D2 // 2
    tm = _tile(S, 256, CONV_PAD)

    def body(pre_ref, dw_ref, dwb_ref, g_ref, b_ref, v_ref, sw_ref, ubuf):
        @pl.when(pl.program_id(0) == 0)
        def _():
            ubuf[pl.ds(0, CONV_PAD), :] = jnp.zeros((CONV_PAD, D), F32)

        @pl.when(pl.program_id(0) > 0)
        def _():
            ubuf[pl.ds(0, CONV_PAD), :] = ubuf[pl.ds(tm, CONV_PAD), :]

        ubuf[pl.ds(CONV_PAD, tm), :] = pre_ref[:, pl.ds(0, D)] * _sig(pre_ref[:, pl.ds(D, D)])
        acc = jnp.zeros((tm, D), F32) + dwb_ref[...]
        for k in range(CONV_W):
            acc = acc + dw_ref[pl.ds(k, 1), :] * ubuf[pl.ds(CONV_PAD - (CONV_W - 1) + k, tm), :]
        v_ref[...] = acc
        vh, _ = _layer_norm_parts(acc)
        ln = vh * g_ref[...] + b_ref[...]
        sw_ref[...] = (ln * _sig(ln)).astype(sw_ref.dtype)

    row = pl.BlockSpec((tm, D), lambda i: (i, 0))
    vec = pl.BlockSpec((1, D), lambda i: (0, 0))
    return _pcall(body, name="conv_mid_fwd", grid=(S // tm,),
                  in_specs=[pl.BlockSpec((tm, D2), lambda i: (i, 0)), pl.BlockSpec((CONV_PAD, D), lambda i: (0, 0)), vec, vec, vec],
                  out_specs=[row, row], out_shape=[_sds((S, D), F32), _sds((S, D), BF)],
                  scratch_shapes=[pltpu.VMEM((tm + CONV_PAD, D), F32)], compiler_params=_cp("arbitrary"))(pre, dw, dw_b, ln_g, ln_b)


def conv_mid_bwd(pre, v, dsw, dw, ln_g, ln_b):
    S, D2 = pre.shape
    D = D2 // 2
    tm = _tile(S, 256, CONV_PAD)
    nt = S // tm

    def body(pre_ref, v_ref, dsw_ref, dw_ref, g_ref, b_ref, dpre_ref, dbin_ref, ddw_ref, ddwb_ref, dg_ref, db_ref, dvbuf):
        first = pl.program_id(0) == 0

        @pl.when(first)
        def _():
            dvbuf[pl.ds(tm, CONV_PAD), :] = jnp.zeros((CONV_PAD, D), F32)
            ddw_ref[...] = jnp.zeros((CONV_PAD, D), F32)

        @pl.when(jnp.logical_not(first))
        def _():
            dvbuf[pl.ds(tm, CONV_PAD), :] = dvbuf[pl.ds(0, CONV_PAD), :]

        a = pre_ref[:, pl.ds(0, D)]
        sb = _sig(pre_ref[:, pl.ds(D, D)])
        u = a * sb
        gv = g_ref[...]
        vh, rs = _layer_norm_parts(v_ref[...])
        ln = vh * gv + b_ref[...]
        sg = _sig(ln)
        dln = dsw_ref[...] * (sg * (1.0 + ln * (1.0 - sg)))
        _acc(dg_ref, first, _colsum(dln * vh))
        _acc(db_ref, first, _colsum(dln))
        dvh = dln * gv
        dv = rs * (dvh - jnp.mean(dvh, axis=-1, keepdims=True) - vh * jnp.mean(dvh * vh, axis=-1, keepdims=True))
        _acc(ddwb_ref, first, _colsum(dv))
        dvbuf[pl.ds(0, tm), :] = dv
        du = jnp.zeros((tm, D), F32)
        for k in range(CONV_W):
            sh = dvbuf[pl.ds(CONV_W - 1 - k, tm), :]
            du = du + dw_ref[pl.ds(k, 1), :] * sh
            ddw_ref[pl.ds(k, 1), :] += _colsum(u * sh)
        da = du * sb
        db = du * a * sb * (1.0 - sb)
        dpre_ref[:, pl.ds(0, D)] = da.astype(dpre_ref.dtype)
        dpre_ref[:, pl.ds(D, D)] = db.astype(dpre_ref.dtype)
        _acc(dbin_ref, first, jnp.concatenate([_colsum(da), _colsum(db)], axis=1))

    rev = lambda i: (nt - 1 - i, 0)
    row = pl.BlockSpec((tm, D), rev)
    row2 = pl.BlockSpec((tm, D2), rev)
    vec = pl.BlockSpec((1, D), lambda i: (0, 0))
    pad = pl.BlockSpec((CONV_PAD, D), lambda i: (0, 0))
    return _pcall(body, name="conv_mid_bwd", grid=(nt,), in_specs=[row2, row, row, pad, vec, vec],
                  out_specs=[row2, pl.BlockSpec((1, D2), lambda i: (0, 0)), pad, vec, vec, vec],
                  out_shape=[_sds((S, D2), BF), _sds((1, D2), F32), _sds((CONV_PAD, D), F32), _sds((1, D), F32),
                             _sds((1, D), F32), _sds((1, D), F32)],
                  scratch_shapes=[pltpu.VMEM((tm + CONV_PAD, D), F32)], compiler_params=_cp("arbitrary"))(pre, v, dsw, dw, ln_g, ln_b)


def _pool_count(i, tm, w):
    t = i * tm + lax.broadcasted_iota(jnp.int32, (tm, 1), 0)
    return jnp.minimum(t + 1, w).astype(F32)


def pool_fwd(h, p_w, p_b, p_scale):
    S, D = h.shape
    G, Dg = p_w.shape[0], p_w.shape[1]
    tm = _tile(S, 256, POOL_PAD)

    def body(h_ref, w_ref, b_ref, s_ref, y_ref, d_ref, hbuf):
        i = pl.program_id(0)

        @pl.when(i == 0)
        def _():
            hbuf[pl.ds(0, POOL_PAD), :] = jnp.zeros((POOL_PAD, D), F32)

        @pl.when(i > 0)
        def _():
            hbuf[pl.ds(0, POOL_PAD), :] = hbuf[pl.ds(tm, POOL_PAD), :]

        hbuf[pl.ds(POOL_PAD, tm), :] = h_ref[...].astype(F32)
        for g, w in enumerate(POOL_WINDOWS):
            cols = pl.ds(g * Dg, Dg)
            hg = hbuf[pl.ds(POOL_PAD, tm), cols]
            win = hg
            for j in range(1, w):
                win = win + hbuf[pl.ds(POOL_PAD - j, tm), cols]
            d = (win / _pool_count(i, tm, w) - hg).astype(BF)
            d_ref[:, cols] = d
            y_ref[:, cols] = (_nn(d, w_ref[g].astype(BF)) + b_ref[:, cols]) * s_ref[:, cols]

    row = pl.BlockSpec((tm, D), lambda i: (i, 0))
    vec = pl.BlockSpec((1, D), lambda i: (0, 0))
    return _pcall(body, name="pool_fwd", grid=(S // tm,),
                  in_specs=[row, pl.BlockSpec((G, Dg, Dg), lambda i: (0, 0, 0)), vec, vec], out_specs=[row, row],
                  out_shape=[_sds((S, D), F32), _sds((S, D), BF)],
                  scratch_shapes=[pltpu.VMEM((tm + POOL_PAD, D), F32)], compiler_params=_cp("arbitrary"))(h, p_w, p_b, p_scale)


def pool_bwd(dy, d, p_w, p_b, p_scale):
    S, D = dy.shape
    G, Dg = p_w.shape[0], p_w.shape[1]
    tm = _tile(S, 256, POOL_PAD)
    nt = S // tm

    def body(dy_ref, d_ref, w_ref, b_ref, s_ref, dh_ref, dyy_ref, dpb_ref, dps_ref, ebuf):
        j = pl.program_id(0)
        i = nt - 1 - j
        first = j == 0

        @pl.when(first)
        def _():
            ebuf[pl.ds(tm, POOL_PAD), :] = jnp.zeros((POOL_PAD, D), F32)

        @pl.when(jnp.logical_not(first))
        def _():
            ebuf[pl.ds(tm, POOL_PAD), :] = ebuf[pl.ds(0, POOL_PAD), :]

        dpb, dps = [], []
        for g, w in enumerate(POOL_WINDOWS):
            cols = pl.ds(g * Dg, Dg)
            wg = w_ref[g].astype(BF)
            dyg = dy_ref[:, cols].astype(F32)
            ypre = _nn(d_ref[:, cols], wg) + b_ref[:, cols]
            dps.append(_colsum(dyg * ypre))
            dyy = dyg * s_ref[:, cols]
            dpb.append(_colsum(dyy))
            dyy = dyy.astype(BF)
            dyy_ref[:, cols] = dyy
            dd = _nt(dyy, wg)
            ebuf[pl.ds(0, tm), cols] = dd / _pool_count(i, tm, w)
            acc = -dd
            for k in range(w):
                acc = acc + ebuf[pl.ds(k, tm), cols]
            dh_ref[:, cols] = acc
        _acc(dpb_ref, first, jnp.concatenate(dpb, axis=1))
        _acc(dps_ref, first, jnp.concatenate(dps, axis=1))

    row = pl.BlockSpec((tm, D), lambda i: (nt - 1 - i, 0))
    vec = pl.BlockSpec((1, D), lambda i: (0, 0))
    return _pcall(body, name="pool_bwd", grid=(nt,),
                  in_specs=[row, row, pl.BlockSpec((G, Dg, Dg), lambda i: (0, 0, 0)), vec, vec],
                  out_specs=[row, row, vec, vec],
                  out_shape=[_sds((S, D), F32), _sds((S, D), BF), _sds((1, D), F32), _sds((1, D), F32)],
                  scratch_shapes=[pltpu.VMEM((tm + POOL_PAD, D), F32)], compiler_params=_cp("arbitrary"))(dy, d, p_w, p_b, p_scale)


def _group_sum(v, low):
    s_lo = jnp.sum(jnp.where(low, v, 0.0), axis=-1, keepdims=True)
    s_hi = jnp.sum(jnp.where(low, 0.0, v), axis=-1, keepdims=True)
    return jnp.where(low, s_lo, s_hi)


def qknorm_fwd(qkv, q_g2, k_g2):
    S, D3 = qkv.shape
    nb = D3 // LANES // 3
    tm = _tile(S, 512, 8)

    def body(x_ref, qg_ref, kg_ref, o_ref):
        j = pl.program_id(1)
        xv = x_ref[...]
        low = lax.broadcasted_iota(jnp.int32, (1, LANES), 1) < HEAD_DIM
        r = lax.rsqrt(_group_sum(xv * xv, low) * (1.0 / HEAD_DIM) + EPS)
        xn = xv * r
        qn = (xn * qg_ref[...]) * Q_SCALE
        kn = xn * kg_ref[...]
        o_ref[...] = jnp.where(j < nb, qn, jnp.where(j < 2 * nb, kn, xv)).astype(o_ref.dtype)

    blk = pl.BlockSpec((tm, LANES), lambda i, j: (i, j))
    vec = pl.BlockSpec((1, LANES), lambda i, j: (0, 0))
    return _pcall(body, name="qknorm_fwd", grid=(S // tm, D3 // LANES), in_specs=[blk, vec, vec], out_specs=blk,
                  out_shape=_sds((S, D3), BF), compiler_params=_cp("parallel", "parallel"))(qkv, q_g2, k_g2)


def qknorm_bwd(qkv, dq, dk, dv, q_g2, k_g2):
    S, D3 = qkv.shape
    nb = D3 // LANES // 3
    tm = _tile(S, 512, 8)

    def body(x_ref, dq_ref, dk_ref, dv_ref, qg_ref, kg_ref, o_ref, dg_ref):
        j = pl.program_id(0)
        is_q, is_k = j < nb, jnp.logical_and(j >= nb, j < 2 * nb)
        xv = x_ref[...]
        low = lax.broadcasted_iota(jnp.int32, (1, LANES), 1) < HEAD_DIM
        r = lax.rsqrt(_group_sum(xv * xv, low) * (1.0 / HEAD_DIM) + EPS)
        xh = xv * r
        dn = jnp.where(is_q, dq_ref[...] * (Q_SCALE * LN2), dk_ref[...] * LN2)
        gv = jnp.where(is_q, qg_ref[...], kg_ref[...])
        u = dn * gv
        dx = r * (u - xh * (_group_sum(u * xh, low) * (1.0 / HEAD_DIM)))
        o_ref[...] = jnp.where(jnp.logical_or(is_q, is_k), dx, dv_ref[...]).astype(o_ref.dtype)
        _acc(dg_ref, pl.program_id(1) == 0, _colsum(dn * xh))

    blk = pl.BlockSpec((tm, LANES), lambda j, i: (i, j))
    part = lambda lo: pl.BlockSpec((tm, LANES), lambda j, i: (i, jnp.clip(j - lo, 0, nb - 1)))
    vec = pl.BlockSpec((1, LANES), lambda j, i: (0, 0))
    return _pcall(body, name="qknorm_bwd", grid=(D3 // LANES, S // tm),
                  in_specs=[blk, part(0), part(nb), part(2 * nb), vec, vec],
                  out_specs=[blk, pl.BlockSpec((1, LANES), lambda j, i: (0, j))],
                  out_shape=[_sds((S, D3), BF), _sds((1, D3), F32)],
                  compiler_params=_cp("parallel", "arbitrary"))(qkv, dq, dk, dv, q_g2, k_g2)


LOG2E = math.log2(math.e)
LN2 = math.log(2.0)
Q_SCALE = HEAD_DIM ** -0.5 * LOG2E


def _softplus2(z2):
    return jnp.maximum(z2, 0.0) + jnp.log(1.0 + jnp.exp2(-jnp.abs(z2))) * LOG2E


def _attn_tile(S):
    return _tile(S, 256, LANES) if S >= 1024 else LANES


def attn_fwd(qkvn):
    S, D3 = qkvn.shape
    D = D3 // 3
    nb = D // LANES
    t = _attn_tile(S)
    nq = S // t
    assert nq <= LANES

    def body(q_ref, k_ref, v_ref, o_ref, r_ref, acc_ref, rall_ref, behind_ref, z_ref):
        qi = pl.program_id(1)
        lane = lax.broadcasted_iota(jnp.int32, (1, LANES), 1)
        rows = lax.broadcasted_iota(jnp.int32, (t, t), 0)
        cols = lax.broadcasted_iota(jnp.int32, (t, t), 1)
        later = (rows > cols).astype(BF)
        q = q_ref[...]
        qh = [jnp.where(lane < HEAD_DIM, q, jnp.zeros_like(q)), jnp.where(lane >= HEAD_DIM, q, jnp.zeros_like(q))]
        acc_ref[...] = jnp.zeros((2, t, LANES), F32)
        rall_ref[...] = jnp.zeros((2, t, LANES), F32)
        behind_ref[...] = jnp.zeros((2, t, LANES), F32)

        def scores(b):
            kb = k_ref[pl.ds(pl.multiple_of(b * t, t), t), :]
            return [_nt(qh[0], kb), _nt(qh[1], kb)]

        def block(b, diagonal):
            vb = v_ref[pl.ds(pl.multiple_of(b * t, t), t), :]
            lsig, sp, within = [None, None], [None, None], [None, None]
            for hh in range(2):
                z = z_ref[hh]
                sp_full = _softplus2(z)
                lsig[hh] = z - sp_full
                sp[hh] = jnp.where(cols < rows, sp_full, 0.0) if diagonal else sp_full
                within[hh] = _nn(sp[hh].astype(BF), later)
            z_next = scores(jnp.maximum(b - 1, 0))
            for hh in range(2):
                behind = behind_ref[hh]
                a = jnp.exp2(lsig[hh] - (within[hh] + jnp.tile(behind, (1, t // LANES))))
                if diagonal:
                    a = jnp.where(cols < rows, a, 0.0)
                acc_ref[hh] += _nn(a.astype(BF), vb)
                rall_ref[hh] = jnp.where(lane == b, behind, rall_ref[hh])
                behind_ref[hh] = behind + jnp.sum(sp[hh], axis=-1, keepdims=True)
            for hh in range(2):
                z_ref[hh] = z_next[hh]

        z_first = scores(qi)
        for hh in range(2):
            z_ref[hh] = z_first[hh]
        block(qi, True)

        def step(n, carry):
            block(qi - 1 - n, False)
            return carry

        lax.fori_loop(0, qi, step, 0)
        r_ref[...] = rall_ref[...]
        o_ref[...] = jnp.where(lane < HEAD_DIM, acc_ref[0], acc_ref[1]).astype(o_ref.dtype)

    return _pcall(body, name="attn_fwd", grid=(nb, nq),
                  in_specs=[pl.BlockSpec((t, LANES), lambda h, i: (i, h)),
                            pl.BlockSpec((S, LANES), lambda h, i: (0, nb + h)),
                            pl.BlockSpec((S, LANES), lambda h, i: (0, 2 * nb + h))],
                  out_specs=[pl.BlockSpec((t, LANES), lambda h, i: (i, h)), pl.BlockSpec((2, t, LANES), lambda h, i: (h, i, 0))],
                  out_shape=[_sds((S, D), BF), _sds((2 * nb, S, LANES), F32)],
                  scratch_shapes=[pltpu.VMEM((2, t, LANES), F32)] * 3 + [pltpu.VMEM((2, t, t), F32)],
                  compiler_params=_cp("parallel", "arbitrary"))(qkvn, qkvn, qkvn)


def attn_bwd(qkvn, do, r_all):
    S, D3 = qkvn.shape
    D = D3 // 3
    nb = D // LANES
    t = _attn_tile(S)
    nq = S // t

    def body(q_ref, k_ref, v_ref, do_ref, r_ref, dq_ref, dk_hbm, dv_hbm, dk_acc, dv_acc, dq_acc, before_ref, z_ref, da_ref, sem):
        hp = pl.program_id(0)
        qi = pl.program_id(1)

        @pl.when(qi == 0)
        def _():
            dk_acc[...] = jnp.zeros((S, LANES), F32)
            dv_acc[...] = jnp.zeros((S, LANES), F32)

        lane = lax.broadcasted_iota(jnp.int32, (1, LANES), 1)
        rows = lax.broadcasted_iota(jnp.int32, (t, t), 0)
        cols = lax.broadcasted_iota(jnp.int32, (t, t), 1)
        later = (rows > cols).astype(BF)
        earlier = (rows < cols).astype(BF)
        q = q_ref[...]
        dov = do_ref[...]
        low, high = lane < HEAD_DIM, lane >= HEAD_DIM
        qh = [jnp.where(low, q, jnp.zeros_like(q)), jnp.where(high, q, jnp.zeros_like(q))]
        doh = [jnp.where(low, dov, jnp.zeros_like(dov)), jnp.where(high, dov, jnp.zeros_like(dov))]
        q_both = jnp.concatenate(qh, axis=0)
        do_both = jnp.concatenate(doh, axis=0)
        dq_acc[...] = jnp.zeros((2, t, LANES), F32)
        before_ref[...] = jnp.zeros((2, t, LANES), F32)

        def scores(b):
            off = pl.multiple_of(b * t, t)
            kb = k_ref[pl.ds(off, t), :]
            vb = v_ref[pl.ds(off, t), :]
            return [_nt(qh[0], kb), _nt(qh[1], kb)], [_nt(doh[0], vb), _nt(doh[1], vb)]

        def block(b, diagonal):
            off = pl.multiple_of(b * t, t)
            kb = k_ref[pl.ds(off, t), :]
            lsig, sigma, within, g, earlier_g, dz, prob = ([None, None] for _ in range(7))
            for hh in range(2):
                z = z_ref[hh]
                sp = _softplus2(z)
                lsig[hh] = z - sp
                if diagonal:
                    sp = jnp.where(cols < rows, sp, 0.0)
                within[hh] = _nn(sp.astype(BF), later)
            z_next, da_next = scores(jnp.minimum(b + 1, qi))
            for hh in range(2):
                behind = jnp.sum(jnp.where(lane == b, r_ref[hh], 0.0), axis=-1, keepdims=True)
                a = jnp.exp2(lsig[hh] - (within[hh] + behind))
                if diagonal:
                    a = jnp.where(cols < rows, a, 0.0)
                prob[hh] = a.astype(BF)
                g[hh] = a * da_ref[hh]
                earlier_g[hh] = _nn(g[hh].astype(BF), earlier)
            for hh in range(2):
                before = before_ref[hh]
                d = g[hh] - jnp.exp2(lsig[hh]) * (g[hh] + (earlier_g[hh] + jnp.tile(before, (1, t // LANES))))
                if diagonal:
                    d = jnp.where(cols < rows, d, 0.0)
                dz[hh] = d.astype(BF)
                dq_acc[hh] += _nn(dz[hh], kb)
                before_ref[hh] = before + jnp.sum(g[hh], axis=-1, keepdims=True)
            dk_acc[pl.ds(off, t), :] += _tn(jnp.concatenate(dz, axis=0), q_both)
            dv_acc[pl.ds(off, t), :] += _tn(jnp.concatenate(prob, axis=0), do_both)
            for hh in range(2):
                z_ref[hh] = z_next[hh]
                da_ref[hh] = da_next[hh]

        z_first, da_first = scores(0)
        for hh in range(2):
            z_ref[hh] = z_first[hh]
            da_ref[hh] = da_first[hh]

        def step(b, carry):
            block(b, False)
            return carry

        lax.fori_loop(0, qi, step, 0)
        block(qi, True)
        dq_ref[...] = jnp.where(low, dq_acc[0], dq_acc[1])

        @pl.when(qi == nq - 1)
        def _():
            col = pl.multiple_of(hp * LANES, LANES)
            c1 = pltpu.make_async_copy(dk_acc, dk_hbm.at[:, pl.ds(col, LANES)], sem.at[0])
            c2 = pltpu.make_async_copy(dv_acc, dv_hbm.at[:, pl.ds(col, LANES)], sem.at[1])
            c1.start()
            c2.start()
            c1.wait()
            c2.wait()

    blk = pl.BlockSpec((t, LANES), lambda h, i: (i, h))
    hbm = pl.BlockSpec(memory_space=pl.ANY)
    return _pcall(body, name="attn_bwd", grid=(nb, nq),
                  in_specs=[blk, pl.BlockSpec((S, LANES), lambda h, i: (0, nb + h)),
                            pl.BlockSpec((S, LANES), lambda h, i: (0, 2 * nb + h)), blk,
                            pl.BlockSpec((2, t, LANES), lambda h, i: (h, i, 0))],
                  out_specs=[blk, hbm, hbm],
                  out_shape=[_sds((S, D), F32), _sds((S, D), F32), _sds((S, D), F32)],
                  scratch_shapes=[pltpu.VMEM((S, LANES), F32), pltpu.VMEM((S, LANES), F32), pltpu.VMEM((2, t, LANES), F32),
                                  pltpu.VMEM((2, t, LANES), F32), pltpu.VMEM((2, t, t), F32), pltpu.VMEM((2, t, t), F32),
                                  pltpu.SemaphoreType.DMA((2,))],
                  compiler_params=_cp("arbitrary", "arbitrary"))(qkvn, qkvn, qkvn, do, r_all)


def cond_embed(parts, cond_b):
    P, B, D = parts.shape

    def body(p_ref, b_ref, pre_ref, e_ref):
        pre = p_ref[0]
        for s in range(1, P):
            pre = pre + p_ref[s]
        pre = pre + b_ref[...]
        pre_ref[...] = pre
        e_ref[...] = pre * _sig(pre)

    return _pcall(body, name="cond_embed", out_shape=[_sds((B, D), F32), _sds((B, D), F32)],
                  compiler_params=_cp())(parts, cond_b)


def cond_embed_bwd(parts, pre):
    P, B, D = parts.shape

    def body(p_ref, pre_ref, dpre_ref, db_ref):
        de = p_ref[0]
        for s in range(1, P):
            de = de + p_ref[s]
        pre = pre_ref[...]
        s = _sig(pre)
        dpre = de * (s * (1.0 + pre * (1.0 - s)))
        dpre_ref[...] = dpre
        db_ref[...] = _colsum(dpre)

    return _pcall(body, name="cond_embed_bwd", out_shape=[_sds((B, D), F32), _sds((1, D), F32)],
                  compiler_params=_cp())(parts, pre)


def adamw(w, parts, m, v):
    R, C = w.shape
    P = parts.shape[0]
    tr = _tile(R, max(8, (1 << 17) // C // 8 * 8), 8)
    c1 = 1.0 - ADAM_B1 ** ADAM_STEP
    c2 = 1.0 - ADAM_B2 ** ADAM_STEP

    def body(w_ref, p_ref, m_ref, v_ref, g_ref, d_ref, nm_ref, nv_ref):
        g = p_ref[0].astype(F32)
        for s in range(1, P):
            g = g + p_ref[s].astype(F32)
        nm = ADAM_B1 * m_ref[...] + (1.0 - ADAM_B1) * g
        nv = ADAM_B2 * v_ref[...] + (1.0 - ADAM_B2) * (g * g)
        g_ref[...] = g
        nm_ref[...] = nm
        nv_ref[...] = nv
        d_ref[...] = -ADAM_LR * ((nm / c1) / (jnp.sqrt(nv / c2) + ADAM_EPS) + ADAM_WD * w_ref[...])

    blk = pl.BlockSpec((tr, C), lambda i: (i, 0))
    return _pcall(body, name="adamw", grid=(R // tr,), in_specs=[blk, pl.BlockSpec((P, tr, C), lambda i: (0, i, 0)), blk, blk],
                  out_specs=[blk] * 4, out_shape=[_sds((R, C), F32)] * 4, compiler_params=_cp("parallel"))(w, parts, m, v)


def exchange(arrs, peer_axes):
    n = len(arrs)
    shapes = [a.shape if ax is None else a.shape[:ax] + a.shape[ax + 1:] for a, ax in zip(arrs, peer_axes)]

    def body(*refs):
        ins, outs = refs[:n], refs[n:2 * n]
        send, recv, local = refs[2 * n:]
        x, y, c = lax.axis_index("x"), lax.axis_index("y"), lax.axis_index("c")
        me = 4 * x + 2 * y + c

        def piece(i, d):
            if peer_axes[i] is None:
                return ins[i]
            return ins[i].at[(slice(None),) * peer_axes[i] + (d,)]

        mine = [pltpu.make_async_copy(piece(i, me), outs[i].at[me], local.at[i]) for i in range(n)]
        for cp in mine:
            cp.start()
        sends, recvs = [], []
        for p in range(1, N_DEV):
            peer = (1 - x if p & 4 else x, 1 - y if p & 2 else y, 1 - c if p & 1 else c)
            them = 4 * peer[0] + 2 * peer[1] + peer[2]
            for i in range(n):
                sends.append(pltpu.make_async_remote_copy(piece(i, them), outs[i].at[me], send.at[i, p - 1], recv.at[i, p - 1],
                                                          device_id=peer, device_id_type=MESH))
                recvs.append(pltpu.make_async_remote_copy(piece(i, me), outs[i].at[them], send.at[i, p - 1], recv.at[i, p - 1],
                                                          device_id=peer, device_id_type=MESH))
        for cp in sends:
            cp.start()
        for cp in recvs:
            cp.wait_recv()
        for cp in sends:
            cp.wait_send()
        for cp in mine:
            cp.wait()

    hbm = pl.BlockSpec(memory_space=pl.ANY)
    return _pcall(body, name="exchange", in_specs=[hbm] * n, out_specs=[hbm] * n,
                  out_shape=[_sds((N_DEV,) + s, a.dtype) for s, a in zip(shapes, arrs)],
                  scratch_shapes=[pltpu.SemaphoreType.DMA((n, N_DEV - 1)), pltpu.SemaphoreType.DMA((n, N_DEV - 1)),
                                  pltpu.SemaphoreType.DMA((n,))])(*arrs)


def _pack(vs):
    flat = jnp.concatenate([v.reshape(-1).astype(F32) for v in vs])
    pad = (-flat.shape[0]) % 1024
    return jnp.pad(flat, (0, pad)).reshape(-1, 1024)


def _unpack(packed, shapes):
    flat = packed.reshape(packed.shape[0], -1)
    out, o = [], 0
    for s in shapes:
        n = math.prod(s)
        out.append(flat[:, o:o + n].reshape((packed.shape[0],) + tuple(s)))
        o += n
    return out


def _cat_dev(g, axis):
    g = jnp.moveaxis(g, 0, axis)
    return g.reshape(g.shape[:axis] + (g.shape[axis] * g.shape[axis + 1],) + g.shape[axis + 2:])


def kernel(x, c, cond_w, cond_b, ada_w, ada_b, norm_g, ffn_w1, ffn_w3, ffn_w2, a_w_in, a_b_in, a_dw, a_dw_b, a_ln_g, a_ln_b, a_w_out, a_b_out, b_w_qkv, b_q_g, b_k_g, b_w_o, p_w, p_b, p_scale, loss_target, m_cond_w, m_cond_b, m_ada_w, m_ada_b, m_norm_g, m_ffn_w1, m_ffn_w3, m_ffn_w2, m_a_w_in, m_a_b_in, m_a_dw, m_a_dw_b, m_a_ln_g, m_a_ln_b, m_a_w_out, m_a_b_out, m_b_w_qkv, m_b_q_g, m_b_k_g, m_b_w_o, m_p_w, m_p_b, m_p_scale, v_cond_w, v_cond_b, v_ada_w, v_ada_b, v_norm_g, v_ffn_w1, v_ffn_w3, v_ffn_w2, v_a_w_in, v_a_b_in, v_a_dw, v_a_dw_b, v_a_ln_g, v_a_ln_b, v_a_w_out, v_a_b_out, v_b_w_qkv, v_b_q_g, v_b_k_g, v_b_w_o, v_p_w, v_p_b, v_p_scale):
    A = dict(zip(ARGS, (x, c, cond_w, cond_b, ada_w, ada_b, norm_g, ffn_w1, ffn_w3, ffn_w2, a_w_in, a_b_in, a_dw, a_dw_b, a_ln_g, a_ln_b, a_w_out, a_b_out, b_w_qkv, b_q_g, b_k_g, b_w_o, p_w, p_b, p_scale, loss_target, m_cond_w, m_cond_b, m_ada_w, m_ada_b, m_norm_g, m_ffn_w1, m_ffn_w3, m_ffn_w2, m_a_w_in, m_a_b_in, m_a_dw, m_a_dw_b, m_a_ln_g, m_a_ln_b, m_a_w_out, m_a_b_out, m_b_w_qkv, m_b_q_g, m_b_k_g, m_b_w_o, m_p_w, m_p_b, m_p_scale, v_cond_w, v_cond_b, v_ada_w, v_ada_b, v_norm_g, v_ffn_w1, v_ffn_w3, v_ffn_w2, v_a_w_in, v_a_b_in, v_a_dw, v_a_dw_b, v_a_ln_g, v_a_ln_b, v_a_w_out, v_a_b_out, v_b_w_qkv, v_b_q_g, v_b_k_g, v_b_w_o, v_p_w, v_p_b, v_p_scale)))
    S, D = x.shape[1], x.shape[2]
    depth = ada_w.shape[0]
    n_a, n_b, n_c = a_w_in.shape[0], b_w_qkv.shape[0], p_w.shape[0]
    me = 4 * lax.axis_index("x") + 2 * lax.axis_index("y") + lax.axis_index("c")
    swap = lambda w: jnp.swapaxes(w, -1, -2)

    small_names = ['norm_g', 'a_b_in', 'a_dw', 'a_dw_b', 'a_ln_g', 'a_ln_b', 'a_b_out', 'p_w', 'p_b', 'p_scale']
    small_in = [c] + [A[n] for n in small_names]
    big_in = [swap(ffn_w1).astype(BF), swap(ffn_w3).astype(BF), ffn_w2.astype(BF), swap(a_w_in).astype(BF),
              a_w_out.astype(BF), swap(b_w_qkv).astype(BF), b_w_o.astype(BF)]
    got = exchange(big_in + [_pack(small_in)], [None] * (len(big_in) + 1))
    w1t, w3t, w2 = _cat_dev(got[0], 2), _cat_dev(got[1], 2), _cat_dev(got[2], 2)
    w_in_t, w_out = _cat_dev(got[3], 1), _cat_dev(got[4], 1)
    w_qkv_t, w_o = _cat_dev(got[5], 1), _cat_dev(got[6], 1)
    sm = dict(zip(['c'] + small_names, _unpack(got[7], [v.shape for v in small_in])))
    c_all = sm['c'][:, 0]
    norm_g_f = _cat_dev(sm['norm_g'], 2)
    a_b_in_f, a_dw_f, a_dw_b_f = _cat_dev(sm['a_b_in'], 1), _cat_dev(sm['a_dw'], 2), _cat_dev(sm['a_dw_b'], 1)
    a_ln_g_f, a_ln_b_f, a_b_out_f = _cat_dev(sm['a_ln_g'], 1), _cat_dev(sm['a_ln_b'], 1), _cat_dev(sm['a_b_out'], 1)
    p_w_f, p_b_f, p_scale_f = _cat_dev(sm['p_w'], 2), _cat_dev(sm['p_b'], 2), _cat_dev(sm['p_scale'], 1)
    a_dw_f = jnp.pad(a_dw_f, ((0, 0), (0, CONV_PAD - CONV_W), (0, 0)))

    rows = D // N_DEV
    c_mine = lax.dynamic_slice_in_dim(c_all, me * rows, rows, axis=1)
    pre_parts = exchange([matmul_nn(c_mine, cond_w)], [None])[0]
    pre_all, e_all = cond_embed(pre_parts, cond_b.reshape(1, D))
    n_mod = ada_w.shape[2]
    ada_b_mine = lax.dynamic_slice_in_dim(ada_b, me * n_mod, n_mod, axis=1)
    mod_part = jnp.stack([matmul_nn(e_all, ada_w[i], ada_b_mine[i:i + 1]) for i in range(depth)], axis=1)
    mod_all = exchange([mod_part], [None])[0]
    mod = lax.dynamic_index_in_dim(mod_all, me, axis=1, keepdims=False)
    mod = jnp.moveaxis(mod, 0, 1).reshape(depth, 3, 3, 1, D)

    xs = x[0]
    saved = []
    ia = ib = ic = 0
    for i in range(depth):
        lay = {}
        for sub in range(3):
            shift, scale, gate = mod[i, sub, 0], mod[i, sub, 1], 1.0 + mod[i, sub, 2]
            g = norm_g_f[i, sub].reshape(1, D)
            h = mod_fwd(xs, g, scale, shift)
            rec = dict(x=xs, h=h, g=g, scale=scale, gate=gate)
            if sub != 1:
                j = 0 if sub == 0 else 1
                y, g1s, g3s = ffn_fwd(h, w1t[i, j], w3t[i, j], w2[i, j])
                rec.update(kind='ffn', j=j, coef=0.5, g1=g1s, g3=g3s)
            elif i % 3 == 0:
                pre = matmul_nt(h, w_in_t[ia], a_b_in_f[ia].reshape(1, 2 * D))
                v, sw = conv_mid_fwd(pre, a_dw_f[ia], a_dw_b_f[ia].reshape(1, D), a_ln_g_f[ia].reshape(1, D), a_ln_b_f[ia].reshape(1, D))
                y = matmul_nn(sw, w_out[ia], a_b_out_f[ia].reshape(1, D))
                rec.update(kind='conv', idx=ia, coef=1.0, pre=pre, v=v, sw=sw)
                ia += 1
            elif i % 3 == 1:
                qkv = matmul_nt(h, w_qkv_t[ib])
                qg2 = jnp.tile(b_q_g[ib].reshape(1, HEAD_DIM), (1, 2))
                kg2 = jnp.tile(b_k_g[ib].reshape(1, HEAD_DIM), (1, 2))
                qkvn = qknorm_fwd(qkv, qg2, kg2)
                o, r_all = attn_fwd(qkvn)
                y = matmul_nn(o, w_o[ib])
                rec.update(kind='attn', idx=ib, coef=1.0, qkv=qkv, qkvn=qkvn, o=o, r_all=r_all, qg2=qg2, kg2=kg2)
                ib += 1
            else:
                pb, ps = p_b_f[ic].reshape(1, D), p_scale_f[ic].reshape(1, D)
                y, dpool = pool_fwd(h, p_w_f[ic], pb, ps)
                rec.update(kind='pool', idx=ic, coef=1.0, d=dpool, pb=pb, ps=ps)
                ic += 1
            rec['y'] = y
            xs = resid_fwd(xs, y, gate, rec['coef'])
            lay[sub] = rec
        saved.append(lay)

    loss_part, dx = loss_fwd_bwd(xs, loss_target[0])
    loss = lax.psum(loss_part[0, 0], ("x", "y", "c"))

    Fd = w2.shape[2]
    zeros = lambda *s: jnp.zeros(s, F32)
    g_w1t, g_w3t, g_w2 = [[None, None] for _ in range(depth)], [[None, None] for _ in range(depth)], [[None, None] for _ in range(depth)]
    g_w_in_t, g_w_out, g_w_qkv_t, g_w_o, g_p_w = [None] * n_a, [None] * n_a, [None] * n_b, [None] * n_b, [None] * n_c
    g_b_in, g_dw, g_dw_b, g_ln_g, g_ln_b, g_b_out = ([None] * n_a for _ in range(6))
    g_q_g, g_k_g, g_p_b, g_p_scale = [None] * n_b, [None] * n_b, [None] * n_c, [None] * n_c
    d_mod = [[None] * 3 for _ in range(depth)]
    d_norm_g = [[None] * 3 for _ in range(depth)]
    for i in reversed(range(depth)):
        for sub in reversed(range(3)):
            rec = saved[i][sub]
            dy, d_gate, dy_sum = gate_bwd(dx, rec['y'], rec['gate'], rec['coef'])
            if rec['kind'] == 'ffn':
                j = rec['j']
                dh, act, d1, d3 = ffn_bwd(dy, rec['g1'], rec['g3'], w1t[i, j], w3t[i, j], w2[i, j])
                g_w1t[i][j] = matmul_tn(d1, rec['h'])
                g_w3t[i][j] = matmul_tn(d3, rec['h'])
                g_w2[i][j] = matmul_tn(act, dy)
            elif rec['kind'] == 'conv':
                k = rec['idx']
                dsw = matmul_nt(dy, w_out[k])
                g_w_out[k] = matmul_tn(rec['sw'], dy)
                g_b_out[k] = dy_sum
                dpre, g_b_in[k], ddw, g_dw_b[k], g_ln_g[k], g_ln_b[k] = conv_mid_bwd(
                    rec['pre'], rec['v'], dsw, a_dw_f[k], a_ln_g_f[k].reshape(1, D), a_ln_b_f[k].reshape(1, D))
                g_dw[k] = ddw[:CONV_W]
                dh = matmul_nn(dpre, w_in_t[k])
                g_w_in_t[k] = matmul_tn(dpre, rec['h'])
            elif rec['kind'] == 'attn':
                k = rec['idx']
                do = matmul_nt(dy, w_o[k], out_dtype=BF)
                g_w_o[k] = matmul_tn(rec['o'], dy)
                dq, dk, dv = attn_bwd(rec['qkvn'], do, rec['r_all'])
                dqkv, dgains = qknorm_bwd(rec['qkv'], dq, dk, dv, rec['qg2'], rec['kg2'])
                dgains = dgains.reshape(3, N_HEADS, HEAD_DIM)
                g_q_g[k], g_k_g[k] = jnp.sum(dgains[0], axis=0), jnp.sum(dgains[1], axis=0)
                dh = matmul_nn(dqkv, w_qkv_t[k])
                g_w_qkv_t[k] = matmul_tn(dqkv, rec['h'])
            else:
                k = rec['idx']
                dh, dyy, g_p_b[k], g_p_scale[k] = pool_bwd(dy, rec['d'], p_w_f[k], rec['pb'], rec['ps'])
                full = matmul_tn(rec['d'], dyy)
                G = p_w_f.shape[1]
                Dg = D // G
                g_p_w[k] = jnp.stack([full[g * Dg:(g + 1) * Dg, g * Dg:(g + 1) * Dg] for g in range(G)])
            dx, d_shift, d_scale, d_norm_g[i][sub] = mod_bwd(rec['x'], dh, dx, rec['g'], rec['scale'])
            d_mod[i][sub] = jnp.concatenate([d_shift, d_scale, d_gate], axis=0)
    grad_x = dx[None]

    d_mod_mine = jnp.stack([jnp.stack(r) for r in d_mod])
    small_g = [d_mod_mine, jnp.stack([jnp.concatenate(r, axis=0) for r in d_norm_g]),
               jnp.stack(g_b_in), jnp.stack(g_dw), jnp.stack(g_dw_b), jnp.stack(g_ln_g), jnp.stack(g_ln_b), jnp.stack(g_b_out),
               jnp.stack(g_q_g), jnp.stack(g_k_g), jnp.stack(g_p_b), jnp.stack(g_p_scale)]
    stack2 = lambda g: jnp.stack([jnp.stack(r) for r in g])
    blocks = lambda g, ax: g.reshape(g.shape[:ax] + (N_DEV, g.shape[ax] // N_DEV) + g.shape[ax + 1:])
    big_g = [blocks(stack2(g_w1t), 2), blocks(stack2(g_w3t), 2), blocks(stack2(g_w2), 2), blocks(jnp.stack(g_w_in_t), 1),
             blocks(jnp.stack(g_w_out), 1), blocks(jnp.stack(g_w_qkv_t), 1), blocks(jnp.stack(g_w_o), 1), blocks(jnp.stack(g_p_w), 2)]
    big_ax = [2, 2, 2, 1, 1, 1, 1, 2]
    got = exchange(big_g + [_pack(small_g)], big_ax + [None])
    p_w1t, p_w3t, p_w2, p_w_in_t, p_w_out, p_w_qkv_t, p_w_o, p_p_w = got[:8]
    (dmod_all, dng_all, dbin_all, ddw_all, ddwb_all, dlng_all, dlnb_all, dbout_all, dqg_all, dkg_all, dpb_all,
     dps_all) = _unpack(got[8], [v.shape for v in small_g])
    dmod_all = dmod_all.reshape(N_DEV, depth, 9 * D)

    dmod_mine = lax.dynamic_slice_in_dim(dmod_all, me * n_mod, n_mod, axis=2)
    e_t = e_all.T
    grad_ada_w = jnp.stack([outer_rows(e_t, dmod_mine[:, i]) for i in range(depth)])
    de_part = matmul_nt(dmod_mine[:, 0], ada_w[0])
    for i in range(1, depth):
        de_part = de_part + matmul_nt(dmod_mine[:, i], ada_w[i])
    de_parts = exchange([de_part], [None])[0]
    dpre_all, grad_cond_b = cond_embed_bwd(de_parts, pre_all)
    grad_cond_w = outer_rows(c_mine.T, dpre_all)

    def mine(g_all, axis, size):
        return lax.dynamic_slice_in_dim(g_all, me * size, size, axis=axis)

    def upd(name, parts, shape2, to_out=lambda t: t, from_in=lambda t: t):
        w, m, v = (from_in(A[p + name]).reshape(shape2) for p in ('', 'm_', 'v_'))
        outs = adamw(w, parts.reshape((parts.shape[0],) + shape2), m, v)
        return [to_out(o).reshape(A[name].shape) for o in outs]

    L = depth
    res = {}
    res['cond_w'] = upd('cond_w', grad_cond_w[None], (rows, D))
    res['cond_b'] = upd('cond_b', grad_cond_b[None], (1, D))
    res['ada_w'] = upd('ada_w', grad_ada_w[None], (L * D, n_mod))
    res['ada_b'] = upd('ada_b', dmod_all, (L, 9 * D))
    res['norm_g'] = upd('norm_g', mine(dng_all, 3, D // N_DEV), (L * 3, D // N_DEV))
    fs = Fd // N_DEV
    res['ffn_w1'] = upd('ffn_w1', p_w1t, (L * 2 * fs, D), to_out=lambda t: swap(t.reshape(L, 2, fs, D)), from_in=swap)
    res['ffn_w3'] = upd('ffn_w3', p_w3t, (L * 2 * fs, D), to_out=lambda t: swap(t.reshape(L, 2, fs, D)), from_in=swap)
    res['ffn_w2'] = upd('ffn_w2', p_w2, (L * 2 * fs, D))
    ws = 2 * D // N_DEV
    res['a_w_in'] = upd('a_w_in', p_w_in_t, (n_a * ws, D), to_out=lambda t: swap(t.reshape(n_a, ws, D)), from_in=swap)
    res['a_b_in'] = upd('a_b_in', mine(dbin_all.reshape(N_DEV, n_a, 2 * D), 2, ws), (n_a, ws))
    res['a_dw'] = upd('a_dw', mine(ddw_all, 3, rows), (n_a * CONV_W, rows))
    res['a_dw_b'] = upd('a_dw_b', mine(ddwb_all.reshape(N_DEV, n_a, D), 2, rows), (n_a, rows))
    res['a_ln_g'] = upd('a_ln_g', mine(dlng_all.reshape(N_DEV, n_a, D), 2, rows), (n_a, rows))
    res['a_ln_b'] = upd('a_ln_b', mine(dlnb_all.reshape(N_DEV, n_a, D), 2, rows), (n_a, rows))
    res['a_w_out'] = upd('a_w_out', p_w_out, (n_a * rows, D))
    res['a_b_out'] = upd('a_b_out', mine(dbout_all.reshape(N_DEV, n_a, D), 2, rows), (n_a, rows))
    qs = 3 * D // N_DEV
    res['b_w_qkv'] = upd('b_w_qkv', p_w_qkv_t, (n_b * qs, D), to_out=lambda t: swap(t.reshape(n_b, qs, D)), from_in=swap)
    res['b_q_g'] = upd('b_q_g', dqg_all, (n_b, HEAD_DIM))
    res['b_k_g'] = upd('b_k_g', dkg_all, (n_b, HEAD_DIM))
    res['b_w_o'] = upd('b_w_o', p_w_o, (n_b * rows, D))
    G = p_w.shape[1]
    Dg = D // G
    res['p_w'] = upd('p_w', p_p_w, (n_c * G * Dg // N_DEV, Dg))
    res['p_b'] = upd('p_b', mine(dpb_all.reshape(N_DEV, n_c, G, Dg), 3, Dg // N_DEV), (n_c * G, Dg // N_DEV))
    res['p_scale'] = upd('p_scale', mine(dps_all.reshape(N_DEV, n_c, D), 2, rows), (n_c, rows))

    outs = [loss, grad_x]
    for k in range(4):
        outs += [res[n][k] for n in WEIGHTS]
    return tuple(outs)
```

```python
import math

import jax
import jax.numpy as jnp
from jax import lax
from jax.experimental import pallas as pl
from jax.experimental.pallas import tpu as pltpu

F32 = jnp.float32
BF = jnp.bfloat16
EPS = 1e-6
N_DEV = 8
N_HEADS = 16
HEAD_DIM = 64
LANES = 128
CONV_W = 31
CONV_PAD = 32
POOL_WINDOWS = (2, 4, 8, 16)
POOL_PAD = 16
VMEM_LIMIT = 56 * 1024 * 1024
ADAM_LR, ADAM_B1, ADAM_B2, ADAM_EPS, ADAM_WD, ADAM_STEP = 0.001, 0.9, 0.999, 1e-08, 0.01, 10
MESH = pl.DeviceIdType.MESH

WEIGHTS = ['cond_w', 'cond_b', 'ada_w', 'ada_b', 'norm_g', 'ffn_w1', 'ffn_w3', 'ffn_w2', 'a_w_in', 'a_b_in',
           'a_dw', 'a_dw_b', 'a_ln_g', 'a_ln_b', 'a_w_out', 'a_b_out', 'b_w_qkv', 'b_q_g', 'b_k_g', 'b_w_o',
           'p_w', 'p_b', 'p_scale']
ARGS = ['x', 'c'] + WEIGHTS + ['loss_target'] + ['m_' + n for n in WEIGHTS] + ['v_' + n for n in WEIGHTS]


def _pcall(body, **kw):
    return pl.pallas_call(body, **kw)


def _cp(*sem):
    return pltpu.CompilerParams(dimension_semantics=sem if sem else None, vmem_limit_bytes=VMEM_LIMIT)


def _nn(a, b):
    return lax.dot_general(a, b, (((1,), (0,)), ((), ())), preferred_element_type=F32)


def _nt(a, b):
    return lax.dot_general(a, b, (((1,), (1,)), ((), ())), preferred_element_type=F32)


def _tn(a, b):
    return lax.dot_general(a, b, (((0,), (0,)), ((), ())), preferred_element_type=F32)


def _sig(z):
    return 1.0 / (1.0 + jnp.exp(-z))


def _tile(n, pref, unit):
    t = (min(n, pref) // unit) * unit
    while t >= unit:
        if n % t == 0:
            return t
        t -= unit
    return n


def _colsum(v):
    return jnp.sum(v, axis=0, keepdims=True)


def _acc(ref, first, val):
    @pl.when(first)
    def _():
        ref[...] = val

    @pl.when(jnp.logical_not(first))
    def _():
        ref[...] += val


def _sds(shape, dt):
    return jax.ShapeDtypeStruct(shape, dt)


def mod_fwd(x, g, scale, shift):
    S, D = x.shape
    tm = _tile(S, 512, 8)

    def body(x_ref, g_ref, sc_ref, sh_ref, h_ref):
        xv = x_ref[...]
        r = lax.rsqrt(jnp.mean(xv * xv, axis=-1, keepdims=True) + EPS)
        h_ref[...] = (((xv * r) * g_ref[...]) * (1.0 + sc_ref[...]) + sh_ref[...]).astype(h_ref.dtype)

    row = pl.BlockSpec((tm, D), lambda i: (i, 0))
    vec = pl.BlockSpec((1, D), lambda i: (0, 0))
    return _pcall(body, name="mod_fwd", grid=(S // tm,), in_specs=[row, vec, vec, vec], out_specs=row,
                  out_shape=_sds((S, D), BF), compiler_params=_cp("parallel"))(x, g, scale, shift)


def resid_fwd(x, y, gate, coef):
    S, D = x.shape
    tm = _tile(S, 512, 8)

    def body(x_ref, y_ref, g_ref, o_ref):
        o_ref[...] = x_ref[...] + (coef * g_ref[...]) * y_ref[...]

    row = pl.BlockSpec((tm, D), lambda i: (i, 0))
    vec = pl.BlockSpec((1, D), lambda i: (0, 0))
    return _pcall(body, name="resid_fwd", grid=(S // tm,), in_specs=[row, row, vec], out_specs=row,
                  out_shape=_sds((S, D), F32), compiler_params=_cp("parallel"))(x, y, gate)


def gate_bwd(dxo, y, gate, coef):
    S, D = dxo.shape
    tm = _tile(S, 512, 8)

    def body(d_ref, y_ref, g_ref, dy_ref, dg_ref, ds_ref):
        first = pl.program_id(0) == 0
        d = d_ref[...]
        dy = (coef * g_ref[...]) * d
        dy_ref[...] = dy.astype(dy_ref.dtype)
        _acc(dg_ref, first, _colsum(coef * d * y_ref[...]))
        _acc(ds_ref, first, _colsum(dy))

    row = pl.BlockSpec((tm, D), lambda i: (i, 0))
    vec = pl.BlockSpec((1, D), lambda i: (0, 0))
    return _pcall(body, name="gate_bwd", grid=(S // tm,), in_specs=[row, row, vec], out_specs=[row, vec, vec],
                  out_shape=[_sds((S, D), BF), _sds((1, D), F32), _sds((1, D), F32)],
                  compiler_params=_cp("arbitrary"))(dxo, y, gate)


def mod_bwd(x, dh, dxo, g, scale):
    S, D = x.shape
    tm = _tile(S, 512, 8)

    def body(x_ref, dh_ref, do_ref, g_ref, sc_ref, dx_ref, dsh_ref, dsc_ref, dg_ref):
        first = pl.program_id(0) == 0
        xv = x_ref[...]
        dh = dh_ref[...].astype(F32)
        gv = g_ref[...]
        r = lax.rsqrt(jnp.mean(xv * xv, axis=-1, keepdims=True) + EPS)
        xh = xv * r
        dn = dh * (1.0 + sc_ref[...])
        u = dn * gv
        dx_ref[...] = do_ref[...] + r * (u - xh * jnp.mean(u * xh, axis=-1, keepdims=True))
        _acc(dsh_ref, first, _colsum(dh))
        _acc(dsc_ref, first, _colsum(dh * (xh * gv)))
        _acc(dg_ref, first, _colsum(dn * xh))

    row = pl.BlockSpec((tm, D), lambda i: (i, 0))
    vec = pl.BlockSpec((1, D), lambda i: (0, 0))
    return _pcall(body, name="mod_bwd", grid=(S // tm,), in_specs=[row, row, row, vec, vec],
                  out_specs=[row, vec, vec, vec],
                  out_shape=[_sds((S, D), F32), _sds((1, D), F32), _sds((1, D), F32), _sds((1, D), F32)],
                  compiler_params=_cp("arbitrary"))(x, dh, dxo, g, scale)


def loss_fwd_bwd(y, target):
    S, D = y.shape
    tm = _tile(S, 512, 8)

    def body(y_ref, t_ref, l_ref, d_ref):
        first = pl.program_id(0) == 0
        e = y_ref[...] - t_ref[...]
        d_ref[...] = e * (1.0 / D)
        part = 0.5 * jnp.sum(jnp.mean(e * e, axis=-1, keepdims=True), axis=0, keepdims=True)
        _acc(l_ref, first, part)

    row = pl.BlockSpec((tm, D), lambda i: (i, 0))
    return _pcall(body, name="loss", grid=(S // tm,), in_specs=[row, row],
                  out_specs=[pl.BlockSpec((1, 1), lambda i: (0, 0)), row],
                  out_shape=[_sds((1, 1), F32), _sds((S, D), F32)], compiler_params=_cp("arbitrary"))(y, target)


def matmul_nt(x, wt, bias=None, out_dtype=F32):
    S, K = x.shape
    N = wt.shape[0]
    tm, tn = _tile(S, 512, 8), _tile(N, 512, LANES)

    def body(*refs):
        x_ref, w_ref = refs[0], refs[1]
        o_ref = refs[-1]
        r = _nt(x_ref[...].astype(BF), w_ref[...].astype(BF))
        if bias is not None:
            r = r + refs[2][...]
        o_ref[...] = r.astype(o_ref.dtype)

    in_specs = [pl.BlockSpec((tm, K), lambda i, j: (i, 0)), pl.BlockSpec((tn, K), lambda i, j: (j, 0))]
    ops = [x, wt]
    if bias is not None:
        in_specs.append(pl.BlockSpec((1, tn), lambda i, j: (0, j)))
        ops.append(bias)
    return _pcall(body, name="matmul_nt", grid=(S // tm, N // tn), in_specs=in_specs,
                  out_specs=pl.BlockSpec((tm, tn), lambda i, j: (i, j)), out_shape=_sds((S, N), out_dtype),
                  compiler_params=_cp("parallel", "parallel"))(*ops)


def matmul_nn(x, w, bias=None, out_dtype=F32):
    S, K = x.shape
    N = w.shape[1]
    tm, tn = _tile(S, 512, 8), _tile(N, 512, LANES)

    def body(*refs):
        x_ref, w_ref = refs[0], refs[1]
        o_ref = refs[-1]
        r = _nn(x_ref[...].astype(BF), w_ref[...].astype(BF))
        if bias is not None:
            r = r + refs[2][...]
        o_ref[...] = r.astype(o_ref.dtype)

    in_specs = [pl.BlockSpec((tm, K), lambda i, j: (i, 0)), pl.BlockSpec((K, tn), lambda i, j: (0, j))]
    ops = [x, w]
    if bias is not None:
        in_specs.append(pl.BlockSpec((1, tn), lambda i, j: (0, j)))
        ops.append(bias)
    return _pcall(body, name="matmul_nn", grid=(S // tm, N // tn), in_specs=in_specs,
                  out_specs=pl.BlockSpec((tm, tn), lambda i, j: (i, j)), out_shape=_sds((S, N), out_dtype),
                  compiler_params=_cp("parallel", "parallel"))(*ops)


def matmul_tn(a, b):
    S, M = a.shape
    N = b.shape[1]
    bm, bn, tk = _tile(M, 1408, LANES), _tile(N, 1024, LANES), _tile(S, 512, 8)

    def body(a_ref, b_ref, o_ref):
        _acc(o_ref, pl.program_id(2) == 0, _tn(a_ref[...].astype(BF), b_ref[...].astype(BF)))

    return _pcall(body, name="matmul_tn", grid=(M // bm, N // bn, S // tk),
                  in_specs=[pl.BlockSpec((tk, bm), lambda i, j, k: (k, i)), pl.BlockSpec((tk, bn), lambda i, j, k: (k, j))],
                  out_specs=pl.BlockSpec((bm, bn), lambda i, j, k: (i, j)), out_shape=_sds((M, N), F32),
                  compiler_params=_cp("parallel", "parallel", "arbitrary"))(a, b)


def outer_rows(at, b):
    M, R = at.shape
    N = b.shape[1]
    tm = _tile(M, 256, 8)

    def body(a_ref, b_ref, o_ref):
        av, bv = a_ref[...], b_ref[...]
        acc = av[:, 0:1] * bv[0:1, :]
        for r in range(1, R):
            acc = acc + av[:, r:r + 1] * bv[r:r + 1, :]
        o_ref[...] = acc

    return _pcall(body, name="outer_rows", grid=(M // tm,),
                  in_specs=[pl.BlockSpec((tm, R), lambda i: (i, 0)), pl.BlockSpec((R, N), lambda i: (0, 0))],
                  out_specs=pl.BlockSpec((tm, N), lambda i: (i, 0)), out_shape=_sds((M, N), F32),
                  compiler_params=_cp("parallel"))(at, b)


def ffn_fwd(h, w1t, w3t, w2):
    S, D = h.shape
    Fd = w2.shape[0]
    tm, tf = _tile(S, 1024, 8), _tile(Fd, 256, LANES)

    def body(h_ref, w1_ref, w3_ref, w2_ref, y_ref, g1_ref, g3_ref):
        hv = h_ref[...]
        g1 = _nt(hv, w1_ref[...])
        g3 = _nt(hv, w3_ref[...])
        g1_ref[...] = g1.astype(g1_ref.dtype)
        g3_ref[...] = g3.astype(g3_ref.dtype)
        a = (g1 * _sig(g1)) * g3
        _acc(y_ref, pl.program_id(1) == 0, _nn(a.astype(BF), w2_ref[...]))

    row = pl.BlockSpec((tm, D), lambda i, f: (i, 0))
    wsp = pl.BlockSpec((tf, D), lambda i, f: (f, 0))
    act = pl.BlockSpec((tm, tf), lambda i, f: (i, f))
    return _pcall(body, name="ffn_fwd", grid=(S // tm, Fd // tf), in_specs=[row, wsp, wsp, wsp], out_specs=[row, act, act],
                  out_shape=[_sds((S, D), F32), _sds((S, Fd), BF), _sds((S, Fd), BF)],
                  compiler_params=_cp("parallel", "arbitrary"))(h, w1t, w3t, w2)


def ffn_bwd(dy, g1s, g3s, w1t, w3t, w2):
    S, D = dy.shape
    Fd = w2.shape[0]
    tm, tf = _tile(S, 1024, 8), _tile(Fd, 256, LANES)

    def body(dy_ref, g1_ref, g3_ref, w1_ref, w3_ref, w2_ref, dh_ref, a_ref, d1_ref, d3_ref):
        w1, w3 = w1_ref[...], w3_ref[...]
        g1 = g1_ref[...].astype(F32)
        g3 = g3_ref[...].astype(F32)
        s = _sig(g1)
        si = g1 * s
        a_ref[...] = (si * g3).astype(a_ref.dtype)
        da = _nt(dy_ref[...], w2_ref[...])
        d3 = (da * si).astype(BF)
        d1 = (da * g3 * (s * (1.0 + g1 * (1.0 - s)))).astype(BF)
        d1_ref[...] = d1
        d3_ref[...] = d3
        _acc(dh_ref, pl.program_id(1) == 0, _nn(d1, w1) + _nn(d3, w3))

    row = pl.BlockSpec((tm, D), lambda i, f: (i, 0))
    wsp = pl.BlockSpec((tf, D), lambda i, f: (f, 0))
    act = pl.BlockSpec((tm, tf), lambda i, f: (i, f))
    return _pcall(body, name="ffn_bwd", grid=(S // tm, Fd // tf), in_specs=[row, act, act, wsp, wsp, wsp],
                  out_specs=[row, act, act, act],
                  out_shape=[_sds((S, D), F32), _sds((S, Fd), BF), _sds((S, Fd), BF), _sds((S, Fd), BF)],
                  compiler_params=_cp("parallel", "arbitrary"))(dy, g1s, g3s, w1t, w3t, w2)


def _layer_norm_parts(v):
    mu = jnp.mean(v, axis=-1, keepdims=True)
    vc = v - mu
    rs = lax.rsqrt(jnp.mean(vc * vc, axis=-1, keepdims=True) + EPS)
    return vc * rs, rs


def conv_mid_fwd(pre, dw, dw_b, ln_g, ln_b):
    S, D2 = pre.shape
    D = D2 // 2
    tm = _tile(S, 256, CONV_PAD)

    def body(pre_ref, dw_ref, dwb_ref, g_ref, b_ref, v_ref, sw_ref, ubuf):
        @pl.when(pl.program_id(0) == 0)
        def _():
            ubuf[pl.ds(0, CONV_PAD), :] = jnp.zeros((CONV_PAD, D), F32)

        @pl.when(pl.program_id(0) > 0)
        def _():
            ubuf[pl.ds(0, CONV_PAD), :] = ubuf[pl.ds(tm, CONV_PAD), :]

        ubuf[pl.ds(CONV_PAD, tm), :] = pre_ref[:, pl.ds(0, D)] * _sig(pre_ref[:, pl.ds(D, D)])
        acc = jnp.zeros((tm, D), F32) + dwb_ref[...]
        for k in range(CONV_W):
            acc = acc + dw_ref[pl.ds(k, 1), :] * ubuf[pl.ds(CONV_PAD - (CONV_W - 1) + k, tm), :]
        v_ref[...] = acc
        vh, _ = _layer_norm_parts(acc)
        ln = vh * g_ref[...] + b_ref[...]
        sw_ref[...] = (ln * _sig(ln)).astype(sw_ref.dtype)

    row = pl.BlockSpec((tm, D), lambda i: (i, 0))
    vec = pl.BlockSpec((1, D), lambda i: (0, 0))
    return _pcall(body, name="conv_mid_fwd", grid=(S // tm,),
                  in_specs=[pl.BlockSpec((tm, D2), lambda i: (i, 0)), pl.BlockSpec((CONV_PAD, D), lambda i: (0, 0)), vec, vec, vec],
                  out_specs=[row, row], out_shape=[_sds((S, D), F32), _sds((S, D), BF)],
                  scratch_shapes=[pltpu.VMEM((tm + CONV_PAD, D), F32)], compiler_params=_cp("arbitrary"))(pre, dw, dw_b, ln_g, ln_b)


def conv_mid_bwd(pre, v, dsw, dw, ln_g, ln_b):
    S, D2 = pre.shape
    D = D2 // 2
    tm = _tile(S, 256, CONV_PAD)
    nt = S // tm

    def body(pre_ref, v_ref, dsw_ref, dw_ref, g_ref, b_ref, dpre_ref, dbin_ref, ddw_ref, ddwb_ref, dg_ref, db_ref, dvbuf):
        first = pl.program_id(0) == 0

        @pl.when(first)
        def _():
            dvbuf[pl.ds(tm, CONV_PAD), :] = jnp.zeros((CONV_PAD, D), F32)
            ddw_ref[...] = jnp.zeros((CONV_PAD, D), F32)

        @pl.when(jnp.logical_not(first))
        def _():
            dvbuf[pl.ds(tm, CONV_PAD), :] = dvbuf[pl.ds(0, CONV_PAD), :]

        a = pre_ref[:, pl.ds(0, D)]
        sb = _sig(pre_ref[:, pl.ds(D, D)])
        u = a * sb
        gv = g_ref[...]
        vh, rs = _layer_norm_parts(v_ref[...])
        ln = vh * gv + b_ref[...]
        sg = _sig(ln)
        dln = dsw_ref[...] * (sg * (1.0 + ln * (1.0 - sg)))
        _acc(dg_ref, first, _colsum(dln * vh))
        _acc(db_ref, first, _colsum(dln))
        dvh = dln * gv
        dv = rs * (dvh - jnp.mean(dvh, axis=-1, keepdims=True) - vh * jnp.mean(dvh * vh, axis=-1, keepdims=True))
        _acc(ddwb_ref, first, _colsum(dv))
        dvbuf[pl.ds(0, tm), :] = dv
        du = jnp.zeros((tm, D), F32)
        for k in range(CONV_W):
            sh = dvbuf[pl.ds(CONV_W - 1 - k, tm), :]
            du = du + dw_ref[pl.ds(k, 1), :] * sh
            ddw_ref[pl.ds(k, 1), :] += _colsum(u * sh)
        da = du * sb
        db = du * a * sb * (1.0 - sb)
        dpre_ref[:, pl.ds(0, D)] = da.astype(dpre_ref.dtype)
        dpre_ref[:, pl.ds(D, D)] = db.astype(dpre_ref.dtype)
        _acc(dbin_ref, first, jnp.concatenate([_colsum(da), _colsum(db)], axis=1))

    rev = lambda i: (nt - 1 - i, 0)
    row = pl.BlockSpec((tm, D), rev)
    row2 = pl.BlockSpec((tm, D2), rev)
    vec = pl.BlockSpec((1, D), lambda i: (0, 0))
    pad = pl.BlockSpec((CONV_PAD, D), lambda i: (0, 0))
    return _pcall(body, name="conv_mid_bwd", grid=(nt,), in_specs=[row2, row, row, pad, vec, vec],
                  out_specs=[row2, pl.BlockSpec((1, D2), lambda i: (0, 0)), pad, vec, vec, vec],
                  out_shape=[_sds((S, D2), BF), _sds((1, D2), F32), _sds((CONV_PAD, D), F32), _sds((1, D), F32),
                             _sds((1, D), F32), _sds((1, D), F32)],
                  scratch_shapes=[pltpu.VMEM((tm + CONV_PAD, D), F32)], compiler_params=_cp("arbitrary"))(pre, v, dsw, dw, ln_g, ln_b)


def _pool_count(i, tm, w):
    t = i * tm + lax.broadcasted_iota(jnp.int32, (tm, 1), 0)
    return jnp.minimum(t + 1, w).astype(F32)


def pool_fwd(h, p_w, p_b, p_scale):
    S, D = h.shape
    G, Dg = p_w.shape[0], p_w.shape[1]
    tm = _tile(S, 256, POOL_PAD)

    def body(h_ref, w_ref, b_ref, s_ref, y_ref, d_ref, hbuf):
        i = pl.program_id(0)

        @pl.when(i == 0)
        def _():
            hbuf[pl.ds(0, POOL_PAD), :] = jnp.zeros((POOL_PAD, D), F32)

        @pl.when(i > 0)
        def _():
            hbuf[pl.ds(0, POOL_PAD), :] = hbuf[pl.ds(tm, POOL_PAD), :]

        hbuf[pl.ds(POOL_PAD, tm), :] = h_ref[...].astype(F32)
        for g, w in enumerate(POOL_WINDOWS):
            cols = pl.ds(g * Dg, Dg)
            hg = hbuf[pl.ds(POOL_PAD, tm), cols]
            win = hg
            for j in range(1, w):
                win = win + hbuf[pl.ds(POOL_PAD - j, tm), cols]
            d = (win / _pool_count(i, tm, w) - hg).astype(BF)
            d_ref[:, cols] = d
            y_ref[:, cols] = (_nn(d, w_ref[g].astype(BF)) + b_ref[:, cols]) * s_ref[:, cols]

    row = pl.BlockSpec((tm, D), lambda i: (i, 0))
    vec = pl.BlockSpec((1, D), lambda i: (0, 0))
    return _pcall(body, name="pool_fwd", grid=(S // tm,),
                  in_specs=[row, pl.BlockSpec((G, Dg, Dg), lambda i: (0, 0, 0)), vec, vec], out_specs=[row, row],
                  out_shape=[_sds((S, D), F32), _sds((S, D), BF)],
                  scratch_shapes=[pltpu.VMEM((tm + POOL_PAD, D), F32)], compiler_params=_cp("arbitrary"))(h, p_w, p_b, p_scale)


def pool_bwd(dy, d, p_w, p_b, p_scale):
    S, D = dy.shape
    G, Dg = p_w.shape[0], p_w.shape[1]
    tm = _tile(S, 256, POOL_PAD)
    nt = S // tm

    def body(dy_ref, d_ref, w_ref, b_ref, s_ref, dh_ref, dyy_ref, dpb_ref, dps_ref, ebuf):
        j = pl.program_id(0)
        i = nt - 1 - j
        first = j == 0

        @pl.when(first)
        def _():
            ebuf[pl.ds(tm, POOL_PAD), :] = jnp.zeros((POOL_PAD, D), F32)

        @pl.when(jnp.logical_not(first))
        def _():
            ebuf[pl.ds(tm, POOL_PAD), :] = ebuf[pl.ds(0, POOL_PAD), :]

        dpb, dps = [], []
        for g, w in enumerate(POOL_WINDOWS):
            cols = pl.ds(g * Dg, Dg)
            wg = w_ref[g].astype(BF)
            dyg = dy_ref[:, cols].astype(F32)
            ypre = _nn(d_ref[:, cols], wg) + b_ref[:, cols]
            dps.append(_colsum(dyg * ypre))
            dyy = dyg * s_ref[:, cols]
            dpb.append(_colsum(dyy))
            dyy = dyy.astype(BF)
            dyy_ref[:, cols] = dyy
            dd = _nt(dyy, wg)
            ebuf[pl.ds(0, tm), cols] = dd / _pool_count(i, tm, w)
            acc = -dd
            for k in range(w):
                acc = acc + ebuf[pl.ds(k, tm), cols]
            dh_ref[:, cols] = acc
        _acc(dpb_ref, first, jnp.concatenate(dpb, axis=1))
        _acc(dps_ref, first, jnp.concatenate(dps, axis=1))

    row = pl.BlockSpec((tm, D), lambda i: (nt - 1 - i, 0))
    vec = pl.BlockSpec((1, D), lambda i: (0, 0))
    return _pcall(body, name="pool_bwd", grid=(nt,),
                  in_specs=[row, row, pl.BlockSpec((G, Dg, Dg), lambda i: (0, 0, 0)), vec, vec],
                  out_specs=[row, row, vec, vec],
                  out_shape=[_sds((S, D), F32), _sds((S, D), BF), _sds((1, D), F32), _sds((1, D), F32)],
                  scratch_shapes=[pltpu.VMEM((tm + POOL_PAD, D), F32)], compiler_params=_cp("arbitrary"))(dy, d, p_w, p_b, p_scale)


def _group_sum(v, low):
    s_lo = jnp.sum(jnp.where(low, v, 0.0), axis=-1, keepdims=True)
    s_hi = jnp.sum(jnp.where(low, 0.0, v), axis=-1, keepdims=True)
    return jnp.where(low, s_lo, s_hi)


def qknorm_fwd(qkv, q_g2, k_g2):
    S, D3 = qkv.shape
    nb = D3 // LANES // 3
    tm = _tile(S, 512, 8)

    def body(x_ref, qg_ref, kg_ref, o_ref):
        j = pl.program_id(1)
        xv = x_ref[...]
        low = lax.broadcasted_iota(jnp.int32, (1, LANES), 1) < HEAD_DIM
        r = lax.rsqrt(_group_sum(xv * xv, low) * (1.0 / HEAD_DIM) + EPS)
        xn = xv * r
        qn = (xn * qg_ref[...]) * Q_SCALE
        kn = xn * kg_ref[...]
        o_ref[...] = jnp.where(j < nb, qn, jnp.where(j < 2 * nb, kn, xv)).astype(o_ref.dtype)

    blk = pl.BlockSpec((tm, LANES), lambda i, j: (i, j))
    vec = pl.BlockSpec((1, LANES), lambda i, j: (0, 0))
    return _pcall(body, name="qknorm_fwd", grid=(S // tm, D3 // LANES), in_specs=[blk, vec, vec], out_specs=blk,
                  out_shape=_sds((S, D3), BF), compiler_params=_cp("parallel", "parallel"))(qkv, q_g2, k_g2)


def qknorm_bwd(qkv, dq, dk, dv, q_g2, k_g2):
    S, D3 = qkv.shape
    nb = D3 // LANES // 3
    tm = _tile(S, 512, 8)

    def body(x_ref, dq_ref, dk_ref, dv_ref, qg_ref, kg_ref, o_ref, dg_ref):
        j = pl.program_id(0)
        is_q, is_k = j < nb, jnp.logical_and(j >= nb, j < 2 * nb)
        xv = x_ref[...]
        low = lax.broadcasted_iota(jnp.int32, (1, LANES), 1) < HEAD_DIM
        r = lax.rsqrt(_group_sum(xv * xv, low) * (1.0 / HEAD_DIM) + EPS)
        xh = xv * r
        dn = jnp.where(is_q, dq_ref[...] * (Q_SCALE * LN2), dk_ref[...] * LN2)
        gv = jnp.where(is_q, qg_ref[...], kg_ref[...])
        u = dn * gv
        dx = r * (u - xh * (_group_sum(u * xh, low) * (1.0 / HEAD_DIM)))
        o_ref[...] = jnp.where(jnp.logical_or(is_q, is_k), dx, dv_ref[...]).astype(o_ref.dtype)
        _acc(dg_ref, pl.program_id(1) == 0, _colsum(dn * xh))

    blk = pl.BlockSpec((tm, LANES), lambda j, i: (i, j))
    part = lambda lo: pl.BlockSpec((tm, LANES), lambda j, i: (i, jnp.clip(j - lo, 0, nb - 1)))
    vec = pl.BlockSpec((1, LANES), lambda j, i: (0, 0))
    return _pcall(body, name="qknorm_bwd", grid=(D3 // LANES, S // tm),
                  in_specs=[blk, part(0), part(nb), part(2 * nb), vec, vec],
                  out_specs=[blk, pl.BlockSpec((1, LANES), lambda j, i: (0, j))],
                  out_shape=[_sds((S, D3), BF), _sds((1, D3), F32)],
                  compiler_params=_cp("parallel", "arbitrary"))(qkv, dq, dk, dv, q_g2, k_g2)


LOG2E = math.log2(math.e)
LN2 = math.log(2.0)
Q_SCALE = HEAD_DIM ** -0.5 * LOG2E
FAR = 160.0
UNSEEN = 1e30


def _softplus2(z2):
    return jnp.maximum(z2, 0.0) + jnp.log(1.0 + jnp.exp2(-jnp.abs(z2))) * LOG2E


def _attn_tile(S):
    return _tile(S, 256, LANES) if S >= 1024 else LANES


def attn_fwd(qkvn):
    S, D3 = qkvn.shape
    D = D3 // 3
    nb = D // LANES
    t = _attn_tile(S)
    nq = S // t
    assert nq <= LANES

    def body(q_ref, k_ref, v_ref, o_ref, r_ref, acc_ref, rall_ref, behind_ref, z_ref):
        qi = pl.program_id(1)
        lane = lax.broadcasted_iota(jnp.int32, (1, LANES), 1)
        rows = lax.broadcasted_iota(jnp.int32, (t, t), 0)
        cols = lax.broadcasted_iota(jnp.int32, (t, t), 1)
        later = (rows > cols).astype(BF)
        q = q_ref[...]
        qh = [jnp.where(lane < HEAD_DIM, q, jnp.zeros_like(q)), jnp.where(lane >= HEAD_DIM, q, jnp.zeros_like(q))]
        acc_ref[...] = jnp.zeros((2, t, LANES), F32)
        rall_ref[...] = jnp.full((2, t, LANES), UNSEEN, F32)
        behind_ref[...] = jnp.zeros((2, t, LANES), F32)

        def scores(b):
            kb = k_ref[pl.ds(pl.multiple_of(b * t, t), t), :]
            return [_nt(qh[0], kb), _nt(qh[1], kb)]

        def block(b, diagonal):
            vb = v_ref[pl.ds(pl.multiple_of(b * t, t), t), :]
            lsig, sp, within = [None, None], [None, None], [None, None]
            for hh in range(2):
                z = z_ref[hh]
                sp_full = _softplus2(z)
                lsig[hh] = z - sp_full
                sp[hh] = jnp.where(cols < rows, sp_full, 0.0) if diagonal else sp_full
                within[hh] = _nn(sp[hh].astype(BF), later)
            z_next = scores(jnp.maximum(b - 1, 0))
            for hh in range(2):
                behind = behind_ref[hh]
                a = jnp.exp2(lsig[hh] - (within[hh] + jnp.tile(behind, (1, t // LANES))))
                if diagonal:
                    a = jnp.where(cols < rows, a, 0.0)
                acc_ref[hh] += _nn(a.astype(BF), vb)
                rall_ref[hh] = jnp.where(lane == b, behind, rall_ref[hh])
                behind_ref[hh] = behind + jnp.sum(sp[hh], axis=-1, keepdims=True)
            for hh in range(2):
                z_ref[hh] = z_next[hh]

        z_first = scores(qi)
        for hh in range(2):
            z_ref[hh] = z_first[hh]
        block(qi, True)

        def reaches():
            return (jnp.min(behind_ref[...]) < FAR).astype(jnp.int32)

        def step(carry):
            n, _ = carry
            block(qi - 1 - n, False)
            return n + 1, reaches()

        lax.while_loop(lambda carry: jnp.logical_and(carry[0] < qi, carry[1] > 0), step, (jnp.int32(0), reaches()))
        r_ref[...] = rall_ref[...]
        o_ref[...] = jnp.where(lane < HEAD_DIM, acc_ref[0], acc_ref[1]).astype(o_ref.dtype)

    return _pcall(body, name="attn_fwd", grid=(nb, nq),
                  in_specs=[pl.BlockSpec((t, LANES), lambda h, i: (i, h)),
                            pl.BlockSpec((S, LANES), lambda h, i: (0, nb + h)),
                            pl.BlockSpec((S, LANES), lambda h, i: (0, 2 * nb + h))],
                  out_specs=[pl.BlockSpec((t, LANES), lambda h, i: (i, h)), pl.BlockSpec((2, t, LANES), lambda h, i: (h, i, 0))],
                  out_shape=[_sds((S, D), BF), _sds((2 * nb, S, LANES), F32)],
                  scratch_shapes=[pltpu.VMEM((2, t, LANES), F32)] * 3 + [pltpu.VMEM((2, t, t), F32)],
                  compiler_params=_cp("parallel", "arbitrary"))(qkvn, qkvn, qkvn)


def attn_bwd(qkvn, do, r_all):
    S, D3 = qkvn.shape
    D = D3 // 3
    nb = D // LANES
    t = _attn_tile(S)
    nq = S // t

    def body(q_ref, k_ref, v_ref, do_ref, r_ref, dq_ref, dk_hbm, dv_hbm, dk_acc, dv_acc, dq_acc, before_ref, z_ref, da_ref, sem):
        hp = pl.program_id(0)
        qi = pl.program_id(1)

        @pl.when(qi == 0)
        def _():
            dk_acc[...] = jnp.zeros((S, LANES), F32)
            dv_acc[...] = jnp.zeros((S, LANES), F32)

        lane = lax.broadcasted_iota(jnp.int32, (1, LANES), 1)
        rows = lax.broadcasted_iota(jnp.int32, (t, t), 0)
        cols = lax.broadcasted_iota(jnp.int32, (t, t), 1)
        later = (rows > cols).astype(BF)
        earlier = (rows < cols).astype(BF)
        q = q_ref[...]
        dov = do_ref[...]
        low, high = lane < HEAD_DIM, lane >= HEAD_DIM
        qh = [jnp.where(low, q, jnp.zeros_like(q)), jnp.where(high, q, jnp.zeros_like(q))]
        doh = [jnp.where(low, dov, jnp.zeros_like(dov)), jnp.where(high, dov, jnp.zeros_like(dov))]
        q_both = jnp.concatenate(qh, axis=0)
        do_both = jnp.concatenate(doh, axis=0)
        dq_acc[...] = jnp.zeros((2, t, LANES), F32)
        before_ref[...] = jnp.zeros((2, t, LANES), F32)

        def scores(b):
            off = pl.multiple_of(b * t, t)
            kb = k_ref[pl.ds(off, t), :]
            vb = v_ref[pl.ds(off, t), :]
            return [_nt(qh[0], kb), _nt(qh[1], kb)], [_nt(doh[0], vb), _nt(doh[1], vb)]

        def block(b, diagonal):
            off = pl.multiple_of(b * t, t)
            kb = k_ref[pl.ds(off, t), :]
            lsig, sigma, within, g, earlier_g, dz, prob = ([None, None] for _ in range(7))
            for hh in range(2):
                z = z_ref[hh]
                sp = _softplus2(z)
                lsig[hh] = z - sp
                if diagonal:
                    sp = jnp.where(cols < rows, sp, 0.0)
                within[hh] = _nn(sp.astype(BF), later)
            z_next, da_next = scores(jnp.minimum(b + 1, qi))
            for hh in range(2):
                behind = jnp.sum(jnp.where(lane == b, r_ref[hh], 0.0), axis=-1, keepdims=True)
                a = jnp.exp2(lsig[hh] - (within[hh] + behind))
                if diagonal:
                    a = jnp.where(cols < rows, a, 0.0)
                prob[hh] = a.astype(BF)
                g[hh] = a * da_ref[hh]
                earlier_g[hh] = _nn(g[hh].astype(BF), earlier)
            for hh in range(2):
                before = before_ref[hh]
                d = g[hh] - jnp.exp2(lsig[hh]) * (g[hh] + (earlier_g[hh] + jnp.tile(before, (1, t // LANES))))
                if diagonal:
                    d = jnp.where(cols < rows, d, 0.0)
                dz[hh] = d.astype(BF)
                dq_acc[hh] += _nn(dz[hh], kb)
                before_ref[hh] = before + jnp.sum(g[hh], axis=-1, keepdims=True)
            dk_acc[pl.ds(off, t), :] += _tn(jnp.concatenate(dz, axis=0), q_both)
            dv_acc[pl.ds(off, t), :] += _tn(jnp.concatenate(prob, axis=0), do_both)
            for hh in range(2):
                z_ref[hh] = z_next[hh]
                da_ref[hh] = da_next[hh]

        nearest = jnp.min(jnp.minimum(r_ref[0], r_ref[1]), axis=0, keepdims=True)
        skip = jnp.sum(jnp.where(jnp.logical_and(lane < qi, nearest >= FAR), 1, 0))
        z_first, da_first = scores(skip)
        for hh in range(2):
            z_ref[hh] = z_first[hh]
            da_ref[hh] = da_first[hh]

        def step(b, carry):
            block(b, False)
            return carry

        lax.fori_loop(skip, qi, step, 0)
        block(qi, True)
        dq_ref[...] = jnp.where(low, dq_acc[0], dq_acc[1])

        @pl.when(qi == nq - 1)
        def _():
            col = pl.multiple_of(hp * LANES, LANES)
            c1 = pltpu.make_async_copy(dk_acc, dk_hbm.at[:, pl.ds(col, LANES)], sem.at[0])
            c2 = pltpu.make_async_copy(dv_acc, dv_hbm.at[:, pl.ds(col, LANES)], sem.at[1])
            c1.start()
            c2.start()
            c1.wait()
            c2.wait()

    blk = pl.BlockSpec((t, LANES), lambda h, i: (i, h))
    hbm = pl.BlockSpec(memory_space=pl.ANY)
    return _pcall(body, name="attn_bwd", grid=(nb, nq),
                  in_specs=[blk, pl.BlockSpec((S, LANES), lambda h, i: (0, nb + h)),
                            pl.BlockSpec((S, LANES), lambda h, i: (0, 2 * nb + h)), blk,
                            pl.BlockSpec((2, t, LANES), lambda h, i: (h, i, 0))],
                  out_specs=[blk, hbm, hbm],
                  out_shape=[_sds((S, D), F32), _sds((S, D), F32), _sds((S, D), F32)],
                  scratch_shapes=[pltpu.VMEM((S, LANES), F32), pltpu.VMEM((S, LANES), F32), pltpu.VMEM((2, t, LANES), F32),
                                  pltpu.VMEM((2, t, LANES), F32), pltpu.VMEM((2, t, t), F32), pltpu.VMEM((2, t, t), F32),
                                  pltpu.SemaphoreType.DMA((2,))],
                  compiler_params=_cp("arbitrary", "arbitrary"))(qkvn, qkvn, qkvn, do, r_all)


def cond_embed(parts, cond_b):
    P, B, D = parts.shape

    def body(p_ref, b_ref, pre_ref, e_ref):
        pre = p_ref[0]
        for s in range(1, P):
            pre = pre + p_ref[s]
        pre = pre + b_ref[...]
        pre_ref[...] = pre
        e_ref[...] = pre * _sig(pre)

    return _pcall(body, name="cond_embed", out_shape=[_sds((B, D), F32), _sds((B, D), F32)],
                  compiler_params=_cp())(parts, cond_b)


def cond_embed_bwd(parts, pre):
    P, B, D = parts.shape

    def body(p_ref, pre_ref, dpre_ref, db_ref):
        de = p_ref[0]
        for s in range(1, P):
            de = de + p_ref[s]
        pre = pre_ref[...]
        s = _sig(pre)
        dpre = de * (s * (1.0 + pre * (1.0 - s)))
        dpre_ref[...] = dpre
        db_ref[...] = _colsum(dpre)

    return _pcall(body, name="cond_embed_bwd", out_shape=[_sds((B, D), F32), _sds((1, D), F32)],
                  compiler_params=_cp())(parts, pre)


def adamw(w, parts, m, v):
    R, C = w.shape
    P = parts.shape[0]
    tr = _tile(R, max(8, (1 << 17) // C // 8 * 8), 8)
    c1 = 1.0 - ADAM_B1 ** ADAM_STEP
    c2 = 1.0 - ADAM_B2 ** ADAM_STEP

    def body(w_ref, p_ref, m_ref, v_ref, g_ref, d_ref, nm_ref, nv_ref):
        g = p_ref[0].astype(F32)
        for s in range(1, P):
            g = g + p_ref[s].astype(F32)
        nm = ADAM_B1 * m_ref[...] + (1.0 - ADAM_B1) * g
        nv = ADAM_B2 * v_ref[...] + (1.0 - ADAM_B2) * (g * g)
        g_ref[...] = g
        nm_ref[...] = nm
        nv_ref[...] = nv
        d_ref[...] = -ADAM_LR * ((nm / c1) / (jnp.sqrt(nv / c2) + ADAM_EPS) + ADAM_WD * w_ref[...])

    blk = pl.BlockSpec((tr, C), lambda i: (i, 0))
    return _pcall(body, name="adamw", grid=(R // tr,), in_specs=[blk, pl.BlockSpec((P, tr, C), lambda i: (0, i, 0)), blk, blk],
                  out_specs=[blk] * 4, out_shape=[_sds((R, C), F32)] * 4, compiler_params=_cp("parallel"))(w, parts, m, v)


def exchange(arrs, peer_axes):
    n = len(arrs)
    shapes = [a.shape if ax is None else a.shape[:ax] + a.shape[ax + 1:] for a, ax in zip(arrs, peer_axes)]

    def body(*refs):
        ins, outs = refs[:n], refs[n:2 * n]
        send, recv, local = refs[2 * n:]
        x, y, c = lax.axis_index("x"), lax.axis_index("y"), lax.axis_index("c")
        me = 4 * x + 2 * y + c

        def piece(i, d):
            if peer_axes[i] is None:
                return ins[i]
            return ins[i].at[(slice(None),) * peer_axes[i] + (d,)]

        mine = [pltpu.make_async_copy(piece(i, me), outs[i].at[me], local.at[i]) for i in range(n)]
        for cp in mine:
            cp.start()
        sends, recvs = [], []
        for p in range(1, N_DEV):
            peer = (1 - x if p & 4 else x, 1 - y if p & 2 else y, 1 - c if p & 1 else c)
            them = 4 * peer[0] + 2 * peer[1] + peer[2]
            for i in range(n):
                sends.append(pltpu.make_async_remote_copy(piece(i, them), outs[i].at[me], send.at[i, p - 1], recv.at[i, p - 1],
                                                          device_id=peer, device_id_type=MESH))
                recvs.append(pltpu.make_async_remote_copy(piece(i, me), outs[i].at[them], send.at[i, p - 1], recv.at[i, p - 1],
                                                          device_id=peer, device_id_type=MESH))
        for cp in sends:
            cp.start()
        for cp in recvs:
            cp.wait_recv()
        for cp in sends:
            cp.wait_send()
        for cp in mine:
            cp.wait()

    hbm = pl.BlockSpec(memory_space=pl.ANY)
    return _pcall(body, name="exchange", in_specs=[hbm] * n, out_specs=[hbm] * n,
                  out_shape=[_sds((N_DEV,) + s, a.dtype) for s, a in zip(shapes, arrs)],
                  scratch_shapes=[pltpu.SemaphoreType.DMA((n, N_DEV - 1)), pltpu.SemaphoreType.DMA((n, N_DEV - 1)),
                                  pltpu.SemaphoreType.DMA((n,))])(*arrs)


def _pack(vs):
    flat = jnp.concatenate([v.reshape(-1).astype(F32) for v in vs])
    pad = (-flat.shape[0]) % 1024
    return jnp.pad(flat, (0, pad)).reshape(-1, 1024)


def _unpack(packed, shapes):
    flat = packed.reshape(packed.shape[0], -1)
    out, o = [], 0
    for s in shapes:
        n = math.prod(s)
        out.append(flat[:, o:o + n].reshape((packed.shape[0],) + tuple(s)))
        o += n
    return out


def _cat_dev(g, axis):
    g = jnp.moveaxis(g, 0, axis)
    return g.reshape(g.shape[:axis] + (g.shape[axis] * g.shape[axis + 1],) + g.shape[axis + 2:])


def kernel(x, c, cond_w, cond_b, ada_w, ada_b, norm_g, ffn_w1, ffn_w3, ffn_w2, a_w_in, a_b_in, a_dw, a_dw_b, a_ln_g, a_ln_b, a_w_out, a_b_out, b_w_qkv, b_q_g, b_k_g, b_w_o, p_w, p_b, p_scale, loss_target, m_cond_w, m_cond_b, m_ada_w, m_ada_b, m_norm_g, m_ffn_w1, m_ffn_w3, m_ffn_w2, m_a_w_in, m_a_b_in, m_a_dw, m_a_dw_b, m_a_ln_g, m_a_ln_b, m_a_w_out, m_a_b_out, m_b_w_qkv, m_b_q_g, m_b_k_g, m_b_w_o, m_p_w, m_p_b, m_p_scale, v_cond_w, v_cond_b, v_ada_w, v_ada_b, v_norm_g, v_ffn_w1, v_ffn_w3, v_ffn_w2, v_a_w_in, v_a_b_in, v_a_dw, v_a_dw_b, v_a_ln_g, v_a_ln_b, v_a_w_out, v_a_b_out, v_b_w_qkv, v_b_q_g, v_b_k_g, v_b_w_o, v_p_w, v_p_b, v_p_scale):
    A = dict(zip(ARGS, (x, c, cond_w, cond_b, ada_w, ada_b, norm_g, ffn_w1, ffn_w3, ffn_w2, a_w_in, a_b_in, a_dw, a_dw_b, a_ln_g, a_ln_b, a_w_out, a_b_out, b_w_qkv, b_q_g, b_k_g, b_w_o, p_w, p_b, p_scale, loss_target, m_cond_w, m_cond_b, m_ada_w, m_ada_b, m_norm_g, m_ffn_w1, m_ffn_w3, m_ffn_w2, m_a_w_in, m_a_b_in, m_a_dw, m_a_dw_b, m_a_ln_g, m_a_ln_b, m_a_w_out, m_a_b_out, m_b_w_qkv, m_b_q_g, m_b_k_g, m_b_w_o, m_p_w, m_p_b, m_p_scale, v_cond_w, v_cond_b, v_ada_w, v_ada_b, v_norm_g, v_ffn_w1, v_ffn_w3, v_ffn_w2, v_a_w_in, v_a_b_in, v_a_dw, v_a_dw_b, v_a_ln_g, v_a_ln_b, v_a_w_out, v_a_b_out, v_b_w_qkv, v_b_q_g, v_b_k_g, v_b_w_o, v_p_w, v_p_b, v_p_scale)))
    S, D = x.shape[1], x.shape[2]
    depth = ada_w.shape[0]
    n_a, n_b, n_c = a_w_in.shape[0], b_w_qkv.shape[0], p_w.shape[0]
    me = 4 * lax.axis_index("x") + 2 * lax.axis_index("y") + lax.axis_index("c")
    swap = lambda w: jnp.swapaxes(w, -1, -2)

    small_names = ['norm_g', 'a_b_in', 'a_dw', 'a_dw_b', 'a_ln_g', 'a_ln_b', 'a_b_out', 'p_w', 'p_b', 'p_scale']
    small_in = [c] + [A[n] for n in small_names]
    big_in = [swap(ffn_w1).astype(BF), swap(ffn_w3).astype(BF), ffn_w2.astype(BF), swap(a_w_in).astype(BF),
              a_w_out.astype(BF), swap(b_w_qkv).astype(BF), b_w_o.astype(BF)]
    got = exchange(big_in + [_pack(small_in)], [None] * (len(big_in) + 1))
    w1t, w3t, w2 = _cat_dev(got[0], 2), _cat_dev(got[1], 2), _cat_dev(got[2], 2)
    w_in_t, w_out = _cat_dev(got[3], 1), _cat_dev(got[4], 1)
    w_qkv_t, w_o = _cat_dev(got[5], 1), _cat_dev(got[6], 1)
    sm = dict(zip(['c'] + small_names, _unpack(got[7], [v.shape for v in small_in])))
    c_all = sm['c'][:, 0]
    norm_g_f = _cat_dev(sm['norm_g'], 2)
    a_b_in_f, a_dw_f, a_dw_b_f = _cat_dev(sm['a_b_in'], 1), _cat_dev(sm['a_dw'], 2), _cat_dev(sm['a_dw_b'], 1)
    a_ln_g_f, a_ln_b_f, a_b_out_f = _cat_dev(sm['a_ln_g'], 1), _cat_dev(sm['a_ln_b'], 1), _cat_dev(sm['a_b_out'], 1)
    p_w_f, p_b_f, p_scale_f = _cat_dev(sm['p_w'], 2), _cat_dev(sm['p_b'], 2), _cat_dev(sm['p_scale'], 1)
    a_dw_f = jnp.pad(a_dw_f, ((0, 0), (0, CONV_PAD - CONV_W), (0, 0)))

    rows = D // N_DEV
    c_mine = lax.dynamic_slice_in_dim(c_all, me * rows, rows, axis=1)
    pre_parts = exchange([matmul_nn(c_mine, cond_w)], [None])[0]
    pre_all, e_all = cond_embed(pre_parts, cond_b.reshape(1, D))
    n_mod = ada_w.shape[2]
    ada_b_mine = lax.dynamic_slice_in_dim(ada_b, me * n_mod, n_mod, axis=1)
    mod_part = jnp.stack([matmul_nn(e_all, ada_w[i], ada_b_mine[i:i + 1]) for i in range(depth)], axis=1)
    mod_all = exchange([mod_part], [None])[0]
    mod = lax.dynamic_index_in_dim(mod_all, me, axis=1, keepdims=False)
    mod = jnp.moveaxis(mod, 0, 1).reshape(depth, 3, 3, 1, D)

    xs = x[0]
    saved = []
    ia = ib = ic = 0
    for i in range(depth):
        lay = {}
        for sub in range(3):
            shift, scale, gate = mod[i, sub, 0], mod[i, sub, 1], 1.0 + mod[i, sub, 2]
            g = norm_g_f[i, sub].reshape(1, D)
            h = mod_fwd(xs, g, scale, shift)
            rec = dict(x=xs, h=h, g=g, scale=scale, gate=gate)
            if sub != 1:
                j = 0 if sub == 0 else 1
                y, g1s, g3s = ffn_fwd(h, w1t[i, j], w3t[i, j], w2[i, j])
                rec.update(kind='ffn', j=j, coef=0.5, g1=g1s, g3=g3s)
            elif i % 3 == 0:
                pre = matmul_nt(h, w_in_t[ia], a_b_in_f[ia].reshape(1, 2 * D))
                v, sw = conv_mid_fwd(pre, a_dw_f[ia], a_dw_b_f[ia].reshape(1, D), a_ln_g_f[ia].reshape(1, D), a_ln_b_f[ia].reshape(1, D))
                y = matmul_nn(sw, w_out[ia], a_b_out_f[ia].reshape(1, D))
                rec.update(kind='conv', idx=ia, coef=1.0, pre=pre, v=v, sw=sw)
                ia += 1
            elif i % 3 == 1:
                qkv = matmul_nt(h, w_qkv_t[ib])
                qg2 = jnp.tile(b_q_g[ib].reshape(1, HEAD_DIM), (1, 2))
                kg2 = jnp.tile(b_k_g[ib].reshape(1, HEAD_DIM), (1, 2))
                qkvn = qknorm_fwd(qkv, qg2, kg2)
                o, r_all = attn_fwd(qkvn)
                y = matmul_nn(o, w_o[ib])
                rec.update(kind='attn', idx=ib, coef=1.0, qkv=qkv, qkvn=qkvn, o=o, r_all=r_all, qg2=qg2, kg2=kg2)
                ib += 1
            else:
                pb, ps = p_b_f[ic].reshape(1, D), p_scale_f[ic].reshape(1, D)
                y, dpool = pool_fwd(h, p_w_f[ic], pb, ps)
                rec.update(kind='pool', idx=ic, coef=1.0, d=dpool, pb=pb, ps=ps)
                ic += 1
            rec['y'] = y
            xs = resid_fwd(xs, y, gate, rec['coef'])
            lay[sub] = rec
        saved.append(lay)

    loss_part, dx = loss_fwd_bwd(xs, loss_target[0])
    loss = lax.psum(loss_part[0, 0], ("x", "y", "c"))

    Fd = w2.shape[2]
    zeros = lambda *s: jnp.zeros(s, F32)
    g_w1t, g_w3t, g_w2 = [[None, None] for _ in range(depth)], [[None, None] for _ in range(depth)], [[None, None] for _ in range(depth)]
    g_w_in_t, g_w_out, g_w_qkv_t, g_w_o, g_p_w = [None] * n_a, [None] * n_a, [None] * n_b, [None] * n_b, [None] * n_c
    g_b_in, g_dw, g_dw_b, g_ln_g, g_ln_b, g_b_out = ([None] * n_a for _ in range(6))
    g_q_g, g_k_g, g_p_b, g_p_scale = [None] * n_b, [None] * n_b, [None] * n_c, [None] * n_c
    d_mod = [[None] * 3 for _ in range(depth)]
    d_norm_g = [[None] * 3 for _ in range(depth)]
    for i in reversed(range(depth)):
        for sub in reversed(range(3)):
            rec = saved[i][sub]
            dy, d_gate, dy_sum = gate_bwd(dx, rec['y'], rec['gate'], rec['coef'])
            if rec['kind'] == 'ffn':
                j = rec['j']
                dh, act, d1, d3 = ffn_bwd(dy, rec['g1'], rec['g3'], w1t[i, j], w3t[i, j], w2[i, j])
                g_w1t[i][j] = matmul_tn(d1, rec['h'])
                g_w3t[i][j] = matmul_tn(d3, rec['h'])
                g_w2[i][j] = matmul_tn(act, dy)
            elif rec['kind'] == 'conv':
                k = rec['idx']
                dsw = matmul_nt(dy, w_out[k])
                g_w_out[k] = matmul_tn(rec['sw'], dy)
                g_b_out[k] = dy_sum
                dpre, g_b_in[k], ddw, g_dw_b[k], g_ln_g[k], g_ln_b[k] = conv_mid_bwd(
                    rec['pre'], rec['v'], dsw, a_dw_f[k], a_ln_g_f[k].reshape(1, D), a_ln_b_f[k].reshape(1, D))
                g_dw[k] = ddw[:CONV_W]
                dh = matmul_nn(dpre, w_in_t[k])
                g_w_in_t[k] = matmul_tn(dpre, rec['h'])
            elif rec['kind'] == 'attn':
                k = rec['idx']
                do = matmul_nt(dy, w_o[k], out_dtype=BF)
                g_w_o[k] = matmul_tn(rec['o'], dy)
                dq, dk, dv = attn_bwd(rec['qkvn'], do, rec['r_all'])
                dqkv, dgains = qknorm_bwd(rec['qkv'], dq, dk, dv, rec['qg2'], rec['kg2'])
                dgains = dgains.reshape(3, N_HEADS, HEAD_DIM)
                g_q_g[k], g_k_g[k] = jnp.sum(dgains[0], axis=0), jnp.sum(dgains[1], axis=0)
                dh = matmul_nn(dqkv, w_qkv_t[k])
                g_w_qkv_t[k] = matmul_tn(dqkv, rec['h'])
            else:
                k = rec['idx']
                dh, dyy, g_p_b[k], g_p_scale[k] = pool_bwd(dy, rec['d'], p_w_f[k], rec['pb'], rec['ps'])
                full = matmul_tn(rec['d'], dyy)
                G = p_w_f.shape[1]
                Dg = D // G
                g_p_w[k] = jnp.stack([full[g * Dg:(g + 1) * Dg, g * Dg:(g + 1) * Dg] for g in range(G)])
            dx, d_shift, d_scale, d_norm_g[i][sub] = mod_bwd(rec['x'], dh, dx, rec['g'], rec['scale'])
            d_mod[i][sub] = jnp.concatenate([d_shift, d_scale, d_gate], axis=0)
    grad_x = dx[None]

    d_mod_mine = jnp.stack([jnp.stack(r) for r in d_mod])
    small_g = [d_mod_mine, jnp.stack([jnp.concatenate(r, axis=0) for r in d_norm_g]),
               jnp.stack(g_b_in), jnp.stack(g_dw), jnp.stack(g_dw_b), jnp.stack(g_ln_g), jnp.stack(g_ln_b), jnp.stack(g_b_out),
               jnp.stack(g_q_g), jnp.stack(g_k_g), jnp.stack(g_p_b), jnp.stack(g_p_scale)]
    stack2 = lambda g: jnp.stack([jnp.stack(r) for r in g])
    blocks = lambda g, ax: g.reshape(g.shape[:ax] + (N_DEV, g.shape[ax] // N_DEV) + g.shape[ax + 1:])
    big_g = [blocks(stack2(g_w1t), 2), blocks(stack2(g_w3t), 2), blocks(stack2(g_w2), 2), blocks(jnp.stack(g_w_in_t), 1),
             blocks(jnp.stack(g_w_out), 1), blocks(jnp.stack(g_w_qkv_t), 1), blocks(jnp.stack(g_w_o), 1), blocks(jnp.stack(g_p_w), 2)]
    big_ax = [2, 2, 2, 1, 1, 1, 1, 2]
    got = exchange(big_g + [_pack(small_g)], big_ax + [None])
    p_w1t, p_w3t, p_w2, p_w_in_t, p_w_out, p_w_qkv_t, p_w_o, p_p_w = got[:8]
    (dmod_all, dng_all, dbin_all, ddw_all, ddwb_all, dlng_all, dlnb_all, dbout_all, dqg_all, dkg_all, dpb_all,
     dps_all) = _unpack(got[8], [v.shape for v in small_g])
    dmod_all = dmod_all.reshape(N_DEV, depth, 9 * D)

    dmod_mine = lax.dynamic_slice_in_dim(dmod_all, me * n_mod, n_mod, axis=2)
    e_t = e_all.T
    grad_ada_w = jnp.stack([outer_rows(e_t, dmod_mine[:, i]) for i in range(depth)])
    de_part = matmul_nt(dmod_mine[:, 0], ada_w[0])
    for i in range(1, depth):
        de_part = de_part + matmul_nt(dmod_mine[:, i], ada_w[i])
    de_parts = exchange([de_part], [None])[0]
    dpre_all, grad_cond_b = cond_embed_bwd(de_parts, pre_all)
    grad_cond_w = outer_rows(c_mine.T, dpre_all)

    def mine(g_all, axis, size):
        return lax.dynamic_slice_in_dim(g_all, me * size, size, axis=axis)

    def upd(name, parts, shape2, to_out=lambda t: t, from_in=lambda t: t):
        w, m, v = (from_in(A[p + name]).reshape(shape2) for p in ('', 'm_', 'v_'))
        outs = adamw(w, parts.reshape((parts.shape[0],) + shape2), m, v)
        return [to_out(o).reshape(A[name].shape) for o in outs]

    L = depth
    res = {}
    res['cond_w'] = upd('cond_w', grad_cond_w[None], (rows, D))
    res['cond_b'] = upd('cond_b', grad_cond_b[None], (1, D))
    res['ada_w'] = upd('ada_w', grad_ada_w[None], (L * D, n_mod))
    res['ada_b'] = upd('ada_b', dmod_all, (L, 9 * D))
    res['norm_g'] = upd('norm_g', mine(dng_all, 3, D // N_DEV), (L * 3, D // N_DEV))
    fs = Fd // N_DEV
    res['ffn_w1'] = upd('ffn_w1', p_w1t, (L * 2 * fs, D), to_out=lambda t: swap(t.reshape(L, 2, fs, D)), from_in=swap)
    res['ffn_w3'] = upd('ffn_w3', p_w3t, (L * 2 * fs, D), to_out=lambda t: swap(t.reshape(L, 2, fs, D)), from_in=swap)
    res['ffn_w2'] = upd('ffn_w2', p_w2, (L * 2 * fs, D))
    ws = 2 * D // N_DEV
    res['a_w_in'] = upd('a_w_in', p_w_in_t, (n_a * ws, D), to_out=lambda t: swap(t.reshape(n_a, ws, D)), from_in=swap)
    res['a_b_in'] = upd('a_b_in', mine(dbin_all.reshape(N_DEV, n_a, 2 * D), 2, ws), (n_a, ws))
    res['a_dw'] = upd('a_dw', mine(ddw_all, 3, rows), (n_a * CONV_W, rows))
    res['a_dw_b'] = upd('a_dw_b', mine(ddwb_all.reshape(N_DEV, n_a, D), 2, rows), (n_a, rows))
    res['a_ln_g'] = upd('a_ln_g', mine(dlng_all.reshape(N_DEV, n_a, D), 2, rows), (n_a, rows))
    res['a_ln_b'] = upd('a_ln_b', mine(dlnb_all.reshape(N_DEV, n_a, D), 2, rows), (n_a, rows))
    res['a_w_out'] = upd('a_w_out', p_w_out, (n_a * rows, D))
    res['a_b_out'] = upd('a_b_out', mine(dbout_all.reshape(N_DEV, n_a, D), 2, rows), (n_a, rows))
    qs = 3 * D // N_DEV
    res['b_w_qkv'] = upd('b_w_qkv', p_w_qkv_t, (n_b * qs, D), to_out=lambda t: swap(t.reshape(n_b, qs, D)), from_in=swap)
    res['b_q_g'] = upd('b_q_g', dqg_all, (n_b, HEAD_DIM))
    res['b_k_g'] = upd('b_k_g', dkg_all, (n_b, HEAD_DIM))
    res['b_w_o'] = upd('b_w_o', p_w_o, (n_b * rows, D))
    G = p_w.shape[1]
    Dg = D // G
    res['p_w'] = upd('p_w', p_p_w, (n_c * G * Dg // N_DEV, Dg))
    res['p_b'] = upd('p_b', mine(dpb_all.reshape(N_DEV, n_c, G, Dg), 3, Dg // N_DEV), (n_c * G, Dg // N_DEV))
    res['p_scale'] = upd('p_scale', mine(dps_all.reshape(N_DEV, n_c, D), 2, rows), (n_c, rows))

    outs = [loss, grad_x]
    for k in range(4):
        outs += [res[n][k] for n in WEIGHTS]
    return tuple(outs)
```

```python
import math

import jax
import jax.numpy as jnp
from jax import lax
from jax.experimental import pallas as pl
from jax.experimental.pallas import tpu as pltpu

F32 = jnp.float32
BF = jnp.bfloat16
EPS = 1e-6
N_DEV = 8
N_HEADS = 16
HEAD_DIM = 64
LANES = 128
CONV_W = 31
CONV_PAD = 32
CONV_ROWS = 128
POOL_WINDOWS = (2, 4, 8, 16)
POOL_PAD = 16
VMEM_LIMIT = 56 * 1024 * 1024
ADAM_LR, ADAM_B1, ADAM_B2, ADAM_EPS, ADAM_WD, ADAM_STEP = 0.001, 0.9, 0.999, 1e-08, 0.01, 10
MESH = pl.DeviceIdType.MESH

WEIGHTS = ['cond_w', 'cond_b', 'ada_w', 'ada_b', 'norm_g', 'ffn_w1', 'ffn_w3', 'ffn_w2', 'a_w_in', 'a_b_in',
           'a_dw', 'a_dw_b', 'a_ln_g', 'a_ln_b', 'a_w_out', 'a_b_out', 'b_w_qkv', 'b_q_g', 'b_k_g', 'b_w_o',
           'p_w', 'p_b', 'p_scale']
ARGS = ['x', 'c'] + WEIGHTS + ['loss_target'] + ['m_' + n for n in WEIGHTS] + ['v_' + n for n in WEIGHTS]


def _pcall(body, **kw):
    return pl.pallas_call(body, **kw)


def _cp(*sem):
    return pltpu.CompilerParams(dimension_semantics=sem if sem else None, vmem_limit_bytes=VMEM_LIMIT)


def _nn(a, b):
    return lax.dot_general(a, b, (((1,), (0,)), ((), ())), preferred_element_type=F32)


def _nt(a, b):
    return lax.dot_general(a, b, (((1,), (1,)), ((), ())), preferred_element_type=F32)


def _tn(a, b):
    return lax.dot_general(a, b, (((0,), (0,)), ((), ())), preferred_element_type=F32)


def _sig(z):
    return 1.0 / (1.0 + jnp.exp(-z))


def _tile(n, pref, unit):
    t = (min(n, pref) // unit) * unit
    while t >= unit:
        if n % t == 0:
            return t
        t -= unit
    return n


def _colsum(v):
    return jnp.sum(v, axis=0, keepdims=True)


def _acc(ref, first, val):
    @pl.when(first)
    def _():
        ref[...] = val

    @pl.when(jnp.logical_not(first))
    def _():
        ref[...] += val


def _sds(shape, dt):
    return jax.ShapeDtypeStruct(shape, dt)


def mod_fwd(x, g, scale, shift):
    S, D = x.shape
    tm = _tile(S, 512, 8)

    def body(x_ref, g_ref, sc_ref, sh_ref, h_ref):
        xv = x_ref[...]
        r = lax.rsqrt(jnp.mean(xv * xv, axis=-1, keepdims=True) + EPS)
        h_ref[...] = (((xv * r) * g_ref[...]) * (1.0 + sc_ref[...]) + sh_ref[...]).astype(h_ref.dtype)

    row = pl.BlockSpec((tm, D), lambda i: (i, 0))
    vec = pl.BlockSpec((1, D), lambda i: (0, 0))
    return _pcall(body, name="mod_fwd", grid=(S // tm,), in_specs=[row, vec, vec, vec], out_specs=row,
                  out_shape=_sds((S, D), BF), compiler_params=_cp("parallel"))(x, g, scale, shift)


def resid_fwd(x, y, gate, coef):
    S, D = x.shape
    tm = _tile(S, 512, 8)

    def body(x_ref, y_ref, g_ref, o_ref):
        o_ref[...] = x_ref[...] + (coef * g_ref[...]) * y_ref[...]

    row = pl.BlockSpec((tm, D), lambda i: (i, 0))
    vec = pl.BlockSpec((1, D), lambda i: (0, 0))
    return _pcall(body, name="resid_fwd", grid=(S // tm,), in_specs=[row, row, vec], out_specs=row,
                  out_shape=_sds((S, D), F32), compiler_params=_cp("parallel"))(x, y, gate)


def gate_bwd(dxo, y, gate, coef):
    S, D = dxo.shape
    tm = _tile(S, 512, 8)

    def body(d_ref, y_ref, g_ref, dy_ref, dg_ref, ds_ref):
        first = pl.program_id(0) == 0
        d = d_ref[...]
        dy = (coef * g_ref[...]) * d
        dy_ref[...] = dy.astype(dy_ref.dtype)
        _acc(dg_ref, first, _colsum(coef * d * y_ref[...]))
        _acc(ds_ref, first, _colsum(dy))

    row = pl.BlockSpec((tm, D), lambda i: (i, 0))
    vec = pl.BlockSpec((1, D), lambda i: (0, 0))
    return _pcall(body, name="gate_bwd", grid=(S // tm,), in_specs=[row, row, vec], out_specs=[row, vec, vec],
                  out_shape=[_sds((S, D), BF), _sds((1, D), F32), _sds((1, D), F32)],
                  compiler_params=_cp("arbitrary"))(dxo, y, gate)


def mod_bwd(x, dh, dxo, g, scale):
    S, D = x.shape
    tm = _tile(S, 512, 8)

    def body(x_ref, dh_ref, do_ref, g_ref, sc_ref, dx_ref, dsh_ref, dsc_ref, dg_ref):
        first = pl.program_id(0) == 0
        xv = x_ref[...]
        dh = dh_ref[...].astype(F32)
        gv = g_ref[...]
        r = lax.rsqrt(jnp.mean(xv * xv, axis=-1, keepdims=True) + EPS)
        xh = xv * r
        dn = dh * (1.0 + sc_ref[...])
        u = dn * gv
        dx_ref[...] = do_ref[...] + r * (u - xh * jnp.mean(u * xh, axis=-1, keepdims=True))
        _acc(dsh_ref, first, _colsum(dh))
        _acc(dsc_ref, first, _colsum(dh * (xh * gv)))
        _acc(dg_ref, first, _colsum(dn * xh))

    row = pl.BlockSpec((tm, D), lambda i: (i, 0))
    vec = pl.BlockSpec((1, D), lambda i: (0, 0))
    return _pcall(body, name="mod_bwd", grid=(S // tm,), in_specs=[row, row, row, vec, vec],
                  out_specs=[row, vec, vec, vec],
                  out_shape=[_sds((S, D), F32), _sds((1, D), F32), _sds((1, D), F32), _sds((1, D), F32)],
                  compiler_params=_cp("arbitrary"))(x, dh, dxo, g, scale)


def loss_fwd_bwd(y, target):
    S, D = y.shape
    tm = _tile(S, 512, 8)

    def body(y_ref, t_ref, l_ref, d_ref):
        first = pl.program_id(0) == 0
        e = y_ref[...] - t_ref[...]
        d_ref[...] = e * (1.0 / D)
        part = 0.5 * jnp.sum(jnp.mean(e * e, axis=-1, keepdims=True), axis=0, keepdims=True)
        _acc(l_ref, first, part)

    row = pl.BlockSpec((tm, D), lambda i: (i, 0))
    return _pcall(body, name="loss", grid=(S // tm,), in_specs=[row, row],
                  out_specs=[pl.BlockSpec((1, 1), lambda i: (0, 0)), row],
                  out_shape=[_sds((1, 1), F32), _sds((S, D), F32)], compiler_params=_cp("arbitrary"))(y, target)


def matmul_nt(x, wt, bias=None, out_dtype=F32):
    S, K = x.shape
    N = wt.shape[0]
    tm, tn = _tile(S, 512, 8), _tile(N, 512, LANES)

    def body(*refs):
        x_ref, w_ref = refs[0], refs[1]
        o_ref = refs[-1]
        r = _nt(x_ref[...].astype(BF), w_ref[...].astype(BF))
        if bias is not None:
            r = r + refs[2][...]
        o_ref[...] = r.astype(o_ref.dtype)

    in_specs = [pl.BlockSpec((tm, K), lambda i, j: (i, 0)), pl.BlockSpec((tn, K), lambda i, j: (j, 0))]
    ops = [x, wt]
    if bias is not None:
        in_specs.append(pl.BlockSpec((1, tn), lambda i, j: (0, j)))
        ops.append(bias)
    return _pcall(body, name="matmul_nt", grid=(S // tm, N // tn), in_specs=in_specs,
                  out_specs=pl.BlockSpec((tm, tn), lambda i, j: (i, j)), out_shape=_sds((S, N), out_dtype),
                  compiler_params=_cp("parallel", "parallel"))(*ops)


def matmul_nn(x, w, bias=None, out_dtype=F32):
    S, K = x.shape
    N = w.shape[1]
    tm, tn = _tile(S, 512, 8), _tile(N, 512, LANES)

    def body(*refs):
        x_ref, w_ref = refs[0], refs[1]
        o_ref = refs[-1]
        r = _nn(x_ref[...].astype(BF), w_ref[...].astype(BF))
        if bias is not None:
            r = r + refs[2][...]
        o_ref[...] = r.astype(o_ref.dtype)

    in_specs = [pl.BlockSpec((tm, K), lambda i, j: (i, 0)), pl.BlockSpec((K, tn), lambda i, j: (0, j))]
    ops = [x, w]
    if bias is not None:
        in_specs.append(pl.BlockSpec((1, tn), lambda i, j: (0, j)))
        ops.append(bias)
    return _pcall(body, name="matmul_nn", grid=(S // tm, N // tn), in_specs=in_specs,
                  out_specs=pl.BlockSpec((tm, tn), lambda i, j: (i, j)), out_shape=_sds((S, N), out_dtype),
                  compiler_params=_cp("parallel", "parallel"))(*ops)


def matmul_tn(a, b, out_dtype=BF):
    S, M = a.shape
    N = b.shape[1]
    bm, bn, tk = _tile(M, 1408, LANES), _tile(N, 1024, LANES), _tile(S, 512, 8)
    nk = S // tk

    def body(a_ref, b_ref, o_ref, acc_ref):
        k = pl.program_id(2)
        _acc(acc_ref, k == 0, _tn(a_ref[...].astype(BF), b_ref[...].astype(BF)))

        @pl.when(k == nk - 1)
        def _():
            o_ref[...] = acc_ref[...].astype(o_ref.dtype)

    return _pcall(body, name="matmul_tn", grid=(M // bm, N // bn, nk),
                  in_specs=[pl.BlockSpec((tk, bm), lambda i, j, k: (k, i)), pl.BlockSpec((tk, bn), lambda i, j, k: (k, j))],
                  out_specs=pl.BlockSpec((bm, bn), lambda i, j, k: (i, j)), out_shape=_sds((M, N), out_dtype),
                  scratch_shapes=[pltpu.VMEM((bm, bn), F32)],
                  compiler_params=_cp("parallel", "parallel", "arbitrary"))(a, b)


def outer_rows(at, b):
    M, R = at.shape
    N = b.shape[1]
    tm = _tile(M, 256, 8)

    def body(a_ref, b_ref, o_ref):
        av, bv = a_ref[...], b_ref[...]
        acc = av[:, 0:1] * bv[0:1, :]
        for r in range(1, R):
            acc = acc + av[:, r:r + 1] * bv[r:r + 1, :]
        o_ref[...] = acc

    return _pcall(body, name="outer_rows", grid=(M // tm,),
                  in_specs=[pl.BlockSpec((tm, R), lambda i: (i, 0)), pl.BlockSpec((R, N), lambda i: (0, 0))],
                  out_specs=pl.BlockSpec((tm, N), lambda i: (i, 0)), out_shape=_sds((M, N), F32),
                  compiler_params=_cp("parallel"))(at, b)


def ffn_fwd(h, w1t, w3t, w2):
    S, D = h.shape
    Fd = w2.shape[0]
    tm, tf = _tile(S, 1024, 8), _tile(Fd, 256, LANES)

    def body(h_ref, w1_ref, w3_ref, w2_ref, y_ref, g1_ref, g3_ref):
        hv = h_ref[...]
        g1 = _nt(hv, w1_ref[...])
        g3 = _nt(hv, w3_ref[...])
        g1_ref[...] = g1.astype(g1_ref.dtype)
        g3_ref[...] = g3.astype(g3_ref.dtype)
        a = (g1 * _sig(g1)) * g3
        _acc(y_ref, pl.program_id(1) == 0, _nn(a.astype(BF), w2_ref[...]))

    row = pl.BlockSpec((tm, D), lambda i, f: (i, 0))
    wsp = pl.BlockSpec((tf, D), lambda i, f: (f, 0))
    act = pl.BlockSpec((tm, tf), lambda i, f: (i, f))
    return _pcall(body, name="ffn_fwd", grid=(S // tm, Fd // tf), in_specs=[row, wsp, wsp, wsp], out_specs=[row, act, act],
                  out_shape=[_sds((S, D), F32), _sds((S, Fd), BF), _sds((S, Fd), BF)],
                  compiler_params=_cp("parallel", "arbitrary"))(h, w1t, w3t, w2)


def ffn_bwd(dy, g1s, g3s, w1t, w3t, w2):
    S, D = dy.shape
    Fd = w2.shape[0]
    tm, tf = _tile(S, 1024, 8), _tile(Fd, 256, LANES)

    def body(dy_ref, g1_ref, g3_ref, w1_ref, w3_ref, w2_ref, dh_ref, a_ref, d1_ref, d3_ref):
        w1, w3 = w1_ref[...], w3_ref[...]
        g1 = g1_ref[...].astype(F32)
        g3 = g3_ref[...].astype(F32)
        s = _sig(g1)
        si = g1 * s
        a_ref[...] = (si * g3).astype(a_ref.dtype)
        da = _nt(dy_ref[...], w2_ref[...])
        d3 = (da * si).astype(BF)
        d1 = (da * g3 * (s * (1.0 + g1 * (1.0 - s)))).astype(BF)
        d1_ref[...] = d1
        d3_ref[...] = d3
        _acc(dh_ref, pl.program_id(1) == 0, _nn(d1, w1) + _nn(d3, w3))

    row = pl.BlockSpec((tm, D), lambda i, f: (i, 0))
    wsp = pl.BlockSpec((tf, D), lambda i, f: (f, 0))
    act = pl.BlockSpec((tm, tf), lambda i, f: (i, f))
    return _pcall(body, name="ffn_bwd", grid=(S // tm, Fd // tf), in_specs=[row, act, act, wsp, wsp, wsp],
                  out_specs=[row, act, act, act],
                  out_shape=[_sds((S, D), F32), _sds((S, Fd), BF), _sds((S, Fd), BF), _sds((S, Fd), BF)],
                  compiler_params=_cp("parallel", "arbitrary"))(dy, g1s, g3s, w1t, w3t, w2)


def _layer_norm_parts(v):
    mu = jnp.mean(v, axis=-1, keepdims=True)
    vc = v - mu
    rs = lax.rsqrt(jnp.mean(vc * vc, axis=-1, keepdims=True) + EPS)
    return vc * rs, rs


def conv_mid_fwd(pre, dw, dw_b, ln_g, ln_b):
    S, D2 = pre.shape
    D = D2 // 2
    tm = _tile(S, 256, CONV_PAD)

    def body(pre_ref, dw_ref, dwb_ref, g_ref, b_ref, v_ref, sw_ref, ubuf):
        @pl.when(pl.program_id(0) == 0)
        def _():
            ubuf[pl.ds(0, CONV_PAD), :] = jnp.zeros((CONV_PAD, D), F32)

        @pl.when(pl.program_id(0) > 0)
        def _():
            ubuf[pl.ds(0, CONV_PAD), :] = ubuf[pl.ds(tm, CONV_PAD), :]

        ubuf[pl.ds(CONV_PAD, tm), :] = pre_ref[:, pl.ds(0, D)] * _sig(pre_ref[:, pl.ds(D, D)])
        for r0 in range(0, tm, CONV_ROWS):
            for c0 in range(0, D, LANES):
                cols = pl.ds(c0, LANES)
                acc = jnp.zeros((CONV_ROWS, LANES), F32) + dwb_ref[:, cols]
                for k in range(CONV_W):
                    acc = acc + dw_ref[pl.ds(k, 1), cols] * ubuf[pl.ds(r0 + CONV_PAD - (CONV_W - 1) + k, CONV_ROWS), cols]
                v_ref[pl.ds(r0, CONV_ROWS), cols] = acc
        vh, _ = _layer_norm_parts(v_ref[...])
        ln = vh * g_ref[...] + b_ref[...]
        sw_ref[...] = (ln * _sig(ln)).astype(sw_ref.dtype)

    row = pl.BlockSpec((tm, D), lambda i: (i, 0))
    vec = pl.BlockSpec((1, D), lambda i: (0, 0))
    return _pcall(body, name="conv_mid_fwd", grid=(S // tm,),
                  in_specs=[pl.BlockSpec((tm, D2), lambda i: (i, 0)), pl.BlockSpec((CONV_PAD, D), lambda i: (0, 0)), vec, vec, vec],
                  out_specs=[row, row], out_shape=[_sds((S, D), F32), _sds((S, D), BF)],
                  scratch_shapes=[pltpu.VMEM((tm + CONV_PAD, D), F32)], compiler_params=_cp("arbitrary"))(pre, dw, dw_b, ln_g, ln_b)


def conv_mid_bwd(pre, v, dsw, dw, ln_g, ln_b):
    S, D2 = pre.shape
    D = D2 // 2
    tm = _tile(S, 256, CONV_PAD)
    nt = S // tm

    def body(pre_ref, v_ref, dsw_ref, dw_ref, g_ref, b_ref, dpre_ref, dbin_ref, ddw_ref, ddwb_ref, dg_ref, db_ref, dvbuf):
        first = pl.program_id(0) == 0

        @pl.when(first)
        def _():
            dvbuf[pl.ds(tm, CONV_PAD), :] = jnp.zeros((CONV_PAD, D), F32)
            ddw_ref[...] = jnp.zeros((CONV_PAD, D), F32)
            dbin_ref[...] = jnp.zeros((1, D2), F32)

        @pl.when(jnp.logical_not(first))
        def _():
            dvbuf[pl.ds(tm, CONV_PAD), :] = dvbuf[pl.ds(0, CONV_PAD), :]

        gv = g_ref[...]
        vh, rs = _layer_norm_parts(v_ref[...])
        ln = vh * gv + b_ref[...]
        sg = _sig(ln)
        dln = dsw_ref[...] * (sg * (1.0 + ln * (1.0 - sg)))
        _acc(dg_ref, first, _colsum(dln * vh))
        _acc(db_ref, first, _colsum(dln))
        dvh = dln * gv
        dv = rs * (dvh - jnp.mean(dvh, axis=-1, keepdims=True) - vh * jnp.mean(dvh * vh, axis=-1, keepdims=True))
        _acc(ddwb_ref, first, _colsum(dv))
        dvbuf[pl.ds(0, tm), :] = dv
        for r0 in range(0, tm, CONV_ROWS):
            rws = pl.ds(r0, CONV_ROWS)
            for c0 in range(0, D, LANES):
                cols, gate_cols = pl.ds(c0, LANES), pl.ds(D + c0, LANES)
                a = pre_ref[rws, cols]
                sb = _sig(pre_ref[rws, gate_cols])
                u = a * sb
                du = jnp.zeros((CONV_ROWS, LANES), F32)
                for k in range(CONV_W):
                    sh = dvbuf[pl.ds(r0 + CONV_W - 1 - k, CONV_ROWS), cols]
                    du = du + dw_ref[pl.ds(k, 1), cols] * sh
                    ddw_ref[pl.ds(k, 1), cols] += _colsum(u * sh)
                da = du * sb
                db = da * a * (1.0 - sb)
                dpre_ref[rws, cols] = da.astype(dpre_ref.dtype)
                dpre_ref[rws, gate_cols] = db.astype(dpre_ref.dtype)
                dbin_ref[:, cols] += _colsum(da)
                dbin_ref[:, gate_cols] += _colsum(db)

    rev = lambda i: (nt - 1 - i, 0)
    row = pl.BlockSpec((tm, D), rev)
    row2 = pl.BlockSpec((tm, D2), rev)
    vec = pl.BlockSpec((1, D), lambda i: (0, 0))
    pad = pl.BlockSpec((CONV_PAD, D), lambda i: (0, 0))
    return _pcall(body, name="conv_mid_bwd", grid=(nt,), in_specs=[row2, row, row, pad, vec, vec],
                  out_specs=[row2, pl.BlockSpec((1, D2), lambda i: (0, 0)), pad, vec, vec, vec],
                  out_shape=[_sds((S, D2), BF), _sds((1, D2), F32), _sds((CONV_PAD, D), F32), _sds((1, D), F32),
                             _sds((1, D), F32), _sds((1, D), F32)],
                  scratch_shapes=[pltpu.VMEM((tm + CONV_PAD, D), F32)], compiler_params=_cp("arbitrary"))(pre, v, dsw, dw, ln_g, ln_b)


def _pool_count(i, tm, w):
    t = i * tm + lax.broadcasted_iota(jnp.int32, (tm, 1), 0)
    return jnp.minimum(t + 1, w).astype(F32)


def pool_fwd(h, p_w, p_b, p_scale):
    S, D = h.shape
    G, Dg = p_w.shape[0], p_w.shape[1]
    tm = _tile(S, 256, POOL_PAD)

    def body(h_ref, w_ref, b_ref, s_ref, y_ref, d_ref, hbuf):
        i = pl.program_id(0)

        @pl.when(i == 0)
        def _():
            hbuf[pl.ds(0, POOL_PAD), :] = jnp.zeros((POOL_PAD, D), F32)

        @pl.when(i > 0)
        def _():
            hbuf[pl.ds(0, POOL_PAD), :] = hbuf[pl.ds(tm, POOL_PAD), :]

        hbuf[pl.ds(POOL_PAD, tm), :] = h_ref[...].astype(F32)
        for g, w in enumerate(POOL_WINDOWS):
            cols = pl.ds(g * Dg, Dg)
            hg = hbuf[pl.ds(POOL_PAD, tm), cols]
            win = hg
            for j in range(1, w):
                win = win + hbuf[pl.ds(POOL_PAD - j, tm), cols]
            d = (win / _pool_count(i, tm, w) - hg).astype(BF)
            d_ref[:, cols] = d
            y_ref[:, cols] = (_nn(d, w_ref[g].astype(BF)) + b_ref[:, cols]) * s_ref[:, cols]

    row = pl.BlockSpec((tm, D), lambda i: (i, 0))
    vec = pl.BlockSpec((1, D), lambda i: (0, 0))
    return _pcall(body, name="pool_fwd", grid=(S // tm,),
                  in_specs=[row, pl.BlockSpec((G, Dg, Dg), lambda i: (0, 0, 0)), vec, vec], out_specs=[row, row],
                  out_shape=[_sds((S, D), F32), _sds((S, D), BF)],
                  scratch_shapes=[pltpu.VMEM((tm + POOL_PAD, D), F32)], compiler_params=_cp("arbitrary"))(h, p_w, p_b, p_scale)


def pool_bwd(dy, d, p_w, p_b, p_scale):
    S, D = dy.shape
    G, Dg = p_w.shape[0], p_w.shape[1]
    tm = _tile(S, 256, POOL_PAD)
    nt = S // tm

    def body(dy_ref, d_ref, w_ref, b_ref, s_ref, dh_ref, dyy_ref, dpb_ref, dps_ref, ebuf):
        j = pl.program_id(0)
        i = nt - 1 - j
        first = j == 0

        @pl.when(first)
        def _():
            ebuf[pl.ds(tm, POOL_PAD), :] = jnp.zeros((POOL_PAD, D), F32)

        @pl.when(jnp.logical_not(first))
        def _():
            ebuf[pl.ds(tm, POOL_PAD), :] = ebuf[pl.ds(0, POOL_PAD), :]

        dpb, dps = [], []
        for g, w in enumerate(POOL_WINDOWS):
            cols = pl.ds(g * Dg, Dg)
            wg = w_ref[g].astype(BF)
            dyg = dy_ref[:, cols].astype(F32)
            ypre = _nn(d_ref[:, cols], wg) + b_ref[:, cols]
            dps.append(_colsum(dyg * ypre))
            dyy = dyg * s_ref[:, cols]
            dpb.append(_colsum(dyy))
            dyy = dyy.astype(BF)
            dyy_ref[:, cols] = dyy
            dd = _nt(dyy, wg)
            ebuf[pl.ds(0, tm), cols] = dd / _pool_count(i, tm, w)
            acc = -dd
            for k in range(w):
                acc = acc + ebuf[pl.ds(k, tm), cols]
            dh_ref[:, cols] = acc
        _acc(dpb_ref, first, jnp.concatenate(dpb, axis=1))
        _acc(dps_ref, first, jnp.concatenate(dps, axis=1))

    row = pl.BlockSpec((tm, D), lambda i: (nt - 1 - i, 0))
    vec = pl.BlockSpec((1, D), lambda i: (0, 0))
    return _pcall(body, name="pool_bwd", grid=(nt,),
                  in_specs=[row, row, pl.BlockSpec((G, Dg, Dg), lambda i: (0, 0, 0)), vec, vec],
                  out_specs=[row, row, vec, vec],
                  out_shape=[_sds((S, D), F32), _sds((S, D), BF), _sds((1, D), F32), _sds((1, D), F32)],
                  scratch_shapes=[pltpu.VMEM((tm + POOL_PAD, D), F32)], compiler_params=_cp("arbitrary"))(dy, d, p_w, p_b, p_scale)


def _group_sum(v, low):
    s_lo = jnp.sum(jnp.where(low, v, 0.0), axis=-1, keepdims=True)
    s_hi = jnp.sum(jnp.where(low, 0.0, v), axis=-1, keepdims=True)
    return jnp.where(low, s_lo, s_hi)


def qknorm_fwd(qkv, q_g2, k_g2):
    S, D3 = qkv.shape
    nb = D3 // LANES // 3
    tm = _tile(S, 512, 8)

    def body(x_ref, qg_ref, kg_ref, o_ref):
        j = pl.program_id(1)
        xv = x_ref[...]
        low = lax.broadcasted_iota(jnp.int32, (1, LANES), 1) < HEAD_DIM
        r = lax.rsqrt(_group_sum(xv * xv, low) * (1.0 / HEAD_DIM) + EPS)
        xn = xv * r
        qn = (xn * qg_ref[...]) * Q_SCALE
        kn = xn * kg_ref[...]
        o_ref[...] = jnp.where(j < nb, qn, jnp.where(j < 2 * nb, kn, xv)).astype(o_ref.dtype)

    blk = pl.BlockSpec((tm, LANES), lambda i, j: (i, j))
    vec = pl.BlockSpec((1, LANES), lambda i, j: (0, 0))
    return _pcall(body, name="qknorm_fwd", grid=(S // tm, D3 // LANES), in_specs=[blk, vec, vec], out_specs=blk,
                  out_shape=_sds((S, D3), BF), compiler_params=_cp("parallel", "parallel"))(qkv, q_g2, k_g2)


def qknorm_bwd(qkv, dq, dk, dv, q_g2, k_g2):
    S, D3 = qkv.shape
    nb = D3 // LANES // 3
    tm = _tile(S, 512, 8)

    def body(x_ref, dq_ref, dk_ref, dv_ref, qg_ref, kg_ref, o_ref, dg_ref):
        j = pl.program_id(0)
        is_q, is_k = j < nb, jnp.logical_and(j >= nb, j < 2 * nb)
        xv = x_ref[...]
        low = lax.broadcasted_iota(jnp.int32, (1, LANES), 1) < HEAD_DIM
        r = lax.rsqrt(_group_sum(xv * xv, low) * (1.0 / HEAD_DIM) + EPS)
        xh = xv * r
        dn = jnp.where(is_q, dq_ref[...] * (Q_SCALE * LN2), dk_ref[...] * LN2)
        gv = jnp.where(is_q, qg_ref[...], kg_ref[...])
        u = dn * gv
        dx = r * (u - xh * (_group_sum(u * xh, low) * (1.0 / HEAD_DIM)))
        o_ref[...] = jnp.where(jnp.logical_or(is_q, is_k), dx, dv_ref[...]).astype(o_ref.dtype)
        _acc(dg_ref, pl.program_id(1) == 0, _colsum(dn * xh))

    blk = pl.BlockSpec((tm, LANES), lambda j, i: (i, j))
    part = lambda lo: pl.BlockSpec((tm, LANES), lambda j, i: (i, jnp.clip(j - lo, 0, nb - 1)))
    vec = pl.BlockSpec((1, LANES), lambda j, i: (0, 0))
    return _pcall(body, name="qknorm_bwd", grid=(D3 // LANES, S // tm),
                  in_specs=[blk, part(0), part(nb), part(2 * nb), vec, vec],
                  out_specs=[blk, pl.BlockSpec((1, LANES), lambda j, i: (0, j))],
                  out_shape=[_sds((S, D3), BF), _sds((1, D3), F32)],
                  compiler_params=_cp("parallel", "arbitrary"))(qkv, dq, dk, dv, q_g2, k_g2)


LOG2E = math.log2(math.e)
LN2 = math.log(2.0)
Q_SCALE = HEAD_DIM ** -0.5 * LOG2E
FAR = 160.0
UNSEEN = 1e30


def _softplus2(z2):
    return jnp.maximum(z2, 0.0) + jnp.log(1.0 + jnp.exp2(-jnp.abs(z2))) * LOG2E


def _attn_tile(S):
    return _tile(S, 256, LANES) if S >= 1024 else LANES


def attn_fwd(qkvn):
    S, D3 = qkvn.shape
    D = D3 // 3
    nb = D // LANES
    t = _attn_tile(S)
    nq = S // t
    assert nq <= LANES

    def body(q_ref, k_ref, v_ref, o_ref, r_ref, acc_ref, rall_ref, behind_ref, z_ref):
        qi = pl.program_id(1)
        lane = lax.broadcasted_iota(jnp.int32, (1, LANES), 1)
        rows = lax.broadcasted_iota(jnp.int32, (t, t), 0)
        cols = lax.broadcasted_iota(jnp.int32, (t, t), 1)
        later = (rows > cols).astype(BF)
        q = q_ref[...]
        qh = [jnp.where(lane < HEAD_DIM, q, jnp.zeros_like(q)), jnp.where(lane >= HEAD_DIM, q, jnp.zeros_like(q))]
        acc_ref[...] = jnp.zeros((2, t, LANES), F32)
        rall_ref[...] = jnp.full((2, t, LANES), UNSEEN, F32)
        behind_ref[...] = jnp.zeros((2, t, LANES), F32)

        def scores(b):
            kb = k_ref[pl.ds(pl.multiple_of(b * t, t), t), :]
            return [_nt(qh[0], kb), _nt(qh[1], kb)]

        def block(b, diagonal):
            vb = v_ref[pl.ds(pl.multiple_of(b * t, t), t), :]
            lsig, sp, within = [None, None], [None, None], [None, None]
            for hh in range(2):
                z = z_ref[hh]
                sp_full = _softplus2(z)
                lsig[hh] = z - sp_full
                sp[hh] = jnp.where(cols < rows, sp_full, 0.0) if diagonal else sp_full
                within[hh] = _nn(sp[hh].astype(BF), later)
            z_next = scores(jnp.maximum(b - 1, 0))
            for hh in range(2):
                behind = behind_ref[hh]
                a = jnp.exp2(lsig[hh] - (within[hh] + jnp.tile(behind, (1, t // LANES))))
                if diagonal:
                    a = jnp.where(cols < rows, a, 0.0)
                acc_ref[hh] += _nn(a.astype(BF), vb)
                rall_ref[hh] = jnp.where(lane == b, behind, rall_ref[hh])
                behind_ref[hh] = behind + jnp.sum(sp[hh], axis=-1, keepdims=True)
            for hh in range(2):
                z_ref[hh] = z_next[hh]

        z_first = scores(qi)
        for hh in range(2):
            z_ref[hh] = z_first[hh]
        block(qi, True)

        def reaches():
            return (jnp.min(behind_ref[...]) < FAR).astype(jnp.int32)

        def step(carry):
            n, _ = carry
            block(qi - 1 - n, False)
            return n + 1, reaches()

        lax.while_loop(lambda carry: jnp.logical_and(carry[0] < qi, carry[1] > 0), step, (jnp.int32(0), reaches()))
        r_ref[...] = rall_ref[...]
        o_ref[...] = jnp.where(lane < HEAD_DIM, acc_ref[0], acc_ref[1]).astype(o_ref.dtype)

    return _pcall(body, name="attn_fwd", grid=(nb, nq),
                  in_specs=[pl.BlockSpec((t, LANES), lambda h, i: (i, h)),
                            pl.BlockSpec((S, LANES), lambda h, i: (0, nb + h)),
                            pl.BlockSpec((S, LANES), lambda h, i: (0, 2 * nb + h))],
                  out_specs=[pl.BlockSpec((t, LANES), lambda h, i: (i, h)), pl.BlockSpec((2, t, LANES), lambda h, i: (h, i, 0))],
                  out_shape=[_sds((S, D), BF), _sds((2 * nb, S, LANES), F32)],
                  scratch_shapes=[pltpu.VMEM((2, t, LANES), F32)] * 3 + [pltpu.VMEM((2, t, t), F32)],
                  compiler_params=_cp("parallel", "arbitrary"))(qkvn, qkvn, qkvn)


def attn_bwd(qkvn, do, r_all):
    S, D3 = qkvn.shape
    D = D3 // 3
    nb = D // LANES
    t = _attn_tile(S)
    nq = S // t

    def body(q_ref, k_ref, v_ref, do_ref, r_ref, dq_ref, dk_hbm, dv_hbm, dk_acc, dv_acc, dq_acc, before_ref, z_ref, da_ref, sem):
        hp = pl.program_id(0)
        qi = pl.program_id(1)

        @pl.when(qi == 0)
        def _():
            dk_acc[...] = jnp.zeros((S, LANES), F32)
            dv_acc[...] = jnp.zeros((S, LANES), F32)

        lane = lax.broadcasted_iota(jnp.int32, (1, LANES), 1)
        rows = lax.broadcasted_iota(jnp.int32, (t, t), 0)
        cols = lax.broadcasted_iota(jnp.int32, (t, t), 1)
        later = (rows > cols).astype(BF)
        earlier = (rows < cols).astype(BF)
        q = q_ref[...]
        dov = do_ref[...]
        low, high = lane < HEAD_DIM, lane >= HEAD_DIM
        qh = [jnp.where(low, q, jnp.zeros_like(q)), jnp.where(high, q, jnp.zeros_like(q))]
        doh = [jnp.where(low, dov, jnp.zeros_like(dov)), jnp.where(high, dov, jnp.zeros_like(dov))]
        q_both = jnp.concatenate(qh, axis=0)
        do_both = jnp.concatenate(doh, axis=0)
        dq_acc[...] = jnp.zeros((2, t, LANES), F32)
        before_ref[...] = jnp.zeros((2, t, LANES), F32)

        def scores(b):
            off = pl.multiple_of(b * t, t)
            kb = k_ref[pl.ds(off, t), :]
            vb = v_ref[pl.ds(off, t), :]
            return [_nt(qh[0], kb), _nt(qh[1], kb)], [_nt(doh[0], vb), _nt(doh[1], vb)]

        def block(b, diagonal):
            off = pl.multiple_of(b * t, t)
            kb = k_ref[pl.ds(off, t), :]
            lsig, sigma, within, g, earlier_g, dz, prob = ([None, None] for _ in range(7))
            for hh in range(2):
                z = z_ref[hh]
                sp = _softplus2(z)
                lsig[hh] = z - sp
                if diagonal:
                    sp = jnp.where(cols < rows, sp, 0.0)
                within[hh] = _nn(sp.astype(BF), later)
            z_next, da_next = scores(jnp.minimum(b + 1, qi))
            for hh in range(2):
                behind = jnp.sum(jnp.where(lane == b, r_ref[hh], 0.0), axis=-1, keepdims=True)
                a = jnp.exp2(lsig[hh] - (within[hh] + behind))
                if diagonal:
                    a = jnp.where(cols < rows, a, 0.0)
                prob[hh] = a.astype(BF)
                g[hh] = a * da_ref[hh]
                earlier_g[hh] = _nn(g[hh].astype(BF), earlier)
            for hh in range(2):
                before = before_ref[hh]
                d = g[hh] - jnp.exp2(lsig[hh]) * (g[hh] + (earlier_g[hh] + jnp.tile(before, (1, t // LANES))))
                if diagonal:
                    d = jnp.where(cols < rows, d, 0.0)
                dz[hh] = d.astype(BF)
                dq_acc[hh] += _nn(dz[hh], kb)
                before_ref[hh] = before + jnp.sum(g[hh], axis=-1, keepdims=True)
            dk_acc[pl.ds(off, t), :] += _tn(jnp.concatenate(dz, axis=0), q_both)
            dv_acc[pl.ds(off, t), :] += _tn(jnp.concatenate(prob, axis=0), do_both)
            for hh in range(2):
                z_ref[hh] = z_next[hh]
                da_ref[hh] = da_next[hh]

        nearest = jnp.min(jnp.minimum(r_ref[0], r_ref[1]), axis=0, keepdims=True)
        skip = jnp.sum(jnp.where(jnp.logical_and(lane < qi, nearest >= FAR), 1, 0))
        z_first, da_first = scores(skip)
        for hh in range(2):
            z_ref[hh] = z_first[hh]
            da_ref[hh] = da_first[hh]

        def step(b, carry):
            block(b, False)
            return carry

        lax.fori_loop(skip, qi, step, 0)
        block(qi, True)
        dq_ref[...] = jnp.where(low, dq_acc[0], dq_acc[1])

        @pl.when(qi == nq - 1)
        def _():
            col = pl.multiple_of(hp * LANES, LANES)
            c1 = pltpu.make_async_copy(dk_acc, dk_hbm.at[:, pl.ds(col, LANES)], sem.at[0])
            c2 = pltpu.make_async_copy(dv_acc, dv_hbm.at[:, pl.ds(col, LANES)], sem.at[1])
            c1.start()
            c2.start()
            c1.wait()
            c2.wait()

    blk = pl.BlockSpec((t, LANES), lambda h, i: (i, h))
    hbm = pl.BlockSpec(memory_space=pl.ANY)
    return _pcall(body, name="attn_bwd", grid=(nb, nq),
                  in_specs=[blk, pl.BlockSpec((S, LANES), lambda h, i: (0, nb + h)),
                            pl.BlockSpec((S, LANES), lambda h, i: (0, 2 * nb + h)), blk,
                            pl.BlockSpec((2, t, LANES), lambda h, i: (h, i, 0))],
                  out_specs=[blk, hbm, hbm],
                  out_shape=[_sds((S, D), F32), _sds((S, D), F32), _sds((S, D), F32)],
                  scratch_shapes=[pltpu.VMEM((S, LANES), F32), pltpu.VMEM((S, LANES), F32), pltpu.VMEM((2, t, LANES), F32),
                                  pltpu.VMEM((2, t, LANES), F32), pltpu.VMEM((2, t, t), F32), pltpu.VMEM((2, t, t), F32),
                                  pltpu.SemaphoreType.DMA((2,))],
                  compiler_params=_cp("arbitrary", "arbitrary"))(qkvn, qkvn, qkvn, do, r_all)


def cond_embed(parts, cond_b):
    P, B, D = parts.shape

    def body(p_ref, b_ref, pre_ref, e_ref):
        pre = p_ref[0]
        for s in range(1, P):
            pre = pre + p_ref[s]
        pre = pre + b_ref[...]
        pre_ref[...] = pre
        e_ref[...] = pre * _sig(pre)

    return _pcall(body, name="cond_embed", out_shape=[_sds((B, D), F32), _sds((B, D), F32)],
                  compiler_params=_cp())(parts, cond_b)


def cond_embed_bwd(parts, pre):
    P, B, D = parts.shape

    def body(p_ref, pre_ref, dpre_ref, db_ref):
        de = p_ref[0]
        for s in range(1, P):
            de = de + p_ref[s]
        pre = pre_ref[...]
        s = _sig(pre)
        dpre = de * (s * (1.0 + pre * (1.0 - s)))
        dpre_ref[...] = dpre
        db_ref[...] = _colsum(dpre)

    return _pcall(body, name="cond_embed_bwd", out_shape=[_sds((B, D), F32), _sds((1, D), F32)],
                  compiler_params=_cp())(parts, pre)


def adamw(w, parts, m, v):
    R, C = w.shape
    P = parts.shape[0]
    tr = _tile(R, max(8, (1 << 17) // C // 8 * 8), 8)
    c1 = 1.0 - ADAM_B1 ** ADAM_STEP
    c2 = 1.0 - ADAM_B2 ** ADAM_STEP

    def body(w_ref, p_ref, m_ref, v_ref, g_ref, d_ref, nm_ref, nv_ref):
        g = p_ref[0].astype(F32)
        for s in range(1, P):
            g = g + p_ref[s].astype(F32)
        nm = ADAM_B1 * m_ref[...] + (1.0 - ADAM_B1) * g
        nv = ADAM_B2 * v_ref[...] + (1.0 - ADAM_B2) * (g * g)
        g_ref[...] = g
        nm_ref[...] = nm
        nv_ref[...] = nv
        d_ref[...] = -ADAM_LR * ((nm / c1) / (jnp.sqrt(nv / c2) + ADAM_EPS) + ADAM_WD * w_ref[...])

    blk = pl.BlockSpec((tr, C), lambda i: (i, 0))
    return _pcall(body, name="adamw", grid=(R // tr,), in_specs=[blk, pl.BlockSpec((P, tr, C), lambda i: (0, i, 0)), blk, blk],
                  out_specs=[blk] * 4, out_shape=[_sds((R, C), F32)] * 4, compiler_params=_cp("parallel"))(w, parts, m, v)


def exchange(arrs, peer_axes):
    n = len(arrs)
    shapes = [a.shape if ax is None else a.shape[:ax] + a.shape[ax + 1:] for a, ax in zip(arrs, peer_axes)]

    def body(*refs):
        ins, outs = refs[:n], refs[n:2 * n]
        send, recv, local = refs[2 * n:]
        x, y, c = lax.axis_index("x"), lax.axis_index("y"), lax.axis_index("c")
        me = 4 * x + 2 * y + c

        def piece(i, d):
            if peer_axes[i] is None:
                return ins[i]
            return ins[i].at[(slice(None),) * peer_axes[i] + (d,)]

        mine = [pltpu.make_async_copy(piece(i, me), outs[i].at[me], local.at[i]) for i in range(n)]
        for cp in mine:
            cp.start()
        sends, recvs = [], []
        for p in range(1, N_DEV):
            peer = (1 - x if p & 4 else x, 1 - y if p & 2 else y, 1 - c if p & 1 else c)
            them = 4 * peer[0] + 2 * peer[1] + peer[2]
            for i in range(n):
                sends.append(pltpu.make_async_remote_copy(piece(i, them), outs[i].at[me], send.at[i, p - 1], recv.at[i, p - 1],
                                                          device_id=peer, device_id_type=MESH))
                recvs.append(pltpu.make_async_remote_copy(piece(i, me), outs[i].at[them], send.at[i, p - 1], recv.at[i, p - 1],
                                                          device_id=peer, device_id_type=MESH))
        for cp in sends:
            cp.start()
        for cp in recvs:
            cp.wait_recv()
        for cp in sends:
            cp.wait_send()
        for cp in mine:
            cp.wait()

    hbm = pl.BlockSpec(memory_space=pl.ANY)
    return _pcall(body, name="exchange", in_specs=[hbm] * n, out_specs=[hbm] * n,
                  out_shape=[_sds((N_DEV,) + s, a.dtype) for s, a in zip(shapes, arrs)],
                  scratch_shapes=[pltpu.SemaphoreType.DMA((n, N_DEV - 1)), pltpu.SemaphoreType.DMA((n, N_DEV - 1)),
                                  pltpu.SemaphoreType.DMA((n,))])(*arrs)


def gather_via_sibling(arrs):
    n = len(arrs)

    def body(*refs):
        ins, outs = refs[:n], refs[n:2 * n]
        send, recv, local = refs[2 * n:]
        x, y, c = lax.axis_index("x"), lax.axis_index("y"), lax.axis_index("c")
        me, sibling = (x, y, c), (x, y, 1 - c)
        chips = [(1 - x, y), (x, 1 - y), (1 - x, 1 - y)]
        index = lambda d: 4 * d[0] + 2 * d[1] + d[2]

        def copy(i, k, block, to, src=None):
            dst = outs[i].at[index(block)]
            return pltpu.make_async_remote_copy(dst if src is None else src, dst, send.at[i, k], recv.at[i, k],
                                                device_id=to, device_id_type=MESH)

        mine = [pltpu.make_async_copy(ins[i], outs[i].at[index(me)], local.at[i]) for i in range(n)]
        for cp in mine:
            cp.start()
        first = []
        for i in range(n):
            first.append(copy(i, 0, me, sibling, src=ins[i]))
            first += [copy(i, 1 + j, me, (*chip, c), src=ins[i]) for j, chip in enumerate(chips)]
        for cp in first:
            cp.start()
        passed = []
        for j, chip in enumerate(chips):
            for i in range(n):
                copy(i, 1 + j, (*chip, c), me).wait_recv()
                passed.append(copy(i, 4 + j, (*chip, c), sibling))
                passed[-1].start()
        for i in range(n):
            copy(i, 0, sibling, me).wait_recv()
            for j, chip in enumerate(chips):
                copy(i, 4 + j, (*chip, 1 - c), me).wait_recv()
        for cp in first + passed:
            cp.wait_send()
        for cp in mine:
            cp.wait()

    hbm = pl.BlockSpec(memory_space=pl.ANY)
    return _pcall(body, name="gather_via_sibling", in_specs=[hbm] * n, out_specs=[hbm] * n,
                  out_shape=[_sds((N_DEV,) + a.shape, a.dtype) for a in arrs],
                  scratch_shapes=[pltpu.SemaphoreType.DMA((n, N_DEV - 1)), pltpu.SemaphoreType.DMA((n, N_DEV - 1)),
                                  pltpu.SemaphoreType.DMA((n,))])(*arrs)


def _pack(vs):
    flat = jnp.concatenate([v.reshape(-1).astype(F32) for v in vs])
    pad = (-flat.shape[0]) % 1024
    return jnp.pad(flat, (0, pad)).reshape(-1, 1024)


def _unpack(packed, shapes):
    flat = packed.reshape(packed.shape[0], -1)
    out, o = [], 0
    for s in shapes:
        n = math.prod(s)
        out.append(flat[:, o:o + n].reshape((packed.shape[0],) + tuple(s)))
        o += n
    return out


def _cat_dev(g, axis):
    g = jnp.moveaxis(g, 0, axis)
    return g.reshape(g.shape[:axis] + (g.shape[axis] * g.shape[axis + 1],) + g.shape[axis + 2:])


def kernel(x, c, cond_w, cond_b, ada_w, ada_b, norm_g, ffn_w1, ffn_w3, ffn_w2, a_w_in, a_b_in, a_dw, a_dw_b, a_ln_g, a_ln_b, a_w_out, a_b_out, b_w_qkv, b_q_g, b_k_g, b_w_o, p_w, p_b, p_scale, loss_target, m_cond_w, m_cond_b, m_ada_w, m_ada_b, m_norm_g, m_ffn_w1, m_ffn_w3, m_ffn_w2, m_a_w_in, m_a_b_in, m_a_dw, m_a_dw_b, m_a_ln_g, m_a_ln_b, m_a_w_out, m_a_b_out, m_b_w_qkv, m_b_q_g, m_b_k_g, m_b_w_o, m_p_w, m_p_b, m_p_scale, v_cond_w, v_cond_b, v_ada_w, v_ada_b, v_norm_g, v_ffn_w1, v_ffn_w3, v_ffn_w2, v_a_w_in, v_a_b_in, v_a_dw, v_a_dw_b, v_a_ln_g, v_a_ln_b, v_a_w_out, v_a_b_out, v_b_w_qkv, v_b_q_g, v_b_k_g, v_b_w_o, v_p_w, v_p_b, v_p_scale):
    A = dict(zip(ARGS, (x, c, cond_w, cond_b, ada_w, ada_b, norm_g, ffn_w1, ffn_w3, ffn_w2, a_w_in, a_b_in, a_dw, a_dw_b, a_ln_g, a_ln_b, a_w_out, a_b_out, b_w_qkv, b_q_g, b_k_g, b_w_o, p_w, p_b, p_scale, loss_target, m_cond_w, m_cond_b, m_ada_w, m_ada_b, m_norm_g, m_ffn_w1, m_ffn_w3, m_ffn_w2, m_a_w_in, m_a_b_in, m_a_dw, m_a_dw_b, m_a_ln_g, m_a_ln_b, m_a_w_out, m_a_b_out, m_b_w_qkv, m_b_q_g, m_b_k_g, m_b_w_o, m_p_w, m_p_b, m_p_scale, v_cond_w, v_cond_b, v_ada_w, v_ada_b, v_norm_g, v_ffn_w1, v_ffn_w3, v_ffn_w2, v_a_w_in, v_a_b_in, v_a_dw, v_a_dw_b, v_a_ln_g, v_a_ln_b, v_a_w_out, v_a_b_out, v_b_w_qkv, v_b_q_g, v_b_k_g, v_b_w_o, v_p_w, v_p_b, v_p_scale)))
    S, D = x.shape[1], x.shape[2]
    depth = ada_w.shape[0]
    n_a, n_b, n_c = a_w_in.shape[0], b_w_qkv.shape[0], p_w.shape[0]
    me = 4 * lax.axis_index("x") + 2 * lax.axis_index("y") + lax.axis_index("c")
    swap = lambda w: jnp.swapaxes(w, -1, -2)

    small_names = ['norm_g', 'a_b_in', 'a_dw', 'a_dw_b', 'a_ln_g', 'a_ln_b', 'a_b_out', 'p_w', 'p_b', 'p_scale']
    small_in = [c] + [A[n] for n in small_names]
    big_in = [swap(ffn_w1).astype(BF), swap(ffn_w3).astype(BF), ffn_w2.astype(BF), swap(a_w_in).astype(BF),
              a_w_out.astype(BF), swap(b_w_qkv).astype(BF), b_w_o.astype(BF)]
    got = gather_via_sibling(big_in + [_pack(small_in)])
    w1t, w3t, w2 = _cat_dev(got[0], 2), _cat_dev(got[1], 2), _cat_dev(got[2], 2)
    w_in_t, w_out = _cat_dev(got[3], 1), _cat_dev(got[4], 1)
    w_qkv_t, w_o = _cat_dev(got[5], 1), _cat_dev(got[6], 1)
    sm = dict(zip(['c'] + small_names, _unpack(got[7], [v.shape for v in small_in])))
    c_all = sm['c'][:, 0]
    norm_g_f = _cat_dev(sm['norm_g'], 2)
    a_b_in_f, a_dw_f, a_dw_b_f = _cat_dev(sm['a_b_in'], 1), _cat_dev(sm['a_dw'], 2), _cat_dev(sm['a_dw_b'], 1)
    a_ln_g_f, a_ln_b_f, a_b_out_f = _cat_dev(sm['a_ln_g'], 1), _cat_dev(sm['a_ln_b'], 1), _cat_dev(sm['a_b_out'], 1)
    p_w_f, p_b_f, p_scale_f = _cat_dev(sm['p_w'], 2), _cat_dev(sm['p_b'], 2), _cat_dev(sm['p_scale'], 1)
    a_dw_f = jnp.pad(a_dw_f, ((0, 0), (0, CONV_PAD - CONV_W), (0, 0)))

    rows = D // N_DEV
    c_mine = lax.dynamic_slice_in_dim(c_all, me * rows, rows, axis=1)
    pre_parts = exchange([matmul_nn(c_mine, cond_w)], [None])[0]
    pre_all, e_all = cond_embed(pre_parts, cond_b.reshape(1, D))
    n_mod = ada_w.shape[2]
    ada_b_mine = lax.dynamic_slice_in_dim(ada_b, me * n_mod, n_mod, axis=1)
    mod_part = jnp.stack([matmul_nn(e_all, ada_w[i], ada_b_mine[i:i + 1]) for i in range(depth)], axis=1)
    mod_all = exchange([mod_part], [None])[0]
    mod = lax.dynamic_index_in_dim(mod_all, me, axis=1, keepdims=False)
    mod = jnp.moveaxis(mod, 0, 1).reshape(depth, 3, 3, 1, D)

    xs = x[0]
    saved = []
    ia = ib = ic = 0
    for i in range(depth):
        lay = {}
        for sub in range(3):
            shift, scale, gate = mod[i, sub, 0], mod[i, sub, 1], 1.0 + mod[i, sub, 2]
            g = norm_g_f[i, sub].reshape(1, D)
            h = mod_fwd(xs, g, scale, shift)
            rec = dict(x=xs, h=h, g=g, scale=scale, gate=gate)
            if sub != 1:
                j = 0 if sub == 0 else 1
                y, g1s, g3s = ffn_fwd(h, w1t[i, j], w3t[i, j], w2[i, j])
                rec.update(kind='ffn', j=j, coef=0.5, g1=g1s, g3=g3s)
            elif i % 3 == 0:
                pre = matmul_nt(h, w_in_t[ia], a_b_in_f[ia].reshape(1, 2 * D))
                v, sw = conv_mid_fwd(pre, a_dw_f[ia], a_dw_b_f[ia].reshape(1, D), a_ln_g_f[ia].reshape(1, D), a_ln_b_f[ia].reshape(1, D))
                y = matmul_nn(sw, w_out[ia], a_b_out_f[ia].reshape(1, D))
                rec.update(kind='conv', idx=ia, coef=1.0, pre=pre, v=v, sw=sw)
                ia += 1
            elif i % 3 == 1:
                qkv = matmul_nt(h, w_qkv_t[ib])
                qg2 = jnp.tile(b_q_g[ib].reshape(1, HEAD_DIM), (1, 2))
                kg2 = jnp.tile(b_k_g[ib].reshape(1, HEAD_DIM), (1, 2))
                qkvn = qknorm_fwd(qkv, qg2, kg2)
                o, r_all = attn_fwd(qkvn)
                y = matmul_nn(o, w_o[ib])
                rec.update(kind='attn', idx=ib, coef=1.0, qkv=qkv, qkvn=qkvn, o=o, r_all=r_all, qg2=qg2, kg2=kg2)
                ib += 1
            else:
                pb, ps = p_b_f[ic].reshape(1, D), p_scale_f[ic].reshape(1, D)
                y, dpool = pool_fwd(h, p_w_f[ic], pb, ps)
                rec.update(kind='pool', idx=ic, coef=1.0, d=dpool, pb=pb, ps=ps)
                ic += 1
            rec['y'] = y
            xs = resid_fwd(xs, y, gate, rec['coef'])
            lay[sub] = rec
        saved.append(lay)

    loss_part, dx = loss_fwd_bwd(xs, loss_target[0])
    loss = lax.psum(loss_part[0, 0], ("x", "y", "c"))

    Fd = w2.shape[2]
    zeros = lambda *s: jnp.zeros(s, F32)
    g_w1t, g_w3t, g_w2 = [[None, None] for _ in range(depth)], [[None, None] for _ in range(depth)], [[None, None] for _ in range(depth)]
    g_w_in_t, g_w_out, g_w_qkv_t, g_w_o, g_p_w = [None] * n_a, [None] * n_a, [None] * n_b, [None] * n_b, [None] * n_c
    g_b_in, g_dw, g_dw_b, g_ln_g, g_ln_b, g_b_out = ([None] * n_a for _ in range(6))
    g_q_g, g_k_g, g_p_b, g_p_scale = [None] * n_b, [None] * n_b, [None] * n_c, [None] * n_c
    d_mod = [[None] * 3 for _ in range(depth)]
    d_norm_g = [[None] * 3 for _ in range(depth)]
    for i in reversed(range(depth)):
        for sub in reversed(range(3)):
            rec = saved[i][sub]
            dy, d_gate, dy_sum = gate_bwd(dx, rec['y'], rec['gate'], rec['coef'])
            if rec['kind'] == 'ffn':
                j = rec['j']
                dh, act, d1, d3 = ffn_bwd(dy, rec['g1'], rec['g3'], w1t[i, j], w3t[i, j], w2[i, j])
                g_w1t[i][j] = matmul_tn(d1, rec['h'])
                g_w3t[i][j] = matmul_tn(d3, rec['h'])
                g_w2[i][j] = matmul_tn(act, dy)
            elif rec['kind'] == 'conv':
                k = rec['idx']
                dsw = matmul_nt(dy, w_out[k])
                g_w_out[k] = matmul_tn(rec['sw'], dy)
                g_b_out[k] = dy_sum
                dpre, g_b_in[k], ddw, g_dw_b[k], g_ln_g[k], g_ln_b[k] = conv_mid_bwd(
                    rec['pre'], rec['v'], dsw, a_dw_f[k], a_ln_g_f[k].reshape(1, D), a_ln_b_f[k].reshape(1, D))
                g_dw[k] = ddw[:CONV_W]
                dh = matmul_nn(dpre, w_in_t[k])
                g_w_in_t[k] = matmul_tn(dpre, rec['h'])
            elif rec['kind'] == 'attn':
                k = rec['idx']
                do = matmul_nt(dy, w_o[k], out_dtype=BF)
                g_w_o[k] = matmul_tn(rec['o'], dy)
                dq, dk, dv = attn_bwd(rec['qkvn'], do, rec['r_all'])
                dqkv, dgains = qknorm_bwd(rec['qkv'], dq, dk, dv, rec['qg2'], rec['kg2'])
                dgains = dgains.reshape(3, N_HEADS, HEAD_DIM)
                g_q_g[k], g_k_g[k] = jnp.sum(dgains[0], axis=0), jnp.sum(dgains[1], axis=0)
                dh = matmul_nn(dqkv, w_qkv_t[k])
                g_w_qkv_t[k] = matmul_tn(dqkv, rec['h'])
            else:
                k = rec['idx']
                dh, dyy, g_p_b[k], g_p_scale[k] = pool_bwd(dy, rec['d'], p_w_f[k], rec['pb'], rec['ps'])
                full = matmul_tn(rec['d'], dyy)
                G = p_w_f.shape[1]
                Dg = D // G
                g_p_w[k] = jnp.stack([full[g * Dg:(g + 1) * Dg, g * Dg:(g + 1) * Dg] for g in range(G)])
            dx, d_shift, d_scale, d_norm_g[i][sub] = mod_bwd(rec['x'], dh, dx, rec['g'], rec['scale'])
            d_mod[i][sub] = jnp.concatenate([d_shift, d_scale, d_gate], axis=0)
    grad_x = dx[None]

    d_mod_mine = jnp.stack([jnp.stack(r) for r in d_mod])
    small_g = [d_mod_mine, jnp.stack([jnp.concatenate(r, axis=0) for r in d_norm_g]),
               jnp.stack(g_b_in), jnp.stack(g_dw), jnp.stack(g_dw_b), jnp.stack(g_ln_g), jnp.stack(g_ln_b), jnp.stack(g_b_out),
               jnp.stack(g_q_g), jnp.stack(g_k_g), jnp.stack(g_p_b), jnp.stack(g_p_scale)]
    stack2 = lambda g: jnp.stack([jnp.stack(r) for r in g])
    blocks = lambda g, ax: g.reshape(g.shape[:ax] + (N_DEV, g.shape[ax] // N_DEV) + g.shape[ax + 1:])
    big_g = [blocks(stack2(g_w1t), 2), blocks(stack2(g_w3t), 2), blocks(stack2(g_w2), 2), blocks(jnp.stack(g_w_in_t), 1),
             blocks(jnp.stack(g_w_out), 1), blocks(jnp.stack(g_w_qkv_t), 1), blocks(jnp.stack(g_w_o), 1), blocks(jnp.stack(g_p_w), 2)]
    big_ax = [2, 2, 2, 1, 1, 1, 1, 2]
    got = exchange(big_g + [_pack(small_g)], big_ax + [None])
    p_w1t, p_w3t, p_w2, p_w_in_t, p_w_out, p_w_qkv_t, p_w_o, p_p_w = got[:8]
    (dmod_all, dng_all, dbin_all, ddw_all, ddwb_all, dlng_all, dlnb_all, dbout_all, dqg_all, dkg_all, dpb_all,
     dps_all) = _unpack(got[8], [v.shape for v in small_g])
    dmod_all = dmod_all.reshape(N_DEV, depth, 9 * D)

    dmod_mine = lax.dynamic_slice_in_dim(dmod_all, me * n_mod, n_mod, axis=2)
    e_t = e_all.T
    grad_ada_w = jnp.stack([outer_rows(e_t, dmod_mine[:, i]) for i in range(depth)])
    de_part = matmul_nt(dmod_mine[:, 0], ada_w[0])
    for i in range(1, depth):
        de_part = de_part + matmul_nt(dmod_mine[:, i], ada_w[i])
    de_parts = exchange([de_part], [None])[0]
    dpre_all, grad_cond_b = cond_embed_bwd(de_parts, pre_all)
    grad_cond_w = outer_rows(c_mine.T, dpre_all)

    def mine(g_all, axis, size):
        return lax.dynamic_slice_in_dim(g_all, me * size, size, axis=axis)

    def upd(name, parts, shape2, to_out=lambda t: t, from_in=lambda t: t):
        w, m, v = (from_in(A[p + name]).reshape(shape2) for p in ('', 'm_', 'v_'))
        outs = adamw(w, parts.reshape((parts.shape[0],) + shape2), m, v)
        return [to_out(o).reshape(A[name].shape) for o in outs]

    L = depth
    res = {}
    res['cond_w'] = upd('cond_w', grad_cond_w[None], (rows, D))
    res['cond_b'] = upd('cond_b', grad_cond_b[None], (1, D))
    res['ada_w'] = upd('ada_w', grad_ada_w[None], (L * D, n_mod))
    res['ada_b'] = upd('ada_b', dmod_all, (L, 9 * D))
    res['norm_g'] = upd('norm_g', mine(dng_all, 3, D // N_DEV), (L * 3, D // N_DEV))
    fs = Fd // N_DEV
    res['ffn_w1'] = upd('ffn_w1', p_w1t, (L * 2 * fs, D), to_out=lambda t: swap(t.reshape(L, 2, fs, D)), from_in=swap)
    res['ffn_w3'] = upd('ffn_w3', p_w3t, (L * 2 * fs, D), to_out=lambda t: swap(t.reshape(L, 2, fs, D)), from_in=swap)
    res['ffn_w2'] = upd('ffn_w2', p_w2, (L * 2 * fs, D))
    ws = 2 * D // N_DEV
    res['a_w_in'] = upd('a_w_in', p_w_in_t, (n_a * ws, D), to_out=lambda t: swap(t.reshape(n_a, ws, D)), from_in=swap)
    res['a_b_in'] = upd('a_b_in', mine(dbin_all.reshape(N_DEV, n_a, 2 * D), 2, ws), (n_a, ws))
    res['a_dw'] = upd('a_dw', mine(ddw_all, 3, rows), (n_a * CONV_W, rows))
    res['a_dw_b'] = upd('a_dw_b', mine(ddwb_all.reshape(N_DEV, n_a, D), 2, rows), (n_a, rows))
    res['a_ln_g'] = upd('a_ln_g', mine(dlng_all.reshape(N_DEV, n_a, D), 2, rows), (n_a, rows))
    res['a_ln_b'] = upd('a_ln_b', mine(dlnb_all.reshape(N_DEV, n_a, D), 2, rows), (n_a, rows))
    res['a_w_out'] = upd('a_w_out', p_w_out, (n_a * rows, D))
    res['a_b_out'] = upd('a_b_out', mine(dbout_all.reshape(N_DEV, n_a, D), 2, rows), (n_a, rows))
    qs = 3 * D // N_DEV
    res['b_w_qkv'] = upd('b_w_qkv', p_w_qkv_t, (n_b * qs, D), to_out=lambda t: swap(t.reshape(n_b, qs, D)), from_in=swap)
    res['b_q_g'] = upd('b_q_g', dqg_all, (n_b, HEAD_DIM))
    res['b_k_g'] = upd('b_k_g', dkg_all, (n_b, HEAD_DIM))
    res['b_w_o'] = upd('b_w_o', p_w_o, (n_b * rows, D))
    G = p_w.shape[1]
    Dg = D // G
    res['p_w'] = upd('p_w', p_p_w, (n_c * G * Dg // N_DEV, Dg))
    res['p_b'] = upd('p_b', mine(dpb_all.reshape(N_DEV, n_c, G, Dg), 3, Dg // N_DEV), (n_c * G, Dg // N_DEV))
    res['p_scale'] = upd('p_scale', mine(dps_all.reshape(N_DEV, n_c, D), 2, rows), (n_c, rows))

    outs = [loss, grad_x]
    for k in range(4):
        outs += [res[n][k] for n in WEIGHTS]
    return tuple(outs)
```

```python
import math

import jax
import jax.numpy as jnp
from jax import lax
from jax.experimental import pallas as pl
from jax.experimental.pallas import tpu as pltpu

F32 = jnp.float32
BF = jnp.bfloat16
EPS = 1e-6
N_DEV = 8
N_HEADS = 16
HEAD_DIM = 64
LANES = 128
CONV_W = 31
CONV_PAD = 32
CONV_ROWS = 128
POOL_WINDOWS = (2, 4, 8, 16)
POOL_PAD = 16
VMEM_LIMIT = 56 * 1024 * 1024
ADAM_LR, ADAM_B1, ADAM_B2, ADAM_EPS, ADAM_WD, ADAM_STEP = 0.001, 0.9, 0.999, 1e-08, 0.01, 10
MESH = pl.DeviceIdType.MESH

WEIGHTS = ['cond_w', 'cond_b', 'ada_w', 'ada_b', 'norm_g', 'ffn_w1', 'ffn_w3', 'ffn_w2', 'a_w_in', 'a_b_in',
           'a_dw', 'a_dw_b', 'a_ln_g', 'a_ln_b', 'a_w_out', 'a_b_out', 'b_w_qkv', 'b_q_g', 'b_k_g', 'b_w_o',
           'p_w', 'p_b', 'p_scale']
ARGS = ['x', 'c'] + WEIGHTS + ['loss_target'] + ['m_' + n for n in WEIGHTS] + ['v_' + n for n in WEIGHTS]


def _pcall(body, **kw):
    return pl.pallas_call(body, **kw)


def _cp(*sem):
    return pltpu.CompilerParams(dimension_semantics=sem if sem else None, vmem_limit_bytes=VMEM_LIMIT)


def _nn(a, b):
    return lax.dot_general(a, b, (((1,), (0,)), ((), ())), preferred_element_type=F32)


def _nt(a, b):
    return lax.dot_general(a, b, (((1,), (1,)), ((), ())), preferred_element_type=F32)


def _tn(a, b):
    return lax.dot_general(a, b, (((0,), (0,)), ((), ())), preferred_element_type=F32)


def _sig(z):
    return 1.0 / (1.0 + jnp.exp(-z))


def _tile(n, pref, unit):
    t = (min(n, pref) // unit) * unit
    while t >= unit:
        if n % t == 0:
            return t
        t -= unit
    return n


def _colsum(v):
    return jnp.sum(v, axis=0, keepdims=True)


def _acc(ref, first, val):
    @pl.when(first)
    def _():
        ref[...] = val

    @pl.when(jnp.logical_not(first))
    def _():
        ref[...] += val


def _sds(shape, dt):
    return jax.ShapeDtypeStruct(shape, dt)


def mod_fwd(x, g, scale, shift):
    S, D = x.shape
    tm = _tile(S, 512, 8)

    def body(x_ref, g_ref, sc_ref, sh_ref, h_ref):
        xv = x_ref[...]
        r = lax.rsqrt(jnp.mean(xv * xv, axis=-1, keepdims=True) + EPS)
        h_ref[...] = (((xv * r) * g_ref[...]) * (1.0 + sc_ref[...]) + sh_ref[...]).astype(h_ref.dtype)

    row = pl.BlockSpec((tm, D), lambda i: (i, 0))
    vec = pl.BlockSpec((1, D), lambda i: (0, 0))
    return _pcall(body, name="mod_fwd", grid=(S // tm,), in_specs=[row, vec, vec, vec], out_specs=row,
                  out_shape=_sds((S, D), BF), compiler_params=_cp("parallel"))(x, g, scale, shift)


def resid_fwd(x, y, gate, coef):
    S, D = x.shape
    tm = _tile(S, 512, 8)

    def body(x_ref, y_ref, g_ref, o_ref):
        o_ref[...] = x_ref[...] + (coef * g_ref[...]) * y_ref[...]

    row = pl.BlockSpec((tm, D), lambda i: (i, 0))
    vec = pl.BlockSpec((1, D), lambda i: (0, 0))
    return _pcall(body, name="resid_fwd", grid=(S // tm,), in_specs=[row, row, vec], out_specs=row,
                  out_shape=_sds((S, D), F32), compiler_params=_cp("parallel"))(x, y, gate)


def gate_bwd(dxo, y, gate, coef):
    S, D = dxo.shape
    tm = _tile(S, 512, 8)

    def body(d_ref, y_ref, g_ref, dy_ref, dg_ref, ds_ref):
        first = pl.program_id(0) == 0
        d = d_ref[...]
        dy = (coef * g_ref[...]) * d
        dy_ref[...] = dy.astype(dy_ref.dtype)
        _acc(dg_ref, first, _colsum(coef * d * y_ref[...]))
        _acc(ds_ref, first, _colsum(dy))

    row = pl.BlockSpec((tm, D), lambda i: (i, 0))
    vec = pl.BlockSpec((1, D), lambda i: (0, 0))
    return _pcall(body, name="gate_bwd", grid=(S // tm,), in_specs=[row, row, vec], out_specs=[row, vec, vec],
                  out_shape=[_sds((S, D), BF), _sds((1, D), F32), _sds((1, D), F32)],
                  compiler_params=_cp("arbitrary"))(dxo, y, gate)


def mod_bwd(x, dh, dxo, g, scale):
    S, D = x.shape
    tm = _tile(S, 512, 8)

    def body(x_ref, dh_ref, do_ref, g_ref, sc_ref, dx_ref, dsh_ref, dsc_ref, dg_ref):
        first = pl.program_id(0) == 0
        xv = x_ref[...]
        dh = dh_ref[...].astype(F32)
        gv = g_ref[...]
        r = lax.rsqrt(jnp.mean(xv * xv, axis=-1, keepdims=True) + EPS)
        xh = xv * r
        dn = dh * (1.0 + sc_ref[...])
        u = dn * gv
        dx_ref[...] = do_ref[...] + r * (u - xh * jnp.mean(u * xh, axis=-1, keepdims=True))
        _acc(dsh_ref, first, _colsum(dh))
        _acc(dsc_ref, first, _colsum(dh * (xh * gv)))
        _acc(dg_ref, first, _colsum(dn * xh))

    row = pl.BlockSpec((tm, D), lambda i: (i, 0))
    vec = pl.BlockSpec((1, D), lambda i: (0, 0))
    return _pcall(body, name="mod_bwd", grid=(S // tm,), in_specs=[row, row, row, vec, vec],
                  out_specs=[row, vec, vec, vec],
                  out_shape=[_sds((S, D), F32), _sds((1, D), F32), _sds((1, D), F32), _sds((1, D), F32)],
                  compiler_params=_cp("arbitrary"))(x, dh, dxo, g, scale)


def loss_fwd_bwd(y, target):
    S, D = y.shape
    tm = _tile(S, 512, 8)

    def body(y_ref, t_ref, l_ref, d_ref):
        first = pl.program_id(0) == 0
        e = y_ref[...] - t_ref[...]
        d_ref[...] = e * (1.0 / D)
        part = 0.5 * jnp.sum(jnp.mean(e * e, axis=-1, keepdims=True), axis=0, keepdims=True)
        _acc(l_ref, first, part)

    row = pl.BlockSpec((tm, D), lambda i: (i, 0))
    return _pcall(body, name="loss", grid=(S // tm,), in_specs=[row, row],
                  out_specs=[pl.BlockSpec((1, 1), lambda i: (0, 0)), row],
                  out_shape=[_sds((1, 1), F32), _sds((S, D), F32)], compiler_params=_cp("arbitrary"))(y, target)


def matmul_nt(x, wt, bias=None, out_dtype=F32):
    S, K = x.shape
    N = wt.shape[0]
    tm, tn = _tile(S, 512, 8), _tile(N, 512, LANES)

    def body(*refs):
        x_ref, w_ref = refs[0], refs[1]
        o_ref = refs[-1]
        r = _nt(x_ref[...].astype(BF), w_ref[...].astype(BF))
        if bias is not None:
            r = r + refs[2][...]
        o_ref[...] = r.astype(o_ref.dtype)

    in_specs = [pl.BlockSpec((tm, K), lambda i, j: (i, 0)), pl.BlockSpec((tn, K), lambda i, j: (j, 0))]
    ops = [x, wt]
    if bias is not None:
        in_specs.append(pl.BlockSpec((1, tn), lambda i, j: (0, j)))
        ops.append(bias)
    return _pcall(body, name="matmul_nt", grid=(S // tm, N // tn), in_specs=in_specs,
                  out_specs=pl.BlockSpec((tm, tn), lambda i, j: (i, j)), out_shape=_sds((S, N), out_dtype),
                  compiler_params=_cp("parallel", "parallel"))(*ops)


def matmul_nn(x, w, bias=None, out_dtype=F32):
    S, K = x.shape
    N = w.shape[1]
    tm, tn = _tile(S, 512, 8), _tile(N, 512, LANES)

    def body(*refs):
        x_ref, w_ref = refs[0], refs[1]
        o_ref = refs[-1]
        r = _nn(x_ref[...].astype(BF), w_ref[...].astype(BF))
        if bias is not None:
            r = r + refs[2][...]
        o_ref[...] = r.astype(o_ref.dtype)

    in_specs = [pl.BlockSpec((tm, K), lambda i, j: (i, 0)), pl.BlockSpec((K, tn), lambda i, j: (0, j))]
    ops = [x, w]
    if bias is not None:
        in_specs.append(pl.BlockSpec((1, tn), lambda i, j: (0, j)))
        ops.append(bias)
    return _pcall(body, name="matmul_nn", grid=(S // tm, N // tn), in_specs=in_specs,
                  out_specs=pl.BlockSpec((tm, tn), lambda i, j: (i, j)), out_shape=_sds((S, N), out_dtype),
                  compiler_params=_cp("parallel", "parallel"))(*ops)


def matmul_tn(a, b, out_dtype=BF):
    S, M = a.shape
    N = b.shape[1]
    bm, bn, tk = _tile(M, 1408, LANES), _tile(N, 1024, LANES), _tile(S, 2048, 8)
    nk = S // tk

    def body(a_ref, b_ref, o_ref, acc_ref):
        k = pl.program_id(2)
        _acc(acc_ref, k == 0, _tn(a_ref[...].astype(BF), b_ref[...].astype(BF)))

        @pl.when(k == nk - 1)
        def _():
            o_ref[...] = acc_ref[...].astype(o_ref.dtype)

    return _pcall(body, name="matmul_tn", grid=(M // bm, N // bn, nk),
                  in_specs=[pl.BlockSpec((tk, bm), lambda i, j, k: (k, i)), pl.BlockSpec((tk, bn), lambda i, j, k: (k, j))],
                  out_specs=pl.BlockSpec((bm, bn), lambda i, j, k: (i, j)), out_shape=_sds((M, N), out_dtype),
                  scratch_shapes=[pltpu.VMEM((bm, bn), F32)],
                  compiler_params=_cp("parallel", "parallel", "arbitrary"))(a, b)


def outer_rows(at, b):
    M, R = at.shape
    N = b.shape[1]
    tm = _tile(M, 256, 8)

    def body(a_ref, b_ref, o_ref):
        av, bv = a_ref[...], b_ref[...]
        acc = av[:, 0:1] * bv[0:1, :]
        for r in range(1, R):
            acc = acc + av[:, r:r + 1] * bv[r:r + 1, :]
        o_ref[...] = acc

    return _pcall(body, name="outer_rows", grid=(M // tm,),
                  in_specs=[pl.BlockSpec((tm, R), lambda i: (i, 0)), pl.BlockSpec((R, N), lambda i: (0, 0))],
                  out_specs=pl.BlockSpec((tm, N), lambda i: (i, 0)), out_shape=_sds((M, N), F32),
                  compiler_params=_cp("parallel"))(at, b)


def ffn_fwd(h, w1t, w3t, w2):
    S, D = h.shape
    Fd = w2.shape[0]
    tm, tf = _tile(S, 1024, 8), _tile(Fd, 256, LANES)

    def body(h_ref, w1_ref, w3_ref, g1_ref, g3_ref, a_ref):
        hv = h_ref[...]
        g1 = _nt(hv, w1_ref[...])
        g3 = _nt(hv, w3_ref[...])
        g1_ref[...] = g1.astype(g1_ref.dtype)
        g3_ref[...] = g3.astype(g3_ref.dtype)
        a_ref[...] = ((g1 * _sig(g1)) * g3).astype(a_ref.dtype)

    row = pl.BlockSpec((tm, D), lambda i, f: (i, 0))
    wsp = pl.BlockSpec((tf, D), lambda i, f: (f, 0))
    act = pl.BlockSpec((tm, tf), lambda i, f: (i, f))
    g1s, g3s, a = _pcall(body, name="ffn_fwd", grid=(S // tm, Fd // tf), in_specs=[row, wsp, wsp], out_specs=[act, act, act],
                         out_shape=[_sds((S, Fd), BF)] * 3, compiler_params=_cp("parallel", "parallel"))(h, w1t, w3t)

    tr, tn = _tile(S, 512, 8), _tile(D, 512, LANES)

    def y_body(a_ref, w2_ref, y_ref):
        y_ref[...] = _nn(a_ref[...], w2_ref[...])

    y = _pcall(y_body, name="ffn_out", grid=(S // tr, D // tn),
               in_specs=[pl.BlockSpec((tr, Fd), lambda i, j: (i, 0)), pl.BlockSpec((Fd, tn), lambda i, j: (0, j))],
               out_specs=pl.BlockSpec((tr, tn), lambda i, j: (i, j)), out_shape=_sds((S, D), F32),
               compiler_params=_cp("parallel", "parallel"))(a, w2)
    return y, g1s, g3s, a


def ffn_bwd(dy, g1s, g3s, w1t, w3t, w2):
    S, D = dy.shape
    Fd = w2.shape[0]
    tm, tf = _tile(S, 1024, 16), _tile(Fd, 256, LANES)
    halves = [pl.ds(0, tm // 2), pl.ds(tm // 2, tm // 2)]

    def body(dy_ref, g1_ref, g3_ref, w2_ref, d1_ref, d3_ref):
        w2v = w2_ref[...]
        das = [_nt(dy_ref[rows, :], w2v) for rows in halves]
        for rows, da in zip(halves, das):
            g1 = g1_ref[rows, :].astype(F32)
            g3 = g3_ref[rows, :].astype(F32)
            s = _sig(g1)
            d3_ref[rows, :] = (da * (g1 * s)).astype(d3_ref.dtype)
            d1_ref[rows, :] = (da * g3 * (s * (1.0 + g1 * (1.0 - s)))).astype(d1_ref.dtype)

    row = pl.BlockSpec((tm, D), lambda i, f: (i, 0))
    act = pl.BlockSpec((tm, tf), lambda i, f: (i, f))
    d1, d3 = _pcall(body, name="ffn_bwd", grid=(S // tm, Fd // tf),
                    in_specs=[row, act, act, pl.BlockSpec((tf, D), lambda i, f: (f, 0))], out_specs=[act, act],
                    out_shape=[_sds((S, Fd), BF)] * 2, compiler_params=_cp("parallel", "parallel"))(dy, g1s, g3s, w2)

    tr, tn = _tile(S, 512, 8), _tile(D, 512, LANES)

    def dh_body(d1_ref, d3_ref, w1_ref, w3_ref, dh_ref):
        dh_ref[...] = _nn(d1_ref[...], w1_ref[...]) + _nn(d3_ref[...], w3_ref[...])

    wide = pl.BlockSpec((tr, Fd), lambda i, j: (i, 0))
    wcol = pl.BlockSpec((Fd, tn), lambda i, j: (0, j))
    dh = _pcall(dh_body, name="ffn_dh", grid=(S // tr, D // tn), in_specs=[wide, wide, wcol, wcol],
                out_specs=pl.BlockSpec((tr, tn), lambda i, j: (i, j)), out_shape=_sds((S, D), F32),
                compiler_params=_cp("parallel", "parallel"))(d1, d3, w1t, w3t)
    return dh, d1, d3


def _layer_norm_parts(v):
    mu = jnp.mean(v, axis=-1, keepdims=True)
    vc = v - mu
    rs = lax.rsqrt(jnp.mean(vc * vc, axis=-1, keepdims=True) + EPS)
    return vc * rs, rs


def conv_mid_fwd(pre, dw, dw_b, ln_g, ln_b):
    S, D2 = pre.shape
    D = D2 // 2
    tm = _tile(S, 256, CONV_PAD)

    def body(pre_ref, dw_ref, dwb_ref, g_ref, b_ref, v_ref, sw_ref, ubuf):
        @pl.when(pl.program_id(0) == 0)
        def _():
            ubuf[pl.ds(0, CONV_PAD), :] = jnp.zeros((CONV_PAD, D), F32)

        @pl.when(pl.program_id(0) > 0)
        def _():
            ubuf[pl.ds(0, CONV_PAD), :] = ubuf[pl.ds(tm, CONV_PAD), :]

        ubuf[pl.ds(CONV_PAD, tm), :] = pre_ref[:, pl.ds(0, D)] * _sig(pre_ref[:, pl.ds(D, D)])
        for r0 in range(0, tm, CONV_ROWS):
            for c0 in range(0, D, LANES):
                cols = pl.ds(c0, LANES)
                acc = jnp.zeros((CONV_ROWS, LANES), F32) + dwb_ref[:, cols]
                for k in range(CONV_W):
                    acc = acc + dw_ref[pl.ds(k, 1), cols] * ubuf[pl.ds(r0 + CONV_PAD - (CONV_W - 1) + k, CONV_ROWS), cols]
                v_ref[pl.ds(r0, CONV_ROWS), cols] = acc
        vh, _ = _layer_norm_parts(v_ref[...])
        ln = vh * g_ref[...] + b_ref[...]
        sw_ref[...] = (ln * _sig(ln)).astype(sw_ref.dtype)

    row = pl.BlockSpec((tm, D), lambda i: (i, 0))
    vec = pl.BlockSpec((1, D), lambda i: (0, 0))
    return _pcall(body, name="conv_mid_fwd", grid=(S // tm,),
                  in_specs=[pl.BlockSpec((tm, D2), lambda i: (i, 0)), pl.BlockSpec((CONV_PAD, D), lambda i: (0, 0)), vec, vec, vec],
                  out_specs=[row, row], out_shape=[_sds((S, D), F32), _sds((S, D), BF)],
                  scratch_shapes=[pltpu.VMEM((tm + CONV_PAD, D), F32)], compiler_params=_cp("arbitrary"))(pre, dw, dw_b, ln_g, ln_b)


def conv_mid_bwd(pre, v, dsw, dw, ln_g, ln_b):
    S, D2 = pre.shape
    D = D2 // 2
    tm = _tile(S, 256, CONV_PAD)
    nt = S // tm

    def body(pre_ref, v_ref, dsw_ref, dw_ref, g_ref, b_ref, dpre_ref, dbin_ref, ddw_ref, ddwb_ref, dg_ref, db_ref, dvbuf):
        first = pl.program_id(0) == 0

        @pl.when(first)
        def _():
            dvbuf[pl.ds(tm, CONV_PAD), :] = jnp.zeros((CONV_PAD, D), F32)
            ddw_ref[...] = jnp.zeros((CONV_PAD, D), F32)
            dbin_ref[...] = jnp.zeros((1, D2), F32)

        @pl.when(jnp.logical_not(first))
        def _():
            dvbuf[pl.ds(tm, CONV_PAD), :] = dvbuf[pl.ds(0, CONV_PAD), :]

        gv = g_ref[...]
        vh, rs = _layer_norm_parts(v_ref[...])
        ln = vh * gv + b_ref[...]
        sg = _sig(ln)
        dln = dsw_ref[...] * (sg * (1.0 + ln * (1.0 - sg)))
        _acc(dg_ref, first, _colsum(dln * vh))
        _acc(db_ref, first, _colsum(dln))
        dvh = dln * gv
        dv = rs * (dvh - jnp.mean(dvh, axis=-1, keepdims=True) - vh * jnp.mean(dvh * vh, axis=-1, keepdims=True))
        _acc(ddwb_ref, first, _colsum(dv))
        dvbuf[pl.ds(0, tm), :] = dv
        for r0 in range(0, tm, CONV_ROWS):
            rws = pl.ds(r0, CONV_ROWS)
            for c0 in range(0, D, LANES):
                cols, gate_cols = pl.ds(c0, LANES), pl.ds(D + c0, LANES)
                a = pre_ref[rws, cols]
                sb = _sig(pre_ref[rws, gate_cols])
                u = a * sb
                du = jnp.zeros((CONV_ROWS, LANES), F32)
                for k in range(CONV_W):
                    sh = dvbuf[pl.ds(r0 + CONV_W - 1 - k, CONV_ROWS), cols]
                    du = du + dw_ref[pl.ds(k, 1), cols] * sh
                    ddw_ref[pl.ds(k, 1), cols] += _colsum(u * sh)
                da = du * sb
                db = da * a * (1.0 - sb)
                dpre_ref[rws, cols] = da.astype(dpre_ref.dtype)
                dpre_ref[rws, gate_cols] = db.astype(dpre_ref.dtype)
                dbin_ref[:, cols] += _colsum(da)
                dbin_ref[:, gate_cols] += _colsum(db)

    rev = lambda i: (nt - 1 - i, 0)
    row = pl.BlockSpec((tm, D), rev)
    row2 = pl.BlockSpec((tm, D2), rev)
    vec = pl.BlockSpec((1, D), lambda i: (0, 0))
    pad = pl.BlockSpec((CONV_PAD, D), lambda i: (0, 0))
    return _pcall(body, name="conv_mid_bwd", grid=(nt,), in_specs=[row2, row, row, pad, vec, vec],
                  out_specs=[row2, pl.BlockSpec((1, D2), lambda i: (0, 0)), pad, vec, vec, vec],
                  out_shape=[_sds((S, D2), BF), _sds((1, D2), F32), _sds((CONV_PAD, D), F32), _sds((1, D), F32),
                             _sds((1, D), F32), _sds((1, D), F32)],
                  scratch_shapes=[pltpu.VMEM((tm + CONV_PAD, D), F32)], compiler_params=_cp("arbitrary"))(pre, v, dsw, dw, ln_g, ln_b)


def _pool_count(i, tm, w):
    t = i * tm + lax.broadcasted_iota(jnp.int32, (tm, 1), 0)
    return jnp.minimum(t + 1, w).astype(F32)


def pool_fwd(h, p_w, p_b, p_scale):
    S, D = h.shape
    G, Dg = p_w.shape[0], p_w.shape[1]
    tm = _tile(S, 256, POOL_PAD)

    def body(h_ref, w_ref, b_ref, s_ref, y_ref, d_ref, hbuf):
        i = pl.program_id(0)

        @pl.when(i == 0)
        def _():
            hbuf[pl.ds(0, POOL_PAD), :] = jnp.zeros((POOL_PAD, D), F32)

        @pl.when(i > 0)
        def _():
            hbuf[pl.ds(0, POOL_PAD), :] = hbuf[pl.ds(tm, POOL_PAD), :]

        hbuf[pl.ds(POOL_PAD, tm), :] = h_ref[...].astype(F32)
        for g, w in enumerate(POOL_WINDOWS):
            cols = pl.ds(g * Dg, Dg)
            hg = hbuf[pl.ds(POOL_PAD, tm), cols]
            win = hg
            for j in range(1, w):
                win = win + hbuf[pl.ds(POOL_PAD - j, tm), cols]
            d = (win / _pool_count(i, tm, w) - hg).astype(BF)
            d_ref[:, cols] = d
            y_ref[:, cols] = (_nn(d, w_ref[g].astype(BF)) + b_ref[:, cols]) * s_ref[:, cols]

    row = pl.BlockSpec((tm, D), lambda i: (i, 0))
    vec = pl.BlockSpec((1, D), lambda i: (0, 0))
    return _pcall(body, name="pool_fwd", grid=(S // tm,),
                  in_specs=[row, pl.BlockSpec((G, Dg, Dg), lambda i: (0, 0, 0)), vec, vec], out_specs=[row, row],
                  out_shape=[_sds((S, D), F32), _sds((S, D), BF)],
                  scratch_shapes=[pltpu.VMEM((tm + POOL_PAD, D), F32)], compiler_params=_cp("arbitrary"))(h, p_w, p_b, p_scale)


def pool_bwd(dy, d, p_w, p_b, p_scale):
    S, D = dy.shape
    G, Dg = p_w.shape[0], p_w.shape[1]
    tm = _tile(S, 256, POOL_PAD)
    nt = S // tm

    def body(dy_ref, d_ref, w_ref, b_ref, s_ref, dh_ref, dyy_ref, dpb_ref, dps_ref, ebuf):
        j = pl.program_id(0)
        i = nt - 1 - j
        first = j == 0

        @pl.when(first)
        def _():
            ebuf[pl.ds(tm, POOL_PAD), :] = jnp.zeros((POOL_PAD, D), F32)

        @pl.when(jnp.logical_not(first))
        def _():
            ebuf[pl.ds(tm, POOL_PAD), :] = ebuf[pl.ds(0, POOL_PAD), :]

        dpb, dps = [], []
        for g, w in enumerate(POOL_WINDOWS):
            cols = pl.ds(g * Dg, Dg)
            wg = w_ref[g].astype(BF)
            dyg = dy_ref[:, cols].astype(F32)
            ypre = _nn(d_ref[:, cols], wg) + b_ref[:, cols]
            dps.append(_colsum(dyg * ypre))
            dyy = dyg * s_ref[:, cols]
            dpb.append(_colsum(dyy))
            dyy = dyy.astype(BF)
            dyy_ref[:, cols] = dyy
            dd = _nt(dyy, wg)
            ebuf[pl.ds(0, tm), cols] = dd / _pool_count(i, tm, w)
            acc = -dd
            for k in range(w):
                acc = acc + ebuf[pl.ds(k, tm), cols]
            dh_ref[:, cols] = acc
        _acc(dpb_ref, first, jnp.concatenate(dpb, axis=1))
        _acc(dps_ref, first, jnp.concatenate(dps, axis=1))

    row = pl.BlockSpec((tm, D), lambda i: (nt - 1 - i, 0))
    vec = pl.BlockSpec((1, D), lambda i: (0, 0))
    return _pcall(body, name="pool_bwd", grid=(nt,),
                  in_specs=[row, row, pl.BlockSpec((G, Dg, Dg), lambda i: (0, 0, 0)), vec, vec],
                  out_specs=[row, row, vec, vec],
                  out_shape=[_sds((S, D), F32), _sds((S, D), BF), _sds((1, D), F32), _sds((1, D), F32)],
                  scratch_shapes=[pltpu.VMEM((tm + POOL_PAD, D), F32)], compiler_params=_cp("arbitrary"))(dy, d, p_w, p_b, p_scale)


def _group_sum(v, low):
    s_lo = jnp.sum(jnp.where(low, v, 0.0), axis=-1, keepdims=True)
    s_hi = jnp.sum(jnp.where(low, 0.0, v), axis=-1, keepdims=True)
    return jnp.where(low, s_lo, s_hi)


def qknorm_fwd(qkv, q_g2, k_g2):
    S, D3 = qkv.shape
    nb = D3 // LANES // 3
    tm = _tile(S, 512, 8)

    def body(x_ref, qg_ref, kg_ref, o_ref):
        j = pl.program_id(1)
        xv = x_ref[...]
        low = lax.broadcasted_iota(jnp.int32, (1, LANES), 1) < HEAD_DIM
        r = lax.rsqrt(_group_sum(xv * xv, low) * (1.0 / HEAD_DIM) + EPS)
        xn = xv * r
        qn = (xn * qg_ref[...]) * Q_SCALE
        kn = xn * kg_ref[...]
        o_ref[...] = jnp.where(j < nb, qn, jnp.where(j < 2 * nb, kn, xv)).astype(o_ref.dtype)

    blk = pl.BlockSpec((tm, LANES), lambda i, j: (i, j))
    vec = pl.BlockSpec((1, LANES), lambda i, j: (0, 0))
    return _pcall(body, name="qknorm_fwd", grid=(S // tm, D3 // LANES), in_specs=[blk, vec, vec], out_specs=blk,
                  out_shape=_sds((S, D3), BF), compiler_params=_cp("parallel", "parallel"))(qkv, q_g2, k_g2)


def qknorm_bwd(qkv, dq, dk, dv, q_g2, k_g2):
    S, D3 = qkv.shape
    nb = D3 // LANES // 3
    tm = _tile(S, 512, 8)

    def body(x_ref, dq_ref, dk_ref, dv_ref, qg_ref, kg_ref, o_ref, dg_ref):
        j = pl.program_id(0)
        is_q, is_k = j < nb, jnp.logical_and(j >= nb, j < 2 * nb)
        xv = x_ref[...]
        low = lax.broadcasted_iota(jnp.int32, (1, LANES), 1) < HEAD_DIM
        r = lax.rsqrt(_group_sum(xv * xv, low) * (1.0 / HEAD_DIM) + EPS)
        xh = xv * r
        dn = jnp.where(is_q, dq_ref[...] * (Q_SCALE * LN2), dk_ref[...] * LN2)
        gv = jnp.where(is_q, qg_ref[...], kg_ref[...])
        u = dn * gv
        dx = r * (u - xh * (_group_sum(u * xh, low) * (1.0 / HEAD_DIM)))
        o_ref[...] = jnp.where(jnp.logical_or(is_q, is_k), dx, dv_ref[...]).astype(o_ref.dtype)
        _acc(dg_ref, pl.program_id(1) == 0, _colsum(dn * xh))

    blk = pl.BlockSpec((tm, LANES), lambda j, i: (i, j))
    part = lambda lo: pl.BlockSpec((tm, LANES), lambda j, i: (i, jnp.clip(j - lo, 0, nb - 1)))
    vec = pl.BlockSpec((1, LANES), lambda j, i: (0, 0))
    return _pcall(body, name="qknorm_bwd", grid=(D3 // LANES, S // tm),
                  in_specs=[blk, part(0), part(nb), part(2 * nb), vec, vec],
                  out_specs=[blk, pl.BlockSpec((1, LANES), lambda j, i: (0, j))],
                  out_shape=[_sds((S, D3), BF), _sds((1, D3), F32)],
                  compiler_params=_cp("parallel", "arbitrary"))(qkv, dq, dk, dv, q_g2, k_g2)


LOG2E = math.log2(math.e)
LN2 = math.log(2.0)
Q_SCALE = HEAD_DIM ** -0.5 * LOG2E
FAR = 160.0
UNSEEN = 1e30


def _softplus2(z2):
    return jnp.maximum(z2, 0.0) + jnp.log(1.0 + jnp.exp2(-jnp.abs(z2))) * LOG2E


def _attn_tile(S):
    return _tile(S, 256, LANES) if S >= 1024 else LANES


def attn_fwd(qkvn):
    S, D3 = qkvn.shape
    D = D3 // 3
    nb = D // LANES
    t = _attn_tile(S)
    nq = S // t
    assert nq <= LANES

    def body(q_ref, k_ref, v_ref, o_ref, r_ref, acc_ref, rall_ref, behind_ref, z_ref):
        qi = pl.program_id(1)
        lane = lax.broadcasted_iota(jnp.int32, (1, LANES), 1)
        rows = lax.broadcasted_iota(jnp.int32, (t, t), 0)
        cols = lax.broadcasted_iota(jnp.int32, (t, t), 1)
        later = (rows > cols).astype(BF)
        q = q_ref[...]
        qh = [jnp.where(lane < HEAD_DIM, q, jnp.zeros_like(q)), jnp.where(lane >= HEAD_DIM, q, jnp.zeros_like(q))]
        acc_ref[...] = jnp.zeros((2, t, LANES), F32)
        rall_ref[...] = jnp.full((2, t, LANES), UNSEEN, F32)
        behind_ref[...] = jnp.zeros((2, t, LANES), F32)

        def scores(b):
            kb = k_ref[pl.ds(pl.multiple_of(b * t, t), t), :]
            return [_nt(qh[0], kb), _nt(qh[1], kb)]

        def block(b, diagonal):
            vb = v_ref[pl.ds(pl.multiple_of(b * t, t), t), :]
            lsig, sp, within = [None, None], [None, None], [None, None]
            for hh in range(2):
                z = z_ref[hh]
                sp_full = _softplus2(z)
                lsig[hh] = z - sp_full
                sp[hh] = jnp.where(cols < rows, sp_full, 0.0) if diagonal else sp_full
                within[hh] = _nn(sp[hh].astype(BF), later)
            z_next = scores(jnp.maximum(b - 1, 0))
            for hh in range(2):
                behind = behind_ref[hh]
                a = jnp.exp2(lsig[hh] - (within[hh] + jnp.tile(behind, (1, t // LANES))))
                if diagonal:
                    a = jnp.where(cols < rows, a, 0.0)
                acc_ref[hh] += _nn(a.astype(BF), vb)
                rall_ref[hh] = jnp.where(lane == b, behind, rall_ref[hh])
                behind_ref[hh] = behind + jnp.sum(sp[hh], axis=-1, keepdims=True)
            for hh in range(2):
                z_ref[hh] = z_next[hh]

        z_first = scores(qi)
        for hh in range(2):
            z_ref[hh] = z_first[hh]
        block(qi, True)

        def reaches():
            return (jnp.min(behind_ref[...]) < FAR).astype(jnp.int32)

        def step(carry):
            n, _ = carry
            block(qi - 1 - n, False)
            return n + 1, reaches()

        lax.while_loop(lambda carry: jnp.logical_and(carry[0] < qi, carry[1] > 0), step, (jnp.int32(0), reaches()))
        r_ref[...] = rall_ref[...]
        o_ref[...] = jnp.where(lane < HEAD_DIM, acc_ref[0], acc_ref[1]).astype(o_ref.dtype)

    return _pcall(body, name="attn_fwd", grid=(nb, nq),
                  in_specs=[pl.BlockSpec((t, LANES), lambda h, i: (i, h)),
                            pl.BlockSpec((S, LANES), lambda h, i: (0, nb + h)),
                            pl.BlockSpec((S, LANES), lambda h, i: (0, 2 * nb + h))],
                  out_specs=[pl.BlockSpec((t, LANES), lambda h, i: (i, h)), pl.BlockSpec((2, t, LANES), lambda h, i: (h, i, 0))],
                  out_shape=[_sds((S, D), BF), _sds((2 * nb, S, LANES), F32)],
                  scratch_shapes=[pltpu.VMEM((2, t, LANES), F32)] * 3 + [pltpu.VMEM((2, t, t), F32)],
                  compiler_params=_cp("parallel", "arbitrary"))(qkvn, qkvn, qkvn)


def attn_bwd(qkvn, do, r_all):
    S, D3 = qkvn.shape
    D = D3 // 3
    nb = D // LANES
    t = _attn_tile(S)
    nq = S // t

    def body(q_ref, k_ref, v_ref, do_ref, r_ref, dq_ref, dk_hbm, dv_hbm, dk_acc, dv_acc, dq_acc, before_ref, z_ref, da_ref, sem):
        hp = pl.program_id(0)
        qi = pl.program_id(1)

        @pl.when(qi == 0)
        def _():
            dk_acc[...] = jnp.zeros((S, LANES), F32)
            dv_acc[...] = jnp.zeros((S, LANES), F32)

        lane = lax.broadcasted_iota(jnp.int32, (1, LANES), 1)
        rows = lax.broadcasted_iota(jnp.int32, (t, t), 0)
        cols = lax.broadcasted_iota(jnp.int32, (t, t), 1)
        later = (rows > cols).astype(BF)
        earlier = (rows < cols).astype(BF)
        q = q_ref[...]
        dov = do_ref[...]
        low, high = lane < HEAD_DIM, lane >= HEAD_DIM
        qh = [jnp.where(low, q, jnp.zeros_like(q)), jnp.where(high, q, jnp.zeros_like(q))]
        doh = [jnp.where(low, dov, jnp.zeros_like(dov)), jnp.where(high, dov, jnp.zeros_like(dov))]
        q_both = jnp.concatenate(qh, axis=0)
        do_both = jnp.concatenate(doh, axis=0)
        dq_acc[...] = jnp.zeros((2, t, LANES), F32)
        before_ref[...] = jnp.zeros((2, t, LANES), F32)

        def scores(b):
            off = pl.multiple_of(b * t, t)
            kb = k_ref[pl.ds(off, t), :]
            vb = v_ref[pl.ds(off, t), :]
            return [_nt(qh[0], kb), _nt(qh[1], kb)], [_nt(doh[0], vb), _nt(doh[1], vb)]

        def block(b, diagonal):
            off = pl.multiple_of(b * t, t)
            kb = k_ref[pl.ds(off, t), :]
            lsig, sigma, within, g, earlier_g, dz, prob = ([None, None] for _ in range(7))
            for hh in range(2):
                z = z_ref[hh]
                sp = _softplus2(z)
                lsig[hh] = z - sp
                if diagonal:
                    sp = jnp.where(cols < rows, sp, 0.0)
                within[hh] = _nn(sp.astype(BF), later)
            z_next, da_next = scores(jnp.minimum(b + 1, qi))
            for hh in range(2):
                behind = jnp.sum(jnp.where(lane == b, r_ref[hh], 0.0), axis=-1, keepdims=True)
                a = jnp.exp2(lsig[hh] - (within[hh] + behind))
                if diagonal:
                    a = jnp.where(cols < rows, a, 0.0)
                prob[hh] = a.astype(BF)
                g[hh] = a * da_ref[hh]
                earlier_g[hh] = _nn(g[hh].astype(BF), earlier)
            for hh in range(2):
                before = before_ref[hh]
                d = g[hh] - jnp.exp2(lsig[hh]) * (g[hh] + (earlier_g[hh] + jnp.tile(before, (1, t // LANES))))
                if diagonal:
                    d = jnp.where(cols < rows, d, 0.0)
                dz[hh] = d.astype(BF)
                dq_acc[hh] += _nn(dz[hh], kb)
                before_ref[hh] = before + jnp.sum(g[hh], axis=-1, keepdims=True)
            dk_acc[pl.ds(off, t), :] += _tn(jnp.concatenate(dz, axis=0), q_both)
            dv_acc[pl.ds(off, t), :] += _tn(jnp.concatenate(prob, axis=0), do_both)
            for hh in range(2):
                z_ref[hh] = z_next[hh]
                da_ref[hh] = da_next[hh]

        nearest = jnp.min(jnp.minimum(r_ref[0], r_ref[1]), axis=0, keepdims=True)
        skip = jnp.sum(jnp.where(jnp.logical_and(lane < qi, nearest >= FAR), 1, 0))
        z_first, da_first = scores(skip)
        for hh in range(2):
            z_ref[hh] = z_first[hh]
            da_ref[hh] = da_first[hh]

        def step(b, carry):
            block(b, False)
            return carry

        lax.fori_loop(skip, qi, step, 0)
        block(qi, True)
        dq_ref[...] = jnp.where(low, dq_acc[0], dq_acc[1])

        @pl.when(qi == nq - 1)
        def _():
            col = pl.multiple_of(hp * LANES, LANES)
            c1 = pltpu.make_async_copy(dk_acc, dk_hbm.at[:, pl.ds(col, LANES)], sem.at[0])
            c2 = pltpu.make_async_copy(dv_acc, dv_hbm.at[:, pl.ds(col, LANES)], sem.at[1])
            c1.start()
            c2.start()
            c1.wait()
            c2.wait()

    blk = pl.BlockSpec((t, LANES), lambda h, i: (i, h))
    hbm = pl.BlockSpec(memory_space=pl.ANY)
    return _pcall(body, name="attn_bwd", grid=(nb, nq),
                  in_specs=[blk, pl.BlockSpec((S, LANES), lambda h, i: (0, nb + h)),
                            pl.BlockSpec((S, LANES), lambda h, i: (0, 2 * nb + h)), blk,
                            pl.BlockSpec((2, t, LANES), lambda h, i: (h, i, 0))],
                  out_specs=[blk, hbm, hbm],
                  out_shape=[_sds((S, D), F32), _sds((S, D), F32), _sds((S, D), F32)],
                  scratch_shapes=[pltpu.VMEM((S, LANES), F32), pltpu.VMEM((S, LANES), F32), pltpu.VMEM((2, t, LANES), F32),
                                  pltpu.VMEM((2, t, LANES), F32), pltpu.VMEM((2, t, t), F32), pltpu.VMEM((2, t, t), F32),
                                  pltpu.SemaphoreType.DMA((2,))],
                  compiler_params=_cp("arbitrary", "arbitrary"))(qkvn, qkvn, qkvn, do, r_all)


def cond_embed(parts, cond_b):
    P, B, D = parts.shape

    def body(p_ref, b_ref, pre_ref, e_ref):
        pre = p_ref[0]
        for s in range(1, P):
            pre = pre + p_ref[s]
        pre = pre + b_ref[...]
        pre_ref[...] = pre
        e_ref[...] = pre * _sig(pre)

    return _pcall(body, name="cond_embed", out_shape=[_sds((B, D), F32), _sds((B, D), F32)],
                  compiler_params=_cp())(parts, cond_b)


def cond_embed_bwd(parts, pre):
    P, B, D = parts.shape

    def body(p_ref, pre_ref, dpre_ref, db_ref):
        de = p_ref[0]
        for s in range(1, P):
            de = de + p_ref[s]
        pre = pre_ref[...]
        s = _sig(pre)
        dpre = de * (s * (1.0 + pre * (1.0 - s)))
        dpre_ref[...] = dpre
        db_ref[...] = _colsum(dpre)

    return _pcall(body, name="cond_embed_bwd", out_shape=[_sds((B, D), F32), _sds((1, D), F32)],
                  compiler_params=_cp())(parts, pre)


def adamw(w, parts, m, v):
    R, C = w.shape
    P = parts.shape[0]
    tr = _tile(R, max(8, (1 << 17) // C // 8 * 8), 8)
    c1 = 1.0 - ADAM_B1 ** ADAM_STEP
    c2 = 1.0 - ADAM_B2 ** ADAM_STEP

    def body(w_ref, p_ref, m_ref, v_ref, g_ref, d_ref, nm_ref, nv_ref):
        g = p_ref[0].astype(F32)
        for s in range(1, P):
            g = g + p_ref[s].astype(F32)
        nm = ADAM_B1 * m_ref[...] + (1.0 - ADAM_B1) * g
        nv = ADAM_B2 * v_ref[...] + (1.0 - ADAM_B2) * (g * g)
        g_ref[...] = g
        nm_ref[...] = nm
        nv_ref[...] = nv
        d_ref[...] = -ADAM_LR * ((nm / c1) / (jnp.sqrt(nv / c2) + ADAM_EPS) + ADAM_WD * w_ref[...])

    blk = pl.BlockSpec((tr, C), lambda i: (i, 0))
    return _pcall(body, name="adamw", grid=(R // tr,), in_specs=[blk, pl.BlockSpec((P, tr, C), lambda i: (0, i, 0)), blk, blk],
                  out_specs=[blk] * 4, out_shape=[_sds((R, C), F32)] * 4, compiler_params=_cp("parallel"))(w, parts, m, v)


def exchange(arrs, peer_axes):
    n = len(arrs)
    shapes = [a.shape if ax is None else a.shape[:ax] + a.shape[ax + 1:] for a, ax in zip(arrs, peer_axes)]

    def body(*refs):
        ins, outs = refs[:n], refs[n:2 * n]
        send, recv, local = refs[2 * n:]
        x, y, c = lax.axis_index("x"), lax.axis_index("y"), lax.axis_index("c")
        me = 4 * x + 2 * y + c

        def piece(i, d):
            if peer_axes[i] is None:
                return ins[i]
            return ins[i].at[(slice(None),) * peer_axes[i] + (d,)]

        mine = [pltpu.make_async_copy(piece(i, me), outs[i].at[me], local.at[i]) for i in range(n)]
        for cp in mine:
            cp.start()
        sends, recvs = [], []
        for p in range(1, N_DEV):
            peer = (1 - x if p & 4 else x, 1 - y if p & 2 else y, 1 - c if p & 1 else c)
            them = 4 * peer[0] + 2 * peer[1] + peer[2]
            for i in range(n):
                sends.append(pltpu.make_async_remote_copy(piece(i, them), outs[i].at[me], send.at[i, p - 1], recv.at[i, p - 1],
                                                          device_id=peer, device_id_type=MESH))
                recvs.append(pltpu.make_async_remote_copy(piece(i, me), outs[i].at[them], send.at[i, p - 1], recv.at[i, p - 1],
                                                          device_id=peer, device_id_type=MESH))
        for cp in sends:
            cp.start()
        for cp in recvs:
            cp.wait_recv()
        for cp in sends:
            cp.wait_send()
        for cp in mine:
            cp.wait()

    hbm = pl.BlockSpec(memory_space=pl.ANY)
    return _pcall(body, name="exchange", in_specs=[hbm] * n, out_specs=[hbm] * n,
                  out_shape=[_sds((N_DEV,) + s, a.dtype) for s, a in zip(shapes, arrs)],
                  scratch_shapes=[pltpu.SemaphoreType.DMA((n, N_DEV - 1)), pltpu.SemaphoreType.DMA((n, N_DEV - 1)),
                                  pltpu.SemaphoreType.DMA((n,))])(*arrs)


def gather_via_sibling(arrs):
    n = len(arrs)

    def body(*refs):
        ins, outs = refs[:n], refs[n:2 * n]
        send, recv, local = refs[2 * n:]
        x, y, c = lax.axis_index("x"), lax.axis_index("y"), lax.axis_index("c")
        me, sibling = (x, y, c), (x, y, 1 - c)
        chips = [(1 - x, y), (x, 1 - y), (1 - x, 1 - y)]
        index = lambda d: 4 * d[0] + 2 * d[1] + d[2]

        def copy(i, k, block, to, src=None):
            dst = outs[i].at[index(block)]
            return pltpu.make_async_remote_copy(dst if src is None else src, dst, send.at[i, k], recv.at[i, k],
                                                device_id=to, device_id_type=MESH)

        mine = [pltpu.make_async_copy(ins[i], outs[i].at[index(me)], local.at[i]) for i in range(n)]
        for cp in mine:
            cp.start()
        first = []
        for i in range(n):
            first.append(copy(i, 0, me, sibling, src=ins[i]))
            first += [copy(i, 1 + j, me, (*chip, c), src=ins[i]) for j, chip in enumerate(chips)]
        for cp in first:
            cp.start()
        passed = []
        for j, chip in enumerate(chips):
            for i in range(n):
                copy(i, 1 + j, (*chip, c), me).wait_recv()
                passed.append(copy(i, 4 + j, (*chip, c), sibling))
                passed[-1].start()
        for i in range(n):
            copy(i, 0, sibling, me).wait_recv()
            for j, chip in enumerate(chips):
                copy(i, 4 + j, (*chip, 1 - c), me).wait_recv()
        for cp in first + passed:
            cp.wait_send()
        for cp in mine:
            cp.wait()

    hbm = pl.BlockSpec(memory_space=pl.ANY)
    return _pcall(body, name="gather_via_sibling", in_specs=[hbm] * n, out_specs=[hbm] * n,
                  out_shape=[_sds((N_DEV,) + a.shape, a.dtype) for a in arrs],
                  scratch_shapes=[pltpu.SemaphoreType.DMA((n, N_DEV - 1)), pltpu.SemaphoreType.DMA((n, N_DEV - 1)),
                                  pltpu.SemaphoreType.DMA((n,))])(*arrs)


def _pack(vs):
    flat = jnp.concatenate([v.reshape(-1).astype(F32) for v in vs])
    pad = (-flat.shape[0]) % 1024
    return jnp.pad(flat, (0, pad)).reshape(-1, 1024)


def _unpack(packed, shapes):
    flat = packed.reshape(packed.shape[0], -1)
    out, o = [], 0
    for s in shapes:
        n = math.prod(s)
        out.append(flat[:, o:o + n].reshape((packed.shape[0],) + tuple(s)))
        o += n
    return out


def _cat_dev(g, axis):
    g = jnp.moveaxis(g, 0, axis)
    return g.reshape(g.shape[:axis] + (g.shape[axis] * g.shape[axis + 1],) + g.shape[axis + 2:])


def kernel(x, c, cond_w, cond_b, ada_w, ada_b, norm_g, ffn_w1, ffn_w3, ffn_w2, a_w_in, a_b_in, a_dw, a_dw_b, a_ln_g, a_ln_b, a_w_out, a_b_out, b_w_qkv, b_q_g, b_k_g, b_w_o, p_w, p_b, p_scale, loss_target, m_cond_w, m_cond_b, m_ada_w, m_ada_b, m_norm_g, m_ffn_w1, m_ffn_w3, m_ffn_w2, m_a_w_in, m_a_b_in, m_a_dw, m_a_dw_b, m_a_ln_g, m_a_ln_b, m_a_w_out, m_a_b_out, m_b_w_qkv, m_b_q_g, m_b_k_g, m_b_w_o, m_p_w, m_p_b, m_p_scale, v_cond_w, v_cond_b, v_ada_w, v_ada_b, v_norm_g, v_ffn_w1, v_ffn_w3, v_ffn_w2, v_a_w_in, v_a_b_in, v_a_dw, v_a_dw_b, v_a_ln_g, v_a_ln_b, v_a_w_out, v_a_b_out, v_b_w_qkv, v_b_q_g, v_b_k_g, v_b_w_o, v_p_w, v_p_b, v_p_scale):
    A = dict(zip(ARGS, (x, c, cond_w, cond_b, ada_w, ada_b, norm_g, ffn_w1, ffn_w3, ffn_w2, a_w_in, a_b_in, a_dw, a_dw_b, a_ln_g, a_ln_b, a_w_out, a_b_out, b_w_qkv, b_q_g, b_k_g, b_w_o, p_w, p_b, p_scale, loss_target, m_cond_w, m_cond_b, m_ada_w, m_ada_b, m_norm_g, m_ffn_w1, m_ffn_w3, m_ffn_w2, m_a_w_in, m_a_b_in, m_a_dw, m_a_dw_b, m_a_ln_g, m_a_ln_b, m_a_w_out, m_a_b_out, m_b_w_qkv, m_b_q_g, m_b_k_g, m_b_w_o, m_p_w, m_p_b, m_p_scale, v_cond_w, v_cond_b, v_ada_w, v_ada_b, v_norm_g, v_ffn_w1, v_ffn_w3, v_ffn_w2, v_a_w_in, v_a_b_in, v_a_dw, v_a_dw_b, v_a_ln_g, v_a_ln_b, v_a_w_out, v_a_b_out, v_b_w_qkv, v_b_q_g, v_b_k_g, v_b_w_o, v_p_w, v_p_b, v_p_scale)))
    S, D = x.shape[1], x.shape[2]
    depth = ada_w.shape[0]
    n_a, n_b, n_c = a_w_in.shape[0], b_w_qkv.shape[0], p_w.shape[0]
    me = 4 * lax.axis_index("x") + 2 * lax.axis_index("y") + lax.axis_index("c")
    swap = lambda w: jnp.swapaxes(w, -1, -2)

    small_names = ['norm_g', 'a_b_in', 'a_dw', 'a_dw_b', 'a_ln_g', 'a_ln_b', 'a_b_out', 'p_w', 'p_b', 'p_scale']
    small_in = [c] + [A[n] for n in small_names]
    big_in = [swap(ffn_w1).astype(BF), swap(ffn_w3).astype(BF), ffn_w2.astype(BF), swap(a_w_in).astype(BF),
              a_w_out.astype(BF), swap(b_w_qkv).astype(BF), b_w_o.astype(BF)]
    got = gather_via_sibling(big_in + [_pack(small_in)])
    w1t, w3t, w2 = _cat_dev(got[0], 2), _cat_dev(got[1], 2), _cat_dev(got[2], 2)
    w_in_t, w_out = _cat_dev(got[3], 1), _cat_dev(got[4], 1)
    w_qkv_t, w_o = _cat_dev(got[5], 1), _cat_dev(got[6], 1)
    sm = dict(zip(['c'] + small_names, _unpack(got[7], [v.shape for v in small_in])))
    c_all = sm['c'][:, 0]
    norm_g_f = _cat_dev(sm['norm_g'], 2)
    a_b_in_f, a_dw_f, a_dw_b_f = _cat_dev(sm['a_b_in'], 1), _cat_dev(sm['a_dw'], 2), _cat_dev(sm['a_dw_b'], 1)
    a_ln_g_f, a_ln_b_f, a_b_out_f = _cat_dev(sm['a_ln_g'], 1), _cat_dev(sm['a_ln_b'], 1), _cat_dev(sm['a_b_out'], 1)
    p_w_f, p_b_f, p_scale_f = _cat_dev(sm['p_w'], 2), _cat_dev(sm['p_b'], 2), _cat_dev(sm['p_scale'], 1)
    a_dw_f = jnp.pad(a_dw_f, ((0, 0), (0, CONV_PAD - CONV_W), (0, 0)))

    rows = D // N_DEV
    c_mine = lax.dynamic_slice_in_dim(c_all, me * rows, rows, axis=1)
    pre_parts = exchange([matmul_nn(c_mine, cond_w)], [None])[0]
    pre_all, e_all = cond_embed(pre_parts, cond_b.reshape(1, D))
    n_mod = ada_w.shape[2]
    ada_b_mine = lax.dynamic_slice_in_dim(ada_b, me * n_mod, n_mod, axis=1)
    mod_part = jnp.stack([matmul_nn(e_all, ada_w[i], ada_b_mine[i:i + 1]) for i in range(depth)], axis=1)
    mod_all = exchange([mod_part], [None])[0]
    mod = lax.dynamic_index_in_dim(mod_all, me, axis=1, keepdims=False)
    mod = jnp.moveaxis(mod, 0, 1).reshape(depth, 3, 3, 1, D)

    xs = x[0]
    saved = []
    ia = ib = ic = 0
    for i in range(depth):
        lay = {}
        for sub in range(3):
            shift, scale, gate = mod[i, sub, 0], mod[i, sub, 1], 1.0 + mod[i, sub, 2]
            g = norm_g_f[i, sub].reshape(1, D)
            h = mod_fwd(xs, g, scale, shift)
            rec = dict(x=xs, h=h, g=g, scale=scale, gate=gate)
            if sub != 1:
                j = 0 if sub == 0 else 1
                y, g1s, g3s, act = ffn_fwd(h, w1t[i, j], w3t[i, j], w2[i, j])
                rec.update(kind='ffn', j=j, coef=0.5, g1=g1s, g3=g3s, act=act)
            elif i % 3 == 0:
                pre = matmul_nt(h, w_in_t[ia], a_b_in_f[ia].reshape(1, 2 * D))
                v, sw = conv_mid_fwd(pre, a_dw_f[ia], a_dw_b_f[ia].reshape(1, D), a_ln_g_f[ia].reshape(1, D), a_ln_b_f[ia].reshape(1, D))
                y = matmul_nn(sw, w_out[ia], a_b_out_f[ia].reshape(1, D))
                rec.update(kind='conv', idx=ia, coef=1.0, pre=pre, v=v, sw=sw)
                ia += 1
            elif i % 3 == 1:
                qkv = matmul_nt(h, w_qkv_t[ib])
                qg2 = jnp.tile(b_q_g[ib].reshape(1, HEAD_DIM), (1, 2))
                kg2 = jnp.tile(b_k_g[ib].reshape(1, HEAD_DIM), (1, 2))
                qkvn = qknorm_fwd(qkv, qg2, kg2)
                o, r_all = attn_fwd(qkvn)
                y = matmul_nn(o, w_o[ib])
                rec.update(kind='attn', idx=ib, coef=1.0, qkv=qkv, qkvn=qkvn, o=o, r_all=r_all, qg2=qg2, kg2=kg2)
                ib += 1
            else:
                pb, ps = p_b_f[ic].reshape(1, D), p_scale_f[ic].reshape(1, D)
                y, dpool = pool_fwd(h, p_w_f[ic], pb, ps)
                rec.update(kind='pool', idx=ic, coef=1.0, d=dpool, pb=pb, ps=ps)
                ic += 1
            rec['y'] = y
            xs = resid_fwd(xs, y, gate, rec['coef'])
            lay[sub] = rec
        saved.append(lay)

    loss_part, dx = loss_fwd_bwd(xs, loss_target[0])
    loss = lax.psum(loss_part[0, 0], ("x", "y", "c"))

    Fd = w2.shape[2]
    zeros = lambda *s: jnp.zeros(s, F32)
    g_w1t, g_w3t, g_w2 = [[None, None] for _ in range(depth)], [[None, None] for _ in range(depth)], [[None, None] for _ in range(depth)]
    g_w_in_t, g_w_out, g_w_qkv_t, g_w_o, g_p_w = [None] * n_a, [None] * n_a, [None] * n_b, [None] * n_b, [None] * n_c
    g_b_in, g_dw, g_dw_b, g_ln_g, g_ln_b, g_b_out = ([None] * n_a for _ in range(6))
    g_q_g, g_k_g, g_p_b, g_p_scale = [None] * n_b, [None] * n_b, [None] * n_c, [None] * n_c
    d_mod = [[None] * 3 for _ in range(depth)]
    d_norm_g = [[None] * 3 for _ in range(depth)]
    for i in reversed(range(depth)):
        for sub in reversed(range(3)):
            rec = saved[i][sub]
            dy, d_gate, dy_sum = gate_bwd(dx, rec['y'], rec['gate'], rec['coef'])
            if rec['kind'] == 'ffn':
                j = rec['j']
                dh, d1, d3 = ffn_bwd(dy, rec['g1'], rec['g3'], w1t[i, j], w3t[i, j], w2[i, j])
                g_w1t[i][j] = matmul_tn(d1, rec['h'])
                g_w3t[i][j] = matmul_tn(d3, rec['h'])
                g_w2[i][j] = matmul_tn(rec['act'], dy)
            elif rec['kind'] == 'conv':
                k = rec['idx']
                dsw = matmul_nt(dy, w_out[k])
                g_w_out[k] = matmul_tn(rec['sw'], dy)
                g_b_out[k] = dy_sum
                dpre, g_b_in[k], ddw, g_dw_b[k], g_ln_g[k], g_ln_b[k] = conv_mid_bwd(
                    rec['pre'], rec['v'], dsw, a_dw_f[k], a_ln_g_f[k].reshape(1, D), a_ln_b_f[k].reshape(1, D))
                g_dw[k] = ddw[:CONV_W]
                dh = matmul_nn(dpre, w_in_t[k])
                g_w_in_t[k] = matmul_tn(dpre, rec['h'])
            elif rec['kind'] == 'attn':
                k = rec['idx']
                do = matmul_nt(dy, w_o[k], out_dtype=BF)
                g_w_o[k] = matmul_tn(rec['o'], dy)
                dq, dk, dv = attn_bwd(rec['qkvn'], do, rec['r_all'])
                dqkv, dgains = qknorm_bwd(rec['qkv'], dq, dk, dv, rec['qg2'], rec['kg2'])
                dgains = dgains.reshape(3, N_HEADS, HEAD_DIM)
                g_q_g[k], g_k_g[k] = jnp.sum(dgains[0], axis=0), jnp.sum(dgains[1], axis=0)
                dh = matmul_nn(dqkv, w_qkv_t[k])
                g_w_qkv_t[k] = matmul_tn(dqkv, rec['h'])
            else:
                k = rec['idx']
                dh, dyy, g_p_b[k], g_p_scale[k] = pool_bwd(dy, rec['d'], p_w_f[k], rec['pb'], rec['ps'])
                full = matmul_tn(rec['d'], dyy)
                G = p_w_f.shape[1]
                Dg = D // G
                g_p_w[k] = jnp.stack([full[g * Dg:(g + 1) * Dg, g * Dg:(g + 1) * Dg] for g in range(G)])
            dx, d_shift, d_scale, d_norm_g[i][sub] = mod_bwd(rec['x'], dh, dx, rec['g'], rec['scale'])
            d_mod[i][sub] = jnp.concatenate([d_shift, d_scale, d_gate], axis=0)
    grad_x = dx[None]

    d_mod_mine = jnp.stack([jnp.stack(r) for r in d_mod])
    small_g = [d_mod_mine, jnp.stack([jnp.concatenate(r, axis=0) for r in d_norm_g]),
               jnp.stack(g_b_in), jnp.stack(g_dw), jnp.stack(g_dw_b), jnp.stack(g_ln_g), jnp.stack(g_ln_b), jnp.stack(g_b_out),
               jnp.stack(g_q_g), jnp.stack(g_k_g), jnp.stack(g_p_b), jnp.stack(g_p_scale)]
    stack2 = lambda g: jnp.stack([jnp.stack(r) for r in g])
    blocks = lambda g, ax: g.reshape(g.shape[:ax] + (N_DEV, g.shape[ax] // N_DEV) + g.shape[ax + 1:])
    big_g = [blocks(stack2(g_w1t), 2), blocks(stack2(g_w3t), 2), blocks(stack2(g_w2), 2), blocks(jnp.stack(g_w_in_t), 1),
             blocks(jnp.stack(g_w_out), 1), blocks(jnp.stack(g_w_qkv_t), 1), blocks(jnp.stack(g_w_o), 1), blocks(jnp.stack(g_p_w), 2)]
    big_ax = [2, 2, 2, 1, 1, 1, 1, 2]
    got = exchange(big_g + [_pack(small_g)], big_ax + [None])
    p_w1t, p_w3t, p_w2, p_w_in_t, p_w_out, p_w_qkv_t, p_w_o, p_p_w = got[:8]
    (dmod_all, dng_all, dbin_all, ddw_all, ddwb_all, dlng_all, dlnb_all, dbout_all, dqg_all, dkg_all, dpb_all,
     dps_all) = _unpack(got[8], [v.shape for v in small_g])
    dmod_all = dmod_all.reshape(N_DEV, depth, 9 * D)

    dmod_mine = lax.dynamic_slice_in_dim(dmod_all, me * n_mod, n_mod, axis=2)
    e_t = e_all.T
    grad_ada_w = jnp.stack([outer_rows(e_t, dmod_mine[:, i]) for i in range(depth)])
    de_part = matmul_nt(dmod_mine[:, 0], ada_w[0])
    for i in range(1, depth):
        de_part = de_part + matmul_nt(dmod_mine[:, i], ada_w[i])
    de_parts = exchange([de_part], [None])[0]
    dpre_all, grad_cond_b = cond_embed_bwd(de_parts, pre_all)
    grad_cond_w = outer_rows(c_mine.T, dpre_all)

    def mine(g_all, axis, size):
        return lax.dynamic_slice_in_dim(g_all, me * size, size, axis=axis)

    def upd(name, parts, shape2, to_out=lambda t: t, from_in=lambda t: t):
        w, m, v = (from_in(A[p + name]).reshape(shape2) for p in ('', 'm_', 'v_'))
        outs = adamw(w, parts.reshape((parts.shape[0],) + shape2), m, v)
        return [to_out(o).reshape(A[name].shape) for o in outs]

    L = depth
    res = {}
    res['cond_w'] = upd('cond_w', grad_cond_w[None], (rows, D))
    res['cond_b'] = upd('cond_b', grad_cond_b[None], (1, D))
    res['ada_w'] = upd('ada_w', grad_ada_w[None], (L * D, n_mod))
    res['ada_b'] = upd('ada_b', dmod_all, (L, 9 * D))
    res['norm_g'] = upd('norm_g', mine(dng_all, 3, D // N_DEV), (L * 3, D // N_DEV))
    fs = Fd // N_DEV
    res['ffn_w1'] = upd('ffn_w1', p_w1t, (L * 2 * fs, D), to_out=lambda t: swap(t.reshape(L, 2, fs, D)), from_in=swap)
    res['ffn_w3'] = upd('ffn_w3', p_w3t, (L * 2 * fs, D), to_out=lambda t: swap(t.reshape(L, 2, fs, D)), from_in=swap)
    res['ffn_w2'] = upd('ffn_w2', p_w2, (L * 2 * fs, D))
    ws = 2 * D // N_DEV
    res['a_w_in'] = upd('a_w_in', p_w_in_t, (n_a * ws, D), to_out=lambda t: swap(t.reshape(n_a, ws, D)), from_in=swap)
    res['a_b_in'] = upd('a_b_in', mine(dbin_all.reshape(N_DEV, n_a, 2 * D), 2, ws), (n_a, ws))
    res['a_dw'] = upd('a_dw', mine(ddw_all, 3, rows), (n_a * CONV_W, rows))
    res['a_dw_b'] = upd('a_dw_b', mine(ddwb_all.reshape(N_DEV, n_a, D), 2, rows), (n_a, rows))
    res['a_ln_g'] = upd('a_ln_g', mine(dlng_all.reshape(N_DEV, n_a, D), 2, rows), (n_a, rows))
    res['a_ln_b'] = upd('a_ln_b', mine(dlnb_all.reshape(N_DEV, n_a, D), 2, rows), (n_a, rows))
    res['a_w_out'] = upd('a_w_out', p_w_out, (n_a * rows, D))
    res['a_b_out'] = upd('a_b_out', mine(dbout_all.reshape(N_DEV, n_a, D), 2, rows), (n_a, rows))
    qs = 3 * D // N_DEV
    res['b_w_qkv'] = upd('b_w_qkv', p_w_qkv_t, (n_b * qs, D), to_out=lambda t: swap(t.reshape(n_b, qs, D)), from_in=swap)
    res['b_q_g'] = upd('b_q_g', dqg_all, (n_b, HEAD_DIM))
    res['b_k_g'] = upd('b_k_g', dkg_all, (n_b, HEAD_DIM))
    res['b_w_o'] = upd('b_w_o', p_w_o, (n_b * rows, D))
    G = p_w.shape[1]
    Dg = D // G
    res['p_w'] = upd('p_w', p_p_w, (n_c * G * Dg // N_DEV, Dg))
    res['p_b'] = upd('p_b', mine(dpb_all.reshape(N_DEV, n_c, G, Dg), 3, Dg // N_DEV), (n_c * G, Dg // N_DEV))
    res['p_scale'] = upd('p_scale', mine(dps_all.reshape(N_DEV, n_c, D), 2, rows), (n_c, rows))

    outs = [loss, grad_x]
    for k in range(4):
        outs += [res[n][k] for n in WEIGHTS]
    return tuple(outs)
```

```python
import math

import jax
import jax.numpy as jnp
from jax import lax
from jax.experimental import pallas as pl
from jax.experimental.pallas import tpu as pltpu

F32 = jnp.float32
BF = jnp.bfloat16
EPS = 1e-6
N_DEV = 8
N_HEADS = 16
HEAD_DIM = 64
LANES = 128
CONV_W = 31
CONV_PAD = 32
CONV_ROWS = 128
POOL_WINDOWS = (2, 4, 8, 16)
POOL_PAD = 16
VMEM_LIMIT = 56 * 1024 * 1024
ADAM_LR, ADAM_B1, ADAM_B2, ADAM_EPS, ADAM_WD, ADAM_STEP = 0.001, 0.9, 0.999, 1e-08, 0.01, 10
MESH = pl.DeviceIdType.MESH

WEIGHTS = ['cond_w', 'cond_b', 'ada_w', 'ada_b', 'norm_g', 'ffn_w1', 'ffn_w3', 'ffn_w2', 'a_w_in', 'a_b_in',
           'a_dw', 'a_dw_b', 'a_ln_g', 'a_ln_b', 'a_w_out', 'a_b_out', 'b_w_qkv', 'b_q_g', 'b_k_g', 'b_w_o',
           'p_w', 'p_b', 'p_scale']
ARGS = ['x', 'c'] + WEIGHTS + ['loss_target'] + ['m_' + n for n in WEIGHTS] + ['v_' + n for n in WEIGHTS]


def _pcall(body, **kw):
    return pl.pallas_call(body, **kw)


def _cp(*sem):
    return pltpu.CompilerParams(dimension_semantics=sem if sem else None, vmem_limit_bytes=VMEM_LIMIT)


def _nn(a, b):
    return lax.dot_general(a, b, (((1,), (0,)), ((), ())), preferred_element_type=F32)


def _nt(a, b):
    return lax.dot_general(a, b, (((1,), (1,)), ((), ())), preferred_element_type=F32)


def _tn(a, b):
    return lax.dot_general(a, b, (((0,), (0,)), ((), ())), preferred_element_type=F32)


def _sig(z):
    return 1.0 / (1.0 + jnp.exp(-z))


def _tile(n, pref, unit):
    t = (min(n, pref) // unit) * unit
    while t >= unit:
        if n % t == 0:
            return t
        t -= unit
    return n


def _colsum(v):
    return jnp.sum(v, axis=0, keepdims=True)


def _acc(ref, first, val):
    @pl.when(first)
    def _():
        ref[...] = val

    @pl.when(jnp.logical_not(first))
    def _():
        ref[...] += val


def _sds(shape, dt):
    return jax.ShapeDtypeStruct(shape, dt)


def mod_fwd(x, g, scale, shift):
    S, D = x.shape
    tm = _tile(S, 512, 8)

    def body(x_ref, g_ref, sc_ref, sh_ref, h_ref):
        xv = x_ref[...]
        r = lax.rsqrt(jnp.mean(xv * xv, axis=-1, keepdims=True) + EPS)
        h_ref[...] = (((xv * r) * g_ref[...]) * (1.0 + sc_ref[...]) + sh_ref[...]).astype(h_ref.dtype)

    row = pl.BlockSpec((tm, D), lambda i: (i, 0))
    vec = pl.BlockSpec((1, D), lambda i: (0, 0))
    return _pcall(body, name="mod_fwd", grid=(S // tm,), in_specs=[row, vec, vec, vec], out_specs=row,
                  out_shape=_sds((S, D), BF), compiler_params=_cp("parallel"))(x, g, scale, shift)


def resid_fwd(x, y, gate, coef):
    S, D = x.shape
    tm = _tile(S, 512, 8)

    def body(x_ref, y_ref, g_ref, o_ref):
        o_ref[...] = x_ref[...] + (coef * g_ref[...]) * y_ref[...]

    row = pl.BlockSpec((tm, D), lambda i: (i, 0))
    vec = pl.BlockSpec((1, D), lambda i: (0, 0))
    return _pcall(body, name="resid_fwd", grid=(S // tm,), in_specs=[row, row, vec], out_specs=row,
                  out_shape=_sds((S, D), F32), compiler_params=_cp("parallel"))(x, y, gate)


def resid_mod_fwd(x, y, gate, coef, g, scale, shift):
    S, D = x.shape
    tm = _tile(S, 512, 8)

    def body(x_ref, y_ref, gate_ref, g_ref, sc_ref, sh_ref, o_ref, h_ref):
        xv = x_ref[...] + (coef * gate_ref[...]) * y_ref[...]
        o_ref[...] = xv
        r = lax.rsqrt(jnp.mean(xv * xv, axis=-1, keepdims=True) + EPS)
        h_ref[...] = (((xv * r) * g_ref[...]) * (1.0 + sc_ref[...]) + sh_ref[...]).astype(h_ref.dtype)

    row = pl.BlockSpec((tm, D), lambda i: (i, 0))
    vec = pl.BlockSpec((1, D), lambda i: (0, 0))
    return _pcall(body, name="resid_mod_fwd", grid=(S // tm,), in_specs=[row, row, vec, vec, vec, vec], out_specs=[row, row],
                  out_shape=[_sds((S, D), F32), _sds((S, D), BF)], compiler_params=_cp("parallel"))(x, y, gate, g, scale, shift)


def gate_bwd(dxo, y, gate, coef):
    S, D = dxo.shape
    tm = _tile(S, 512, 8)

    def body(d_ref, y_ref, g_ref, dy_ref, dg_ref, ds_ref):
        first = pl.program_id(0) == 0
        d = d_ref[...]
        dy = (coef * g_ref[...]) * d
        dy_ref[...] = dy.astype(dy_ref.dtype)
        _acc(dg_ref, first, _colsum(coef * d * y_ref[...]))
        _acc(ds_ref, first, _colsum(dy))

    row = pl.BlockSpec((tm, D), lambda i: (i, 0))
    vec = pl.BlockSpec((1, D), lambda i: (0, 0))
    return _pcall(body, name="gate_bwd", grid=(S // tm,), in_specs=[row, row, vec], out_specs=[row, vec, vec],
                  out_shape=[_sds((S, D), BF), _sds((1, D), F32), _sds((1, D), F32)],
                  compiler_params=_cp("arbitrary"))(dxo, y, gate)


def mod_bwd(x, dh, dxo, g, scale):
    S, D = x.shape
    tm = _tile(S, 512, 8)

    def body(x_ref, dh_ref, do_ref, g_ref, sc_ref, dx_ref, dsh_ref, dsc_ref, dg_ref):
        first = pl.program_id(0) == 0
        xv = x_ref[...]
        dh = dh_ref[...].astype(F32)
        gv = g_ref[...]
        r = lax.rsqrt(jnp.mean(xv * xv, axis=-1, keepdims=True) + EPS)
        xh = xv * r
        dn = dh * (1.0 + sc_ref[...])
        u = dn * gv
        dx_ref[...] = do_ref[...] + r * (u - xh * jnp.mean(u * xh, axis=-1, keepdims=True))
        _acc(dsh_ref, first, _colsum(dh))
        _acc(dsc_ref, first, _colsum(dh * (xh * gv)))
        _acc(dg_ref, first, _colsum(dn * xh))

    row = pl.BlockSpec((tm, D), lambda i: (i, 0))
    vec = pl.BlockSpec((1, D), lambda i: (0, 0))
    return _pcall(body, name="mod_bwd", grid=(S // tm,), in_specs=[row, row, row, vec, vec],
                  out_specs=[row, vec, vec, vec],
                  out_shape=[_sds((S, D), F32), _sds((1, D), F32), _sds((1, D), F32), _sds((1, D), F32)],
                  compiler_params=_cp("arbitrary"))(x, dh, dxo, g, scale)


def mod_gate_bwd(x, dh, dxo, g, scale, y_prev, gate_prev, coef_prev):
    S, D = x.shape
    tm = _tile(S, 512, 8)

    def body(x_ref, dh_ref, do_ref, g_ref, sc_ref, y_ref, gp_ref, dx_ref, dsh_ref, dsc_ref, dg_ref, dy_ref, dgp_ref, dys_ref):
        first = pl.program_id(0) == 0
        xv = x_ref[...]
        dh = dh_ref[...].astype(F32)
        gv = g_ref[...]
        r = lax.rsqrt(jnp.mean(xv * xv, axis=-1, keepdims=True) + EPS)
        xh = xv * r
        dn = dh * (1.0 + sc_ref[...])
        u = dn * gv
        dx = do_ref[...] + r * (u - xh * jnp.mean(u * xh, axis=-1, keepdims=True))
        dx_ref[...] = dx
        _acc(dsh_ref, first, _colsum(dh))
        _acc(dsc_ref, first, _colsum(dh * (xh * gv)))
        _acc(dg_ref, first, _colsum(dn * xh))
        dy = (coef_prev * gp_ref[...]) * dx
        dy_ref[...] = dy.astype(dy_ref.dtype)
        _acc(dgp_ref, first, _colsum(coef_prev * dx * y_ref[...]))
        _acc(dys_ref, first, _colsum(dy))

    row = pl.BlockSpec((tm, D), lambda i: (i, 0))
    vec = pl.BlockSpec((1, D), lambda i: (0, 0))
    return _pcall(body, name="mod_gate_bwd", grid=(S // tm,), in_specs=[row, row, row, vec, vec, row, vec],
                  out_specs=[row, vec, vec, vec, row, vec, vec],
                  out_shape=[_sds((S, D), F32)] + [_sds((1, D), F32)] * 3 + [_sds((S, D), BF)] + [_sds((1, D), F32)] * 2,
                  compiler_params=_cp("arbitrary"))(x, dh, dxo, g, scale, y_prev, gate_prev)


def loss_fwd_bwd(y, target):
    S, D = y.shape
    tm = _tile(S, 512, 8)

    def body(y_ref, t_ref, l_ref, d_ref):
        first = pl.program_id(0) == 0
        e = y_ref[...] - t_ref[...]
        d_ref[...] = e * (1.0 / D)
        part = 0.5 * jnp.sum(jnp.mean(e * e, axis=-1, keepdims=True), axis=0, keepdims=True)
        _acc(l_ref, first, part)

    row = pl.BlockSpec((tm, D), lambda i: (i, 0))
    return _pcall(body, name="loss", grid=(S // tm,), in_specs=[row, row],
                  out_specs=[pl.BlockSpec((1, 1), lambda i: (0, 0)), row],
                  out_shape=[_sds((1, 1), F32), _sds((S, D), F32)], compiler_params=_cp("arbitrary"))(y, target)


def matmul_nt(x, wt, bias=None, out_dtype=F32):
    S, K = x.shape
    N = wt.shape[0]
    tm, tn = _tile(S, 512, 8), _tile(N, 512, LANES)

    def body(*refs):
        x_ref, w_ref = refs[0], refs[1]
        o_ref = refs[-1]
        r = _nt(x_ref[...].astype(BF), w_ref[...].astype(BF))
        if bias is not None:
            r = r + refs[2][...]
        o_ref[...] = r.astype(o_ref.dtype)

    in_specs = [pl.BlockSpec((tm, K), lambda i, j: (i, 0)), pl.BlockSpec((tn, K), lambda i, j: (j, 0))]
    ops = [x, wt]
    if bias is not None:
        in_specs.append(pl.BlockSpec((1, tn), lambda i, j: (0, j)))
        ops.append(bias)
    return _pcall(body, name="matmul_nt", grid=(S // tm, N // tn), in_specs=in_specs,
                  out_specs=pl.BlockSpec((tm, tn), lambda i, j: (i, j)), out_shape=_sds((S, N), out_dtype),
                  compiler_params=_cp("parallel", "parallel"))(*ops)


def matmul_nn(x, w, bias=None, out_dtype=F32):
    S, K = x.shape
    N = w.shape[1]
    tm, tn = _tile(S, 512, 8), _tile(N, 512, LANES)

    def body(*refs):
        x_ref, w_ref = refs[0], refs[1]
        o_ref = refs[-1]
        r = _nn(x_ref[...].astype(BF), w_ref[...].astype(BF))
        if bias is not None:
            r = r + refs[2][...]
        o_ref[...] = r.astype(o_ref.dtype)

    in_specs = [pl.BlockSpec((tm, K), lambda i, j: (i, 0)), pl.BlockSpec((K, tn), lambda i, j: (0, j))]
    ops = [x, w]
    if bias is not None:
        in_specs.append(pl.BlockSpec((1, tn), lambda i, j: (0, j)))
        ops.append(bias)
    return _pcall(body, name="matmul_nn", grid=(S // tm, N // tn), in_specs=in_specs,
                  out_specs=pl.BlockSpec((tm, tn), lambda i, j: (i, j)), out_shape=_sds((S, N), out_dtype),
                  compiler_params=_cp("parallel", "parallel"))(*ops)


def matmul_tn(a, b, out_dtype=BF):
    S, M = a.shape
    N = b.shape[1]
    bm, bn, tk = _tile(M, 1408, LANES), _tile(N, 1024, LANES), _tile(S, 2048, 8)
    nk = S // tk

    def body(a_ref, b_ref, o_ref, acc_ref):
        k = pl.program_id(2)
        _acc(acc_ref, k == 0, _tn(a_ref[...].astype(BF), b_ref[...].astype(BF)))

        @pl.when(k == nk - 1)
        def _():
            o_ref[...] = acc_ref[...].astype(o_ref.dtype)

    return _pcall(body, name="matmul_tn", grid=(M // bm, N // bn, nk),
                  in_specs=[pl.BlockSpec((tk, bm), lambda i, j, k: (k, i)), pl.BlockSpec((tk, bn), lambda i, j, k: (k, j))],
                  out_specs=pl.BlockSpec((bm, bn), lambda i, j, k: (i, j)), out_shape=_sds((M, N), out_dtype),
                  scratch_shapes=[pltpu.VMEM((bm, bn), F32)],
                  compiler_params=_cp("parallel", "parallel", "arbitrary"))(a, b)


def outer_rows(at, b):
    M, R = at.shape
    N = b.shape[1]
    tm = _tile(M, 256, 8)

    def body(a_ref, b_ref, o_ref):
        av, bv = a_ref[...], b_ref[...]
        acc = av[:, 0:1] * bv[0:1, :]
        for r in range(1, R):
            acc = acc + av[:, r:r + 1] * bv[r:r + 1, :]
        o_ref[...] = acc

    return _pcall(body, name="outer_rows", grid=(M // tm,),
                  in_specs=[pl.BlockSpec((tm, R), lambda i: (i, 0)), pl.BlockSpec((R, N), lambda i: (0, 0))],
                  out_specs=pl.BlockSpec((tm, N), lambda i: (i, 0)), out_shape=_sds((M, N), F32),
                  compiler_params=_cp("parallel"))(at, b)


def ffn_fwd(h, w1t, w3t, w2):
    S, D = h.shape
    Fd = w2.shape[0]
    tm, tf = _tile(S, 1024, 8), _tile(Fd, 256, LANES)

    def body(h_ref, w1_ref, w3_ref, g1_ref, g3_ref, a_ref):
        hv = h_ref[...]
        g1 = _nt(hv, w1_ref[...])
        g3 = _nt(hv, w3_ref[...])
        g1_ref[...] = g1.astype(g1_ref.dtype)
        g3_ref[...] = g3.astype(g3_ref.dtype)
        a_ref[...] = ((g1 * _sig(g1)) * g3).astype(a_ref.dtype)

    row = pl.BlockSpec((tm, D), lambda i, f: (i, 0))
    wsp = pl.BlockSpec((tf, D), lambda i, f: (f, 0))
    act = pl.BlockSpec((tm, tf), lambda i, f: (i, f))
    g1s, g3s, a = _pcall(body, name="ffn_fwd", grid=(S // tm, Fd // tf), in_specs=[row, wsp, wsp], out_specs=[act, act, act],
                         out_shape=[_sds((S, Fd), BF)] * 3, compiler_params=_cp("parallel", "parallel"))(h, w1t, w3t)

    tr, tn = _tile(S, 512, 8), _tile(D, 512, LANES)

    def y_body(a_ref, w2_ref, y_ref):
        y_ref[...] = _nn(a_ref[...], w2_ref[...])

    y = _pcall(y_body, name="ffn_out", grid=(S // tr, D // tn),
               in_specs=[pl.BlockSpec((tr, Fd), lambda i, j: (i, 0)), pl.BlockSpec((Fd, tn), lambda i, j: (0, j))],
               out_specs=pl.BlockSpec((tr, tn), lambda i, j: (i, j)), out_shape=_sds((S, D), F32),
               compiler_params=_cp("parallel", "parallel"))(a, w2)
    return y, g1s, g3s, a


def ffn_bwd(dy, g1s, g3s, w1t, w3t, w2):
    S, D = dy.shape
    Fd = w2.shape[0]
    tm, tf = _tile(S, 1024, 16), _tile(Fd, 256, LANES)
    halves = [pl.ds(0, tm // 2), pl.ds(tm // 2, tm // 2)]

    def body(dy_ref, g1_ref, g3_ref, w2_ref, d1_ref, d3_ref):
        w2v = w2_ref[...]
        das = [_nt(dy_ref[rows, :], w2v) for rows in halves]
        for rows, da in zip(halves, das):
            g1 = g1_ref[rows, :].astype(F32)
            g3 = g3_ref[rows, :].astype(F32)
            s = _sig(g1)
            d3_ref[rows, :] = (da * (g1 * s)).astype(d3_ref.dtype)
            d1_ref[rows, :] = (da * g3 * (s * (1.0 + g1 * (1.0 - s)))).astype(d1_ref.dtype)

    row = pl.BlockSpec((tm, D), lambda i, f: (i, 0))
    act = pl.BlockSpec((tm, tf), lambda i, f: (i, f))
    d1, d3 = _pcall(body, name="ffn_bwd", grid=(S // tm, Fd // tf),
                    in_specs=[row, act, act, pl.BlockSpec((tf, D), lambda i, f: (f, 0))], out_specs=[act, act],
                    out_shape=[_sds((S, Fd), BF)] * 2, compiler_params=_cp("parallel", "parallel"))(dy, g1s, g3s, w2)

    tr, tn = _tile(S, 512, 8), _tile(D, 512, LANES)

    def dh_body(d1_ref, d3_ref, w1_ref, w3_ref, dh_ref):
        dh_ref[...] = _nn(d1_ref[...], w1_ref[...]) + _nn(d3_ref[...], w3_ref[...])

    wide = pl.BlockSpec((tr, Fd), lambda i, j: (i, 0))
    wcol = pl.BlockSpec((Fd, tn), lambda i, j: (0, j))
    dh = _pcall(dh_body, name="ffn_dh", grid=(S // tr, D // tn), in_specs=[wide, wide, wcol, wcol],
                out_specs=pl.BlockSpec((tr, tn), lambda i, j: (i, j)), out_shape=_sds((S, D), F32),
                compiler_params=_cp("parallel", "parallel"))(d1, d3, w1t, w3t)
    return dh, d1, d3


def _layer_norm_parts(v):
    mu = jnp.mean(v, axis=-1, keepdims=True)
    vc = v - mu
    rs = lax.rsqrt(jnp.mean(vc * vc, axis=-1, keepdims=True) + EPS)
    return vc * rs, rs


def conv_mid_fwd(pre, dw, dw_b, ln_g, ln_b):
    S, D2 = pre.shape
    D = D2 // 2
    tm = _tile(S, 256, CONV_PAD)

    def body(pre_ref, dw_ref, dwb_ref, g_ref, b_ref, v_ref, sw_ref, ubuf):
        @pl.when(pl.program_id(0) == 0)
        def _():
            ubuf[pl.ds(0, CONV_PAD), :] = jnp.zeros((CONV_PAD, D), F32)

        @pl.when(pl.program_id(0) > 0)
        def _():
            ubuf[pl.ds(0, CONV_PAD), :] = ubuf[pl.ds(tm, CONV_PAD), :]

        ubuf[pl.ds(CONV_PAD, tm), :] = pre_ref[:, pl.ds(0, D)] * _sig(pre_ref[:, pl.ds(D, D)])
        for r0 in range(0, tm, CONV_ROWS):
            for c0 in range(0, D, LANES):
                cols = pl.ds(c0, LANES)
                acc = jnp.zeros((CONV_ROWS, LANES), F32) + dwb_ref[:, cols]
                for k in range(CONV_W):
                    acc = acc + dw_ref[pl.ds(k, 1), cols] * ubuf[pl.ds(r0 + CONV_PAD - (CONV_W - 1) + k, CONV_ROWS), cols]
                v_ref[pl.ds(r0, CONV_ROWS), cols] = acc
        vh, _ = _layer_norm_parts(v_ref[...])
        ln = vh * g_ref[...] + b_ref[...]
        sw_ref[...] = (ln * _sig(ln)).astype(sw_ref.dtype)

    row = pl.BlockSpec((tm, D), lambda i: (i, 0))
    vec = pl.BlockSpec((1, D), lambda i: (0, 0))
    return _pcall(body, name="conv_mid_fwd", grid=(S // tm,),
                  in_specs=[pl.BlockSpec((tm, D2), lambda i: (i, 0)), pl.BlockSpec((CONV_PAD, D), lambda i: (0, 0)), vec, vec, vec],
                  out_specs=[row, row], out_shape=[_sds((S, D), F32), _sds((S, D), BF)],
                  scratch_shapes=[pltpu.VMEM((tm + CONV_PAD, D), F32)], compiler_params=_cp("arbitrary"))(pre, dw, dw_b, ln_g, ln_b)


def conv_mid_bwd(pre, v, dsw, dw, ln_g, ln_b):
    S, D2 = pre.shape
    D = D2 // 2
    tm = _tile(S, 256, CONV_PAD)
    nt = S // tm

    def body(pre_ref, v_ref, dsw_ref, dw_ref, g_ref, b_ref, dpre_ref, dbin_ref, ddw_ref, ddwb_ref, dg_ref, db_ref, dvbuf):
        first = pl.program_id(0) == 0

        @pl.when(first)
        def _():
            dvbuf[pl.ds(tm, CONV_PAD), :] = jnp.zeros((CONV_PAD, D), F32)
            ddw_ref[...] = jnp.zeros((CONV_PAD, D), F32)
            dbin_ref[...] = jnp.zeros((1, D2), F32)

        @pl.when(jnp.logical_not(first))
        def _():
            dvbuf[pl.ds(tm, CONV_PAD), :] = dvbuf[pl.ds(0, CONV_PAD), :]

        gv = g_ref[...]
        vh, rs = _layer_norm_parts(v_ref[...])
        ln = vh * gv + b_ref[...]
        sg = _sig(ln)
        dln = dsw_ref[...] * (sg * (1.0 + ln * (1.0 - sg)))
        _acc(dg_ref, first, _colsum(dln * vh))
        _acc(db_ref, first, _colsum(dln))
        dvh = dln * gv
        dv = rs * (dvh - jnp.mean(dvh, axis=-1, keepdims=True) - vh * jnp.mean(dvh * vh, axis=-1, keepdims=True))
        _acc(ddwb_ref, first, _colsum(dv))
        dvbuf[pl.ds(0, tm), :] = dv
        for r0 in range(0, tm, CONV_ROWS):
            rws = pl.ds(r0, CONV_ROWS)
            for c0 in range(0, D, LANES):
                cols, gate_cols = pl.ds(c0, LANES), pl.ds(D + c0, LANES)
                a = pre_ref[rws, cols]
                sb = _sig(pre_ref[rws, gate_cols])
                u = a * sb
                du = jnp.zeros((CONV_ROWS, LANES), F32)
                for k in range(CONV_W):
                    sh = dvbuf[pl.ds(r0 + CONV_W - 1 - k, CONV_ROWS), cols]
                    du = du + dw_ref[pl.ds(k, 1), cols] * sh
                    ddw_ref[pl.ds(k, 1), cols] += _colsum(u * sh)
                da = du * sb
                db = da * a * (1.0 - sb)
                dpre_ref[rws, cols] = da.astype(dpre_ref.dtype)
                dpre_ref[rws, gate_cols] = db.astype(dpre_ref.dtype)
                dbin_ref[:, cols] += _colsum(da)
                dbin_ref[:, gate_cols] += _colsum(db)

    rev = lambda i: (nt - 1 - i, 0)
    row = pl.BlockSpec((tm, D), rev)
    row2 = pl.BlockSpec((tm, D2), rev)
    vec = pl.BlockSpec((1, D), lambda i: (0, 0))
    pad = pl.BlockSpec((CONV_PAD, D), lambda i: (0, 0))
    return _pcall(body, name="conv_mid_bwd", grid=(nt,), in_specs=[row2, row, row, pad, vec, vec],
                  out_specs=[row2, pl.BlockSpec((1, D2), lambda i: (0, 0)), pad, vec, vec, vec],
                  out_shape=[_sds((S, D2), BF), _sds((1, D2), F32), _sds((CONV_PAD, D), F32), _sds((1, D), F32),
                             _sds((1, D), F32), _sds((1, D), F32)],
                  scratch_shapes=[pltpu.VMEM((tm + CONV_PAD, D), F32)], compiler_params=_cp("arbitrary"))(pre, v, dsw, dw, ln_g, ln_b)


def _pool_count(i, tm, w):
    t = i * tm + lax.broadcasted_iota(jnp.int32, (tm, 1), 0)
    return jnp.minimum(t + 1, w).astype(F32)


def pool_fwd(h, p_w, p_b, p_scale):
    S, D = h.shape
    G, Dg = p_w.shape[0], p_w.shape[1]
    tm = _tile(S, 256, POOL_PAD)

    def body(h_ref, w_ref, b_ref, s_ref, y_ref, d_ref, hbuf):
        i = pl.program_id(0)

        @pl.when(i == 0)
        def _():
            hbuf[pl.ds(0, POOL_PAD), :] = jnp.zeros((POOL_PAD, D), F32)

        @pl.when(i > 0)
        def _():
            hbuf[pl.ds(0, POOL_PAD), :] = hbuf[pl.ds(tm, POOL_PAD), :]

        hbuf[pl.ds(POOL_PAD, tm), :] = h_ref[...].astype(F32)
        for g, w in enumerate(POOL_WINDOWS):
            cols = pl.ds(g * Dg, Dg)
            hg = hbuf[pl.ds(POOL_PAD, tm), cols]
            win = hg
            for j in range(1, w):
                win = win + hbuf[pl.ds(POOL_PAD - j, tm), cols]
            d = (win / _pool_count(i, tm, w) - hg).astype(BF)
            d_ref[:, cols] = d
            y_ref[:, cols] = (_nn(d, w_ref[g].astype(BF)) + b_ref[:, cols]) * s_ref[:, cols]

    row = pl.BlockSpec((tm, D), lambda i: (i, 0))
    vec = pl.BlockSpec((1, D), lambda i: (0, 0))
    return _pcall(body, name="pool_fwd", grid=(S // tm,),
                  in_specs=[row, pl.BlockSpec((G, Dg, Dg), lambda i: (0, 0, 0)), vec, vec], out_specs=[row, row],
                  out_shape=[_sds((S, D), F32), _sds((S, D), BF)],
                  scratch_shapes=[pltpu.VMEM((tm + POOL_PAD, D), F32)], compiler_params=_cp("arbitrary"))(h, p_w, p_b, p_scale)


def pool_bwd(dy, d, p_w, p_b, p_scale):
    S, D = dy.shape
    G, Dg = p_w.shape[0], p_w.shape[1]
    tm = _tile(S, 256, POOL_PAD)
    nt = S // tm

    def body(dy_ref, d_ref, w_ref, b_ref, s_ref, dh_ref, dyy_ref, dpb_ref, dps_ref, ebuf):
        j = pl.program_id(0)
        i = nt - 1 - j
        first = j == 0

        @pl.when(first)
        def _():
            ebuf[pl.ds(tm, POOL_PAD), :] = jnp.zeros((POOL_PAD, D), F32)

        @pl.when(jnp.logical_not(first))
        def _():
            ebuf[pl.ds(tm, POOL_PAD), :] = ebuf[pl.ds(0, POOL_PAD), :]

        dpb, dps = [], []
        for g, w in enumerate(POOL_WINDOWS):
            cols = pl.ds(g * Dg, Dg)
            wg = w_ref[g].astype(BF)
            dyg = dy_ref[:, cols].astype(F32)
            ypre = _nn(d_ref[:, cols], wg) + b_ref[:, cols]
            dps.append(_colsum(dyg * ypre))
            dyy = dyg * s_ref[:, cols]
            dpb.append(_colsum(dyy))
            dyy = dyy.astype(BF)
            dyy_ref[:, cols] = dyy
            dd = _nt(dyy, wg)
            ebuf[pl.ds(0, tm), cols] = dd / _pool_count(i, tm, w)
            acc = -dd
            for k in range(w):
                acc = acc + ebuf[pl.ds(k, tm), cols]
            dh_ref[:, cols] = acc
        _acc(dpb_ref, first, jnp.concatenate(dpb, axis=1))
        _acc(dps_ref, first, jnp.concatenate(dps, axis=1))

    row = pl.BlockSpec((tm, D), lambda i: (nt - 1 - i, 0))
    vec = pl.BlockSpec((1, D), lambda i: (0, 0))
    return _pcall(body, name="pool_bwd", grid=(nt,),
                  in_specs=[row, row, pl.BlockSpec((G, Dg, Dg), lambda i: (0, 0, 0)), vec, vec],
                  out_specs=[row, row, vec, vec],
                  out_shape=[_sds((S, D), F32), _sds((S, D), BF), _sds((1, D), F32), _sds((1, D), F32)],
                  scratch_shapes=[pltpu.VMEM((tm + POOL_PAD, D), F32)], compiler_params=_cp("arbitrary"))(dy, d, p_w, p_b, p_scale)


def _group_sum(v, low):
    s_lo = jnp.sum(jnp.where(low, v, 0.0), axis=-1, keepdims=True)
    s_hi = jnp.sum(jnp.where(low, 0.0, v), axis=-1, keepdims=True)
    return jnp.where(low, s_lo, s_hi)


def qknorm_fwd(qkv, q_g2, k_g2):
    S, D3 = qkv.shape
    nb = D3 // LANES // 3
    tm = _tile(S, 512, 8)

    def body(x_ref, qg_ref, kg_ref, o_ref):
        low = lax.broadcasted_iota(jnp.int32, (1, LANES), 1) < HEAD_DIM
        for j in range(3 * nb):
            cols = pl.ds(j * LANES, LANES)
            xv = x_ref[:, cols]
            if j < 2 * nb:
                r = lax.rsqrt(_group_sum(xv * xv, low) * (1.0 / HEAD_DIM) + EPS)
                xv = ((xv * r) * qg_ref[...]) * Q_SCALE if j < nb else (xv * r) * kg_ref[...]
            o_ref[:, cols] = xv.astype(o_ref.dtype)

    blk = pl.BlockSpec((tm, D3), lambda i: (i, 0))
    vec = pl.BlockSpec((1, LANES), lambda i: (0, 0))
    return _pcall(body, name="qknorm_fwd", grid=(S // tm,), in_specs=[blk, vec, vec], out_specs=blk,
                  out_shape=_sds((S, D3), BF), compiler_params=_cp("parallel"))(qkv, q_g2, k_g2)


def qknorm_bwd(qkv, dq, dk, dv, q_g2, k_g2):
    S, D3 = qkv.shape
    D = D3 // 3
    nb = D // LANES
    tm = _tile(S, 256, 8)

    def body(x_ref, dq_ref, dk_ref, dv_ref, qg_ref, kg_ref, o_ref, dg_ref):
        low = lax.broadcasted_iota(jnp.int32, (1, LANES), 1) < HEAD_DIM
        dgs = []
        for j in range(3 * nb):
            cols, part = pl.ds(j * LANES, LANES), pl.ds((j % nb) * LANES, LANES)
            if j >= 2 * nb:
                o_ref[:, cols] = dv_ref[:, part].astype(o_ref.dtype)
                dgs.append(jnp.zeros((1, LANES), F32))
                continue
            xv = x_ref[:, cols]
            r = lax.rsqrt(_group_sum(xv * xv, low) * (1.0 / HEAD_DIM) + EPS)
            xh = xv * r
            dn = dq_ref[:, part] * (Q_SCALE * LN2) if j < nb else dk_ref[:, part] * LN2
            u = dn * (qg_ref[...] if j < nb else kg_ref[...])
            o_ref[:, cols] = (r * (u - xh * (_group_sum(u * xh, low) * (1.0 / HEAD_DIM)))).astype(o_ref.dtype)
            dgs.append(_colsum(dn * xh))
        _acc(dg_ref, pl.program_id(0) == 0, jnp.concatenate(dgs, axis=1))

    wide = pl.BlockSpec((tm, D3), lambda i: (i, 0))
    row = pl.BlockSpec((tm, D), lambda i: (i, 0))
    vec = pl.BlockSpec((1, LANES), lambda i: (0, 0))
    return _pcall(body, name="qknorm_bwd", grid=(S // tm,), in_specs=[wide, row, row, row, vec, vec],
                  out_specs=[wide, pl.BlockSpec((1, D3), lambda i: (0, 0))],
                  out_shape=[_sds((S, D3), BF), _sds((1, D3), F32)], compiler_params=_cp("arbitrary"))(qkv, dq, dk, dv, q_g2, k_g2)


LOG2E = math.log2(math.e)
LN2 = math.log(2.0)
Q_SCALE = HEAD_DIM ** -0.5 * LOG2E
FAR = 160.0
UNSEEN = 1e30


def _softplus2(z2):
    return jnp.maximum(z2, 0.0) + jnp.log(1.0 + jnp.exp2(-jnp.abs(z2))) * LOG2E


def _attn_tile(S):
    return _tile(S, 256, LANES) if S >= 1024 else LANES


def attn_fwd(qkvn):
    S, D3 = qkvn.shape
    D = D3 // 3
    nb = D // LANES
    t = _attn_tile(S)
    nq = S // t
    assert nq <= LANES

    def body(q_ref, k_ref, v_ref, o_ref, r_ref, acc_ref, rall_ref, behind_ref, z_ref):
        qi = pl.program_id(1)
        lane = lax.broadcasted_iota(jnp.int32, (1, LANES), 1)
        rows = lax.broadcasted_iota(jnp.int32, (t, t), 0)
        cols = lax.broadcasted_iota(jnp.int32, (t, t), 1)
        later = (rows > cols).astype(BF)
        q = q_ref[...]
        qh = [jnp.where(lane < HEAD_DIM, q, jnp.zeros_like(q)), jnp.where(lane >= HEAD_DIM, q, jnp.zeros_like(q))]
        acc_ref[...] = jnp.zeros((2, t, LANES), F32)
        rall_ref[...] = jnp.full((2, t, LANES), UNSEEN, F32)
        behind_ref[...] = jnp.zeros((2, t, LANES), F32)

        def scores(b):
            kb = k_ref[pl.ds(pl.multiple_of(b * t, t), t), :]
            return [_nt(qh[0], kb), _nt(qh[1], kb)]

        def block(b, diagonal):
            vb = v_ref[pl.ds(pl.multiple_of(b * t, t), t), :]
            lsig, sp, within = [None, None], [None, None], [None, None]
            for hh in range(2):
                z = z_ref[hh]
                sp_full = _softplus2(z)
                lsig[hh] = z - sp_full
                sp[hh] = jnp.where(cols < rows, sp_full, 0.0) if diagonal else sp_full
                within[hh] = _nn(sp[hh].astype(BF), later)
            z_next = scores(jnp.maximum(b - 1, 0))
            for hh in range(2):
                behind = behind_ref[hh]
                a = jnp.exp2(lsig[hh] - (within[hh] + jnp.tile(behind, (1, t // LANES))))
                if diagonal:
                    a = jnp.where(cols < rows, a, 0.0)
                acc_ref[hh] += _nn(a.astype(BF), vb)
                rall_ref[hh] = jnp.where(lane == b, behind, rall_ref[hh])
                behind_ref[hh] = behind + jnp.sum(sp[hh], axis=-1, keepdims=True)
            for hh in range(2):
                z_ref[hh] = z_next[hh]

        z_first = scores(qi)
        for hh in range(2):
            z_ref[hh] = z_first[hh]
        block(qi, True)

        def reaches():
            return (jnp.min(behind_ref[...]) < FAR).astype(jnp.int32)

        def step(carry):
            n, _ = carry
            block(qi - 1 - n, False)
            return n + 1, reaches()

        lax.while_loop(lambda carry: jnp.logical_and(carry[0] < qi, carry[1] > 0), step, (jnp.int32(0), reaches()))
        r_ref[...] = rall_ref[...]
        o_ref[...] = jnp.where(lane < HEAD_DIM, acc_ref[0], acc_ref[1]).astype(o_ref.dtype)

    return _pcall(body, name="attn_fwd", grid=(nb, nq),
                  in_specs=[pl.BlockSpec((t, LANES), lambda h, i: (i, h)),
                            pl.BlockSpec((S, LANES), lambda h, i: (0, nb + h)),
                            pl.BlockSpec((S, LANES), lambda h, i: (0, 2 * nb + h))],
                  out_specs=[pl.BlockSpec((t, LANES), lambda h, i: (i, h)), pl.BlockSpec((2, t, LANES), lambda h, i: (h, i, 0))],
                  out_shape=[_sds((S, D), BF), _sds((2 * nb, S, LANES), F32)],
                  scratch_shapes=[pltpu.VMEM((2, t, LANES), F32)] * 3 + [pltpu.VMEM((2, t, t), F32)],
                  compiler_params=_cp("parallel", "arbitrary"))(qkvn, qkvn, qkvn)


def attn_bwd(qkvn, do, r_all):
    S, D3 = qkvn.shape
    D = D3 // 3
    nb = D // LANES
    t = _attn_tile(S)
    nq = S // t

    def body(q_ref, k_ref, v_ref, do_ref, r_ref, dq_ref, dk_hbm, dv_hbm, dk_acc, dv_acc, dq_acc, before_ref, z_ref, da_ref, sem):
        hp = pl.program_id(0)
        qi = pl.program_id(1)

        @pl.when(qi == 0)
        def _():
            dk_acc[...] = jnp.zeros((S, LANES), F32)
            dv_acc[...] = jnp.zeros((S, LANES), F32)

        lane = lax.broadcasted_iota(jnp.int32, (1, LANES), 1)
        rows = lax.broadcasted_iota(jnp.int32, (t, t), 0)
        cols = lax.broadcasted_iota(jnp.int32, (t, t), 1)
        later = (rows > cols).astype(BF)
        earlier = (rows < cols).astype(BF)
        q = q_ref[...]
        dov = do_ref[...]
        low, high = lane < HEAD_DIM, lane >= HEAD_DIM
        qh = [jnp.where(low, q, jnp.zeros_like(q)), jnp.where(high, q, jnp.zeros_like(q))]
        doh = [jnp.where(low, dov, jnp.zeros_like(dov)), jnp.where(high, dov, jnp.zeros_like(dov))]
        q_both = jnp.concatenate(qh, axis=0)
        do_both = jnp.concatenate(doh, axis=0)
        dq_acc[...] = jnp.zeros((2, t, LANES), F32)
        before_ref[...] = jnp.zeros((2, t, LANES), F32)

        def scores(b):
            off = pl.multiple_of(b * t, t)
            kb = k_ref[pl.ds(off, t), :]
            vb = v_ref[pl.ds(off, t), :]
            return [_nt(qh[0], kb), _nt(qh[1], kb)], [_nt(doh[0], vb), _nt(doh[1], vb)]

        def block(b, diagonal):
            off = pl.multiple_of(b * t, t)
            kb = k_ref[pl.ds(off, t), :]
            lsig, sigma, within, g, earlier_g, dz, prob = ([None, None] for _ in range(7))
            for hh in range(2):
                z = z_ref[hh]
                sp = _softplus2(z)
                lsig[hh] = z - sp
                if diagonal:
                    sp = jnp.where(cols < rows, sp, 0.0)
                within[hh] = _nn(sp.astype(BF), later)
            z_next, da_next = scores(jnp.minimum(b + 1, qi))
            for hh in range(2):
                behind = jnp.sum(jnp.where(lane == b, r_ref[hh], 0.0), axis=-1, keepdims=True)
                a = jnp.exp2(lsig[hh] - (within[hh] + behind))
                if diagonal:
                    a = jnp.where(cols < rows, a, 0.0)
                prob[hh] = a.astype(BF)
                g[hh] = a * da_ref[hh]
                earlier_g[hh] = _nn(g[hh].astype(BF), earlier)
            for hh in range(2):
                before = before_ref[hh]
                d = g[hh] - jnp.exp2(lsig[hh]) * (g[hh] + (earlier_g[hh] + jnp.tile(before, (1, t // LANES))))
                if diagonal:
                    d = jnp.where(cols < rows, d, 0.0)
                dz[hh] = d.astype(BF)
                dq_acc[hh] += _nn(dz[hh], kb)
                before_ref[hh] = before + jnp.sum(g[hh], axis=-1, keepdims=True)
            dk_acc[pl.ds(off, t), :] += _tn(jnp.concatenate(dz, axis=0), q_both)
            dv_acc[pl.ds(off, t), :] += _tn(jnp.concatenate(prob, axis=0), do_both)
            for hh in range(2):
                z_ref[hh] = z_next[hh]
                da_ref[hh] = da_next[hh]

        nearest = jnp.min(jnp.minimum(r_ref[0], r_ref[1]), axis=0, keepdims=True)
        skip = jnp.sum(jnp.where(jnp.logical_and(lane < qi, nearest >= FAR), 1, 0))
        z_first, da_first = scores(skip)
        for hh in range(2):
            z_ref[hh] = z_first[hh]
            da_ref[hh] = da_first[hh]

        def step(b, carry):
            block(b, False)
            return carry

        lax.fori_loop(skip, qi, step, 0)
        block(qi, True)
        dq_ref[...] = jnp.where(low, dq_acc[0], dq_acc[1])

        @pl.when(qi == nq - 1)
        def _():
            col = pl.multiple_of(hp * LANES, LANES)
            c1 = pltpu.make_async_copy(dk_acc, dk_hbm.at[:, pl.ds(col, LANES)], sem.at[0])
            c2 = pltpu.make_async_copy(dv_acc, dv_hbm.at[:, pl.ds(col, LANES)], sem.at[1])
            c1.start()
            c2.start()
            c1.wait()
            c2.wait()

    blk = pl.BlockSpec((t, LANES), lambda h, i: (i, h))
    hbm = pl.BlockSpec(memory_space=pl.ANY)
    return _pcall(body, name="attn_bwd", grid=(nb, nq),
                  in_specs=[blk, pl.BlockSpec((S, LANES), lambda h, i: (0, nb + h)),
                            pl.BlockSpec((S, LANES), lambda h, i: (0, 2 * nb + h)), blk,
                            pl.BlockSpec((2, t, LANES), lambda h, i: (h, i, 0))],
                  out_specs=[blk, hbm, hbm],
                  out_shape=[_sds((S, D), F32), _sds((S, D), F32), _sds((S, D), F32)],
                  scratch_shapes=[pltpu.VMEM((S, LANES), F32), pltpu.VMEM((S, LANES), F32), pltpu.VMEM((2, t, LANES), F32),
                                  pltpu.VMEM((2, t, LANES), F32), pltpu.VMEM((2, t, t), F32), pltpu.VMEM((2, t, t), F32),
                                  pltpu.SemaphoreType.DMA((2,))],
                  compiler_params=_cp("arbitrary", "arbitrary"))(qkvn, qkvn, qkvn, do, r_all)


def cond_embed(parts, cond_b):
    P, B, D = parts.shape

    def body(p_ref, b_ref, pre_ref, e_ref):
        pre = p_ref[0]
        for s in range(1, P):
            pre = pre + p_ref[s]
        pre = pre + b_ref[...]
        pre_ref[...] = pre
        e_ref[...] = pre * _sig(pre)

    return _pcall(body, name="cond_embed", out_shape=[_sds((B, D), F32), _sds((B, D), F32)],
                  compiler_params=_cp())(parts, cond_b)


def cond_embed_bwd(parts, pre):
    P, B, D = parts.shape

    def body(p_ref, pre_ref, dpre_ref, db_ref):
        de = p_ref[0]
        for s in range(1, P):
            de = de + p_ref[s]
        pre = pre_ref[...]
        s = _sig(pre)
        dpre = de * (s * (1.0 + pre * (1.0 - s)))
        dpre_ref[...] = dpre
        db_ref[...] = _colsum(dpre)

    return _pcall(body, name="cond_embed_bwd", out_shape=[_sds((B, D), F32), _sds((1, D), F32)],
                  compiler_params=_cp())(parts, pre)


def adamw(w, parts, m, v):
    R, C = w.shape
    P = parts.shape[0]
    tr = _tile(R, max(8, (1 << 17) // C // 8 * 8), 8)
    c1 = 1.0 - ADAM_B1 ** ADAM_STEP
    c2 = 1.0 - ADAM_B2 ** ADAM_STEP

    def body(w_ref, p_ref, m_ref, v_ref, g_ref, d_ref, nm_ref, nv_ref):
        g = p_ref[0].astype(F32)
        for s in range(1, P):
            g = g + p_ref[s].astype(F32)
        nm = ADAM_B1 * m_ref[...] + (1.0 - ADAM_B1) * g
        nv = ADAM_B2 * v_ref[...] + (1.0 - ADAM_B2) * (g * g)
        g_ref[...] = g
        nm_ref[...] = nm
        nv_ref[...] = nv
        d_ref[...] = -ADAM_LR * ((nm / c1) / (jnp.sqrt(nv / c2) + ADAM_EPS) + ADAM_WD * w_ref[...])

    blk = pl.BlockSpec((tr, C), lambda i: (i, 0))
    return _pcall(body, name="adamw", grid=(R // tr,), in_specs=[blk, pl.BlockSpec((P, tr, C), lambda i: (0, i, 0)), blk, blk],
                  out_specs=[blk] * 4, out_shape=[_sds((R, C), F32)] * 4, compiler_params=_cp("parallel"))(w, parts, m, v)


def exchange(arrs, peer_axes):
    n = len(arrs)
    shapes = [a.shape if ax is None else a.shape[:ax] + a.shape[ax + 1:] for a, ax in zip(arrs, peer_axes)]

    def body(*refs):
        ins, outs = refs[:n], refs[n:2 * n]
        send, recv, local = refs[2 * n:]
        x, y, c = lax.axis_index("x"), lax.axis_index("y"), lax.axis_index("c")
        me = 4 * x + 2 * y + c

        def piece(i, d):
            if peer_axes[i] is None:
                return ins[i]
            return ins[i].at[(slice(None),) * peer_axes[i] + (d,)]

        mine = [pltpu.make_async_copy(piece(i, me), outs[i].at[me], local.at[i]) for i in range(n)]
        for cp in mine:
            cp.start()
        sends, recvs = [], []
        for p in range(1, N_DEV):
            peer = (1 - x if p & 4 else x, 1 - y if p & 2 else y, 1 - c if p & 1 else c)
            them = 4 * peer[0] + 2 * peer[1] + peer[2]
            for i in range(n):
                sends.append(pltpu.make_async_remote_copy(piece(i, them), outs[i].at[me], send.at[i, p - 1], recv.at[i, p - 1],
                                                          device_id=peer, device_id_type=MESH))
                recvs.append(pltpu.make_async_remote_copy(piece(i, me), outs[i].at[them], send.at[i, p - 1], recv.at[i, p - 1],
                                                          device_id=peer, device_id_type=MESH))
        for cp in sends:
            cp.start()
        for cp in recvs:
            cp.wait_recv()
        for cp in sends:
            cp.wait_send()
        for cp in mine:
            cp.wait()

    hbm = pl.BlockSpec(memory_space=pl.ANY)
    return _pcall(body, name="exchange", in_specs=[hbm] * n, out_specs=[hbm] * n,
                  out_shape=[_sds((N_DEV,) + s, a.dtype) for s, a in zip(shapes, arrs)],
                  scratch_shapes=[pltpu.SemaphoreType.DMA((n, N_DEV - 1)), pltpu.SemaphoreType.DMA((n, N_DEV - 1)),
                                  pltpu.SemaphoreType.DMA((n,))])(*arrs)


def gather_via_sibling(arrs):
    n = len(arrs)

    def body(*refs):
        ins, outs = refs[:n], refs[n:2 * n]
        send, recv, local = refs[2 * n:]
        x, y, c = lax.axis_index("x"), lax.axis_index("y"), lax.axis_index("c")
        me, sibling = (x, y, c), (x, y, 1 - c)
        chips = [(1 - x, y), (x, 1 - y), (1 - x, 1 - y)]
        index = lambda d: 4 * d[0] + 2 * d[1] + d[2]

        def copy(i, k, block, to, src=None):
            dst = outs[i].at[index(block)]
            return pltpu.make_async_remote_copy(dst if src is None else src, dst, send.at[i, k], recv.at[i, k],
                                                device_id=to, device_id_type=MESH)

        mine = [pltpu.make_async_copy(ins[i], outs[i].at[index(me)], local.at[i]) for i in range(n)]
        for cp in mine:
            cp.start()
        first = []
        for i in range(n):
            first.append(copy(i, 0, me, sibling, src=ins[i]))
            first += [copy(i, 1 + j, me, (*chip, c), src=ins[i]) for j, chip in enumerate(chips)]
        for cp in first:
            cp.start()
        passed = []
        for j, chip in enumerate(chips):
            for i in range(n):
                copy(i, 1 + j, (*chip, c), me).wait_recv()
                passed.append(copy(i, 4 + j, (*chip, c), sibling))
                passed[-1].start()
        for i in range(n):
            copy(i, 0, sibling, me).wait_recv()
            for j, chip in enumerate(chips):
                copy(i, 4 + j, (*chip, 1 - c), me).wait_recv()
        for cp in first + passed:
            cp.wait_send()
        for cp in mine:
            cp.wait()

    hbm = pl.BlockSpec(memory_space=pl.ANY)
    return _pcall(body, name="gather_via_sibling", in_specs=[hbm] * n, out_specs=[hbm] * n,
                  out_shape=[_sds((N_DEV,) + a.shape, a.dtype) for a in arrs],
                  scratch_shapes=[pltpu.SemaphoreType.DMA((n, N_DEV - 1)), pltpu.SemaphoreType.DMA((n, N_DEV - 1)),
                                  pltpu.SemaphoreType.DMA((n,))])(*arrs)


def _pack(vs):
    flat = jnp.concatenate([v.reshape(-1).astype(F32) for v in vs])
    pad = (-flat.shape[0]) % 1024
    return jnp.pad(flat, (0, pad)).reshape(-1, 1024)


def _unpack(packed, shapes):
    flat = packed.reshape(packed.shape[0], -1)
    out, o = [], 0
    for s in shapes:
        n = math.prod(s)
        out.append(flat[:, o:o + n].reshape((packed.shape[0],) + tuple(s)))
        o += n
    return out


def _cat_dev(g, axis):
    g = jnp.moveaxis(g, 0, axis)
    return g.reshape(g.shape[:axis] + (g.shape[axis] * g.shape[axis + 1],) + g.shape[axis + 2:])


def kernel(x, c, cond_w, cond_b, ada_w, ada_b, norm_g, ffn_w1, ffn_w3, ffn_w2, a_w_in, a_b_in, a_dw, a_dw_b, a_ln_g, a_ln_b, a_w_out, a_b_out, b_w_qkv, b_q_g, b_k_g, b_w_o, p_w, p_b, p_scale, loss_target, m_cond_w, m_cond_b, m_ada_w, m_ada_b, m_norm_g, m_ffn_w1, m_ffn_w3, m_ffn_w2, m_a_w_in, m_a_b_in, m_a_dw, m_a_dw_b, m_a_ln_g, m_a_ln_b, m_a_w_out, m_a_b_out, m_b_w_qkv, m_b_q_g, m_b_k_g, m_b_w_o, m_p_w, m_p_b, m_p_scale, v_cond_w, v_cond_b, v_ada_w, v_ada_b, v_norm_g, v_ffn_w1, v_ffn_w3, v_ffn_w2, v_a_w_in, v_a_b_in, v_a_dw, v_a_dw_b, v_a_ln_g, v_a_ln_b, v_a_w_out, v_a_b_out, v_b_w_qkv, v_b_q_g, v_b_k_g, v_b_w_o, v_p_w, v_p_b, v_p_scale):
    A = dict(zip(ARGS, (x, c, cond_w, cond_b, ada_w, ada_b, norm_g, ffn_w1, ffn_w3, ffn_w2, a_w_in, a_b_in, a_dw, a_dw_b, a_ln_g, a_ln_b, a_w_out, a_b_out, b_w_qkv, b_q_g, b_k_g, b_w_o, p_w, p_b, p_scale, loss_target, m_cond_w, m_cond_b, m_ada_w, m_ada_b, m_norm_g, m_ffn_w1, m_ffn_w3, m_ffn_w2, m_a_w_in, m_a_b_in, m_a_dw, m_a_dw_b, m_a_ln_g, m_a_ln_b, m_a_w_out, m_a_b_out, m_b_w_qkv, m_b_q_g, m_b_k_g, m_b_w_o, m_p_w, m_p_b, m_p_scale, v_cond_w, v_cond_b, v_ada_w, v_ada_b, v_norm_g, v_ffn_w1, v_ffn_w3, v_ffn_w2, v_a_w_in, v_a_b_in, v_a_dw, v_a_dw_b, v_a_ln_g, v_a_ln_b, v_a_w_out, v_a_b_out, v_b_w_qkv, v_b_q_g, v_b_k_g, v_b_w_o, v_p_w, v_p_b, v_p_scale)))
    S, D = x.shape[1], x.shape[2]
    depth = ada_w.shape[0]
    n_a, n_b, n_c = a_w_in.shape[0], b_w_qkv.shape[0], p_w.shape[0]
    me = 4 * lax.axis_index("x") + 2 * lax.axis_index("y") + lax.axis_index("c")
    swap = lambda w: jnp.swapaxes(w, -1, -2)

    small_names = ['norm_g', 'a_b_in', 'a_dw', 'a_dw_b', 'a_ln_g', 'a_ln_b', 'a_b_out', 'p_w', 'p_b', 'p_scale']
    small_in = [c] + [A[n] for n in small_names]
    big_in = [swap(ffn_w1).astype(BF), swap(ffn_w3).astype(BF), ffn_w2.astype(BF), swap(a_w_in).astype(BF),
              a_w_out.astype(BF), swap(b_w_qkv).astype(BF), b_w_o.astype(BF)]
    got = gather_via_sibling(big_in + [_pack(small_in)])
    w1t, w3t, w2 = _cat_dev(got[0], 2), _cat_dev(got[1], 2), _cat_dev(got[2], 2)
    w_in_t, w_out = _cat_dev(got[3], 1), _cat_dev(got[4], 1)
    w_qkv_t, w_o = _cat_dev(got[5], 1), _cat_dev(got[6], 1)
    sm = dict(zip(['c'] + small_names, _unpack(got[7], [v.shape for v in small_in])))
    c_all = sm['c'][:, 0]
    norm_g_f = _cat_dev(sm['norm_g'], 2)
    a_b_in_f, a_dw_f, a_dw_b_f = _cat_dev(sm['a_b_in'], 1), _cat_dev(sm['a_dw'], 2), _cat_dev(sm['a_dw_b'], 1)
    a_ln_g_f, a_ln_b_f, a_b_out_f = _cat_dev(sm['a_ln_g'], 1), _cat_dev(sm['a_ln_b'], 1), _cat_dev(sm['a_b_out'], 1)
    p_w_f, p_b_f, p_scale_f = _cat_dev(sm['p_w'], 2), _cat_dev(sm['p_b'], 2), _cat_dev(sm['p_scale'], 1)
    a_dw_f = jnp.pad(a_dw_f, ((0, 0), (0, CONV_PAD - CONV_W), (0, 0)))

    rows = D // N_DEV
    c_mine = lax.dynamic_slice_in_dim(c_all, me * rows, rows, axis=1)
    pre_parts = exchange([matmul_nn(c_mine, cond_w)], [None])[0]
    pre_all, e_all = cond_embed(pre_parts, cond_b.reshape(1, D))
    n_mod = ada_w.shape[2]
    ada_b_mine = lax.dynamic_slice_in_dim(ada_b, me * n_mod, n_mod, axis=1)
    mod_part = jnp.stack([matmul_nn(e_all, ada_w[i], ada_b_mine[i:i + 1]) for i in range(depth)], axis=1)
    mod_all = exchange([mod_part], [None])[0]
    mod = lax.dynamic_index_in_dim(mod_all, me, axis=1, keepdims=False)
    mod = jnp.moveaxis(mod, 0, 1).reshape(depth, 3, 3, 1, D)

    xs = x[0]
    saved = []
    ia = ib = ic = 0

    def mod_of(i, sub):
        return norm_g_f[i, sub].reshape(1, D), mod[i, sub, 1], mod[i, sub, 0], 1.0 + mod[i, sub, 2]

    h = mod_fwd(xs, *mod_of(0, 0)[:3])
    for i in range(depth):
        lay = {}
        for sub in range(3):
            g, scale, shift, gate = mod_of(i, sub)
            rec = dict(x=xs, h=h, g=g, scale=scale, gate=gate)
            if sub != 1:
                j = 0 if sub == 0 else 1
                y, g1s, g3s, act = ffn_fwd(h, w1t[i, j], w3t[i, j], w2[i, j])
                rec.update(kind='ffn', j=j, coef=0.5, g1=g1s, g3=g3s, act=act)
            elif i % 3 == 0:
                pre = matmul_nt(h, w_in_t[ia], a_b_in_f[ia].reshape(1, 2 * D))
                v, sw = conv_mid_fwd(pre, a_dw_f[ia], a_dw_b_f[ia].reshape(1, D), a_ln_g_f[ia].reshape(1, D), a_ln_b_f[ia].reshape(1, D))
                y = matmul_nn(sw, w_out[ia], a_b_out_f[ia].reshape(1, D))
                rec.update(kind='conv', idx=ia, coef=1.0, pre=pre, v=v, sw=sw)
                ia += 1
            elif i % 3 == 1:
                qkv = matmul_nt(h, w_qkv_t[ib])
                qg2 = jnp.tile(b_q_g[ib].reshape(1, HEAD_DIM), (1, 2))
                kg2 = jnp.tile(b_k_g[ib].reshape(1, HEAD_DIM), (1, 2))
                qkvn = qknorm_fwd(qkv, qg2, kg2)
                o, r_all = attn_fwd(qkvn)
                y = matmul_nn(o, w_o[ib])
                rec.update(kind='attn', idx=ib, coef=1.0, qkv=qkv, qkvn=qkvn, o=o, r_all=r_all, qg2=qg2, kg2=kg2)
                ib += 1
            else:
                pb, ps = p_b_f[ic].reshape(1, D), p_scale_f[ic].reshape(1, D)
                y, dpool = pool_fwd(h, p_w_f[ic], pb, ps)
                rec.update(kind='pool', idx=ic, coef=1.0, d=dpool, pb=pb, ps=ps)
                ic += 1
            rec['y'] = y
            if (i, sub) == (depth - 1, 2):
                xs = resid_fwd(xs, y, gate, rec['coef'])
            else:
                xs, h = resid_mod_fwd(xs, y, gate, rec['coef'], *mod_of(*((i, sub + 1) if sub < 2 else (i + 1, 0)))[:3])
            lay[sub] = rec
        saved.append(lay)

    loss_part, dx = loss_fwd_bwd(xs, loss_target[0])
    loss = lax.psum(loss_part[0, 0], ("x", "y", "c"))

    Fd = w2.shape[2]
    zeros = lambda *s: jnp.zeros(s, F32)
    g_w1t, g_w3t, g_w2 = [[None, None] for _ in range(depth)], [[None, None] for _ in range(depth)], [[None, None] for _ in range(depth)]
    g_w_in_t, g_w_out, g_w_qkv_t, g_w_o, g_p_w = [None] * n_a, [None] * n_a, [None] * n_b, [None] * n_b, [None] * n_c
    g_b_in, g_dw, g_dw_b, g_ln_g, g_ln_b, g_b_out = ([None] * n_a for _ in range(6))
    g_q_g, g_k_g, g_p_b, g_p_scale = [None] * n_b, [None] * n_b, [None] * n_c, [None] * n_c
    d_mod = [[None] * 3 for _ in range(depth)]
    d_norm_g = [[None] * 3 for _ in range(depth)]
    for i in reversed(range(depth)):
        for sub in reversed(range(3)):
            rec = saved[i][sub]
            if (i, sub) == (depth - 1, 2):
                dy, d_gate, dy_sum = gate_bwd(dx, rec['y'], rec['gate'], rec['coef'])
            if rec['kind'] == 'ffn':
                j = rec['j']
                dh, d1, d3 = ffn_bwd(dy, rec['g1'], rec['g3'], w1t[i, j], w3t[i, j], w2[i, j])
                g_w1t[i][j] = matmul_tn(d1, rec['h'])
                g_w3t[i][j] = matmul_tn(d3, rec['h'])
                g_w2[i][j] = matmul_tn(rec['act'], dy)
            elif rec['kind'] == 'conv':
                k = rec['idx']
                dsw = matmul_nt(dy, w_out[k])
                g_w_out[k] = matmul_tn(rec['sw'], dy)
                g_b_out[k] = dy_sum
                dpre, g_b_in[k], ddw, g_dw_b[k], g_ln_g[k], g_ln_b[k] = conv_mid_bwd(
                    rec['pre'], rec['v'], dsw, a_dw_f[k], a_ln_g_f[k].reshape(1, D), a_ln_b_f[k].reshape(1, D))
                g_dw[k] = ddw[:CONV_W]
                dh = matmul_nn(dpre, w_in_t[k])
                g_w_in_t[k] = matmul_tn(dpre, rec['h'])
            elif rec['kind'] == 'attn':
                k = rec['idx']
                do = matmul_nt(dy, w_o[k], out_dtype=BF)
                g_w_o[k] = matmul_tn(rec['o'], dy)
                dq, dk, dv = attn_bwd(rec['qkvn'], do, rec['r_all'])
                dqkv, dgains = qknorm_bwd(rec['qkv'], dq, dk, dv, rec['qg2'], rec['kg2'])
                dgains = dgains.reshape(3, N_HEADS, HEAD_DIM)
                g_q_g[k], g_k_g[k] = jnp.sum(dgains[0], axis=0), jnp.sum(dgains[1], axis=0)
                dh = matmul_nn(dqkv, w_qkv_t[k])
                g_w_qkv_t[k] = matmul_tn(dqkv, rec['h'])
            else:
                k = rec['idx']
                dh, dyy, g_p_b[k], g_p_scale[k] = pool_bwd(dy, rec['d'], p_w_f[k], rec['pb'], rec['ps'])
                full = matmul_tn(rec['d'], dyy)
                G = p_w_f.shape[1]
                Dg = D // G
                g_p_w[k] = jnp.stack([full[g * Dg:(g + 1) * Dg, g * Dg:(g + 1) * Dg] for g in range(G)])
            if (i, sub) == (0, 0):
                dx, d_shift, d_scale, d_norm_g[i][sub] = mod_bwd(rec['x'], dh, dx, rec['g'], rec['scale'])
                nxt = None
            else:
                before = saved[i][sub - 1] if sub > 0 else saved[i - 1][2]
                dx, d_shift, d_scale, d_norm_g[i][sub], *nxt = mod_gate_bwd(rec['x'], dh, dx, rec['g'], rec['scale'],
                                                                             before['y'], before['gate'], before['coef'])
            d_mod[i][sub] = jnp.concatenate([d_shift, d_scale, d_gate], axis=0)
            if nxt is not None:
                dy, d_gate, dy_sum = nxt
    grad_x = dx[None]

    d_mod_mine = jnp.stack([jnp.stack(r) for r in d_mod])
    small_g = [d_mod_mine, jnp.stack([jnp.concatenate(r, axis=0) for r in d_norm_g]),
               jnp.stack(g_b_in), jnp.stack(g_dw), jnp.stack(g_dw_b), jnp.stack(g_ln_g), jnp.stack(g_ln_b), jnp.stack(g_b_out),
               jnp.stack(g_q_g), jnp.stack(g_k_g), jnp.stack(g_p_b), jnp.stack(g_p_scale)]
    stack2 = lambda g: jnp.stack([jnp.stack(r) for r in g])
    blocks = lambda g, ax: g.reshape(g.shape[:ax] + (N_DEV, g.shape[ax] // N_DEV) + g.shape[ax + 1:])
    big_g = [blocks(stack2(g_w1t), 2), blocks(stack2(g_w3t), 2), blocks(stack2(g_w2), 2), blocks(jnp.stack(g_w_in_t), 1),
             blocks(jnp.stack(g_w_out), 1), blocks(jnp.stack(g_w_qkv_t), 1), blocks(jnp.stack(g_w_o), 1), blocks(jnp.stack(g_p_w), 2)]
    big_ax = [2, 2, 2, 1, 1, 1, 1, 2]
    got = exchange(big_g + [_pack(small_g)], big_ax + [None])
    p_w1t, p_w3t, p_w2, p_w_in_t, p_w_out, p_w_qkv_t, p_w_o, p_p_w = got[:8]
    (dmod_all, dng_all, dbin_all, ddw_all, ddwb_all, dlng_all, dlnb_all, dbout_all, dqg_all, dkg_all, dpb_all,
     dps_all) = _unpack(got[8], [v.shape for v in small_g])
    dmod_all = dmod_all.reshape(N_DEV, depth, 9 * D)

    dmod_mine = lax.dynamic_slice_in_dim(dmod_all, me * n_mod, n_mod, axis=2)
    e_t = e_all.T
    grad_ada_w = jnp.stack([outer_rows(e_t, dmod_mine[:, i]) for i in range(depth)])
    de_part = matmul_nt(dmod_mine[:, 0], ada_w[0])
    for i in range(1, depth):
        de_part = de_part + matmul_nt(dmod_mine[:, i], ada_w[i])
    de_parts = exchange([de_part], [None])[0]
    dpre_all, grad_cond_b = cond_embed_bwd(de_parts, pre_all)
    grad_cond_w = outer_rows(c_mine.T, dpre_all)

    def mine(g_all, axis, size):
        return lax.dynamic_slice_in_dim(g_all, me * size, size, axis=axis)

    def upd(name, parts, shape2, to_out=lambda t: t, from_in=lambda t: t):
        w, m, v = (from_in(A[p + name]).reshape(shape2) for p in ('', 'm_', 'v_'))
        outs = adamw(w, parts.reshape((parts.shape[0],) + shape2), m, v)
        return [to_out(o).reshape(A[name].shape) for o in outs]

    L = depth
    res = {}
    res['cond_w'] = upd('cond_w', grad_cond_w[None], (rows, D))
    res['cond_b'] = upd('cond_b', grad_cond_b[None], (1, D))
    res['ada_w'] = upd('ada_w', grad_ada_w[None], (L * D, n_mod))
    res['ada_b'] = upd('ada_b', dmod_all, (L, 9 * D))
    res['norm_g'] = upd('norm_g', mine(dng_all, 3, D // N_DEV), (L * 3, D // N_DEV))
    fs = Fd // N_DEV
    res['ffn_w1'] = upd('ffn_w1', p_w1t, (L * 2 * fs, D), to_out=lambda t: swap(t.reshape(L, 2, fs, D)), from_in=swap)
    res['ffn_w3'] = upd('ffn_w3', p_w3t, (L * 2 * fs, D), to_out=lambda t: swap(t.reshape(L, 2, fs, D)), from_in=swap)
    res['ffn_w2'] = upd('ffn_w2', p_w2, (L * 2 * fs, D))
    ws = 2 * D // N_DEV
    res['a_w_in'] = upd('a_w_in', p_w_in_t, (n_a * ws, D), to_out=lambda t: swap(t.reshape(n_a, ws, D)), from_in=swap)
    res['a_b_in'] = upd('a_b_in', mine(dbin_all.reshape(N_DEV, n_a, 2 * D), 2, ws), (n_a, ws))
    res['a_dw'] = upd('a_dw', mine(ddw_all, 3, rows), (n_a * CONV_W, rows))
    res['a_dw_b'] = upd('a_dw_b', mine(ddwb_all.reshape(N_DEV, n_a, D), 2, rows), (n_a, rows))
    res['a_ln_g'] = upd('a_ln_g', mine(dlng_all.reshape(N_DEV, n_a, D), 2, rows), (n_a, rows))
    res['a_ln_b'] = upd('a_ln_b', mine(dlnb_all.reshape(N_DEV, n_a, D), 2, rows), (n_a, rows))
    res['a_w_out'] = upd('a_w_out', p_w_out, (n_a * rows, D))
    res['a_b_out'] = upd('a_b_out', mine(dbout_all.reshape(N_DEV, n_a, D), 2, rows), (n_a, rows))
    qs = 3 * D // N_DEV
    res['b_w_qkv'] = upd('b_w_qkv', p_w_qkv_t, (n_b * qs, D), to_out=lambda t: swap(t.reshape(n_b, qs, D)), from_in=swap)
    res['b_q_g'] = upd('b_q_g', dqg_all, (n_b, HEAD_DIM))
    res['b_k_g'] = upd('b_k_g', dkg_all, (n_b, HEAD_DIM))
    res['b_w_o'] = upd('b_w_o', p_w_o, (n_b * rows, D))
    G = p_w.shape[1]
    Dg = D // G
    res['p_w'] = upd('p_w', p_p_w, (n_c * G * Dg // N_DEV, Dg))
    res['p_b'] = upd('p_b', mine(dpb_all.reshape(N_DEV, n_c, G, Dg), 3, Dg // N_DEV), (n_c * G, Dg // N_DEV))
    res['p_scale'] = upd('p_scale', mine(dps_all.reshape(N_DEV, n_c, D), 2, rows), (n_c, rows))

    outs = [loss, grad_x]
    for k in range(4):
        outs += [res[n][k] for n in WEIGHTS]
    return tuple(outs)
```

```python
import math

import jax
import jax.numpy as jnp
from jax import lax
from jax.experimental import pallas as pl
from jax.experimental.pallas import tpu as pltpu

F32 = jnp.float32
BF = jnp.bfloat16
EPS = 1e-6
N_DEV = 8
N_HEADS = 16
HEAD_DIM = 64
LANES = 128
CONV_W = 31
CONV_PAD = 32
CONV_ROWS = 128
POOL_WINDOWS = (2, 4, 8, 16)
POOL_PAD = 16
VMEM_LIMIT = 56 * 1024 * 1024
ADAM_LR, ADAM_B1, ADAM_B2, ADAM_EPS, ADAM_WD, ADAM_STEP = 0.001, 0.9, 0.999, 1e-08, 0.01, 10
MESH = pl.DeviceIdType.MESH

WEIGHTS = ['cond_w', 'cond_b', 'ada_w', 'ada_b', 'norm_g', 'ffn_w1', 'ffn_w3', 'ffn_w2', 'a_w_in', 'a_b_in',
           'a_dw', 'a_dw_b', 'a_ln_g', 'a_ln_b', 'a_w_out', 'a_b_out', 'b_w_qkv', 'b_q_g', 'b_k_g', 'b_w_o',
           'p_w', 'p_b', 'p_scale']
ARGS = ['x', 'c'] + WEIGHTS + ['loss_target'] + ['m_' + n for n in WEIGHTS] + ['v_' + n for n in WEIGHTS]


def _pcall(body, **kw):
    return pl.pallas_call(body, **kw)


def _cp(*sem):
    return pltpu.CompilerParams(dimension_semantics=sem if sem else None, vmem_limit_bytes=VMEM_LIMIT)


def _nn(a, b):
    return lax.dot_general(a, b, (((1,), (0,)), ((), ())), preferred_element_type=F32)


def _nt(a, b):
    return lax.dot_general(a, b, (((1,), (1,)), ((), ())), preferred_element_type=F32)


def _tn(a, b):
    return lax.dot_general(a, b, (((0,), (0,)), ((), ())), preferred_element_type=F32)


def _sig(z):
    return 1.0 / (1.0 + jnp.exp(-z))


def _tile(n, pref, unit):
    t = (min(n, pref) // unit) * unit
    while t >= unit:
        if n % t == 0:
            return t
        t -= unit
    return n


def _colsum(v):
    return jnp.sum(v, axis=0, keepdims=True)


def _acc(ref, first, val):
    @pl.when(first)
    def _():
        ref[...] = val

    @pl.when(jnp.logical_not(first))
    def _():
        ref[...] += val


def _sds(shape, dt):
    return jax.ShapeDtypeStruct(shape, dt)


def mod_fwd(x, g, scale, shift):
    S, D = x.shape
    tm = _tile(S, 512, 8)

    def body(x_ref, g_ref, sc_ref, sh_ref, h_ref):
        xv = x_ref[...]
        r = lax.rsqrt(jnp.mean(xv * xv, axis=-1, keepdims=True) + EPS)
        h_ref[...] = (((xv * r) * g_ref[...]) * (1.0 + sc_ref[...]) + sh_ref[...]).astype(h_ref.dtype)

    row = pl.BlockSpec((tm, D), lambda i: (i, 0))
    vec = pl.BlockSpec((1, D), lambda i: (0, 0))
    return _pcall(body, name="mod_fwd", grid=(S // tm,), in_specs=[row, vec, vec, vec], out_specs=row,
                  out_shape=_sds((S, D), BF), compiler_params=_cp("parallel"))(x, g, scale, shift)


def resid_fwd(x, y, gate, coef):
    S, D = x.shape
    tm = _tile(S, 512, 8)

    def body(x_ref, y_ref, g_ref, o_ref):
        o_ref[...] = x_ref[...] + (coef * g_ref[...]) * y_ref[...]

    row = pl.BlockSpec((tm, D), lambda i: (i, 0))
    vec = pl.BlockSpec((1, D), lambda i: (0, 0))
    return _pcall(body, name="resid_fwd", grid=(S // tm,), in_specs=[row, row, vec], out_specs=row,
                  out_shape=_sds((S, D), F32), compiler_params=_cp("parallel"))(x, y, gate)


def resid_mod_fwd(x, y, gate, coef, g, scale, shift):
    S, D = x.shape
    tm = _tile(S, 512, 8)

    def body(x_ref, y_ref, gate_ref, g_ref, sc_ref, sh_ref, o_ref, h_ref):
        xv = x_ref[...] + (coef * gate_ref[...]) * y_ref[...]
        o_ref[...] = xv
        r = lax.rsqrt(jnp.mean(xv * xv, axis=-1, keepdims=True) + EPS)
        h_ref[...] = (((xv * r) * g_ref[...]) * (1.0 + sc_ref[...]) + sh_ref[...]).astype(h_ref.dtype)

    row = pl.BlockSpec((tm, D), lambda i: (i, 0))
    vec = pl.BlockSpec((1, D), lambda i: (0, 0))
    return _pcall(body, name="resid_mod_fwd", grid=(S // tm,), in_specs=[row, row, vec, vec, vec, vec], out_specs=[row, row],
                  out_shape=[_sds((S, D), F32), _sds((S, D), BF)], compiler_params=_cp("parallel"))(x, y, gate, g, scale, shift)


def gate_bwd(dxo, y, gate, coef):
    S, D = dxo.shape
    tm = _tile(S, 512, 8)

    def body(d_ref, y_ref, g_ref, dy_ref, dg_ref, ds_ref):
        first = pl.program_id(0) == 0
        d = d_ref[...]
        dy = (coef * g_ref[...]) * d
        dy_ref[...] = dy.astype(dy_ref.dtype)
        _acc(dg_ref, first, _colsum(coef * d * y_ref[...]))
        _acc(ds_ref, first, _colsum(dy))

    row = pl.BlockSpec((tm, D), lambda i: (i, 0))
    vec = pl.BlockSpec((1, D), lambda i: (0, 0))
    return _pcall(body, name="gate_bwd", grid=(S // tm,), in_specs=[row, row, vec], out_specs=[row, vec, vec],
                  out_shape=[_sds((S, D), BF), _sds((1, D), F32), _sds((1, D), F32)],
                  compiler_params=_cp("arbitrary"))(dxo, y, gate)


def mod_bwd(x, dh, dxo, g, scale):
    S, D = x.shape
    tm = _tile(S, 512, 8)

    def body(x_ref, dh_ref, do_ref, g_ref, sc_ref, dx_ref, dsh_ref, dsc_ref, dg_ref):
        first = pl.program_id(0) == 0
        xv = x_ref[...]
        dh = dh_ref[...].astype(F32)
        gv = g_ref[...]
        r = lax.rsqrt(jnp.mean(xv * xv, axis=-1, keepdims=True) + EPS)
        xh = xv * r
        dn = dh * (1.0 + sc_ref[...])
        u = dn * gv
        dx_ref[...] = do_ref[...] + r * (u - xh * jnp.mean(u * xh, axis=-1, keepdims=True))
        _acc(dsh_ref, first, _colsum(dh))
        _acc(dsc_ref, first, _colsum(dh * (xh * gv)))
        _acc(dg_ref, first, _colsum(dn * xh))

    row = pl.BlockSpec((tm, D), lambda i: (i, 0))
    vec = pl.BlockSpec((1, D), lambda i: (0, 0))
    return _pcall(body, name="mod_bwd", grid=(S // tm,), in_specs=[row, row, row, vec, vec],
                  out_specs=[row, vec, vec, vec],
                  out_shape=[_sds((S, D), F32), _sds((1, D), F32), _sds((1, D), F32), _sds((1, D), F32)],
                  compiler_params=_cp("arbitrary"))(x, dh, dxo, g, scale)


def mod_gate_bwd(x, dh, dxo, g, scale, y_prev, gate_prev, coef_prev):
    S, D = x.shape
    tm = _tile(S, 512, 8)

    def body(x_ref, dh_ref, do_ref, g_ref, sc_ref, y_ref, gp_ref, dx_ref, dsh_ref, dsc_ref, dg_ref, dy_ref, dgp_ref, dys_ref):
        first = pl.program_id(0) == 0
        xv = x_ref[...]
        dh = dh_ref[...].astype(F32)
        gv = g_ref[...]
        r = lax.rsqrt(jnp.mean(xv * xv, axis=-1, keepdims=True) + EPS)
        xh = xv * r
        dn = dh * (1.0 + sc_ref[...])
        u = dn * gv
        dx = do_ref[...] + r * (u - xh * jnp.mean(u * xh, axis=-1, keepdims=True))
        dx_ref[...] = dx
        _acc(dsh_ref, first, _colsum(dh))
        _acc(dsc_ref, first, _colsum(dh * (xh * gv)))
        _acc(dg_ref, first, _colsum(dn * xh))
        dy = (coef_prev * gp_ref[...]) * dx
        dy_ref[...] = dy.astype(dy_ref.dtype)
        _acc(dgp_ref, first, _colsum(coef_prev * dx * y_ref[...]))
        _acc(dys_ref, first, _colsum(dy))

    row = pl.BlockSpec((tm, D), lambda i: (i, 0))
    vec = pl.BlockSpec((1, D), lambda i: (0, 0))
    return _pcall(body, name="mod_gate_bwd", grid=(S // tm,), in_specs=[row, row, row, vec, vec, row, vec],
                  out_specs=[row, vec, vec, vec, row, vec, vec],
                  out_shape=[_sds((S, D), F32)] + [_sds((1, D), F32)] * 3 + [_sds((S, D), BF)] + [_sds((1, D), F32)] * 2,
                  compiler_params=_cp("arbitrary"))(x, dh, dxo, g, scale, y_prev, gate_prev)


def loss_fwd_bwd(y, target):
    S, D = y.shape
    tm = _tile(S, 512, 8)

    def body(y_ref, t_ref, l_ref, d_ref):
        first = pl.program_id(0) == 0
        e = y_ref[...] - t_ref[...]
        d_ref[...] = e * (1.0 / D)
        part = 0.5 * jnp.sum(jnp.mean(e * e, axis=-1, keepdims=True), axis=0, keepdims=True)
        _acc(l_ref, first, part)

    row = pl.BlockSpec((tm, D), lambda i: (i, 0))
    return _pcall(body, name="loss", grid=(S // tm,), in_specs=[row, row],
                  out_specs=[pl.BlockSpec((1, 1), lambda i: (0, 0)), row],
                  out_shape=[_sds((1, 1), F32), _sds((S, D), F32)], compiler_params=_cp("arbitrary"))(y, target)


def matmul_nt(x, wt, bias=None, out_dtype=F32):
    S, K = x.shape
    N = wt.shape[0]
    tm, tn = _tile(S, 512, 8), _tile(N, 512, LANES)

    def body(*refs):
        x_ref, w_ref = refs[0], refs[1]
        o_ref = refs[-1]
        r = _nt(x_ref[...].astype(BF), w_ref[...].astype(BF))
        if bias is not None:
            r = r + refs[2][...]
        o_ref[...] = r.astype(o_ref.dtype)

    in_specs = [pl.BlockSpec((tm, K), lambda i, j: (i, 0)), pl.BlockSpec((tn, K), lambda i, j: (j, 0))]
    ops = [x, wt]
    if bias is not None:
        in_specs.append(pl.BlockSpec((1, tn), lambda i, j: (0, j)))
        ops.append(bias)
    return _pcall(body, name="matmul_nt", grid=(S // tm, N // tn), in_specs=in_specs,
                  out_specs=pl.BlockSpec((tm, tn), lambda i, j: (i, j)), out_shape=_sds((S, N), out_dtype),
                  compiler_params=_cp("parallel", "parallel"))(*ops)


def matmul_nn(x, w, bias=None, out_dtype=F32):
    S, K = x.shape
    N = w.shape[1]
    tm, tn = _tile(S, 512, 8), _tile(N, 512, LANES)

    def body(*refs):
        x_ref, w_ref = refs[0], refs[1]
        o_ref = refs[-1]
        r = _nn(x_ref[...].astype(BF), w_ref[...].astype(BF))
        if bias is not None:
            r = r + refs[2][...]
        o_ref[...] = r.astype(o_ref.dtype)

    in_specs = [pl.BlockSpec((tm, K), lambda i, j: (i, 0)), pl.BlockSpec((K, tn), lambda i, j: (0, j))]
    ops = [x, w]
    if bias is not None:
        in_specs.append(pl.BlockSpec((1, tn), lambda i, j: (0, j)))
        ops.append(bias)
    return _pcall(body, name="matmul_nn", grid=(S // tm, N // tn), in_specs=in_specs,
                  out_specs=pl.BlockSpec((tm, tn), lambda i, j: (i, j)), out_shape=_sds((S, N), out_dtype),
                  compiler_params=_cp("parallel", "parallel"))(*ops)


def matmul_tn(a, b, out_dtype=BF):
    S, M = a.shape
    N = b.shape[1]
    bm, bn, tk = _tile(M, 1408, LANES), _tile(N, 1024, LANES), _tile(S, 2048, 8)
    nk = S // tk

    def body(a_ref, b_ref, o_ref, acc_ref):
        k = pl.program_id(2)
        _acc(acc_ref, k == 0, _tn(a_ref[...].astype(BF), b_ref[...].astype(BF)))

        @pl.when(k == nk - 1)
        def _():
            o_ref[...] = acc_ref[...].astype(o_ref.dtype)

    return _pcall(body, name="matmul_tn", grid=(M // bm, N // bn, nk),
                  in_specs=[pl.BlockSpec((tk, bm), lambda i, j, k: (k, i)), pl.BlockSpec((tk, bn), lambda i, j, k: (k, j))],
                  out_specs=pl.BlockSpec((bm, bn), lambda i, j, k: (i, j)), out_shape=_sds((M, N), out_dtype),
                  scratch_shapes=[pltpu.VMEM((bm, bn), F32)],
                  compiler_params=_cp("parallel", "parallel", "arbitrary"))(a, b)


def outer_rows(at, b):
    M, R = at.shape
    N = b.shape[1]
    tm = _tile(M, 256, 8)

    def body(a_ref, b_ref, o_ref):
        av, bv = a_ref[...], b_ref[...]
        acc = av[:, 0:1] * bv[0:1, :]
        for r in range(1, R):
            acc = acc + av[:, r:r + 1] * bv[r:r + 1, :]
        o_ref[...] = acc

    return _pcall(body, name="outer_rows", grid=(M // tm,),
                  in_specs=[pl.BlockSpec((tm, R), lambda i: (i, 0)), pl.BlockSpec((R, N), lambda i: (0, 0))],
                  out_specs=pl.BlockSpec((tm, N), lambda i: (i, 0)), out_shape=_sds((M, N), F32),
                  compiler_params=_cp("parallel"))(at, b)


def ffn_fwd(h, w1t, w3t, w2):
    S, D = h.shape
    Fd = w2.shape[0]
    tm, tf = _tile(S, 1024, 8), _tile(Fd, 256, LANES)

    def body(h_ref, w1_ref, w3_ref, g1_ref, g3_ref, a_ref):
        hv = h_ref[...]
        g1 = _nt(hv, w1_ref[...])
        g3 = _nt(hv, w3_ref[...])
        s = _sig(g1)
        si = g1 * s
        g1_ref[...] = (g3 * (s * (1.0 + g1 * (1.0 - s)))).astype(g1_ref.dtype)
        g3_ref[...] = si.astype(g3_ref.dtype)
        a_ref[...] = (si * g3).astype(a_ref.dtype)

    row = pl.BlockSpec((tm, D), lambda i, f: (i, 0))
    wsp = pl.BlockSpec((tf, D), lambda i, f: (f, 0))
    act = pl.BlockSpec((tm, tf), lambda i, f: (i, f))
    g1s, g3s, a = _pcall(body, name="ffn_fwd", grid=(S // tm, Fd // tf), in_specs=[row, wsp, wsp], out_specs=[act, act, act],
                         out_shape=[_sds((S, Fd), BF)] * 3, compiler_params=_cp("parallel", "parallel"))(h, w1t, w3t)

    tr, tn = _tile(S, 512, 8), _tile(D, 512, LANES)

    def y_body(a_ref, w2_ref, y_ref):
        y_ref[...] = _nn(a_ref[...], w2_ref[...])

    y = _pcall(y_body, name="ffn_out", grid=(S // tr, D // tn),
               in_specs=[pl.BlockSpec((tr, Fd), lambda i, j: (i, 0)), pl.BlockSpec((Fd, tn), lambda i, j: (0, j))],
               out_specs=pl.BlockSpec((tr, tn), lambda i, j: (i, j)), out_shape=_sds((S, D), F32),
               compiler_params=_cp("parallel", "parallel"))(a, w2)
    return y, g1s, g3s, a


def ffn_bwd(dy, g1s, g3s, w1t, w3t, w2):
    S, D = dy.shape
    Fd = w2.shape[0]
    tm, tf = _tile(S, 1024, 16), _tile(Fd, 256, LANES)
    halves = [pl.ds(0, tm // 2), pl.ds(tm // 2, tm // 2)]

    def body(dy_ref, g1_ref, g3_ref, w2_ref, d1_ref, d3_ref):
        w2v = w2_ref[...]
        das = [_nt(dy_ref[rows, :], w2v) for rows in halves]
        for rows, da in zip(halves, das):
            d1_ref[rows, :] = (da * g1_ref[rows, :].astype(F32)).astype(d1_ref.dtype)
            d3_ref[rows, :] = (da * g3_ref[rows, :].astype(F32)).astype(d3_ref.dtype)

    row = pl.BlockSpec((tm, D), lambda i, f: (i, 0))
    act = pl.BlockSpec((tm, tf), lambda i, f: (i, f))
    d1, d3 = _pcall(body, name="ffn_bwd", grid=(S // tm, Fd // tf),
                    in_specs=[row, act, act, pl.BlockSpec((tf, D), lambda i, f: (f, 0))], out_specs=[act, act],
                    out_shape=[_sds((S, Fd), BF)] * 2, compiler_params=_cp("parallel", "parallel"))(dy, g1s, g3s, w2)

    tr, tn = _tile(S, 512, 8), _tile(D, 512, LANES)

    def dh_body(d1_ref, d3_ref, w1_ref, w3_ref, dh_ref):
        dh_ref[...] = _nn(d1_ref[...], w1_ref[...]) + _nn(d3_ref[...], w3_ref[...])

    wide = pl.BlockSpec((tr, Fd), lambda i, j: (i, 0))
    wcol = pl.BlockSpec((Fd, tn), lambda i, j: (0, j))
    dh = _pcall(dh_body, name="ffn_dh", grid=(S // tr, D // tn), in_specs=[wide, wide, wcol, wcol],
                out_specs=pl.BlockSpec((tr, tn), lambda i, j: (i, j)), out_shape=_sds((S, D), F32),
                compiler_params=_cp("parallel", "parallel"))(d1, d3, w1t, w3t)
    return dh, d1, d3


def _layer_norm_parts(v):
    mu = jnp.mean(v, axis=-1, keepdims=True)
    vc = v - mu
    rs = lax.rsqrt(jnp.mean(vc * vc, axis=-1, keepdims=True) + EPS)
    return vc * rs, rs


def conv_mid_fwd(pre, dw, dw_b, ln_g, ln_b):
    S, D2 = pre.shape
    D = D2 // 2
    tm = _tile(S, 256, CONV_PAD)

    def body(pre_ref, dw_ref, dwb_ref, g_ref, b_ref, v_ref, sw_ref, ubuf):
        @pl.when(pl.program_id(0) == 0)
        def _():
            ubuf[pl.ds(0, CONV_PAD), :] = jnp.zeros((CONV_PAD, D), F32)

        @pl.when(pl.program_id(0) > 0)
        def _():
            ubuf[pl.ds(0, CONV_PAD), :] = ubuf[pl.ds(tm, CONV_PAD), :]

        ubuf[pl.ds(CONV_PAD, tm), :] = pre_ref[:, pl.ds(0, D)] * _sig(pre_ref[:, pl.ds(D, D)])
        for r0 in range(0, tm, CONV_ROWS):
            for c0 in range(0, D, LANES):
                cols = pl.ds(c0, LANES)
                acc = jnp.zeros((CONV_ROWS, LANES), F32) + dwb_ref[:, cols]
                for k in range(CONV_W):
                    acc = acc + dw_ref[pl.ds(k, 1), cols] * ubuf[pl.ds(r0 + CONV_PAD - (CONV_W - 1) + k, CONV_ROWS), cols]
                v_ref[pl.ds(r0, CONV_ROWS), cols] = acc
        vh, _ = _layer_norm_parts(v_ref[...])
        ln = vh * g_ref[...] + b_ref[...]
        sw_ref[...] = (ln * _sig(ln)).astype(sw_ref.dtype)

    row = pl.BlockSpec((tm, D), lambda i: (i, 0))
    vec = pl.BlockSpec((1, D), lambda i: (0, 0))
    return _pcall(body, name="conv_mid_fwd", grid=(S // tm,),
                  in_specs=[pl.BlockSpec((tm, D2), lambda i: (i, 0)), pl.BlockSpec((CONV_PAD, D), lambda i: (0, 0)), vec, vec, vec],
                  out_specs=[row, row], out_shape=[_sds((S, D), F32), _sds((S, D), BF)],
                  scratch_shapes=[pltpu.VMEM((tm + CONV_PAD, D), F32)], compiler_params=_cp("arbitrary"))(pre, dw, dw_b, ln_g, ln_b)


def conv_mid_bwd(pre, v, dsw, dw, ln_g, ln_b):
    S, D2 = pre.shape
    D = D2 // 2
    tm = _tile(S, 256, CONV_PAD)
    nt = S // tm

    def body(pre_ref, v_ref, dsw_ref, dw_ref, g_ref, b_ref, dpre_ref, dbin_ref, ddw_ref, ddwb_ref, dg_ref, db_ref, dvbuf):
        first = pl.program_id(0) == 0

        @pl.when(first)
        def _():
            dvbuf[pl.ds(tm, CONV_PAD), :] = jnp.zeros((CONV_PAD, D), F32)
            ddw_ref[...] = jnp.zeros((CONV_PAD, D), F32)
            dbin_ref[...] = jnp.zeros((1, D2), F32)

        @pl.when(jnp.logical_not(first))
        def _():
            dvbuf[pl.ds(tm, CONV_PAD), :] = dvbuf[pl.ds(0, CONV_PAD), :]

        gv = g_ref[...]
        vh, rs = _layer_norm_parts(v_ref[...])
        ln = vh * gv + b_ref[...]
        sg = _sig(ln)
        dln = dsw_ref[...] * (sg * (1.0 + ln * (1.0 - sg)))
        _acc(dg_ref, first, _colsum(dln * vh))
        _acc(db_ref, first, _colsum(dln))
        dvh = dln * gv
        dv = rs * (dvh - jnp.mean(dvh, axis=-1, keepdims=True) - vh * jnp.mean(dvh * vh, axis=-1, keepdims=True))
        _acc(ddwb_ref, first, _colsum(dv))
        dvbuf[pl.ds(0, tm), :] = dv
        for r0 in range(0, tm, CONV_ROWS):
            rws = pl.ds(r0, CONV_ROWS)
            for c0 in range(0, D, LANES):
                cols, gate_cols = pl.ds(c0, LANES), pl.ds(D + c0, LANES)
                a = pre_ref[rws, cols]
                sb = _sig(pre_ref[rws, gate_cols])
                u = a * sb
                du = jnp.zeros((CONV_ROWS, LANES), F32)
                for k in range(CONV_W):
                    sh = dvbuf[pl.ds(r0 + CONV_W - 1 - k, CONV_ROWS), cols]
                    du = du + dw_ref[pl.ds(k, 1), cols] * sh
                    ddw_ref[pl.ds(k, 1), cols] += _colsum(u * sh)
                da = du * sb
                db = da * a * (1.0 - sb)
                dpre_ref[rws, cols] = da.astype(dpre_ref.dtype)
                dpre_ref[rws, gate_cols] = db.astype(dpre_ref.dtype)
                dbin_ref[:, cols] += _colsum(da)
                dbin_ref[:, gate_cols] += _colsum(db)

    rev = lambda i: (nt - 1 - i, 0)
    row = pl.BlockSpec((tm, D), rev)
    row2 = pl.BlockSpec((tm, D2), rev)
    vec = pl.BlockSpec((1, D), lambda i: (0, 0))
    pad = pl.BlockSpec((CONV_PAD, D), lambda i: (0, 0))
    return _pcall(body, name="conv_mid_bwd", grid=(nt,), in_specs=[row2, row, row, pad, vec, vec],
                  out_specs=[row2, pl.BlockSpec((1, D2), lambda i: (0, 0)), pad, vec, vec, vec],
                  out_shape=[_sds((S, D2), BF), _sds((1, D2), F32), _sds((CONV_PAD, D), F32), _sds((1, D), F32),
                             _sds((1, D), F32), _sds((1, D), F32)],
                  scratch_shapes=[pltpu.VMEM((tm + CONV_PAD, D), F32)], compiler_params=_cp("arbitrary"))(pre, v, dsw, dw, ln_g, ln_b)


def _pool_count(i, tm, w):
    t = i * tm + lax.broadcasted_iota(jnp.int32, (tm, 1), 0)
    return jnp.minimum(t + 1, w).astype(F32)


def pool_fwd(h, p_w, p_b, p_scale):
    S, D = h.shape
    G, Dg = p_w.shape[0], p_w.shape[1]
    tm = _tile(S, 256, POOL_PAD)

    def body(h_ref, w_ref, b_ref, s_ref, y_ref, d_ref, hbuf):
        i = pl.program_id(0)

        @pl.when(i == 0)
        def _():
            hbuf[pl.ds(0, POOL_PAD), :] = jnp.zeros((POOL_PAD, D), F32)

        @pl.when(i > 0)
        def _():
            hbuf[pl.ds(0, POOL_PAD), :] = hbuf[pl.ds(tm, POOL_PAD), :]

        hbuf[pl.ds(POOL_PAD, tm), :] = h_ref[...].astype(F32)
        for g, w in enumerate(POOL_WINDOWS):
            cols = pl.ds(g * Dg, Dg)
            hg = hbuf[pl.ds(POOL_PAD, tm), cols]
            win = hg
            for j in range(1, w):
                win = win + hbuf[pl.ds(POOL_PAD - j, tm), cols]
            d = (win / _pool_count(i, tm, w) - hg).astype(BF)
            d_ref[:, cols] = d
            y_ref[:, cols] = (_nn(d, w_ref[g].astype(BF)) + b_ref[:, cols]) * s_ref[:, cols]

    row = pl.BlockSpec((tm, D), lambda i: (i, 0))
    vec = pl.BlockSpec((1, D), lambda i: (0, 0))
    return _pcall(body, name="pool_fwd", grid=(S // tm,),
                  in_specs=[row, pl.BlockSpec((G, Dg, Dg), lambda i: (0, 0, 0)), vec, vec], out_specs=[row, row],
                  out_shape=[_sds((S, D), F32), _sds((S, D), BF)],
                  scratch_shapes=[pltpu.VMEM((tm + POOL_PAD, D), F32)], compiler_params=_cp("arbitrary"))(h, p_w, p_b, p_scale)


def pool_bwd(dy, d, p_w, p_b, p_scale):
    S, D = dy.shape
    G, Dg = p_w.shape[0], p_w.shape[1]
    tm = _tile(S, 256, POOL_PAD)
    nt = S // tm

    def body(dy_ref, d_ref, w_ref, b_ref, s_ref, dh_ref, dyy_ref, dpb_ref, dps_ref, ebuf):
        j = pl.program_id(0)
        i = nt - 1 - j
        first = j == 0

        @pl.when(first)
        def _():
            ebuf[pl.ds(tm, POOL_PAD), :] = jnp.zeros((POOL_PAD, D), F32)

        @pl.when(jnp.logical_not(first))
        def _():
            ebuf[pl.ds(tm, POOL_PAD), :] = ebuf[pl.ds(0, POOL_PAD), :]

        dpb, dps = [], []
        for g, w in enumerate(POOL_WINDOWS):
            cols = pl.ds(g * Dg, Dg)
            wg = w_ref[g].astype(BF)
            dyg = dy_ref[:, cols].astype(F32)
            ypre = _nn(d_ref[:, cols], wg) + b_ref[:, cols]
            dps.append(_colsum(dyg * ypre))
            dyy = dyg * s_ref[:, cols]
            dpb.append(_colsum(dyy))
            dyy = dyy.astype(BF)
            dyy_ref[:, cols] = dyy
            dd = _nt(dyy, wg)
            ebuf[pl.ds(0, tm), cols] = dd / _pool_count(i, tm, w)
            acc = -dd
            for k in range(w):
                acc = acc + ebuf[pl.ds(k, tm), cols]
            dh_ref[:, cols] = acc
        _acc(dpb_ref, first, jnp.concatenate(dpb, axis=1))
        _acc(dps_ref, first, jnp.concatenate(dps, axis=1))

    row = pl.BlockSpec((tm, D), lambda i: (nt - 1 - i, 0))
    vec = pl.BlockSpec((1, D), lambda i: (0, 0))
    return _pcall(body, name="pool_bwd", grid=(nt,),
                  in_specs=[row, row, pl.BlockSpec((G, Dg, Dg), lambda i: (0, 0, 0)), vec, vec],
                  out_specs=[row, row, vec, vec],
                  out_shape=[_sds((S, D), F32), _sds((S, D), BF), _sds((1, D), F32), _sds((1, D), F32)],
                  scratch_shapes=[pltpu.VMEM((tm + POOL_PAD, D), F32)], compiler_params=_cp("arbitrary"))(dy, d, p_w, p_b, p_scale)


def _group_sum(v, low):
    s_lo = jnp.sum(jnp.where(low, v, 0.0), axis=-1, keepdims=True)
    s_hi = jnp.sum(jnp.where(low, 0.0, v), axis=-1, keepdims=True)
    return jnp.where(low, s_lo, s_hi)


def qknorm_fwd(qkv, q_g2, k_g2):
    S, D3 = qkv.shape
    nb = D3 // LANES // 3
    tm = _tile(S, 512, 8)

    def body(x_ref, qg_ref, kg_ref, o_ref):
        low = lax.broadcasted_iota(jnp.int32, (1, LANES), 1) < HEAD_DIM
        for j in range(3 * nb):
            cols = pl.ds(j * LANES, LANES)
            xv = x_ref[:, cols]
            if j < 2 * nb:
                r = lax.rsqrt(_group_sum(xv * xv, low) * (1.0 / HEAD_DIM) + EPS)
                xv = ((xv * r) * qg_ref[...]) * Q_SCALE if j < nb else (xv * r) * kg_ref[...]
            o_ref[:, cols] = xv.astype(o_ref.dtype)

    blk = pl.BlockSpec((tm, D3), lambda i: (i, 0))
    vec = pl.BlockSpec((1, LANES), lambda i: (0, 0))
    return _pcall(body, name="qknorm_fwd", grid=(S // tm,), in_specs=[blk, vec, vec], out_specs=blk,
                  out_shape=_sds((S, D3), BF), compiler_params=_cp("parallel"))(qkv, q_g2, k_g2)


def qknorm_bwd(qkv, dq, dk, dv, q_g2, k_g2):
    S, D3 = qkv.shape
    D = D3 // 3
    nb = D // LANES
    tm = _tile(S, 256, 8)

    def body(x_ref, dq_ref, dk_ref, dv_ref, qg_ref, kg_ref, o_ref, dg_ref):
        low = lax.broadcasted_iota(jnp.int32, (1, LANES), 1) < HEAD_DIM
        dgs = []
        for j in range(3 * nb):
            cols, part = pl.ds(j * LANES, LANES), pl.ds((j % nb) * LANES, LANES)
            if j >= 2 * nb:
                o_ref[:, cols] = dv_ref[:, part].astype(o_ref.dtype)
                dgs.append(jnp.zeros((1, LANES), F32))
                continue
            xv = x_ref[:, cols]
            r = lax.rsqrt(_group_sum(xv * xv, low) * (1.0 / HEAD_DIM) + EPS)
            xh = xv * r
            dn = dq_ref[:, part] * (Q_SCALE * LN2) if j < nb else dk_ref[:, part] * LN2
            u = dn * (qg_ref[...] if j < nb else kg_ref[...])
            o_ref[:, cols] = (r * (u - xh * (_group_sum(u * xh, low) * (1.0 / HEAD_DIM)))).astype(o_ref.dtype)
            dgs.append(_colsum(dn * xh))
        _acc(dg_ref, pl.program_id(0) == 0, jnp.concatenate(dgs, axis=1))

    wide = pl.BlockSpec((tm, D3), lambda i: (i, 0))
    row = pl.BlockSpec((tm, D), lambda i: (i, 0))
    vec = pl.BlockSpec((1, LANES), lambda i: (0, 0))
    return _pcall(body, name="qknorm_bwd", grid=(S // tm,), in_specs=[wide, row, row, row, vec, vec],
                  out_specs=[wide, pl.BlockSpec((1, D3), lambda i: (0, 0))],
                  out_shape=[_sds((S, D3), BF), _sds((1, D3), F32)], compiler_params=_cp("arbitrary"))(qkv, dq, dk, dv, q_g2, k_g2)


LOG2E = math.log2(math.e)
LN2 = math.log(2.0)
Q_SCALE = HEAD_DIM ** -0.5 * LOG2E
FAR = 160.0
UNSEEN = 1e30


def _softplus2(z2):
    return jnp.maximum(z2, 0.0) + jnp.log(1.0 + jnp.exp2(-jnp.abs(z2))) * LOG2E


def _attn_tile(S):
    return _tile(S, 256, LANES) if S >= 1024 else LANES


def attn_fwd(qkvn):
    S, D3 = qkvn.shape
    D = D3 // 3
    nb = D // LANES
    t = _attn_tile(S)
    nq = S // t
    assert nq <= LANES

    def body(q_ref, k_ref, v_ref, o_ref, r_ref, acc_ref, rall_ref, behind_ref, z_ref):
        qi = pl.program_id(1)
        lane = lax.broadcasted_iota(jnp.int32, (1, LANES), 1)
        rows = lax.broadcasted_iota(jnp.int32, (t, t), 0)
        cols = lax.broadcasted_iota(jnp.int32, (t, t), 1)
        later = (rows > cols).astype(BF)
        q = q_ref[...]
        qh = [jnp.where(lane < HEAD_DIM, q, jnp.zeros_like(q)), jnp.where(lane >= HEAD_DIM, q, jnp.zeros_like(q))]
        acc_ref[...] = jnp.zeros((2, t, LANES), F32)
        rall_ref[...] = jnp.full((2, t, LANES), UNSEEN, F32)
        behind_ref[...] = jnp.zeros((2, t, LANES), F32)

        def scores(b):
            kb = k_ref[pl.ds(pl.multiple_of(b * t, t), t), :]
            return [_nt(qh[0], kb), _nt(qh[1], kb)]

        def block(b, diagonal):
            vb = v_ref[pl.ds(pl.multiple_of(b * t, t), t), :]
            lsig, sp, within = [None, None], [None, None], [None, None]
            for hh in range(2):
                z = z_ref[hh]
                sp_full = _softplus2(z)
                lsig[hh] = z - sp_full
                sp[hh] = jnp.where(cols < rows, sp_full, 0.0) if diagonal else sp_full
                within[hh] = _nn(sp[hh].astype(BF), later)
            z_next = scores(jnp.maximum(b - 1, 0))
            for hh in range(2):
                behind = behind_ref[hh]
                a = jnp.exp2(lsig[hh] - (within[hh] + jnp.tile(behind, (1, t // LANES))))
                if diagonal:
                    a = jnp.where(cols < rows, a, 0.0)
                acc_ref[hh] += _nn(a.astype(BF), vb)
                rall_ref[hh] = jnp.where(lane == b, behind, rall_ref[hh])
                behind_ref[hh] = behind + jnp.sum(sp[hh], axis=-1, keepdims=True)
            for hh in range(2):
                z_ref[hh] = z_next[hh]

        z_first = scores(qi)
        for hh in range(2):
            z_ref[hh] = z_first[hh]
        block(qi, True)

        def reaches():
            return (jnp.min(behind_ref[...]) < FAR).astype(jnp.int32)

        def step(carry):
            n, _ = carry
            block(qi - 1 - n, False)
            return n + 1, reaches()

        lax.while_loop(lambda carry: jnp.logical_and(carry[0] < qi, carry[1] > 0), step, (jnp.int32(0), reaches()))
        r_ref[...] = rall_ref[...]
        o_ref[...] = jnp.where(lane < HEAD_DIM, acc_ref[0], acc_ref[1]).astype(o_ref.dtype)

    return _pcall(body, name="attn_fwd", grid=(nb, nq),
                  in_specs=[pl.BlockSpec((t, LANES), lambda h, i: (i, h)),
                            pl.BlockSpec((S, LANES), lambda h, i: (0, nb + h)),
                            pl.BlockSpec((S, LANES), lambda h, i: (0, 2 * nb + h))],
                  out_specs=[pl.BlockSpec((t, LANES), lambda h, i: (i, h)), pl.BlockSpec((2, t, LANES), lambda h, i: (h, i, 0))],
                  out_shape=[_sds((S, D), BF), _sds((2 * nb, S, LANES), F32)],
                  scratch_shapes=[pltpu.VMEM((2, t, LANES), F32)] * 3 + [pltpu.VMEM((2, t, t), F32)],
                  compiler_params=_cp("parallel", "arbitrary"))(qkvn, qkvn, qkvn)


def attn_bwd(qkvn, do, r_all):
    S, D3 = qkvn.shape
    D = D3 // 3
    nb = D // LANES
    t = _attn_tile(S)
    nq = S // t

    def body(q_ref, k_ref, v_ref, do_ref, r_ref, dq_ref, dk_hbm, dv_hbm, dk_acc, dv_acc, dq_acc, before_ref, z_ref, da_ref, sem):
        hp = pl.program_id(0)
        qi = pl.program_id(1)

        @pl.when(qi == 0)
        def _():
            dk_acc[...] = jnp.zeros((S, LANES), F32)
            dv_acc[...] = jnp.zeros((S, LANES), F32)

        lane = lax.broadcasted_iota(jnp.int32, (1, LANES), 1)
        rows = lax.broadcasted_iota(jnp.int32, (t, t), 0)
        cols = lax.broadcasted_iota(jnp.int32, (t, t), 1)
        later = (rows > cols).astype(BF)
        earlier = (rows < cols).astype(BF)
        q = q_ref[...]
        dov = do_ref[...]
        low, high = lane < HEAD_DIM, lane >= HEAD_DIM
        qh = [jnp.where(low, q, jnp.zeros_like(q)), jnp.where(high, q, jnp.zeros_like(q))]
        doh = [jnp.where(low, dov, jnp.zeros_like(dov)), jnp.where(high, dov, jnp.zeros_like(dov))]
        q_both = jnp.concatenate(qh, axis=0)
        do_both = jnp.concatenate(doh, axis=0)
        dq_acc[...] = jnp.zeros((2, t, LANES), F32)
        before_ref[...] = jnp.zeros((2, t, LANES), F32)

        def scores(b):
            off = pl.multiple_of(b * t, t)
            kb = k_ref[pl.ds(off, t), :]
            vb = v_ref[pl.ds(off, t), :]
            return [_nt(qh[0], kb), _nt(qh[1], kb)], [_nt(doh[0], vb), _nt(doh[1], vb)]

        def block(b, diagonal):
            off = pl.multiple_of(b * t, t)
            kb = k_ref[pl.ds(off, t), :]
            lsig, sigma, within, g, earlier_g, dz, prob = ([None, None] for _ in range(7))
            for hh in range(2):
                z = z_ref[hh]
                sp = _softplus2(z)
                lsig[hh] = z - sp
                if diagonal:
                    sp = jnp.where(cols < rows, sp, 0.0)
                within[hh] = _nn(sp.astype(BF), later)
            z_next, da_next = scores(jnp.minimum(b + 1, qi))
            for hh in range(2):
                behind = jnp.sum(jnp.where(lane == b, r_ref[hh], 0.0), axis=-1, keepdims=True)
                a = jnp.exp2(lsig[hh] - (within[hh] + behind))
                if diagonal:
                    a = jnp.where(cols < rows, a, 0.0)
                prob[hh] = a.astype(BF)
                g[hh] = a * da_ref[hh]
                earlier_g[hh] = _nn(g[hh].astype(BF), earlier)
            for hh in range(2):
                before = before_ref[hh]
                d = g[hh] - jnp.exp2(lsig[hh]) * (g[hh] + (earlier_g[hh] + jnp.tile(before, (1, t // LANES))))
                if diagonal:
                    d = jnp.where(cols < rows, d, 0.0)
                dz[hh] = d.astype(BF)
                dq_acc[hh] += _nn(dz[hh], kb)
                before_ref[hh] = before + jnp.sum(g[hh], axis=-1, keepdims=True)
            dk_acc[pl.ds(off, t), :] += _tn(jnp.concatenate(dz, axis=0), q_both)
            dv_acc[pl.ds(off, t), :] += _tn(jnp.concatenate(prob, axis=0), do_both)
            for hh in range(2):
                z_ref[hh] = z_next[hh]
                da_ref[hh] = da_next[hh]

        nearest = jnp.min(jnp.minimum(r_ref[0], r_ref[1]), axis=0, keepdims=True)
        skip = jnp.sum(jnp.where(jnp.logical_and(lane < qi, nearest >= FAR), 1, 0))
        z_first, da_first = scores(skip)
        for hh in range(2):
            z_ref[hh] = z_first[hh]
            da_ref[hh] = da_first[hh]

        def step(b, carry):
            block(b, False)
            return carry

        lax.fori_loop(skip, qi, step, 0)
        block(qi, True)
        dq_ref[...] = jnp.where(low, dq_acc[0], dq_acc[1])

        @pl.when(qi == nq - 1)
        def _():
            col = pl.multiple_of(hp * LANES, LANES)
            c1 = pltpu.make_async_copy(dk_acc, dk_hbm.at[:, pl.ds(col, LANES)], sem.at[0])
            c2 = pltpu.make_async_copy(dv_acc, dv_hbm.at[:, pl.ds(col, LANES)], sem.at[1])
            c1.start()
            c2.start()
            c1.wait()
            c2.wait()

    blk = pl.BlockSpec((t, LANES), lambda h, i: (i, h))
    hbm = pl.BlockSpec(memory_space=pl.ANY)
    return _pcall(body, name="attn_bwd", grid=(nb, nq),
                  in_specs=[blk, pl.BlockSpec((S, LANES), lambda h, i: (0, nb + h)),
                            pl.BlockSpec((S, LANES), lambda h, i: (0, 2 * nb + h)), blk,
                            pl.BlockSpec((2, t, LANES), lambda h, i: (h, i, 0))],
                  out_specs=[blk, hbm, hbm],
                  out_shape=[_sds((S, D), F32), _sds((S, D), F32), _sds((S, D), F32)],
                  scratch_shapes=[pltpu.VMEM((S, LANES), F32), pltpu.VMEM((S, LANES), F32), pltpu.VMEM((2, t, LANES), F32),
                                  pltpu.VMEM((2, t, LANES), F32), pltpu.VMEM((2, t, t), F32), pltpu.VMEM((2, t, t), F32),
                                  pltpu.SemaphoreType.DMA((2,))],
                  compiler_params=_cp("arbitrary", "arbitrary"))(qkvn, qkvn, qkvn, do, r_all)


def cond_embed(parts, cond_b):
    P, B, D = parts.shape

    def body(p_ref, b_ref, pre_ref, e_ref):
        pre = p_ref[0]
        for s in range(1, P):
            pre = pre + p_ref[s]
        pre = pre + b_ref[...]
        pre_ref[...] = pre
        e_ref[...] = pre * _sig(pre)

    return _pcall(body, name="cond_embed", out_shape=[_sds((B, D), F32), _sds((B, D), F32)],
                  compiler_params=_cp())(parts, cond_b)


def cond_embed_bwd(parts, pre):
    P, B, D = parts.shape

    def body(p_ref, pre_ref, dpre_ref, db_ref):
        de = p_ref[0]
        for s in range(1, P):
            de = de + p_ref[s]
        pre = pre_ref[...]
        s = _sig(pre)
        dpre = de * (s * (1.0 + pre * (1.0 - s)))
        dpre_ref[...] = dpre
        db_ref[...] = _colsum(dpre)

    return _pcall(body, name="cond_embed_bwd", out_shape=[_sds((B, D), F32), _sds((1, D), F32)],
                  compiler_params=_cp())(parts, pre)


def adamw(w, parts, m, v):
    R, C = w.shape
    P = parts.shape[0]
    tr = _tile(R, max(8, (1 << 17) // C // 8 * 8), 8)
    c1 = 1.0 - ADAM_B1 ** ADAM_STEP
    c2 = 1.0 - ADAM_B2 ** ADAM_STEP

    def body(w_ref, p_ref, m_ref, v_ref, g_ref, d_ref, nm_ref, nv_ref):
        g = p_ref[0].astype(F32)
        for s in range(1, P):
            g = g + p_ref[s].astype(F32)
        nm = ADAM_B1 * m_ref[...] + (1.0 - ADAM_B1) * g
        nv = ADAM_B2 * v_ref[...] + (1.0 - ADAM_B2) * (g * g)
        g_ref[...] = g
        nm_ref[...] = nm
        nv_ref[...] = nv
        d_ref[...] = -ADAM_LR * ((nm / c1) / (jnp.sqrt(nv / c2) + ADAM_EPS) + ADAM_WD * w_ref[...])

    blk = pl.BlockSpec((tr, C), lambda i: (i, 0))
    return _pcall(body, name="adamw", grid=(R // tr,), in_specs=[blk, pl.BlockSpec((P, tr, C), lambda i: (0, i, 0)), blk, blk],
                  out_specs=[blk] * 4, out_shape=[_sds((R, C), F32)] * 4, compiler_params=_cp("parallel"))(w, parts, m, v)


def exchange(arrs, peer_axes):
    n = len(arrs)
    shapes = [a.shape if ax is None else a.shape[:ax] + a.shape[ax + 1:] for a, ax in zip(arrs, peer_axes)]

    def body(*refs):
        ins, outs = refs[:n], refs[n:2 * n]
        send, recv, local = refs[2 * n:]
        x, y, c = lax.axis_index("x"), lax.axis_index("y"), lax.axis_index("c")
        me = 4 * x + 2 * y + c

        def piece(i, d):
            if peer_axes[i] is None:
                return ins[i]
            return ins[i].at[(slice(None),) * peer_axes[i] + (d,)]

        mine = [pltpu.make_async_copy(piece(i, me), outs[i].at[me], local.at[i]) for i in range(n)]
        for cp in mine:
            cp.start()
        sends, recvs = [], []
        for p in range(1, N_DEV):
            peer = (1 - x if p & 4 else x, 1 - y if p & 2 else y, 1 - c if p & 1 else c)
            them = 4 * peer[0] + 2 * peer[1] + peer[2]
            for i in range(n):
                sends.append(pltpu.make_async_remote_copy(piece(i, them), outs[i].at[me], send.at[i, p - 1], recv.at[i, p - 1],
                                                          device_id=peer, device_id_type=MESH))
                recvs.append(pltpu.make_async_remote_copy(piece(i, me), outs[i].at[them], send.at[i, p - 1], recv.at[i, p - 1],
                                                          device_id=peer, device_id_type=MESH))
        for cp in sends:
            cp.start()
        for cp in recvs:
            cp.wait_recv()
        for cp in sends:
            cp.wait_send()
        for cp in mine:
            cp.wait()

    hbm = pl.BlockSpec(memory_space=pl.ANY)
    return _pcall(body, name="exchange", in_specs=[hbm] * n, out_specs=[hbm] * n,
                  out_shape=[_sds((N_DEV,) + s, a.dtype) for s, a in zip(shapes, arrs)],
                  scratch_shapes=[pltpu.SemaphoreType.DMA((n, N_DEV - 1)), pltpu.SemaphoreType.DMA((n, N_DEV - 1)),
                                  pltpu.SemaphoreType.DMA((n,))])(*arrs)


def gather_via_sibling(arrs):
    n = len(arrs)

    def body(*refs):
        ins, outs = refs[:n], refs[n:2 * n]
        send, recv, local = refs[2 * n:]
        x, y, c = lax.axis_index("x"), lax.axis_index("y"), lax.axis_index("c")
        me, sibling = (x, y, c), (x, y, 1 - c)
        chips = [(1 - x, y), (x, 1 - y), (1 - x, 1 - y)]
        index = lambda d: 4 * d[0] + 2 * d[1] + d[2]

        def copy(i, k, block, to, src=None):
            dst = outs[i].at[index(block)]
            return pltpu.make_async_remote_copy(dst if src is None else src, dst, send.at[i, k], recv.at[i, k],
                                                device_id=to, device_id_type=MESH)

        mine = [pltpu.make_async_copy(ins[i], outs[i].at[index(me)], local.at[i]) for i in range(n)]
        for cp in mine:
            cp.start()
        first = []
        for i in range(n):
            first.append(copy(i, 0, me, sibling, src=ins[i]))
            first += [copy(i, 1 + j, me, (*chip, c), src=ins[i]) for j, chip in enumerate(chips)]
        for cp in first:
            cp.start()
        passed = []
        for j, chip in enumerate(chips):
            for i in range(n):
                copy(i, 1 + j, (*chip, c), me).wait_recv()
                passed.append(copy(i, 4 + j, (*chip, c), sibling))
                passed[-1].start()
        for i in range(n):
            copy(i, 0, sibling, me).wait_recv()
            for j, chip in enumerate(chips):
                copy(i, 4 + j, (*chip, 1 - c), me).wait_recv()
        for cp in first + passed:
            cp.wait_send()
        for cp in mine:
            cp.wait()

    hbm = pl.BlockSpec(memory_space=pl.ANY)
    return _pcall(body, name="gather_via_sibling", in_specs=[hbm] * n, out_specs=[hbm] * n,
                  out_shape=[_sds((N_DEV,) + a.shape, a.dtype) for a in arrs],
                  scratch_shapes=[pltpu.SemaphoreType.DMA((n, N_DEV - 1)), pltpu.SemaphoreType.DMA((n, N_DEV - 1)),
                                  pltpu.SemaphoreType.DMA((n,))])(*arrs)


def _pack(vs):
    flat = jnp.concatenate([v.reshape(-1).astype(F32) for v in vs])
    pad = (-flat.shape[0]) % 1024
    return jnp.pad(flat, (0, pad)).reshape(-1, 1024)


def _unpack(packed, shapes):
    flat = packed.reshape(packed.shape[0], -1)
    out, o = [], 0
    for s in shapes:
        n = math.prod(s)
        out.append(flat[:, o:o + n].reshape((packed.shape[0],) + tuple(s)))
        o += n
    return out


def _cat_dev(g, axis):
    g = jnp.moveaxis(g, 0, axis)
    return g.reshape(g.shape[:axis] + (g.shape[axis] * g.shape[axis + 1],) + g.shape[axis + 2:])


def kernel(x, c, cond_w, cond_b, ada_w, ada_b, norm_g, ffn_w1, ffn_w3, ffn_w2, a_w_in, a_b_in, a_dw, a_dw_b, a_ln_g, a_ln_b, a_w_out, a_b_out, b_w_qkv, b_q_g, b_k_g, b_w_o, p_w, p_b, p_scale, loss_target, m_cond_w, m_cond_b, m_ada_w, m_ada_b, m_norm_g, m_ffn_w1, m_ffn_w3, m_ffn_w2, m_a_w_in, m_a_b_in, m_a_dw, m_a_dw_b, m_a_ln_g, m_a_ln_b, m_a_w_out, m_a_b_out, m_b_w_qkv, m_b_q_g, m_b_k_g, m_b_w_o, m_p_w, m_p_b, m_p_scale, v_cond_w, v_cond_b, v_ada_w, v_ada_b, v_norm_g, v_ffn_w1, v_ffn_w3, v_ffn_w2, v_a_w_in, v_a_b_in, v_a_dw, v_a_dw_b, v_a_ln_g, v_a_ln_b, v_a_w_out, v_a_b_out, v_b_w_qkv, v_b_q_g, v_b_k_g, v_b_w_o, v_p_w, v_p_b, v_p_scale):
    A = dict(zip(ARGS, (x, c, cond_w, cond_b, ada_w, ada_b, norm_g, ffn_w1, ffn_w3, ffn_w2, a_w_in, a_b_in, a_dw, a_dw_b, a_ln_g, a_ln_b, a_w_out, a_b_out, b_w_qkv, b_q_g, b_k_g, b_w_o, p_w, p_b, p_scale, loss_target, m_cond_w, m_cond_b, m_ada_w, m_ada_b, m_norm_g, m_ffn_w1, m_ffn_w3, m_ffn_w2, m_a_w_in, m_a_b_in, m_a_dw, m_a_dw_b, m_a_ln_g, m_a_ln_b, m_a_w_out, m_a_b_out, m_b_w_qkv, m_b_q_g, m_b_k_g, m_b_w_o, m_p_w, m_p_b, m_p_scale, v_cond_w, v_cond_b, v_ada_w, v_ada_b, v_norm_g, v_ffn_w1, v_ffn_w3, v_ffn_w2, v_a_w_in, v_a_b_in, v_a_dw, v_a_dw_b, v_a_ln_g, v_a_ln_b, v_a_w_out, v_a_b_out, v_b_w_qkv, v_b_q_g, v_b_k_g, v_b_w_o, v_p_w, v_p_b, v_p_scale)))
    S, D = x.shape[1], x.shape[2]
    depth = ada_w.shape[0]
    n_a, n_b, n_c = a_w_in.shape[0], b_w_qkv.shape[0], p_w.shape[0]
    me = 4 * lax.axis_index("x") + 2 * lax.axis_index("y") + lax.axis_index("c")
    swap = lambda w: jnp.swapaxes(w, -1, -2)

    small_names = ['norm_g', 'a_b_in', 'a_dw', 'a_dw_b', 'a_ln_g', 'a_ln_b', 'a_b_out', 'p_w', 'p_b', 'p_scale']
    small_in = [c] + [A[n] for n in small_names]
    big_in = [swap(ffn_w1).astype(BF), swap(ffn_w3).astype(BF), ffn_w2.astype(BF), swap(a_w_in).astype(BF),
              a_w_out.astype(BF), swap(b_w_qkv).astype(BF), b_w_o.astype(BF)]
    got = gather_via_sibling(big_in + [_pack(small_in)])
    w1t, w3t, w2 = _cat_dev(got[0], 2), _cat_dev(got[1], 2), _cat_dev(got[2], 2)
    w_in_t, w_out = _cat_dev(got[3], 1), _cat_dev(got[4], 1)
    w_qkv_t, w_o = _cat_dev(got[5], 1), _cat_dev(got[6], 1)
    sm = dict(zip(['c'] + small_names, _unpack(got[7], [v.shape for v in small_in])))
    c_all = sm['c'][:, 0]
    norm_g_f = _cat_dev(sm['norm_g'], 2)
    a_b_in_f, a_dw_f, a_dw_b_f = _cat_dev(sm['a_b_in'], 1), _cat_dev(sm['a_dw'], 2), _cat_dev(sm['a_dw_b'], 1)
    a_ln_g_f, a_ln_b_f, a_b_out_f = _cat_dev(sm['a_ln_g'], 1), _cat_dev(sm['a_ln_b'], 1), _cat_dev(sm['a_b_out'], 1)
    p_w_f, p_b_f, p_scale_f = _cat_dev(sm['p_w'], 2), _cat_dev(sm['p_b'], 2), _cat_dev(sm['p_scale'], 1)
    a_dw_f = jnp.pad(a_dw_f, ((0, 0), (0, CONV_PAD - CONV_W), (0, 0)))

    rows = D // N_DEV
    c_mine = lax.dynamic_slice_in_dim(c_all, me * rows, rows, axis=1)
    pre_parts = exchange([matmul_nn(c_mine, cond_w)], [None])[0]
    pre_all, e_all = cond_embed(pre_parts, cond_b.reshape(1, D))
    n_mod = ada_w.shape[2]
    ada_b_mine = lax.dynamic_slice_in_dim(ada_b, me * n_mod, n_mod, axis=1)
    mod_part = jnp.stack([matmul_nn(e_all, ada_w[i], ada_b_mine[i:i + 1]) for i in range(depth)], axis=1)
    mod_all = exchange([mod_part], [None])[0]
    mod = lax.dynamic_index_in_dim(mod_all, me, axis=1, keepdims=False)
    mod = jnp.moveaxis(mod, 0, 1).reshape(depth, 3, 3, 1, D)

    xs = x[0]
    saved = []
    ia = ib = ic = 0

    def mod_of(i, sub):
        return norm_g_f[i, sub].reshape(1, D), mod[i, sub, 1], mod[i, sub, 0], 1.0 + mod[i, sub, 2]

    h = mod_fwd(xs, *mod_of(0, 0)[:3])
    for i in range(depth):
        lay = {}
        for sub in range(3):
            g, scale, shift, gate = mod_of(i, sub)
            rec = dict(x=xs, h=h, g=g, scale=scale, gate=gate)
            if sub != 1:
                j = 0 if sub == 0 else 1
                y, g1s, g3s, act = ffn_fwd(h, w1t[i, j], w3t[i, j], w2[i, j])
                rec.update(kind='ffn', j=j, coef=0.5, g1=g1s, g3=g3s, act=act)
            elif i % 3 == 0:
                pre = matmul_nt(h, w_in_t[ia], a_b_in_f[ia].reshape(1, 2 * D))
                v, sw = conv_mid_fwd(pre, a_dw_f[ia], a_dw_b_f[ia].reshape(1, D), a_ln_g_f[ia].reshape(1, D), a_ln_b_f[ia].reshape(1, D))
                y = matmul_nn(sw, w_out[ia], a_b_out_f[ia].reshape(1, D))
                rec.update(kind='conv', idx=ia, coef=1.0, pre=pre, v=v, sw=sw)
                ia += 1
            elif i % 3 == 1:
                qkv = matmul_nt(h, w_qkv_t[ib])
                qg2 = jnp.tile(b_q_g[ib].reshape(1, HEAD_DIM), (1, 2))
                kg2 = jnp.tile(b_k_g[ib].reshape(1, HEAD_DIM), (1, 2))
                qkvn = qknorm_fwd(qkv, qg2, kg2)
                o, r_all = attn_fwd(qkvn)
                y = matmul_nn(o, w_o[ib])
                rec.update(kind='attn', idx=ib, coef=1.0, qkv=qkv, qkvn=qkvn, o=o, r_all=r_all, qg2=qg2, kg2=kg2)
                ib += 1
            else:
                pb, ps = p_b_f[ic].reshape(1, D), p_scale_f[ic].reshape(1, D)
                y, dpool = pool_fwd(h, p_w_f[ic], pb, ps)
                rec.update(kind='pool', idx=ic, coef=1.0, d=dpool, pb=pb, ps=ps)
                ic += 1
            rec['y'] = y
            if (i, sub) == (depth - 1, 2):
                xs = resid_fwd(xs, y, gate, rec['coef'])
            else:
                xs, h = resid_mod_fwd(xs, y, gate, rec['coef'], *mod_of(*((i, sub + 1) if sub < 2 else (i + 1, 0)))[:3])
            lay[sub] = rec
        saved.append(lay)

    loss_part, dx = loss_fwd_bwd(xs, loss_target[0])
    loss = lax.psum(loss_part[0, 0], ("x", "y", "c"))

    Fd = w2.shape[2]
    zeros = lambda *s: jnp.zeros(s, F32)
    g_w1t, g_w3t, g_w2 = [[None, None] for _ in range(depth)], [[None, None] for _ in range(depth)], [[None, None] for _ in range(depth)]
    g_w_in_t, g_w_out, g_w_qkv_t, g_w_o, g_p_w = [None] * n_a, [None] * n_a, [None] * n_b, [None] * n_b, [None] * n_c
    g_b_in, g_dw, g_dw_b, g_ln_g, g_ln_b, g_b_out = ([None] * n_a for _ in range(6))
    g_q_g, g_k_g, g_p_b, g_p_scale = [None] * n_b, [None] * n_b, [None] * n_c, [None] * n_c
    d_mod = [[None] * 3 for _ in range(depth)]
    d_norm_g = [[None] * 3 for _ in range(depth)]
    for i in reversed(range(depth)):
        for sub in reversed(range(3)):
            rec = saved[i][sub]
            if (i, sub) == (depth - 1, 2):
                dy, d_gate, dy_sum = gate_bwd(dx, rec['y'], rec['gate'], rec['coef'])
            if rec['kind'] == 'ffn':
                j = rec['j']
                dh, d1, d3 = ffn_bwd(dy, rec['g1'], rec['g3'], w1t[i, j], w3t[i, j], w2[i, j])
                g_w1t[i][j] = matmul_tn(d1, rec['h'])
                g_w3t[i][j] = matmul_tn(d3, rec['h'])
                g_w2[i][j] = matmul_tn(rec['act'], dy)
            elif rec['kind'] == 'conv':
                k = rec['idx']
                dsw = matmul_nt(dy, w_out[k])
                g_w_out[k] = matmul_tn(rec['sw'], dy)
                g_b_out[k] = dy_sum
                dpre, g_b_in[k], ddw, g_dw_b[k], g_ln_g[k], g_ln_b[k] = conv_mid_bwd(
                    rec['pre'], rec['v'], dsw, a_dw_f[k], a_ln_g_f[k].reshape(1, D), a_ln_b_f[k].reshape(1, D))
                g_dw[k] = ddw[:CONV_W]
                dh = matmul_nn(dpre, w_in_t[k])
                g_w_in_t[k] = matmul_tn(dpre, rec['h'])
            elif rec['kind'] == 'attn':
                k = rec['idx']
                do = matmul_nt(dy, w_o[k], out_dtype=BF)
                g_w_o[k] = matmul_tn(rec['o'], dy)
                dq, dk, dv = attn_bwd(rec['qkvn'], do, rec['r_all'])
                dqkv, dgains = qknorm_bwd(rec['qkv'], dq, dk, dv, rec['qg2'], rec['kg2'])
                dgains = dgains.reshape(3, N_HEADS, HEAD_DIM)
                g_q_g[k], g_k_g[k] = jnp.sum(dgains[0], axis=0), jnp.sum(dgains[1], axis=0)
                dh = matmul_nn(dqkv, w_qkv_t[k])
                g_w_qkv_t[k] = matmul_tn(dqkv, rec['h'])
            else:
                k = rec['idx']
                dh, dyy, g_p_b[k], g_p_scale[k] = pool_bwd(dy, rec['d'], p_w_f[k], rec['pb'], rec['ps'])
                full = matmul_tn(rec['d'], dyy)
                G = p_w_f.shape[1]
                Dg = D // G
                g_p_w[k] = jnp.stack([full[g * Dg:(g + 1) * Dg, g * Dg:(g + 1) * Dg] for g in range(G)])
            if (i, sub) == (0, 0):
                dx, d_shift, d_scale, d_norm_g[i][sub] = mod_bwd(rec['x'], dh, dx, rec['g'], rec['scale'])
                nxt = None
            else:
                before = saved[i][sub - 1] if sub > 0 else saved[i - 1][2]
                dx, d_shift, d_scale, d_norm_g[i][sub], *nxt = mod_gate_bwd(rec['x'], dh, dx, rec['g'], rec['scale'],
                                                                             before['y'], before['gate'], before['coef'])
            d_mod[i][sub] = jnp.concatenate([d_shift, d_scale, d_gate], axis=0)
            if nxt is not None:
                dy, d_gate, dy_sum = nxt
    grad_x = dx[None]

    d_mod_mine = jnp.stack([jnp.stack(r) for r in d_mod])
    small_g = [d_mod_mine, jnp.stack([jnp.concatenate(r, axis=0) for r in d_norm_g]),
               jnp.stack(g_b_in), jnp.stack(g_dw), jnp.stack(g_dw_b), jnp.stack(g_ln_g), jnp.stack(g_ln_b), jnp.stack(g_b_out),
               jnp.stack(g_q_g), jnp.stack(g_k_g), jnp.stack(g_p_b), jnp.stack(g_p_scale)]
    stack2 = lambda g: jnp.stack([jnp.stack(r) for r in g])
    blocks = lambda g, ax: g.reshape(g.shape[:ax] + (N_DEV, g.shape[ax] // N_DEV) + g.shape[ax + 1:])
    big_g = [blocks(stack2(g_w1t), 2), blocks(stack2(g_w3t), 2), blocks(stack2(g_w2), 2), blocks(jnp.stack(g_w_in_t), 1),
             blocks(jnp.stack(g_w_out), 1), blocks(jnp.stack(g_w_qkv_t), 1), blocks(jnp.stack(g_w_o), 1), blocks(jnp.stack(g_p_w), 2)]
    big_ax = [2, 2, 2, 1, 1, 1, 1, 2]
    got = exchange(big_g + [_pack(small_g)], big_ax + [None])
    p_w1t, p_w3t, p_w2, p_w_in_t, p_w_out, p_w_qkv_t, p_w_o, p_p_w = got[:8]
    (dmod_all, dng_all, dbin_all, ddw_all, ddwb_all, dlng_all, dlnb_all, dbout_all, dqg_all, dkg_all, dpb_all,
     dps_all) = _unpack(got[8], [v.shape for v in small_g])
    dmod_all = dmod_all.reshape(N_DEV, depth, 9 * D)

    dmod_mine = lax.dynamic_slice_in_dim(dmod_all, me * n_mod, n_mod, axis=2)
    e_t = e_all.T
    grad_ada_w = jnp.stack([outer_rows(e_t, dmod_mine[:, i]) for i in range(depth)])
    de_part = matmul_nt(dmod_mine[:, 0], ada_w[0])
    for i in range(1, depth):
        de_part = de_part + matmul_nt(dmod_mine[:, i], ada_w[i])
    de_parts = exchange([de_part], [None])[0]
    dpre_all, grad_cond_b = cond_embed_bwd(de_parts, pre_all)
    grad_cond_w = outer_rows(c_mine.T, dpre_all)

    def mine(g_all, axis, size):
        return lax.dynamic_slice_in_dim(g_all, me * size, size, axis=axis)

    def upd(name, parts, shape2, to_out=lambda t: t, from_in=lambda t: t):
        w, m, v = (from_in(A[p + name]).reshape(shape2) for p in ('', 'm_', 'v_'))
        outs = adamw(w, parts.reshape((parts.shape[0],) + shape2), m, v)
        return [to_out(o).reshape(A[name].shape) for o in outs]

    L = depth
    res = {}
    res['cond_w'] = upd('cond_w', grad_cond_w[None], (rows, D))
    res['cond_b'] = upd('cond_b', grad_cond_b[None], (1, D))
    res['ada_w'] = upd('ada_w', grad_ada_w[None], (L * D, n_mod))
    res['ada_b'] = upd('ada_b', dmod_all, (L, 9 * D))
    res['norm_g'] = upd('norm_g', mine(dng_all, 3, D // N_DEV), (L * 3, D // N_DEV))
    fs = Fd // N_DEV
    res['ffn_w1'] = upd('ffn_w1', p_w1t, (L * 2 * fs, D), to_out=lambda t: swap(t.reshape(L, 2, fs, D)), from_in=swap)
    res['ffn_w3'] = upd('ffn_w3', p_w3t, (L * 2 * fs, D), to_out=lambda t: swap(t.reshape(L, 2, fs, D)), from_in=swap)
    res['ffn_w2'] = upd('ffn_w2', p_w2, (L * 2 * fs, D))
    ws = 2 * D // N_DEV
    res['a_w_in'] = upd('a_w_in', p_w_in_t, (n_a * ws, D), to_out=lambda t: swap(t.reshape(n_a, ws, D)), from_in=swap)
    res['a_b_in'] = upd('a_b_in', mine(dbin_all.reshape(N_DEV, n_a, 2 * D), 2, ws), (n_a, ws))
    res['a_dw'] = upd('a_dw', mine(ddw_all, 3, rows), (n_a * CONV_W, rows))
    res['a_dw_b'] = upd('a_dw_b', mine(ddwb_all.reshape(N_DEV, n_a, D), 2, rows), (n_a, rows))
    res['a_ln_g'] = upd('a_ln_g', mine(dlng_all.reshape(N_DEV, n_a, D), 2, rows), (n_a, rows))
    res['a_ln_b'] = upd('a_ln_b', mine(dlnb_all.reshape(N_DEV, n_a, D), 2, rows), (n_a, rows))
    res['a_w_out'] = upd('a_w_out', p_w_out, (n_a * rows, D))
    res['a_b_out'] = upd('a_b_out', mine(dbout_all.reshape(N_DEV, n_a, D), 2, rows), (n_a, rows))
    qs = 3 * D // N_DEV
    res['b_w_qkv'] = upd('b_w_qkv', p_w_qkv_t, (n_b * qs, D), to_out=lambda t: swap(t.reshape(n_b, qs, D)), from_in=swap)
    res['b_q_g'] = upd('b_q_g', dqg_all, (n_b, HEAD_DIM))
    res['b_k_g'] = upd('b_k_g', dkg_all, (n_b, HEAD_DIM))
    res['b_w_o'] = upd('b_w_o', p_w_o, (n_b * rows, D))
    G = p_w.shape[1]
    Dg = D // G
    res['p_w'] = upd('p_w', p_p_w, (n_c * G * Dg // N_DEV, Dg))
    res['p_b'] = upd('p_b', mine(dpb_all.reshape(N_DEV, n_c, G, Dg), 3, Dg // N_DEV), (n_c * G, Dg // N_DEV))
    res['p_scale'] = upd('p_scale', mine(dps_all.reshape(N_DEV, n_c, D), 2, rows), (n_c, rows))

    outs = [loss, grad_x]
    for k in range(4):
        outs += [res[n][k] for n in WEIGHTS]
    return tuple(outs)
```

```python
import math

import jax
import jax.numpy as jnp
from jax import lax
from jax.experimental import pallas as pl
from jax.experimental.pallas import tpu as pltpu

F32 = jnp.float32
BF = jnp.bfloat16
EPS = 1e-6
N_DEV = 8
N_HEADS = 16
HEAD_DIM = 64
LANES = 128
CONV_W = 31
CONV_PAD = 32
CONV_ROWS = 128
POOL_WINDOWS = (2, 4, 8, 16)
POOL_PAD = 16
VMEM_LIMIT = 56 * 1024 * 1024
ADAM_LR, ADAM_B1, ADAM_B2, ADAM_EPS, ADAM_WD, ADAM_STEP = 0.001, 0.9, 0.999, 1e-08, 0.01, 10
MESH = pl.DeviceIdType.MESH

WEIGHTS = ['cond_w', 'cond_b', 'ada_w', 'ada_b', 'norm_g', 'ffn_w1', 'ffn_w3', 'ffn_w2', 'a_w_in', 'a_b_in',
           'a_dw', 'a_dw_b', 'a_ln_g', 'a_ln_b', 'a_w_out', 'a_b_out', 'b_w_qkv', 'b_q_g', 'b_k_g', 'b_w_o',
           'p_w', 'p_b', 'p_scale']
ARGS = ['x', 'c'] + WEIGHTS + ['loss_target'] + ['m_' + n for n in WEIGHTS] + ['v_' + n for n in WEIGHTS]


def _pcall(body, **kw):
    return pl.pallas_call(body, **kw)


def _cp(*sem):
    return pltpu.CompilerParams(dimension_semantics=sem if sem else None, vmem_limit_bytes=VMEM_LIMIT)


def _nn(a, b):
    return lax.dot_general(a, b, (((1,), (0,)), ((), ())), preferred_element_type=F32)


def _nt(a, b):
    return lax.dot_general(a, b, (((1,), (1,)), ((), ())), preferred_element_type=F32)


def _tn(a, b):
    return lax.dot_general(a, b, (((0,), (0,)), ((), ())), preferred_element_type=F32)


def _sig(z):
    return 1.0 / (1.0 + jnp.exp(-z))


def _tile(n, pref, unit):
    t = (min(n, pref) // unit) * unit
    while t >= unit:
        if n % t == 0:
            return t
        t -= unit
    return n


def _colsum(v):
    return jnp.sum(v, axis=0, keepdims=True)


def _acc(ref, first, val):
    @pl.when(first)
    def _():
        ref[...] = val

    @pl.when(jnp.logical_not(first))
    def _():
        ref[...] += val


def _sds(shape, dt):
    return jax.ShapeDtypeStruct(shape, dt)


def mod_fwd(x, g, scale, shift):
    S, D = x.shape
    tm = _tile(S, 512, 8)

    def body(x_ref, g_ref, sc_ref, sh_ref, h_ref):
        xv = x_ref[...]
        r = lax.rsqrt(jnp.mean(xv * xv, axis=-1, keepdims=True) + EPS)
        h_ref[...] = (((xv * r) * g_ref[...]) * (1.0 + sc_ref[...]) + sh_ref[...]).astype(h_ref.dtype)

    row = pl.BlockSpec((tm, D), lambda i: (i, 0))
    vec = pl.BlockSpec((1, D), lambda i: (0, 0))
    return _pcall(body, name="mod_fwd", grid=(S // tm,), in_specs=[row, vec, vec, vec], out_specs=row,
                  out_shape=_sds((S, D), BF), compiler_params=_cp("parallel"))(x, g, scale, shift)


def resid_fwd(x, y, gate, coef):
    S, D = x.shape
    tm = _tile(S, 512, 8)

    def body(x_ref, y_ref, g_ref, o_ref):
        o_ref[...] = x_ref[...] + (coef * g_ref[...]) * y_ref[...]

    row = pl.BlockSpec((tm, D), lambda i: (i, 0))
    vec = pl.BlockSpec((1, D), lambda i: (0, 0))
    return _pcall(body, name="resid_fwd", grid=(S // tm,), in_specs=[row, row, vec], out_specs=row,
                  out_shape=_sds((S, D), F32), compiler_params=_cp("parallel"))(x, y, gate)


def resid_mod_fwd(x, y, gate, coef, g, scale, shift):
    S, D = x.shape
    tm = _tile(S, 512, 8)

    def body(x_ref, y_ref, gate_ref, g_ref, sc_ref, sh_ref, o_ref, h_ref):
        xv = x_ref[...] + (coef * gate_ref[...]) * y_ref[...]
        o_ref[...] = xv
        r = lax.rsqrt(jnp.mean(xv * xv, axis=-1, keepdims=True) + EPS)
        h_ref[...] = (((xv * r) * g_ref[...]) * (1.0 + sc_ref[...]) + sh_ref[...]).astype(h_ref.dtype)

    row = pl.BlockSpec((tm, D), lambda i: (i, 0))
    vec = pl.BlockSpec((1, D), lambda i: (0, 0))
    return _pcall(body, name="resid_mod_fwd", grid=(S // tm,), in_specs=[row, row, vec, vec, vec, vec], out_specs=[row, row],
                  out_shape=[_sds((S, D), F32), _sds((S, D), BF)], compiler_params=_cp("parallel"))(x, y, gate, g, scale, shift)


def gate_bwd(dxo, y, gate, coef):
    S, D = dxo.shape
    tm = _tile(S, 512, 8)

    def body(d_ref, y_ref, g_ref, dy_ref, dg_ref, ds_ref):
        first = pl.program_id(0) == 0
        d = d_ref[...]
        dy = (coef * g_ref[...]) * d
        dy_ref[...] = dy.astype(dy_ref.dtype)
        _acc(dg_ref, first, _colsum(coef * d * y_ref[...]))
        _acc(ds_ref, first, _colsum(dy))

    row = pl.BlockSpec((tm, D), lambda i: (i, 0))
    vec = pl.BlockSpec((1, D), lambda i: (0, 0))
    return _pcall(body, name="gate_bwd", grid=(S // tm,), in_specs=[row, row, vec], out_specs=[row, vec, vec],
                  out_shape=[_sds((S, D), BF), _sds((1, D), F32), _sds((1, D), F32)],
                  compiler_params=_cp("arbitrary"))(dxo, y, gate)


def mod_bwd(x, dh, dxo, g, scale):
    S, D = x.shape
    tm = _tile(S, 512, 8)

    def body(x_ref, dh_ref, do_ref, g_ref, sc_ref, dx_ref, dsh_ref, dsc_ref, dg_ref):
        first = pl.program_id(0) == 0
        xv = x_ref[...]
        dh = dh_ref[...].astype(F32)
        gv = g_ref[...]
        r = lax.rsqrt(jnp.mean(xv * xv, axis=-1, keepdims=True) + EPS)
        xh = xv * r
        dn = dh * (1.0 + sc_ref[...])
        u = dn * gv
        dx_ref[...] = do_ref[...] + r * (u - xh * jnp.mean(u * xh, axis=-1, keepdims=True))
        _acc(dsh_ref, first, _colsum(dh))
        _acc(dsc_ref, first, _colsum(dh * (xh * gv)))
        _acc(dg_ref, first, _colsum(dn * xh))

    row = pl.BlockSpec((tm, D), lambda i: (i, 0))
    vec = pl.BlockSpec((1, D), lambda i: (0, 0))
    return _pcall(body, name="mod_bwd", grid=(S // tm,), in_specs=[row, row, row, vec, vec],
                  out_specs=[row, vec, vec, vec],
                  out_shape=[_sds((S, D), F32), _sds((1, D), F32), _sds((1, D), F32), _sds((1, D), F32)],
                  compiler_params=_cp("arbitrary"))(x, dh, dxo, g, scale)


def mod_gate_bwd(x, dh, dxo, g, scale, y_prev, gate_prev, coef_prev):
    S, D = x.shape
    tm = _tile(S, 512, 8)

    def body(x_ref, dh_ref, do_ref, g_ref, sc_ref, y_ref, gp_ref, dx_ref, dsh_ref, dsc_ref, dg_ref, dy_ref, dgp_ref, dys_ref):
        first = pl.program_id(0) == 0
        xv = x_ref[...]
        dh = dh_ref[...].astype(F32)
        gv = g_ref[...]
        r = lax.rsqrt(jnp.mean(xv * xv, axis=-1, keepdims=True) + EPS)
        xh = xv * r
        dn = dh * (1.0 + sc_ref[...])
        u = dn * gv
        dx = do_ref[...] + r * (u - xh * jnp.mean(u * xh, axis=-1, keepdims=True))
        dx_ref[...] = dx
        _acc(dsh_ref, first, _colsum(dh))
        _acc(dsc_ref, first, _colsum(dh * (xh * gv)))
        _acc(dg_ref, first, _colsum(dn * xh))
        dy = (coef_prev * gp_ref[...]) * dx
        dy_ref[...] = dy.astype(dy_ref.dtype)
        _acc(dgp_ref, first, _colsum(coef_prev * dx * y_ref[...]))
        _acc(dys_ref, first, _colsum(dy))

    row = pl.BlockSpec((tm, D), lambda i: (i, 0))
    vec = pl.BlockSpec((1, D), lambda i: (0, 0))
    return _pcall(body, name="mod_gate_bwd", grid=(S // tm,), in_specs=[row, row, row, vec, vec, row, vec],
                  out_specs=[row, vec, vec, vec, row, vec, vec],
                  out_shape=[_sds((S, D), F32)] + [_sds((1, D), F32)] * 3 + [_sds((S, D), BF)] + [_sds((1, D), F32)] * 2,
                  compiler_params=_cp("arbitrary"))(x, dh, dxo, g, scale, y_prev, gate_prev)


def loss_fwd_bwd(y, target):
    S, D = y.shape
    tm = _tile(S, 512, 8)

    def body(y_ref, t_ref, l_ref, d_ref):
        first = pl.program_id(0) == 0
        e = y_ref[...] - t_ref[...]
        d_ref[...] = e * (1.0 / D)
        part = 0.5 * jnp.sum(jnp.mean(e * e, axis=-1, keepdims=True), axis=0, keepdims=True)
        _acc(l_ref, first, part)

    row = pl.BlockSpec((tm, D), lambda i: (i, 0))
    return _pcall(body, name="loss", grid=(S // tm,), in_specs=[row, row],
                  out_specs=[pl.BlockSpec((1, 1), lambda i: (0, 0)), row],
                  out_shape=[_sds((1, 1), F32), _sds((S, D), F32)], compiler_params=_cp("arbitrary"))(y, target)


def matmul_nt(x, wt, bias=None, out_dtype=F32):
    S, K = x.shape
    N = wt.shape[0]
    tm, tn = _tile(S, 512, 8), _tile(N, 512, LANES)

    def body(*refs):
        x_ref, w_ref = refs[0], refs[1]
        o_ref = refs[-1]
        r = _nt(x_ref[...].astype(BF), w_ref[...].astype(BF))
        if bias is not None:
            r = r + refs[2][...]
        o_ref[...] = r.astype(o_ref.dtype)

    in_specs = [pl.BlockSpec((tm, K), lambda i, j: (i, 0)), pl.BlockSpec((tn, K), lambda i, j: (j, 0))]
    ops = [x, wt]
    if bias is not None:
        in_specs.append(pl.BlockSpec((1, tn), lambda i, j: (0, j)))
        ops.append(bias)
    return _pcall(body, name="matmul_nt", grid=(S // tm, N // tn), in_specs=in_specs,
                  out_specs=pl.BlockSpec((tm, tn), lambda i, j: (i, j)), out_shape=_sds((S, N), out_dtype),
                  compiler_params=_cp("parallel", "parallel"))(*ops)


def matmul_nn(x, w, bias=None, out_dtype=F32):
    S, K = x.shape
    N = w.shape[1]
    tm, tn = _tile(S, 512, 8), _tile(N, 512, LANES)

    def body(*refs):
        x_ref, w_ref = refs[0], refs[1]
        o_ref = refs[-1]
        r = _nn(x_ref[...].astype(BF), w_ref[...].astype(BF))
        if bias is not None:
            r = r + refs[2][...]
        o_ref[...] = r.astype(o_ref.dtype)

    in_specs = [pl.BlockSpec((tm, K), lambda i, j: (i, 0)), pl.BlockSpec((K, tn), lambda i, j: (0, j))]
    ops = [x, w]
    if bias is not None:
        in_specs.append(pl.BlockSpec((1, tn), lambda i, j: (0, j)))
        ops.append(bias)
    return _pcall(body, name="matmul_nn", grid=(S // tm, N // tn), in_specs=in_specs,
                  out_specs=pl.BlockSpec((tm, tn), lambda i, j: (i, j)), out_shape=_sds((S, N), out_dtype),
                  compiler_params=_cp("parallel", "parallel"))(*ops)


def matmul_tn(a, b, out_dtype=BF):
    S, M = a.shape
    N = b.shape[1]
    bm, bn, tk = _tile(M, 1408, LANES), _tile(N, 1024, LANES), _tile(S, 2048, 8)
    nk = S // tk

    def body(a_ref, b_ref, o_ref, acc_ref):
        k = pl.program_id(2)
        _acc(acc_ref, k == 0, _tn(a_ref[...].astype(BF), b_ref[...].astype(BF)))

        @pl.when(k == nk - 1)
        def _():
            o_ref[...] = acc_ref[...].astype(o_ref.dtype)

    return _pcall(body, name="matmul_tn", grid=(M // bm, N // bn, nk),
                  in_specs=[pl.BlockSpec((tk, bm), lambda i, j, k: (k, i)), pl.BlockSpec((tk, bn), lambda i, j, k: (k, j))],
                  out_specs=pl.BlockSpec((bm, bn), lambda i, j, k: (i, j)), out_shape=_sds((M, N), out_dtype),
                  scratch_shapes=[pltpu.VMEM((bm, bn), F32)],
                  compiler_params=_cp("parallel", "parallel", "arbitrary"))(a, b)


def outer_rows(at, b):
    M, R = at.shape
    N = b.shape[1]
    tm = _tile(M, 256, 8)

    def body(a_ref, b_ref, o_ref):
        av, bv = a_ref[...], b_ref[...]
        acc = av[:, 0:1] * bv[0:1, :]
        for r in range(1, R):
            acc = acc + av[:, r:r + 1] * bv[r:r + 1, :]
        o_ref[...] = acc

    return _pcall(body, name="outer_rows", grid=(M // tm,),
                  in_specs=[pl.BlockSpec((tm, R), lambda i: (i, 0)), pl.BlockSpec((R, N), lambda i: (0, 0))],
                  out_specs=pl.BlockSpec((tm, N), lambda i: (i, 0)), out_shape=_sds((M, N), F32),
                  compiler_params=_cp("parallel"))(at, b)


def ffn_fwd(h, w1t, w3t, w2):
    S, D = h.shape
    Fd = w2.shape[0]
    tm, tf = _tile(S, 1024, 8), _tile(Fd, 256, LANES)

    def body(h_ref, w1_ref, w3_ref, g1_ref, g3_ref, a_ref):
        hv = h_ref[...]
        g1 = _nt(hv, w1_ref[...])
        g3 = _nt(hv, w3_ref[...])
        g1_ref[...] = g1.astype(g1_ref.dtype)
        g3_ref[...] = g3.astype(g3_ref.dtype)
        a_ref[...] = ((g1 * _sig(g1)) * g3).astype(a_ref.dtype)

    row = pl.BlockSpec((tm, D), lambda i, f: (i, 0))
    wsp = pl.BlockSpec((tf, D), lambda i, f: (f, 0))
    act = pl.BlockSpec((tm, tf), lambda i, f: (i, f))
    g1s, g3s, a = _pcall(body, name="ffn_fwd", grid=(S // tm, Fd // tf), in_specs=[row, wsp, wsp], out_specs=[act, act, act],
                         out_shape=[_sds((S, Fd), BF)] * 3, compiler_params=_cp("parallel", "parallel"))(h, w1t, w3t)

    tr, tn = _tile(S, 512, 8), _tile(D, 512, LANES)

    def y_body(a_ref, w2_ref, y_ref):
        y_ref[...] = _nn(a_ref[...], w2_ref[...])

    y = _pcall(y_body, name="ffn_out", grid=(S // tr, D // tn),
               in_specs=[pl.BlockSpec((tr, Fd), lambda i, j: (i, 0)), pl.BlockSpec((Fd, tn), lambda i, j: (0, j))],
               out_specs=pl.BlockSpec((tr, tn), lambda i, j: (i, j)), out_shape=_sds((S, D), F32),
               compiler_params=_cp("parallel", "parallel"))(a, w2)
    return y, g1s, g3s, a


def ffn_bwd(dy, g1s, g3s, w1t, w3t, w2):
    S, D = dy.shape
    Fd = w2.shape[0]
    tm, tf = _tile(S, 1024, 16), _tile(Fd, 256, LANES)
    halves = [pl.ds(0, tm // 2), pl.ds(tm // 2, tm // 2)]

    def body(dy_ref, g1_ref, g3_ref, w2_ref, d1_ref, d3_ref):
        w2v = w2_ref[...]
        das = [_nt(dy_ref[rows, :], w2v) for rows in halves]
        for rows, da in zip(halves, das):
            g1 = g1_ref[rows, :].astype(F32)
            g3 = g3_ref[rows, :].astype(F32)
            s = _sig(g1)
            d3_ref[rows, :] = (da * (g1 * s)).astype(d3_ref.dtype)
            d1_ref[rows, :] = (da * g3 * (s * (1.0 + g1 * (1.0 - s)))).astype(d1_ref.dtype)

    row = pl.BlockSpec((tm, D), lambda i, f: (i, 0))
    act = pl.BlockSpec((tm, tf), lambda i, f: (i, f))
    d1, d3 = _pcall(body, name="ffn_bwd", grid=(S // tm, Fd // tf),
                    in_specs=[row, act, act, pl.BlockSpec((tf, D), lambda i, f: (f, 0))], out_specs=[act, act],
                    out_shape=[_sds((S, Fd), BF)] * 2, compiler_params=_cp("parallel", "parallel"))(dy, g1s, g3s, w2)

    tr, tn = _tile(S, 512, 8), _tile(D, 512, LANES)

    def dh_body(d1_ref, d3_ref, w1_ref, w3_ref, dh_ref):
        dh_ref[...] = _nn(d1_ref[...], w1_ref[...]) + _nn(d3_ref[...], w3_ref[...])

    wide = pl.BlockSpec((tr, Fd), lambda i, j: (i, 0))
    wcol = pl.BlockSpec((Fd, tn), lambda i, j: (0, j))
    dh = _pcall(dh_body, name="ffn_dh", grid=(S // tr, D // tn), in_specs=[wide, wide, wcol, wcol],
                out_specs=pl.BlockSpec((tr, tn), lambda i, j: (i, j)), out_shape=_sds((S, D), F32),
                compiler_params=_cp("parallel", "parallel"))(d1, d3, w1t, w3t)
    return dh, d1, d3


def _layer_norm_parts(v):
    mu = jnp.mean(v, axis=-1, keepdims=True)
    vc = v - mu
    rs = lax.rsqrt(jnp.mean(vc * vc, axis=-1, keepdims=True) + EPS)
    return vc * rs, rs


def conv_mid_fwd(pre, dw, dw_b, ln_g, ln_b):
    S, D2 = pre.shape
    D = D2 // 2
    tm = _tile(S, 256, CONV_PAD)

    def body(pre_ref, dw_ref, dwb_ref, g_ref, b_ref, v_ref, sw_ref, ubuf):
        @pl.when(pl.program_id(0) == 0)
        def _():
            ubuf[pl.ds(0, CONV_PAD), :] = jnp.zeros((CONV_PAD, D), F32)

        @pl.when(pl.program_id(0) > 0)
        def _():
            ubuf[pl.ds(0, CONV_PAD), :] = ubuf[pl.ds(tm, CONV_PAD), :]

        ubuf[pl.ds(CONV_PAD, tm), :] = pre_ref[:, pl.ds(0, D)] * _sig(pre_ref[:, pl.ds(D, D)])
        for r0 in range(0, tm, CONV_ROWS):
            for c0 in range(0, D, LANES):
                cols = pl.ds(c0, LANES)
                acc = jnp.zeros((CONV_ROWS, LANES), F32) + dwb_ref[:, cols]
                for k in range(CONV_W):
                    acc = acc + dw_ref[pl.ds(k, 1), cols] * ubuf[pl.ds(r0 + CONV_PAD - (CONV_W - 1) + k, CONV_ROWS), cols]
                v_ref[pl.ds(r0, CONV_ROWS), cols] = acc
        vh, _ = _layer_norm_parts(v_ref[...])
        ln = vh * g_ref[...] + b_ref[...]
        sw_ref[...] = (ln * _sig(ln)).astype(sw_ref.dtype)

    row = pl.BlockSpec((tm, D), lambda i: (i, 0))
    vec = pl.BlockSpec((1, D), lambda i: (0, 0))
    return _pcall(body, name="conv_mid_fwd", grid=(S // tm,),
                  in_specs=[pl.BlockSpec((tm, D2), lambda i: (i, 0)), pl.BlockSpec((CONV_PAD, D), lambda i: (0, 0)), vec, vec, vec],
                  out_specs=[row, row], out_shape=[_sds((S, D), F32), _sds((S, D), BF)],
                  scratch_shapes=[pltpu.VMEM((tm + CONV_PAD, D), F32)], compiler_params=_cp("arbitrary"))(pre, dw, dw_b, ln_g, ln_b)


def conv_mid_bwd(pre, v, dsw, dw, ln_g, ln_b):
    S, D2 = pre.shape
    D = D2 // 2
    tm = _tile(S, 256, CONV_PAD)
    nt = S // tm

    def body(pre_ref, v_ref, dsw_ref, dw_ref, g_ref, b_ref, dpre_ref, dbin_ref, ddw_ref, ddwb_ref, dg_ref, db_ref, dvbuf):
        first = pl.program_id(0) == 0

        @pl.when(first)
        def _():
            dvbuf[pl.ds(tm, CONV_PAD), :] = jnp.zeros((CONV_PAD, D), F32)
            ddw_ref[...] = jnp.zeros((CONV_PAD, D), F32)
            dbin_ref[...] = jnp.zeros((1, D2), F32)

        @pl.when(jnp.logical_not(first))
        def _():
            dvbuf[pl.ds(tm, CONV_PAD), :] = dvbuf[pl.ds(0, CONV_PAD), :]

        gv = g_ref[...]
        vh, rs = _layer_norm_parts(v_ref[...])
        ln = vh * gv + b_ref[...]
        sg = _sig(ln)
        dln = dsw_ref[...] * (sg * (1.0 + ln * (1.0 - sg)))
        _acc(dg_ref, first, _colsum(dln * vh))
        _acc(db_ref, first, _colsum(dln))
        dvh = dln * gv
        dv = rs * (dvh - jnp.mean(dvh, axis=-1, keepdims=True) - vh * jnp.mean(dvh * vh, axis=-1, keepdims=True))
        _acc(ddwb_ref, first, _colsum(dv))
        dvbuf[pl.ds(0, tm), :] = dv
        for r0 in range(0, tm, CONV_ROWS):
            rws = pl.ds(r0, CONV_ROWS)
            for c0 in range(0, D, LANES):
                cols, gate_cols = pl.ds(c0, LANES), pl.ds(D + c0, LANES)
                a = pre_ref[rws, cols]
                sb = _sig(pre_ref[rws, gate_cols])
                u = a * sb
                du = jnp.zeros((CONV_ROWS, LANES), F32)
                for k in range(CONV_W):
                    sh = dvbuf[pl.ds(r0 + CONV_W - 1 - k, CONV_ROWS), cols]
                    du = du + dw_ref[pl.ds(k, 1), cols] * sh
                    ddw_ref[pl.ds(k, 1), cols] += _colsum(u * sh)
                da = du * sb
                db = da * a * (1.0 - sb)
                dpre_ref[rws, cols] = da.astype(dpre_ref.dtype)
                dpre_ref[rws, gate_cols] = db.astype(dpre_ref.dtype)
                dbin_ref[:, cols] += _colsum(da)
                dbin_ref[:, gate_cols] += _colsum(db)

    rev = lambda i: (nt - 1 - i, 0)
    row = pl.BlockSpec((tm, D), rev)
    row2 = pl.BlockSpec((tm, D2), rev)
    vec = pl.BlockSpec((1, D), lambda i: (0, 0))
    pad = pl.BlockSpec((CONV_PAD, D), lambda i: (0, 0))
    return _pcall(body, name="conv_mid_bwd", grid=(nt,), in_specs=[row2, row, row, pad, vec, vec],
                  out_specs=[row2, pl.BlockSpec((1, D2), lambda i: (0, 0)), pad, vec, vec, vec],
                  out_shape=[_sds((S, D2), BF), _sds((1, D2), F32), _sds((CONV_PAD, D), F32), _sds((1, D), F32),
                             _sds((1, D), F32), _sds((1, D), F32)],
                  scratch_shapes=[pltpu.VMEM((tm + CONV_PAD, D), F32)], compiler_params=_cp("arbitrary"))(pre, v, dsw, dw, ln_g, ln_b)


def _pool_count(i, tm, w):
    t = i * tm + lax.broadcasted_iota(jnp.int32, (tm, 1), 0)
    return jnp.minimum(t + 1, w).astype(F32)


def pool_fwd(h, p_w, p_b, p_scale):
    S, D = h.shape
    G, Dg = p_w.shape[0], p_w.shape[1]
    tm = _tile(S, 256, POOL_PAD)

    def body(h_ref, w_ref, b_ref, s_ref, y_ref, d_ref, hbuf):
        i = pl.program_id(0)

        @pl.when(i == 0)
        def _():
            hbuf[pl.ds(0, POOL_PAD), :] = jnp.zeros((POOL_PAD, D), F32)

        @pl.when(i > 0)
        def _():
            hbuf[pl.ds(0, POOL_PAD), :] = hbuf[pl.ds(tm, POOL_PAD), :]

        hbuf[pl.ds(POOL_PAD, tm), :] = h_ref[...].astype(F32)
        for g, w in enumerate(POOL_WINDOWS):
            cols = pl.ds(g * Dg, Dg)
            hg = hbuf[pl.ds(POOL_PAD, tm), cols]
            win = hg
            for j in range(1, w):
                win = win + hbuf[pl.ds(POOL_PAD - j, tm), cols]
            d = (win / _pool_count(i, tm, w) - hg).astype(BF)
            d_ref[:, cols] = d
            y_ref[:, cols] = (_nn(d, w_ref[g].astype(BF)) + b_ref[:, cols]) * s_ref[:, cols]

    row = pl.BlockSpec((tm, D), lambda i: (i, 0))
    vec = pl.BlockSpec((1, D), lambda i: (0, 0))
    return _pcall(body, name="pool_fwd", grid=(S // tm,),
                  in_specs=[row, pl.BlockSpec((G, Dg, Dg), lambda i: (0, 0, 0)), vec, vec], out_specs=[row, row],
                  out_shape=[_sds((S, D), F32), _sds((S, D), BF)],
                  scratch_shapes=[pltpu.VMEM((tm + POOL_PAD, D), F32)], compiler_params=_cp("arbitrary"))(h, p_w, p_b, p_scale)


def pool_bwd(dy, d, p_w, p_b, p_scale):
    S, D = dy.shape
    G, Dg = p_w.shape[0], p_w.shape[1]
    tm = _tile(S, 256, POOL_PAD)
    nt = S // tm

    def body(dy_ref, d_ref, w_ref, b_ref, s_ref, dh_ref, dyy_ref, dpb_ref, dps_ref, ebuf):
        j = pl.program_id(0)
        i = nt - 1 - j
        first = j == 0

        @pl.when(first)
        def _():
            ebuf[pl.ds(tm, POOL_PAD), :] = jnp.zeros((POOL_PAD, D), F32)

        @pl.when(jnp.logical_not(first))
        def _():
            ebuf[pl.ds(tm, POOL_PAD), :] = ebuf[pl.ds(0, POOL_PAD), :]

        dpb, dps = [], []
        for g, w in enumerate(POOL_WINDOWS):
            cols = pl.ds(g * Dg, Dg)
            wg = w_ref[g].astype(BF)
            dyg = dy_ref[:, cols].astype(F32)
            ypre = _nn(d_ref[:, cols], wg) + b_ref[:, cols]
            dps.append(_colsum(dyg * ypre))
            dyy = dyg * s_ref[:, cols]
            dpb.append(_colsum(dyy))
            dyy = dyy.astype(BF)
            dyy_ref[:, cols] = dyy
            dd = _nt(dyy, wg)
            ebuf[pl.ds(0, tm), cols] = dd / _pool_count(i, tm, w)
            acc = -dd
            for k in range(w):
                acc = acc + ebuf[pl.ds(k, tm), cols]
            dh_ref[:, cols] = acc
        _acc(dpb_ref, first, jnp.concatenate(dpb, axis=1))
        _acc(dps_ref, first, jnp.concatenate(dps, axis=1))

    row = pl.BlockSpec((tm, D), lambda i: (nt - 1 - i, 0))
    vec = pl.BlockSpec((1, D), lambda i: (0, 0))
    return _pcall(body, name="pool_bwd", grid=(nt,),
                  in_specs=[row, row, pl.BlockSpec((G, Dg, Dg), lambda i: (0, 0, 0)), vec, vec],
                  out_specs=[row, row, vec, vec],
                  out_shape=[_sds((S, D), F32), _sds((S, D), BF), _sds((1, D), F32), _sds((1, D), F32)],
                  scratch_shapes=[pltpu.VMEM((tm + POOL_PAD, D), F32)], compiler_params=_cp("arbitrary"))(dy, d, p_w, p_b, p_scale)


def _group_sum(v, low):
    s_lo = jnp.sum(jnp.where(low, v, 0.0), axis=-1, keepdims=True)
    s_hi = jnp.sum(jnp.where(low, 0.0, v), axis=-1, keepdims=True)
    return jnp.where(low, s_lo, s_hi)


def qknorm_fwd(qkv, q_g2, k_g2):
    S, D3 = qkv.shape
    nb = D3 // LANES // 3
    tm = _tile(S, 512, 8)

    def body(x_ref, qg_ref, kg_ref, o_ref):
        low = lax.broadcasted_iota(jnp.int32, (1, LANES), 1) < HEAD_DIM
        for j in range(3 * nb):
            cols = pl.ds(j * LANES, LANES)
            xv = x_ref[:, cols]
            if j < 2 * nb:
                r = lax.rsqrt(_group_sum(xv * xv, low) * (1.0 / HEAD_DIM) + EPS)
                xv = ((xv * r) * qg_ref[...]) * Q_SCALE if j < nb else (xv * r) * kg_ref[...]
            o_ref[:, cols] = xv.astype(o_ref.dtype)

    blk = pl.BlockSpec((tm, D3), lambda i: (i, 0))
    vec = pl.BlockSpec((1, LANES), lambda i: (0, 0))
    return _pcall(body, name="qknorm_fwd", grid=(S // tm,), in_specs=[blk, vec, vec], out_specs=blk,
                  out_shape=_sds((S, D3), BF), compiler_params=_cp("parallel"))(qkv, q_g2, k_g2)


def qknorm_bwd(qkv, dq, dk, dv, q_g2, k_g2):
    S, D3 = qkv.shape
    D = D3 // 3
    nb = D // LANES
    tm = _tile(S, 256, 8)

    def body(x_ref, dq_ref, dk_ref, dv_ref, qg_ref, kg_ref, o_ref, dg_ref):
        low = lax.broadcasted_iota(jnp.int32, (1, LANES), 1) < HEAD_DIM
        dgs = []
        for j in range(3 * nb):
            cols, part = pl.ds(j * LANES, LANES), pl.ds((j % nb) * LANES, LANES)
            if j >= 2 * nb:
                o_ref[:, cols] = dv_ref[:, part].astype(o_ref.dtype)
                dgs.append(jnp.zeros((1, LANES), F32))
                continue
            xv = x_ref[:, cols]
            r = lax.rsqrt(_group_sum(xv * xv, low) * (1.0 / HEAD_DIM) + EPS)
            xh = xv * r
            dn = dq_ref[:, part] * (Q_SCALE * LN2) if j < nb else dk_ref[:, part] * LN2
            u = dn * (qg_ref[...] if j < nb else kg_ref[...])
            o_ref[:, cols] = (r * (u - xh * (_group_sum(u * xh, low) * (1.0 / HEAD_DIM)))).astype(o_ref.dtype)
            dgs.append(_colsum(dn * xh))
        _acc(dg_ref, pl.program_id(0) == 0, jnp.concatenate(dgs, axis=1))

    wide = pl.BlockSpec((tm, D3), lambda i: (i, 0))
    row = pl.BlockSpec((tm, D), lambda i: (i, 0))
    vec = pl.BlockSpec((1, LANES), lambda i: (0, 0))
    return _pcall(body, name="qknorm_bwd", grid=(S // tm,), in_specs=[wide, row, row, row, vec, vec],
                  out_specs=[wide, pl.BlockSpec((1, D3), lambda i: (0, 0))],
                  out_shape=[_sds((S, D3), BF), _sds((1, D3), F32)], compiler_params=_cp("arbitrary"))(qkv, dq, dk, dv, q_g2, k_g2)


LOG2E = math.log2(math.e)
LN2 = math.log(2.0)
Q_SCALE = HEAD_DIM ** -0.5 * LOG2E
FAR = 160.0
UNSEEN = 1e30


def _softplus2(z2):
    return jnp.maximum(z2, 0.0) + jnp.log(1.0 + jnp.exp2(-jnp.abs(z2))) * LOG2E


def _attn_tile(S):
    return LANES


def attn_fwd(qkvn):
    S, D3 = qkvn.shape
    D = D3 // 3
    nb = D // LANES
    t = _attn_tile(S)
    nq = S // t
    assert nq <= LANES

    def body(q_ref, k_ref, v_ref, o_ref, r_ref, acc_ref, rall_ref, behind_ref, z_ref):
        qi = pl.program_id(1)
        lane = lax.broadcasted_iota(jnp.int32, (1, LANES), 1)
        rows = lax.broadcasted_iota(jnp.int32, (t, t), 0)
        cols = lax.broadcasted_iota(jnp.int32, (t, t), 1)
        later = (rows > cols).astype(BF)
        q = q_ref[...]
        qh = [jnp.where(lane < HEAD_DIM, q, jnp.zeros_like(q)), jnp.where(lane >= HEAD_DIM, q, jnp.zeros_like(q))]
        acc_ref[...] = jnp.zeros((2, t, LANES), F32)
        rall_ref[...] = jnp.full((2, t, LANES), UNSEEN, F32)
        behind_ref[...] = jnp.zeros((2, t, LANES), F32)

        def scores(b):
            kb = k_ref[pl.ds(pl.multiple_of(b * t, t), t), :]
            return [_nt(qh[0], kb), _nt(qh[1], kb)]

        def block(b, diagonal):
            vb = v_ref[pl.ds(pl.multiple_of(b * t, t), t), :]
            lsig, sp, within = [None, None], [None, None], [None, None]
            for hh in range(2):
                z = z_ref[hh]
                sp_full = _softplus2(z)
                lsig[hh] = z - sp_full
                sp[hh] = jnp.where(cols < rows, sp_full, 0.0) if diagonal else sp_full
                within[hh] = _nn(sp[hh].astype(BF), later)
            z_next = scores(jnp.maximum(b - 1, 0))
            for hh in range(2):
                behind = behind_ref[hh]
                a = jnp.exp2(lsig[hh] - (within[hh] + jnp.tile(behind, (1, t // LANES))))
                if diagonal:
                    a = jnp.where(cols < rows, a, 0.0)
                acc_ref[hh] += _nn(a.astype(BF), vb)
                rall_ref[hh] = jnp.where(lane == b, behind, rall_ref[hh])
                behind_ref[hh] = behind + jnp.sum(sp[hh], axis=-1, keepdims=True)
            for hh in range(2):
                z_ref[hh] = z_next[hh]

        z_first = scores(qi)
        for hh in range(2):
            z_ref[hh] = z_first[hh]
        block(qi, True)

        def reaches():
            return (jnp.min(behind_ref[...]) < FAR).astype(jnp.int32)

        def step(carry):
            n, _ = carry
            block(qi - 1 - n, False)
            return n + 1, reaches()

        lax.while_loop(lambda carry: jnp.logical_and(carry[0] < qi, carry[1] > 0), step, (jnp.int32(0), reaches()))
        r_ref[...] = rall_ref[...]
        o_ref[...] = jnp.where(lane < HEAD_DIM, acc_ref[0], acc_ref[1]).astype(o_ref.dtype)

    return _pcall(body, name="attn_fwd", grid=(nb, nq),
                  in_specs=[pl.BlockSpec((t, LANES), lambda h, i: (i, h)),
                            pl.BlockSpec((S, LANES), lambda h, i: (0, nb + h)),
                            pl.BlockSpec((S, LANES), lambda h, i: (0, 2 * nb + h))],
                  out_specs=[pl.BlockSpec((t, LANES), lambda h, i: (i, h)), pl.BlockSpec((2, t, LANES), lambda h, i: (h, i, 0))],
                  out_shape=[_sds((S, D), BF), _sds((2 * nb, S, LANES), F32)],
                  scratch_shapes=[pltpu.VMEM((2, t, LANES), F32)] * 3 + [pltpu.VMEM((2, t, t), F32)],
                  compiler_params=_cp("parallel", "arbitrary"))(qkvn, qkvn, qkvn)


def attn_bwd(qkvn, do, r_all):
    S, D3 = qkvn.shape
    D = D3 // 3
    nb = D // LANES
    t = _attn_tile(S)
    nq = S // t

    def body(q_ref, k_ref, v_ref, do_ref, r_ref, dq_ref, dk_hbm, dv_hbm, dk_acc, dv_acc, dq_acc, before_ref, z_ref, da_ref, sem):
        hp = pl.program_id(0)
        qi = pl.program_id(1)

        @pl.when(qi == 0)
        def _():
            dk_acc[...] = jnp.zeros((S, LANES), F32)
            dv_acc[...] = jnp.zeros((S, LANES), F32)

        lane = lax.broadcasted_iota(jnp.int32, (1, LANES), 1)
        rows = lax.broadcasted_iota(jnp.int32, (t, t), 0)
        cols = lax.broadcasted_iota(jnp.int32, (t, t), 1)
        later = (rows > cols).astype(BF)
        earlier = (rows < cols).astype(BF)
        q = q_ref[...]
        dov = do_ref[...]
        low, high = lane < HEAD_DIM, lane >= HEAD_DIM
        qh = [jnp.where(low, q, jnp.zeros_like(q)), jnp.where(high, q, jnp.zeros_like(q))]
        doh = [jnp.where(low, dov, jnp.zeros_like(dov)), jnp.where(high, dov, jnp.zeros_like(dov))]
        q_both = jnp.concatenate(qh, axis=0)
        do_both = jnp.concatenate(doh, axis=0)
        dq_acc[...] = jnp.zeros((2, t, LANES), F32)
        before_ref[...] = jnp.zeros((2, t, LANES), F32)

        def scores(b):
            off = pl.multiple_of(b * t, t)
            kb = k_ref[pl.ds(off, t), :]
            vb = v_ref[pl.ds(off, t), :]
            return [_nt(qh[0], kb), _nt(qh[1], kb)], [_nt(doh[0], vb), _nt(doh[1], vb)]

        def block(b, diagonal):
            off = pl.multiple_of(b * t, t)
            kb = k_ref[pl.ds(off, t), :]
            lsig, sigma, within, g, earlier_g, dz, prob = ([None, None] for _ in range(7))
            for hh in range(2):
                z = z_ref[hh]
                sp = _softplus2(z)
                lsig[hh] = z - sp
                if diagonal:
                    sp = jnp.where(cols < rows, sp, 0.0)
                within[hh] = _nn(sp.astype(BF), later)
            z_next, da_next = scores(jnp.minimum(b + 1, qi))
            for hh in range(2):
                behind = jnp.sum(jnp.where(lane == b, r_ref[hh], 0.0), axis=-1, keepdims=True)
                a = jnp.exp2(lsig[hh] - (within[hh] + behind))
                if diagonal:
                    a = jnp.where(cols < rows, a, 0.0)
                prob[hh] = a.astype(BF)
                g[hh] = a * da_ref[hh]
                earlier_g[hh] = _nn(g[hh].astype(BF), earlier)
            for hh in range(2):
                before = before_ref[hh]
                d = g[hh] - jnp.exp2(lsig[hh]) * (g[hh] + (earlier_g[hh] + jnp.tile(before, (1, t // LANES))))
                if diagonal:
                    d = jnp.where(cols < rows, d, 0.0)
                dz[hh] = d.astype(BF)
                dq_acc[hh] += _nn(dz[hh], kb)
                before_ref[hh] = before + jnp.sum(g[hh], axis=-1, keepdims=True)
            dk_acc[pl.ds(off, t), :] += _tn(jnp.concatenate(dz, axis=0), q_both)
            dv_acc[pl.ds(off, t), :] += _tn(jnp.concatenate(prob, axis=0), do_both)
            for hh in range(2):
                z_ref[hh] = z_next[hh]
                da_ref[hh] = da_next[hh]

        nearest = jnp.min(jnp.minimum(r_ref[0], r_ref[1]), axis=0, keepdims=True)
        skip = jnp.sum(jnp.where(jnp.logical_and(lane < qi, nearest >= FAR), 1, 0))
        z_first, da_first = scores(skip)
        for hh in range(2):
            z_ref[hh] = z_first[hh]
            da_ref[hh] = da_first[hh]

        def step(b, carry):
            block(b, False)
            return carry

        lax.fori_loop(skip, qi, step, 0)
        block(qi, True)
        dq_ref[...] = jnp.where(low, dq_acc[0], dq_acc[1])

        @pl.when(qi == nq - 1)
        def _():
            col = pl.multiple_of(hp * LANES, LANES)
            c1 = pltpu.make_async_copy(dk_acc, dk_hbm.at[:, pl.ds(col, LANES)], sem.at[0])
            c2 = pltpu.make_async_copy(dv_acc, dv_hbm.at[:, pl.ds(col, LANES)], sem.at[1])
            c1.start()
            c2.start()
            c1.wait()
            c2.wait()

    blk = pl.BlockSpec((t, LANES), lambda h, i: (i, h))
    hbm = pl.BlockSpec(memory_space=pl.ANY)
    return _pcall(body, name="attn_bwd", grid=(nb, nq),
                  in_specs=[blk, pl.BlockSpec((S, LANES), lambda h, i: (0, nb + h)),
                            pl.BlockSpec((S, LANES), lambda h, i: (0, 2 * nb + h)), blk,
                            pl.BlockSpec((2, t, LANES), lambda h, i: (h, i, 0))],
                  out_specs=[blk, hbm, hbm],
                  out_shape=[_sds((S, D), F32), _sds((S, D), F32), _sds((S, D), F32)],
                  scratch_shapes=[pltpu.VMEM((S, LANES), F32), pltpu.VMEM((S, LANES), F32), pltpu.VMEM((2, t, LANES), F32),
                                  pltpu.VMEM((2, t, LANES), F32), pltpu.VMEM((2, t, t), F32), pltpu.VMEM((2, t, t), F32),
                                  pltpu.SemaphoreType.DMA((2,))],
                  compiler_params=_cp("arbitrary", "arbitrary"))(qkvn, qkvn, qkvn, do, r_all)


def cond_embed(parts, cond_b):
    P, B, D = parts.shape

    def body(p_ref, b_ref, pre_ref, e_ref):
        pre = p_ref[0]
        for s in range(1, P):
            pre = pre + p_ref[s]
        pre = pre + b_ref[...]
        pre_ref[...] = pre
        e_ref[...] = pre * _sig(pre)

    return _pcall(body, name="cond_embed", out_shape=[_sds((B, D), F32), _sds((B, D), F32)],
                  compiler_params=_cp())(parts, cond_b)


def cond_embed_bwd(parts, pre):
    P, B, D = parts.shape

    def body(p_ref, pre_ref, dpre_ref, db_ref):
        de = p_ref[0]
        for s in range(1, P):
            de = de + p_ref[s]
        pre = pre_ref[...]
        s = _sig(pre)
        dpre = de * (s * (1.0 + pre * (1.0 - s)))
        dpre_ref[...] = dpre
        db_ref[...] = _colsum(dpre)

    return _pcall(body, name="cond_embed_bwd", out_shape=[_sds((B, D), F32), _sds((1, D), F32)],
                  compiler_params=_cp())(parts, pre)


def adamw(w, parts, m, v):
    R, C = w.shape
    P = parts.shape[0]
    tr = _tile(R, max(8, (1 << 17) // C // 8 * 8), 8)
    c1 = 1.0 - ADAM_B1 ** ADAM_STEP
    c2 = 1.0 - ADAM_B2 ** ADAM_STEP

    def body(w_ref, p_ref, m_ref, v_ref, g_ref, d_ref, nm_ref, nv_ref):
        g = p_ref[0].astype(F32)
        for s in range(1, P):
            g = g + p_ref[s].astype(F32)
        nm = ADAM_B1 * m_ref[...] + (1.0 - ADAM_B1) * g
        nv = ADAM_B2 * v_ref[...] + (1.0 - ADAM_B2) * (g * g)
        g_ref[...] = g
        nm_ref[...] = nm
        nv_ref[...] = nv
        d_ref[...] = -ADAM_LR * ((nm / c1) / (jnp.sqrt(nv / c2) + ADAM_EPS) + ADAM_WD * w_ref[...])

    blk = pl.BlockSpec((tr, C), lambda i: (i, 0))
    return _pcall(body, name="adamw", grid=(R // tr,), in_specs=[blk, pl.BlockSpec((P, tr, C), lambda i: (0, i, 0)), blk, blk],
                  out_specs=[blk] * 4, out_shape=[_sds((R, C), F32)] * 4, compiler_params=_cp("parallel"))(w, parts, m, v)


def exchange(arrs, peer_axes):
    n = len(arrs)
    shapes = [a.shape if ax is None else a.shape[:ax] + a.shape[ax + 1:] for a, ax in zip(arrs, peer_axes)]

    def body(*refs):
        ins, outs = refs[:n], refs[n:2 * n]
        send, recv, local = refs[2 * n:]
        x, y, c = lax.axis_index("x"), lax.axis_index("y"), lax.axis_index("c")
        me = 4 * x + 2 * y + c

        def piece(i, d):
            if peer_axes[i] is None:
                return ins[i]
            return ins[i].at[(slice(None),) * peer_axes[i] + (d,)]

        mine = [pltpu.make_async_copy(piece(i, me), outs[i].at[me], local.at[i]) for i in range(n)]
        for cp in mine:
            cp.start()
        sends, recvs = [], []
        for p in range(1, N_DEV):
            peer = (1 - x if p & 4 else x, 1 - y if p & 2 else y, 1 - c if p & 1 else c)
            them = 4 * peer[0] + 2 * peer[1] + peer[2]
            for i in range(n):
                sends.append(pltpu.make_async_remote_copy(piece(i, them), outs[i].at[me], send.at[i, p - 1], recv.at[i, p - 1],
                                                          device_id=peer, device_id_type=MESH))
                recvs.append(pltpu.make_async_remote_copy(piece(i, me), outs[i].at[them], send.at[i, p - 1], recv.at[i, p - 1],
                                                          device_id=peer, device_id_type=MESH))
        for cp in sends:
            cp.start()
        for cp in recvs:
            cp.wait_recv()
        for cp in sends:
            cp.wait_send()
        for cp in mine:
            cp.wait()

    hbm = pl.BlockSpec(memory_space=pl.ANY)
    return _pcall(body, name="exchange", in_specs=[hbm] * n, out_specs=[hbm] * n,
                  out_shape=[_sds((N_DEV,) + s, a.dtype) for s, a in zip(shapes, arrs)],
                  scratch_shapes=[pltpu.SemaphoreType.DMA((n, N_DEV - 1)), pltpu.SemaphoreType.DMA((n, N_DEV - 1)),
                                  pltpu.SemaphoreType.DMA((n,))])(*arrs)


def gather_via_sibling(arrs):
    n = len(arrs)

    def body(*refs):
        ins, outs = refs[:n], refs[n:2 * n]
        send, recv, local = refs[2 * n:]
        x, y, c = lax.axis_index("x"), lax.axis_index("y"), lax.axis_index("c")
        me, sibling = (x, y, c), (x, y, 1 - c)
        chips = [(1 - x, y), (x, 1 - y), (1 - x, 1 - y)]
        index = lambda d: 4 * d[0] + 2 * d[1] + d[2]

        def copy(i, k, block, to, src=None):
            dst = outs[i].at[index(block)]
            return pltpu.make_async_remote_copy(dst if src is None else src, dst, send.at[i, k], recv.at[i, k],
                                                device_id=to, device_id_type=MESH)

        mine = [pltpu.make_async_copy(ins[i], outs[i].at[index(me)], local.at[i]) for i in range(n)]
        for cp in mine:
            cp.start()
        first = []
        for i in range(n):
            first.append(copy(i, 0, me, sibling, src=ins[i]))
            first += [copy(i, 1 + j, me, (*chip, c), src=ins[i]) for j, chip in enumerate(chips)]
        for cp in first:
            cp.start()
        passed = []
        for j, chip in enumerate(chips):
            for i in range(n):
                copy(i, 1 + j, (*chip, c), me).wait_recv()
                passed.append(copy(i, 4 + j, (*chip, c), sibling))
                passed[-1].start()
        for i in range(n):
            copy(i, 0, sibling, me).wait_recv()
            for j, chip in enumerate(chips):
                copy(i, 4 + j, (*chip, 1 - c), me).wait_recv()
        for cp in first + passed:
            cp.wait_send()
        for cp in mine:
            cp.wait()

    hbm = pl.BlockSpec(memory_space=pl.ANY)
    return _pcall(body, name="gather_via_sibling", in_specs=[hbm] * n, out_specs=[hbm] * n,
                  out_shape=[_sds((N_DEV,) + a.shape, a.dtype) for a in arrs],
                  scratch_shapes=[pltpu.SemaphoreType.DMA((n, N_DEV - 1)), pltpu.SemaphoreType.DMA((n, N_DEV - 1)),
                                  pltpu.SemaphoreType.DMA((n,))])(*arrs)


def _pack(vs):
    flat = jnp.concatenate([v.reshape(-1).astype(F32) for v in vs])
    pad = (-flat.shape[0]) % 1024
    return jnp.pad(flat, (0, pad)).reshape(-1, 1024)


def _unpack(packed, shapes):
    flat = packed.reshape(packed.shape[0], -1)
    out, o = [], 0
    for s in shapes:
        n = math.prod(s)
        out.append(flat[:, o:o + n].reshape((packed.shape[0],) + tuple(s)))
        o += n
    return out


def _cat_dev(g, axis):
    g = jnp.moveaxis(g, 0, axis)
    return g.reshape(g.shape[:axis] + (g.shape[axis] * g.shape[axis + 1],) + g.shape[axis + 2:])


def kernel(x, c, cond_w, cond_b, ada_w, ada_b, norm_g, ffn_w1, ffn_w3, ffn_w2, a_w_in, a_b_in, a_dw, a_dw_b, a_ln_g, a_ln_b, a_w_out, a_b_out, b_w_qkv, b_q_g, b_k_g, b_w_o, p_w, p_b, p_scale, loss_target, m_cond_w, m_cond_b, m_ada_w, m_ada_b, m_norm_g, m_ffn_w1, m_ffn_w3, m_ffn_w2, m_a_w_in, m_a_b_in, m_a_dw, m_a_dw_b, m_a_ln_g, m_a_ln_b, m_a_w_out, m_a_b_out, m_b_w_qkv, m_b_q_g, m_b_k_g, m_b_w_o, m_p_w, m_p_b, m_p_scale, v_cond_w, v_cond_b, v_ada_w, v_ada_b, v_norm_g, v_ffn_w1, v_ffn_w3, v_ffn_w2, v_a_w_in, v_a_b_in, v_a_dw, v_a_dw_b, v_a_ln_g, v_a_ln_b, v_a_w_out, v_a_b_out, v_b_w_qkv, v_b_q_g, v_b_k_g, v_b_w_o, v_p_w, v_p_b, v_p_scale):
    A = dict(zip(ARGS, (x, c, cond_w, cond_b, ada_w, ada_b, norm_g, ffn_w1, ffn_w3, ffn_w2, a_w_in, a_b_in, a_dw, a_dw_b, a_ln_g, a_ln_b, a_w_out, a_b_out, b_w_qkv, b_q_g, b_k_g, b_w_o, p_w, p_b, p_scale, loss_target, m_cond_w, m_cond_b, m_ada_w, m_ada_b, m_norm_g, m_ffn_w1, m_ffn_w3, m_ffn_w2, m_a_w_in, m_a_b_in, m_a_dw, m_a_dw_b, m_a_ln_g, m_a_ln_b, m_a_w_out, m_a_b_out, m_b_w_qkv, m_b_q_g, m_b_k_g, m_b_w_o, m_p_w, m_p_b, m_p_scale, v_cond_w, v_cond_b, v_ada_w, v_ada_b, v_norm_g, v_ffn_w1, v_ffn_w3, v_ffn_w2, v_a_w_in, v_a_b_in, v_a_dw, v_a_dw_b, v_a_ln_g, v_a_ln_b, v_a_w_out, v_a_b_out, v_b_w_qkv, v_b_q_g, v_b_k_g, v_b_w_o, v_p_w, v_p_b, v_p_scale)))
    S, D = x.shape[1], x.shape[2]
    depth = ada_w.shape[0]
    n_a, n_b, n_c = a_w_in.shape[0], b_w_qkv.shape[0], p_w.shape[0]
    me = 4 * lax.axis_index("x") + 2 * lax.axis_index("y") + lax.axis_index("c")
    swap = lambda w: jnp.swapaxes(w, -1, -2)

    small_names = ['norm_g', 'a_b_in', 'a_dw', 'a_dw_b', 'a_ln_g', 'a_ln_b', 'a_b_out', 'p_w', 'p_b', 'p_scale']
    small_in = [c] + [A[n] for n in small_names]
    big_in = [swap(ffn_w1).astype(BF), swap(ffn_w3).astype(BF), ffn_w2.astype(BF), swap(a_w_in).astype(BF),
              a_w_out.astype(BF), swap(b_w_qkv).astype(BF), b_w_o.astype(BF)]
    got = gather_via_sibling(big_in + [_pack(small_in)])
    w1t, w3t, w2 = _cat_dev(got[0], 2), _cat_dev(got[1], 2), _cat_dev(got[2], 2)
    w_in_t, w_out = _cat_dev(got[3], 1), _cat_dev(got[4], 1)
    w_qkv_t, w_o = _cat_dev(got[5], 1), _cat_dev(got[6], 1)
    sm = dict(zip(['c'] + small_names, _unpack(got[7], [v.shape for v in small_in])))
    c_all = sm['c'][:, 0]
    norm_g_f = _cat_dev(sm['norm_g'], 2)
    a_b_in_f, a_dw_f, a_dw_b_f = _cat_dev(sm['a_b_in'], 1), _cat_dev(sm['a_dw'], 2), _cat_dev(sm['a_dw_b'], 1)
    a_ln_g_f, a_ln_b_f, a_b_out_f = _cat_dev(sm['a_ln_g'], 1), _cat_dev(sm['a_ln_b'], 1), _cat_dev(sm['a_b_out'], 1)
    p_w_f, p_b_f, p_scale_f = _cat_dev(sm['p_w'], 2), _cat_dev(sm['p_b'], 2), _cat_dev(sm['p_scale'], 1)
    a_dw_f = jnp.pad(a_dw_f, ((0, 0), (0, CONV_PAD - CONV_W), (0, 0)))

    rows = D // N_DEV
    c_mine = lax.dynamic_slice_in_dim(c_all, me * rows, rows, axis=1)
    pre_parts = exchange([matmul_nn(c_mine, cond_w)], [None])[0]
    pre_all, e_all = cond_embed(pre_parts, cond_b.reshape(1, D))
    n_mod = ada_w.shape[2]
    ada_b_mine = lax.dynamic_slice_in_dim(ada_b, me * n_mod, n_mod, axis=1)
    mod_part = jnp.stack([matmul_nn(e_all, ada_w[i], ada_b_mine[i:i + 1]) for i in range(depth)], axis=1)
    mod_all = exchange([mod_part], [None])[0]
    mod = lax.dynamic_index_in_dim(mod_all, me, axis=1, keepdims=False)
    mod = jnp.moveaxis(mod, 0, 1).reshape(depth, 3, 3, 1, D)

    xs = x[0]
    saved = []
    ia = ib = ic = 0

    def mod_of(i, sub):
        return norm_g_f[i, sub].reshape(1, D), mod[i, sub, 1], mod[i, sub, 0], 1.0 + mod[i, sub, 2]

    h = mod_fwd(xs, *mod_of(0, 0)[:3])
    for i in range(depth):
        lay = {}
        for sub in range(3):
            g, scale, shift, gate = mod_of(i, sub)
            rec = dict(x=xs, h=h, g=g, scale=scale, gate=gate)
            if sub != 1:
                j = 0 if sub == 0 else 1
                y, g1s, g3s, act = ffn_fwd(h, w1t[i, j], w3t[i, j], w2[i, j])
                rec.update(kind='ffn', j=j, coef=0.5, g1=g1s, g3=g3s, act=act)
            elif i % 3 == 0:
                pre = matmul_nt(h, w_in_t[ia], a_b_in_f[ia].reshape(1, 2 * D))
                v, sw = conv_mid_fwd(pre, a_dw_f[ia], a_dw_b_f[ia].reshape(1, D), a_ln_g_f[ia].reshape(1, D), a_ln_b_f[ia].reshape(1, D))
                y = matmul_nn(sw, w_out[ia], a_b_out_f[ia].reshape(1, D))
                rec.update(kind='conv', idx=ia, coef=1.0, pre=pre, v=v, sw=sw)
                ia += 1
            elif i % 3 == 1:
                qkv = matmul_nt(h, w_qkv_t[ib])
                qg2 = jnp.tile(b_q_g[ib].reshape(1, HEAD_DIM), (1, 2))
                kg2 = jnp.tile(b_k_g[ib].reshape(1, HEAD_DIM), (1, 2))
                qkvn = qknorm_fwd(qkv, qg2, kg2)
                o, r_all = attn_fwd(qkvn)
                y = matmul_nn(o, w_o[ib])
                rec.update(kind='attn', idx=ib, coef=1.0, qkv=qkv, qkvn=qkvn, o=o, r_all=r_all, qg2=qg2, kg2=kg2)
                ib += 1
            else:
                pb, ps = p_b_f[ic].reshape(1, D), p_scale_f[ic].reshape(1, D)
                y, dpool = pool_fwd(h, p_w_f[ic], pb, ps)
                rec.update(kind='pool', idx=ic, coef=1.0, d=dpool, pb=pb, ps=ps)
                ic += 1
            rec['y'] = y
            if (i, sub) == (depth - 1, 2):
                xs = resid_fwd(xs, y, gate, rec['coef'])
            else:
                xs, h = resid_mod_fwd(xs, y, gate, rec['coef'], *mod_of(*((i, sub + 1) if sub < 2 else (i + 1, 0)))[:3])
            lay[sub] = rec
        saved.append(lay)

    loss_part, dx = loss_fwd_bwd(xs, loss_target[0])
    loss = lax.psum(loss_part[0, 0], ("x", "y", "c"))

    Fd = w2.shape[2]
    zeros = lambda *s: jnp.zeros(s, F32)
    g_w1t, g_w3t, g_w2 = [[None, None] for _ in range(depth)], [[None, None] for _ in range(depth)], [[None, None] for _ in range(depth)]
    g_w_in_t, g_w_out, g_w_qkv_t, g_w_o, g_p_w = [None] * n_a, [None] * n_a, [None] * n_b, [None] * n_b, [None] * n_c
    g_b_in, g_dw, g_dw_b, g_ln_g, g_ln_b, g_b_out = ([None] * n_a for _ in range(6))
    g_q_g, g_k_g, g_p_b, g_p_scale = [None] * n_b, [None] * n_b, [None] * n_c, [None] * n_c
    d_mod = [[None] * 3 for _ in range(depth)]
    d_norm_g = [[None] * 3 for _ in range(depth)]
    for i in reversed(range(depth)):
        for sub in reversed(range(3)):
            rec = saved[i][sub]
            if (i, sub) == (depth - 1, 2):
                dy, d_gate, dy_sum = gate_bwd(dx, rec['y'], rec['gate'], rec['coef'])
            if rec['kind'] == 'ffn':
                j = rec['j']
                dh, d1, d3 = ffn_bwd(dy, rec['g1'], rec['g3'], w1t[i, j], w3t[i, j], w2[i, j])
                g_w1t[i][j] = matmul_tn(d1, rec['h'])
                g_w3t[i][j] = matmul_tn(d3, rec['h'])
                g_w2[i][j] = matmul_tn(rec['act'], dy)
            elif rec['kind'] == 'conv':
                k = rec['idx']
                dsw = matmul_nt(dy, w_out[k])
                g_w_out[k] = matmul_tn(rec['sw'], dy)
                g_b_out[k] = dy_sum
                dpre, g_b_in[k], ddw, g_dw_b[k], g_ln_g[k], g_ln_b[k] = conv_mid_bwd(
                    rec['pre'], rec['v'], dsw, a_dw_f[k], a_ln_g_f[k].reshape(1, D), a_ln_b_f[k].reshape(1, D))
                g_dw[k] = ddw[:CONV_W]
                dh = matmul_nn(dpre, w_in_t[k])
                g_w_in_t[k] = matmul_tn(dpre, rec['h'])
            elif rec['kind'] == 'attn':
                k = rec['idx']
                do = matmul_nt(dy, w_o[k], out_dtype=BF)
                g_w_o[k] = matmul_tn(rec['o'], dy)
                dq, dk, dv = attn_bwd(rec['qkvn'], do, rec['r_all'])
                dqkv, dgains = qknorm_bwd(rec['qkv'], dq, dk, dv, rec['qg2'], rec['kg2'])
                dgains = dgains.reshape(3, N_HEADS, HEAD_DIM)
                g_q_g[k], g_k_g[k] = jnp.sum(dgains[0], axis=0), jnp.sum(dgains[1], axis=0)
                dh = matmul_nn(dqkv, w_qkv_t[k])
                g_w_qkv_t[k] = matmul_tn(dqkv, rec['h'])
            else:
                k = rec['idx']
                dh, dyy, g_p_b[k], g_p_scale[k] = pool_bwd(dy, rec['d'], p_w_f[k], rec['pb'], rec['ps'])
                full = matmul_tn(rec['d'], dyy)
                G = p_w_f.shape[1]
                Dg = D // G
                g_p_w[k] = jnp.stack([full[g * Dg:(g + 1) * Dg, g * Dg:(g + 1) * Dg] for g in range(G)])
            if (i, sub) == (0, 0):
                dx, d_shift, d_scale, d_norm_g[i][sub] = mod_bwd(rec['x'], dh, dx, rec['g'], rec['scale'])
                nxt = None
            else:
                before = saved[i][sub - 1] if sub > 0 else saved[i - 1][2]
                dx, d_shift, d_scale, d_norm_g[i][sub], *nxt = mod_gate_bwd(rec['x'], dh, dx, rec['g'], rec['scale'],
                                                                             before['y'], before['gate'], before['coef'])
            d_mod[i][sub] = jnp.concatenate([d_shift, d_scale, d_gate], axis=0)
            if nxt is not None:
                dy, d_gate, dy_sum = nxt
    grad_x = dx[None]

    d_mod_mine = jnp.stack([jnp.stack(r) for r in d_mod])
    small_g = [d_mod_mine, jnp.stack([jnp.concatenate(r, axis=0) for r in d_norm_g]),
               jnp.stack(g_b_in), jnp.stack(g_dw), jnp.stack(g_dw_b), jnp.stack(g_ln_g), jnp.stack(g_ln_b), jnp.stack(g_b_out),
               jnp.stack(g_q_g), jnp.stack(g_k_g), jnp.stack(g_p_b), jnp.stack(g_p_scale)]
    stack2 = lambda g: jnp.stack([jnp.stack(r) for r in g])
    blocks = lambda g, ax: g.reshape(g.shape[:ax] + (N_DEV, g.shape[ax] // N_DEV) + g.shape[ax + 1:])
    big_g = [blocks(stack2(g_w1t), 2), blocks(stack2(g_w3t), 2), blocks(stack2(g_w2), 2), blocks(jnp.stack(g_w_in_t), 1),
             blocks(jnp.stack(g_w_out), 1), blocks(jnp.stack(g_w_qkv_t), 1), blocks(jnp.stack(g_w_o), 1), blocks(jnp.stack(g_p_w), 2)]
    big_ax = [2, 2, 2, 1, 1, 1, 1, 2]
    got = exchange(big_g + [_pack(small_g)], big_ax + [None])
    p_w1t, p_w3t, p_w2, p_w_in_t, p_w_out, p_w_qkv_t, p_w_o, p_p_w = got[:8]
    (dmod_all, dng_all, dbin_all, ddw_all, ddwb_all, dlng_all, dlnb_all, dbout_all, dqg_all, dkg_all, dpb_all,
     dps_all) = _unpack(got[8], [v.shape for v in small_g])
    dmod_all = dmod_all.reshape(N_DEV, depth, 9 * D)

    dmod_mine = lax.dynamic_slice_in_dim(dmod_all, me * n_mod, n_mod, axis=2)
    e_t = e_all.T
    grad_ada_w = jnp.stack([outer_rows(e_t, dmod_mine[:, i]) for i in range(depth)])
    de_part = matmul_nt(dmod_mine[:, 0], ada_w[0])
    for i in range(1, depth):
        de_part = de_part + matmul_nt(dmod_mine[:, i], ada_w[i])
    de_parts = exchange([de_part], [None])[0]
    dpre_all, grad_cond_b = cond_embed_bwd(de_parts, pre_all)
    grad_cond_w = outer_rows(c_mine.T, dpre_all)

    def mine(g_all, axis, size):
        return lax.dynamic_slice_in_dim(g_all, me * size, size, axis=axis)

    def upd(name, parts, shape2, to_out=lambda t: t, from_in=lambda t: t):
        w, m, v = (from_in(A[p + name]).reshape(shape2) for p in ('', 'm_', 'v_'))
        outs = adamw(w, parts.reshape((parts.shape[0],) + shape2), m, v)
        return [to_out(o).reshape(A[name].shape) for o in outs]

    L = depth
    res = {}
    res['cond_w'] = upd('cond_w', grad_cond_w[None], (rows, D))
    res['cond_b'] = upd('cond_b', grad_cond_b[None], (1, D))
    res['ada_w'] = upd('ada_w', grad_ada_w[None], (L * D, n_mod))
    res['ada_b'] = upd('ada_b', dmod_all, (L, 9 * D))
    res['norm_g'] = upd('norm_g', mine(dng_all, 3, D // N_DEV), (L * 3, D // N_DEV))
    fs = Fd // N_DEV
    res['ffn_w1'] = upd('ffn_w1', p_w1t, (L * 2 * fs, D), to_out=lambda t: swap(t.reshape(L, 2, fs, D)), from_in=swap)
    res['ffn_w3'] = upd('ffn_w3', p_w3t, (L * 2 * fs, D), to_out=lambda t: swap(t.reshape(L, 2, fs, D)), from_in=swap)
    res['ffn_w2'] = upd('ffn_w2', p_w2, (L * 2 * fs, D))
    ws = 2 * D // N_DEV
    res['a_w_in'] = upd('a_w_in', p_w_in_t, (n_a * ws, D), to_out=lambda t: swap(t.reshape(n_a, ws, D)), from_in=swap)
    res['a_b_in'] = upd('a_b_in', mine(dbin_all.reshape(N_DEV, n_a, 2 * D), 2, ws), (n_a, ws))
    res['a_dw'] = upd('a_dw', mine(ddw_all, 3, rows), (n_a * CONV_W, rows))
    res['a_dw_b'] = upd('a_dw_b', mine(ddwb_all.reshape(N_DEV, n_a, D), 2, rows), (n_a, rows))
    res['a_ln_g'] = upd('a_ln_g', mine(dlng_all.reshape(N_DEV, n_a, D), 2, rows), (n_a, rows))
    res['a_ln_b'] = upd('a_ln_b', mine(dlnb_all.reshape(N_DEV, n_a, D), 2, rows), (n_a, rows))
    res['a_w_out'] = upd('a_w_out', p_w_out, (n_a * rows, D))
    res['a_b_out'] = upd('a_b_out', mine(dbout_all.reshape(N_DEV, n_a, D), 2, rows), (n_a, rows))
    qs = 3 * D // N_DEV
    res['b_w_qkv'] = upd('b_w_qkv', p_w_qkv_t, (n_b * qs, D), to_out=lambda t: swap(t.reshape(n_b, qs, D)), from_in=swap)
    res['b_q_g'] = upd('b_q_g', dqg_all, (n_b, HEAD_DIM))
    res['b_k_g'] = upd('b_k_g', dkg_all, (n_b, HEAD_DIM))
    res['b_w_o'] = upd('b_w_o', p_w_o, (n_b * rows, D))
    G = p_w.shape[1]
    Dg = D // G
    res['p_w'] = upd('p_w', p_p_w, (n_c * G * Dg // N_DEV, Dg))
    res['p_b'] = upd('p_b', mine(dpb_all.reshape(N_DEV, n_c, G, Dg), 3, Dg // N_DEV), (n_c * G, Dg // N_DEV))
    res['p_scale'] = upd('p_scale', mine(dps_all.reshape(N_DEV, n_c, D), 2, rows), (n_c, rows))

    outs = [loss, grad_x]
    for k in range(4):
        outs += [res[n][k] for n in WEIGHTS]
    return tuple(outs)
```
